```python
import math
import jax, jax.numpy as jnp
from jax import lax
import numpy as np

D_MODEL = 1024
BATCH = 8
SEQ = 4096
DEPTH = 2

GRID_W = 64
CTX_LEN = 256
BLOCK = 128
WINDOW = 128
ROPE_THETA = 10000.0
EPS = 1e-6
NEG_INF = -1e30

HEAD_DIM = 64
A_HEADS = 6
A_KV_HEADS = 2
B_HEADS = 6
B_Q_RANK = 192
B_KV_RANK = 128
B_NOPE = 64
B_ROPE = 32
B_V = 64
C_HEADS = 4
C_KV_HEADS = 2
N_EXPERTS = 16
CAP_FACTOR = 2
D_EXPERT = 1024

IN_SPLITS = (A_HEADS * HEAD_DIM, A_KV_HEADS * HEAD_DIM, A_KV_HEADS * HEAD_DIM,
             B_Q_RANK, B_KV_RANK, B_ROPE,
             C_HEADS * HEAD_DIM, C_KV_HEADS * HEAD_DIM, C_KV_HEADS * HEAD_DIM)
IN_WIDTH = sum(IN_SPLITS)
IN_OFFSETS = [int(v) for v in np.cumsum(IN_SPLITS)[:-1]]
MIX_WIDTH = A_HEADS * HEAD_DIM + B_HEADS * B_V + C_HEADS * HEAD_DIM

kernel_name = "hybrid_dit_gqa_mla_swa_ecmoe"


def rms_norm(x, g):
    xf = x.astype(jnp.float32)
    y = xf * lax.rsqrt(jnp.mean(xf * xf, axis=-1, keepdims=True) + EPS)
    return y.astype(x.dtype) * g


def modulate(x, shift, scale):
    return x * (1 + scale) + shift


def axial_rope_tables(length, dim):
    rows = length // GRID_W
    t = jnp.arange(rows * GRID_W)
    row = jnp.repeat(jnp.arange(rows), GRID_W).astype(jnp.float32)
    col = (t % GRID_W).astype(jnp.float32)
    axis_dim = dim // 2
    inv = ROPE_THETA ** (-jnp.arange(0, axis_dim, 2, dtype=jnp.float32) / axis_dim)
    ang_r = row[:, None] * inv[None, :]
    ang_c = col[:, None] * inv[None, :]
    return (jnp.cos(ang_r), jnp.sin(ang_r), jnp.cos(ang_c), jnp.sin(ang_c))


def _rotate(x, cos, sin):
    half = x.shape[-1] // 2
    x1, x2 = x[..., :half], x[..., half:]
    cos = cos[:, None, :].astype(x.dtype)
    sin = sin[:, None, :].astype(x.dtype)
    return jnp.concatenate([x1 * cos - x2 * sin, x2 * cos + x1 * sin], axis=-1)


def apply_axial_rope(x, tables):
    cos_r, sin_r, cos_c, sin_c = tables
    a = x.shape[-1] // 2
    return jnp.concatenate([_rotate(x[..., :a], cos_r, sin_r), _rotate(x[..., a:], cos_c, sin_c)], axis=-1)


def gqa_heads(pq, pk, pv, gq, gk, n_heads, n_kv, tables):
    b, l, _ = pq.shape
    q = rms_norm(pq.reshape(b, l, n_heads, HEAD_DIM), gq)
    k = rms_norm(pk.reshape(b, l, n_kv, HEAD_DIM), gk)
    v = pv.reshape(b, l, n_kv, HEAD_DIM)
    if tables is not None:
        q = apply_axial_rope(q, tables)
        k = apply_axial_rope(k, tables)
    return q, k, v


def mla_heads(pcq, pckv, pkr, g_cq, g_ckv, w_uq, w_ukv, g_qn, g_kn, g_qr, g_kr, tables):
    b, l, _ = pcq.shape
    q = (rms_norm(pcq, g_cq) @ w_uq).reshape(b, l, B_HEADS, B_NOPE + B_ROPE)
    kv = (rms_norm(pckv, g_ckv) @ w_ukv).reshape(b, l, B_HEADS, B_NOPE + B_V)
    q_nope = rms_norm(q[..., :B_NOPE], g_qn)
    q_pe = rms_norm(q[..., B_NOPE:], g_qr)
    k_nope = rms_norm(kv[..., :B_NOPE], g_kn)
    v = kv[..., B_NOPE:]
    k_pe = rms_norm(pkr.reshape(b, l, 1, B_ROPE), g_kr)
    if tables is not None:
        q_pe = apply_axial_rope(q_pe, tables)
        k_pe = apply_axial_rope(k_pe, tables)
    q = jnp.concatenate([q_nope, q_pe], axis=-1)
    k = jnp.concatenate([k_nope, jnp.broadcast_to(k_pe, (b, l, B_HEADS, B_ROPE))], axis=-1)
    return q, k, v


def attend(q, k, v):
    b, t, h, dq = q.shape
    kvh = k.shape[2]
    qg = q.reshape(b, t, kvh, h // kvh, dq)
    s = jnp.einsum('btkgd,bskd->bkgts', qg, k).astype(jnp.float32) * (dq ** -0.5)
    p = jax.nn.softmax(s, axis=-1).astype(v.dtype)
    o = jnp.einsum('bkgts,bskd->btkgd', p, v)
    return o.reshape(b, t, h * v.shape[-1])


def attend_blocked(q, k, v):
    b, l, h, dq = q.shape
    nb = l // BLOCK
    qb = jnp.swapaxes(q.reshape(b, nb, BLOCK, h, dq), 0, 1)
    ob = lax.map(lambda qi: attend(qi, k, v), qb)
    return jnp.swapaxes(ob, 0, 1).reshape(b, l, -1)


def sink_softmax(s, sink):
    sink = jnp.broadcast_to(sink.astype(jnp.float32), s.shape[:-1] + (1,))
    return jax.nn.softmax(jnp.concatenate([s, sink], axis=-1), axis=-1)[..., :-1]


def attend_sink(q, k, v, sink):
    b, t, h, dq = q.shape
    kvh = k.shape[2]
    g = h // kvh
    s = jnp.einsum('btkgd,bskd->bkgts', q.reshape(b, t, kvh, g, dq), k).astype(jnp.float32) * (dq ** -0.5)
    p = sink_softmax(s, sink.reshape(kvh, g, 1, 1)).astype(v.dtype)
    return jnp.einsum('bkgts,bskd->btkgd', p, v).reshape(b, t, h * v.shape[-1])


def attend_window(q, k, v, k_ctx, v_ctx, sink):
    b, l, h, dq = q.shape
    kvh = k.shape[2]
    g = h // kvh
    nb = l // BLOCK
    n_ctx = k_ctx.shape[1]
    span = BLOCK + 2 * WINDOW
    qb = q.reshape(b, nb, BLOCK, kvh, g, dq)
    pad = ((0, 0), (WINDOW, WINDOW), (0, 0), (0, 0))
    idx = jnp.arange(nb)[:, None] * BLOCK + jnp.arange(span)[None, :]
    kb = jnp.pad(k, pad)[:, idx]
    vb = jnp.pad(v, pad)[:, idx]
    qpos = jnp.arange(nb)[:, None] * BLOCK + jnp.arange(BLOCK)[None, :]
    kpos = (idx - WINDOW)[:, None, :]
    valid = (jnp.abs(qpos[:, :, None] - kpos) <= WINDOW) & (kpos >= 0) & (kpos < l)
    scale = dq ** -0.5
    s_win = jnp.einsum('bntkgd,bnskd->bnkgts', qb, kb).astype(jnp.float32) * scale
    s_win = jnp.where(valid[None, :, None, None], s_win, NEG_INF)
    s_ctx = jnp.einsum('bntkgd,bskd->bnkgts', qb, k_ctx).astype(jnp.float32) * scale
    p = sink_softmax(jnp.concatenate([s_ctx, s_win], axis=-1), sink.reshape(1, 1, kvh, g, 1, 1)).astype(v.dtype)
    o = (jnp.einsum('bnkgts,bskd->bntkgd', p[..., :n_ctx], v_ctx)
         + jnp.einsum('bnkgts,bnskd->bntkgd', p[..., n_ctx:], vb))
    return o.reshape(b, l, h * v.shape[-1])


def expert_choice_ffn(h, w_router, w_gate, w_up, w_down):
    b, n, d = h.shape
    cap = max(1, CAP_FACTOR * n // N_EXPERTS)
    aff = jax.nn.softmax((h @ w_router).astype(jnp.float32), axis=-1)
    gate, idx = lax.top_k(jnp.swapaxes(aff, 1, 2), cap)
    xe = jax.vmap(lambda hb, ib: hb[ib])(h, idx)
    hid = jax.nn.silu(jnp.einsum('becd,edf->becf', xe, w_gate)) * jnp.einsum('becd,edf->becf', xe, w_up)
    ye = jnp.einsum('becf,efd->becd', hid, w_down) * gate[..., None].astype(h.dtype)
    return jax.vmap(lambda ib, yb: jnp.zeros((n, d), yb.dtype).at[ib.reshape(-1)].add(yb.reshape(-1, d)))(idx, ye)


def setup_inputs(seed: int = 0) -> dict:
    key = jax.random.key(seed)
    ks = jax.random.split(key, 32)

    def nrm(k, shape, scale=1.0):
        return jax.random.normal(k, shape, jnp.float32) * scale

    def gain(k, shape):
        return 1.0 + 0.05 * jax.random.normal(k, shape, jnp.float32)

    L, D = DEPTH, D_MODEL
    return {
        "x": nrm(ks[0], (BATCH, SEQ, D)),
        "c": nrm(ks[1], (BATCH, D)),
        "ctx": nrm(ks[2], (BATCH, CTX_LEN, D)),
        "c_ctx": nrm(ks[3], (D,)),
        "w_ada": nrm(ks[4], (L, D, 6 * D), D ** -0.5),
        "b_ada": nrm(ks[5], (L, 6 * D), 0.01),
        "norm1_g": gain(ks[6], (L, D)),
        "norm2_g": gain(ks[7], (L, D)),
        "w_in": nrm(ks[8], (L, D, IN_WIDTH), D ** -0.5),
        "a_q_norm": gain(ks[9], (L, HEAD_DIM)),
        "a_k_norm": gain(ks[10], (L, HEAD_DIM)),
        "b_cq_norm": gain(ks[11], (L, B_Q_RANK)),
        "b_ckv_norm": gain(ks[12], (L, B_KV_RANK)),
        "w_uq": nrm(ks[13], (L, B_Q_RANK, B_HEADS * (B_NOPE + B_ROPE)), B_Q_RANK ** -0.5),
        "w_ukv": nrm(ks[14], (L, B_KV_RANK, B_HEADS * (B_NOPE + B_V)), B_KV_RANK ** -0.5),
        "b_qn_norm": gain(ks[15], (L, B_NOPE)),
        "b_kn_norm": gain(ks[16], (L, B_NOPE)),
        "b_qr_norm": gain(ks[17], (L, B_ROPE)),
        "b_kr_norm": gain(ks[18], (L, B_ROPE)),
        "c_q_norm": gain(ks[19], (L, HEAD_DIM)),
        "c_k_norm": gain(ks[20], (L, HEAD_DIM)),
        "c_sink": nrm(ks[21], (L, C_HEADS), 0.5),
        "w_out": nrm(ks[22], (L, MIX_WIDTH, D), MIX_WIDTH ** -0.5),
        "w_router": nrm(ks[23], (L, D, N_EXPERTS), D ** -0.5),
        "w_e_gate": nrm(ks[24], (L, N_EXPERTS, D, D_EXPERT), D ** -0.5),
        "w_e_up": nrm(ks[25], (L, N_EXPERTS, D, D_EXPERT), D ** -0.5),
        "w_e_down": nrm(ks[26], (L, N_EXPERTS, D_EXPERT, D), D_EXPERT ** -0.5),
    }


def reference(x, c, ctx, c_ctx, w_ada, b_ada, norm1_g, norm2_g, w_in, a_q_norm, a_k_norm,
              b_cq_norm, b_ckv_norm, w_uq, w_ukv, b_qn_norm, b_kn_norm, b_qr_norm, b_kr_norm,
              c_q_norm, c_k_norm, c_sink, w_out, w_router, w_e_gate, w_e_up, w_e_down):
    length = x.shape[1]
    rope_hd = axial_rope_tables(length, HEAD_DIM)
    rope_b = axial_rope_tables(length, B_ROPE)
    for l in range(DEPTH):
        last = l == DEPTH - 1
        mod = (jax.nn.silu(c) @ w_ada[l] + b_ada[l])[:, None, :]
        mod_c = (jax.nn.silu(c_ctx) @ w_ada[l] + b_ada[l])[None, None, :]
        sh1, sc1, g1, sh2, sc2, g2 = jnp.split(mod, 6, axis=-1)
        csh1, csc1, cg1, csh2, csc2, cg2 = jnp.split(mod_c, 6, axis=-1)

        h = modulate(rms_norm(x, norm1_g[l]), sh1, sc1)
        hc = modulate(rms_norm(ctx, norm1_g[l]), csh1, csc1)
        aq, ak, av, bcq, bckv, bkr, cq, ck, cv = jnp.split(h @ w_in[l], IN_OFFSETS, axis=-1)
        aqc, akc, avc, bcqc, bckvc, bkrc, cqc, ckc, cvc = jnp.split(hc @ w_in[l], IN_OFFSETS, axis=-1)

        qa, ka, va = gqa_heads(aq, ak, av, a_q_norm[l], a_k_norm[l], A_HEADS, A_KV_HEADS, rope_hd)
        qac, kac, vac = gqa_heads(aqc, akc, avc, a_q_norm[l], a_k_norm[l], A_HEADS, A_KV_HEADS, None)
        o_a = attend_blocked(qa, jnp.concatenate([kac, ka], axis=1), jnp.concatenate([vac, va], axis=1))

        qb, kb, vb = mla_heads(bcq, bckv, bkr, b_cq_norm[l], b_ckv_norm[l], w_uq[l], w_ukv[l],
                               b_qn_norm[l], b_kn_norm[l], b_qr_norm[l], b_kr_norm[l], rope_b)
        qbc, kbc, vbc = mla_heads(bcqc, bckvc, bkrc, b_cq_norm[l], b_ckv_norm[l], w_uq[l], w_ukv[l],
                                  b_qn_norm[l], b_kn_norm[l], b_qr_norm[l], b_kr_norm[l], None)
        o_b = attend_blocked(qb, jnp.concatenate([kbc, kb], axis=1), jnp.concatenate([vbc, vb], axis=1))

        qc, kc, vc = gqa_heads(cq, ck, cv, c_q_norm[l], c_k_norm[l], C_HEADS, C_KV_HEADS, rope_hd)
        qcc, kcc, vcc = gqa_heads(cqc, ckc, cvc, c_q_norm[l], c_k_norm[l], C_HEADS, C_KV_HEADS, None)
        o_c = attend_window(qc, kc, vc, kcc, vcc, c_sink[l])

        x = x + g1 * (jnp.concatenate([o_a, o_b, o_c], axis=-1) @ w_out[l])
        if not last:
            o_ctx = jnp.concatenate([attend(qac, kac, vac), attend(qbc, kbc, vbc),
                                     attend_sink(qcc, kcc, vcc, c_sink[l])], axis=-1) @ w_out[l]
            ctx = ctx + cg1 * o_ctx

        h2 = modulate(rms_norm(x, norm2_g[l]), sh2, sc2)
        x = x + g2 * expert_choice_ffn(h2, w_router[l], w_e_gate[l], w_e_up[l], w_e_down[l])
        if not last:
            hc2 = modulate(rms_norm(ctx, norm2_g[l]), csh2, csc2)
            ctx = ctx + cg2 * expert_choice_ffn(hc2, w_router[l], w_e_gate[l], w_e_up[l], w_e_down[l])
    return x
```

```python
import functools
import math

import numpy as np
import jax
import jax.numpy as jnp
from jax import lax
from jax.experimental import pallas as pl
from jax.experimental.pallas import tpu as pltpu

F32 = jnp.float32
BF16 = jnp.bfloat16

GRID_W = 64
ROPE_THETA = 10000.0
EPS = 1e-6
NEG_INF = -1e30
HEAD_DIM = 64
A_HEADS, A_KV_HEADS = 6, 2
B_HEADS, B_Q_RANK, B_KV_RANK, B_NOPE, B_ROPE, B_V = 6, 192, 128, 64, 32, 64
C_HEADS, C_KV_HEADS = 4, 2
N_EXPERTS = 16
CAP_FACTOR = 2
WINDOW = 128

LANE = 128
ROW_TILE = 256
KEY_CHUNK = 512
SLOT_BLOCK = 128
TOK_CHUNK = 256
VMEM_LIMIT = 56 * 1024 * 1024

_QA, _KA, _VA = 0, 384, 512
_QC, _KC, _VC = 640, 896, 1024
_BCQ, _BCKV, _BKR = 1152, 1408, 1536
IN_PAD = 1664
_ORIG = dict(aq=0, ak=384, av=512, bcq=640, bckv=832, bkr=960, cq=992, ck=1248, cv=1376)


def _cparams(sem, vmem=VMEM_LIMIT):
    return pltpu.CompilerParams(dimension_semantics=sem, vmem_limit_bytes=vmem)


def _pair_cols(base, n_heads):
    half = n_heads // 2
    cols = []
    for i in range(half):
        cols += list(range(base + i * 64, base + (i + 1) * 64))
        cols += list(range(base + (half + i) * 64, base + (half + i + 1) * 64))
    return cols


def _in_perm():
    perm = -np.ones((IN_PAD,), np.int64)
    perm[_QA:_QA + 384] = _pair_cols(_ORIG["aq"], A_HEADS)
    perm[_KA:_KA + 128] = np.arange(_ORIG["ak"], _ORIG["ak"] + 128)
    perm[_VA:_VA + 128] = np.arange(_ORIG["av"], _ORIG["av"] + 128)
    perm[_QC:_QC + 256] = _pair_cols(_ORIG["cq"], C_HEADS)
    perm[_KC:_KC + 128] = np.arange(_ORIG["ck"], _ORIG["ck"] + 128)
    perm[_VC:_VC + 128] = np.arange(_ORIG["cv"], _ORIG["cv"] + 128)
    perm[_BCQ:_BCQ + B_Q_RANK] = np.arange(_ORIG["bcq"], _ORIG["bcq"] + B_Q_RANK)
    perm[_BCKV:_BCKV + 128] = np.arange(_ORIG["bckv"], _ORIG["bckv"] + 128)
    perm[_BKR + 64:_BKR + 96] = np.arange(_ORIG["bkr"], _ORIG["bkr"] + 32)
    return perm


def _take_cols(w, perm):
    safe = np.where(perm >= 0, perm, 0)
    out = jnp.take(w, jnp.asarray(safe, jnp.int32), axis=1)
    return jnp.where(jnp.asarray(perm >= 0)[None, :], out, 0.0)


def _uq_perm():
    perm = -np.ones((B_HEADS * LANE,), np.int64)
    for h in range(B_HEADS):
        perm[h * LANE:h * LANE + 96] = np.arange(h * 96, h * 96 + 96)
    return perm


def _ukv_perms():
    pk = -np.ones((B_HEADS * LANE,), np.int64)
    pv = np.zeros((B_HEADS * 64,), np.int64)
    for h in range(B_HEADS):
        pk[h * LANE:h * LANE + 64] = np.arange(h * 128, h * 128 + 64)
        pv[h * 64:(h + 1) * 64] = np.arange(h * 128 + 64, h * 128 + 128)
    return pk, pv


def _seg_matrices():
    i = np.arange(LANE)
    m64 = (i[:, None] // 64 == i[None, :] // 64).astype(np.float32) / 64.0
    mb = np.zeros((LANE, LANE), np.float32)
    mb[:64, :64] = 1.0 / 64.0
    mb[64:96, 64:96] = 1.0 / 32.0
    return m64, mb


def _rope_tables(length):
    rows = length // GRID_W
    t = jnp.arange(rows * GRID_W)
    row = jnp.repeat(jnp.arange(rows), GRID_W).astype(F32)
    col = (t % GRID_W).astype(F32)

    def tabs(dim):
        axis_dim = dim // 2
        inv = ROPE_THETA ** (-jnp.arange(0, axis_dim, 2, dtype=F32) / axis_dim)
        ar = row[:, None] * inv[None, :]
        ac = col[:, None] * inv[None, :]
        cos = jnp.concatenate([jnp.cos(ar), jnp.cos(ar), jnp.cos(ac), jnp.cos(ac)], axis=-1)
        sin = jnp.concatenate([-jnp.sin(ar), jnp.sin(ar), -jnp.sin(ac), jnp.sin(ac)], axis=-1)
        return cos, sin

    c64, s64 = tabs(HEAD_DIM)
    cb, sb = tabs(B_ROPE)
    n = c64.shape[0]
    ones64, zeros64 = jnp.ones((n, 64), F32), jnp.zeros((n, 64), F32)
    ones32, zeros32 = jnp.ones((n, 32), F32), jnp.zeros((n, 32), F32)
    return jnp.concatenate([c64, c64, s64, s64,
                            ones64, cb, ones32, zeros64, sb, zeros32], axis=-1)


def _mod_kernel(c_ref, w_ref, b_ref, o_ref):
    cv = c_ref[...]
    a = (cv * jax.nn.sigmoid(cv)).astype(BF16)
    o_ref[0] = jnp.dot(a, w_ref[0].astype(BF16), preferred_element_type=F32) + b_ref[0]


def _modulation(cc, w_ada, b_ada):
    n_layers, d, n6 = w_ada.shape
    rows = cc.shape[0]
    tn = 1536
    return pl.pallas_call(
        _mod_kernel,
        grid=(n_layers, n6 // tn),
        in_specs=[pl.BlockSpec((rows, d), lambda l, j: (0, 0)),
                  pl.BlockSpec((1, d, tn), lambda l, j: (l, 0, j)),
                  pl.BlockSpec((1, 1, tn), lambda l, j: (l, 0, j))],
        out_specs=pl.BlockSpec((1, rows, tn), lambda l, j: (l, 0, j)),
        out_shape=jax.ShapeDtypeStruct((n_layers, rows, n6), F32),
        compiler_params=_cparams(("parallel", "parallel")),
        name="adaln_mod",
    )(cc, w_ada, b_ada.reshape(n_layers, 1, n6))


def _seg_mean(xsq, m_ref):
    hi = xsq.astype(BF16)
    lo = (xsq - hi.astype(F32)).astype(BF16)
    m = m_ref[...]
    return jnp.dot(hi, m, preferred_element_type=F32) + jnp.dot(lo, m, preferred_element_type=F32)


def _rope(x, cos, sin, half):
    lane = lax.broadcasted_iota(jnp.int32, x.shape, 1)
    lo = (lane % (2 * half)) < half
    partner = jnp.where(lo, pltpu.roll(x, LANE - half, 1), pltpu.roll(x, half, 1))
    return x * cos + partner * sin


def _inproj_kernel(x_ref, mod_ref, g1_ref, w_ref, gains_ref, gcq_ref, gckv_ref, wuq_ref, wukvk_ref,
                   wukvv_ref, m64_ref, mb_ref, *rest, use_rope):
    if use_rope:
        rope_ref = rest[0]
        outs = rest[1:]
    else:
        rope_ref = None
        outs = rest
    qa_ref, ka_ref, va_ref, qb_ref, kb_ref, vb_ref, qc_ref, kc_ref, vc_ref = outs

    x = x_ref[0]
    y = x * lax.rsqrt(jnp.mean(x * x, axis=-1, keepdims=True) + EPS) * g1_ref[...]
    h = y * (1.0 + mod_ref[0, 0, 1:2, :]) + mod_ref[0, 0, 0:1, :]
    p = jnp.dot(h.astype(BF16), w_ref[...], preferred_element_type=F32)

    def head_block(xb, m_ref, gain_row, rope_cols, half):
        y = xb * lax.rsqrt(_seg_mean(xb * xb, m_ref) + EPS) * gains_ref[gain_row:gain_row + 1, :]
        if rope_ref is not None and rope_cols is not None:
            c0, s0 = rope_cols
            y = _rope(y, rope_ref[:, c0:c0 + LANE], rope_ref[:, s0:s0 + LANE], half)
        return y

    r64 = (0, 128)
    rb = (256, 384)
    for i in range(3):
        qa_ref[0, :, i * LANE:(i + 1) * LANE] = head_block(
            p[:, _QA + i * LANE:_QA + (i + 1) * LANE], m64_ref, 0, r64, 16).astype(BF16)
    ka_ref[0] = head_block(p[:, _KA:_KA + LANE], m64_ref, 1, r64, 16).astype(BF16)
    va_ref[0] = p[:, _VA:_VA + LANE].astype(BF16)
    for i in range(2):
        qc_ref[0, :, i * LANE:(i + 1) * LANE] = head_block(
            p[:, _QC + i * LANE:_QC + (i + 1) * LANE], m64_ref, 2, r64, 16).astype(BF16)
    kc_ref[0] = head_block(p[:, _KC:_KC + LANE], m64_ref, 3, r64, 16).astype(BF16)
    vc_ref[0] = p[:, _VC:_VC + LANE].astype(BF16)

    cq = p[:, _BCQ:_BCQ + 256]
    cq = cq * lax.rsqrt(jnp.sum(cq * cq, axis=-1, keepdims=True) * (1.0 / B_Q_RANK) + EPS) * gcq_ref[...]
    qb = jnp.dot(cq.astype(BF16), wuq_ref[...], preferred_element_type=F32)
    ckv = p[:, _BCKV:_BCKV + LANE]
    ckv = ckv * lax.rsqrt(jnp.mean(ckv * ckv, axis=-1, keepdims=True) + EPS) * gckv_ref[...]
    ckv = ckv.astype(BF16)
    kbn = jnp.dot(ckv, wukvk_ref[...], preferred_element_type=F32)
    vb_ref[0] = jnp.dot(ckv, wukvv_ref[...], preferred_element_type=F32).astype(BF16)
    kpe = head_block(p[:, _BKR:_BKR + LANE], mb_ref, 6, rb, 8)
    for hh in range(B_HEADS):
        sl = slice(hh * LANE, (hh + 1) * LANE)
        qb_ref[0, :, sl] = head_block(qb[:, sl], mb_ref, 4, rb, 8).astype(BF16)
        kb_ref[0, :, sl] = (head_block(kbn[:, sl], mb_ref, 5, None, 8) + kpe).astype(BF16)


def _inproj(xs, modv, kind, g1, w_in_p, gains, gcq, gckv, wuq, wukvk, wukvv, m64, mb, rope):
    b, n, d = xs.shape
    tm = ROW_TILE
    use_rope = rope is not None
    const = lambda *shape: pl.BlockSpec(shape, lambda i, j: (0,) * len(shape))
    in_specs = [pl.BlockSpec((1, tm, d), lambda i, j: (i, j, 0)),
                pl.BlockSpec((1, 1, 8, d), lambda i, j: (i, kind, 0, 0)),
                const(1, d), const(d, IN_PAD), const(8, LANE), const(1, 256), const(1, LANE),
                const(256, 768), const(LANE, 768), const(LANE, 384), const(LANE, LANE), const(LANE, LANE)]
    args = [xs, modv, g1, w_in_p, gains, gcq, gckv, wuq, wukvk, wukvv, m64, mb]
    if use_rope:
        in_specs.append(pl.BlockSpec((tm, 512), lambda i, j: (j, 0)))
        args.append(rope)
    widths = (384, 128, 128, 768, 768, 384, 256, 128, 128)
    return pl.pallas_call(
        functools.partial(_inproj_kernel, use_rope=use_rope),
        grid=(b, n // tm),
        in_specs=in_specs,
        out_specs=[pl.BlockSpec((1, tm, w), lambda i, j: (i, j, 0)) for w in widths],
        out_shape=[jax.ShapeDtypeStruct((b, n, w), BF16) for w in widths],
        compiler_params=_cparams(("parallel", "parallel")),
        name="inproj_rope" if use_rope else "inproj_ctx",
    )(*args)


_A_HEADS_SPEC = tuple((h % 3, h // 3, 0, 0) for h in range(A_HEADS))
_A_OUT_SPEC = tuple((i, 3 + i) for i in range(3))
_B_HEADS_SPEC = tuple((h, None, h, h // 2) for h in range(B_HEADS))
_B_OUT_SPEC = tuple((2 * j, 2 * j + 1) for j in range(3))


def _masked_q(q_ref, qb, half):
    q = q_ref[0, :, qb * LANE:(qb + 1) * LANE]
    if half is None:
        return q
    lane = lax.broadcasted_iota(jnp.int32, q.shape, 1)
    keep = (lane < 64) if half == 0 else (lane >= 64)
    return jnp.where(keep, q, jnp.zeros_like(q))


def _qk(q, k):
    return lax.dot_general(q, k, (((1,), (1,)), ((), ())), preferred_element_type=F32)


def _dense_kernel(q_ref, kc_ref, vc_ref, *rest, heads, outs, n_lat_chunks):
    if n_lat_chunks:
        kl_ref, vl_ref, o_ref, m_sc, l_sc, acc_sc = rest
    else:
        o_ref, m_sc, l_sc, acc_sc = rest
    tq = q_ref.shape[1]

    for hi, (qb, half, kb, vb) in enumerate(heads):
        q = _masked_q(q_ref, qb, half)
        rows = slice(hi * tq, (hi + 1) * tq)
        ksl = slice(kb * LANE, (kb + 1) * LANE)
        vsl = slice(vb * LANE, (vb + 1) * LANE)

        s = _qk(q, kc_ref[0, :, ksl])
        m0 = jnp.max(s, axis=1, keepdims=True)
        p = jnp.exp(s - m0)
        m_sc[rows] = m0
        l_sc[rows] = jnp.sum(p, axis=1, keepdims=True)
        acc_sc[rows] = jnp.dot(p.astype(BF16), vc_ref[0, :, vsl], preferred_element_type=F32)

        if n_lat_chunks:
            def body(c, carry, q=q, rows=rows, ksl=ksl, vsl=vsl):
                start = pl.multiple_of(c * KEY_CHUNK, KEY_CHUNK)
                s = _qk(q, kl_ref[0, pl.ds(start, KEY_CHUNK), ksl])
                m_prev = m_sc[rows]
                m_new = jnp.maximum(m_prev, jnp.max(s, axis=1, keepdims=True))
                alpha = jnp.exp(m_prev - m_new)
                p = jnp.exp(s - m_new)
                l_sc[rows] = alpha * l_sc[rows] + jnp.sum(p, axis=1, keepdims=True)
                acc_sc[rows] = alpha * acc_sc[rows] + jnp.dot(
                    p.astype(BF16), vl_ref[0, pl.ds(start, KEY_CHUNK), vsl], preferred_element_type=F32)
                m_sc[rows] = m_new
                return carry

            lax.fori_loop(0, n_lat_chunks, body, 0)

    lane = lax.broadcasted_iota(jnp.int32, (tq, LANE), 1)
    for ob, (h_lo, h_hi) in enumerate(outs):
        r_lo = slice(h_lo * tq, (h_lo + 1) * tq)
        r_hi = slice(h_hi * tq, (h_hi + 1) * tq)
        o_lo = acc_sc[r_lo] / l_sc[r_lo]
        o_hi = acc_sc[r_hi] / l_sc[r_hi]
        o_ref[0, :, ob * LANE:(ob + 1) * LANE] = jnp.where(lane < 64, o_lo, o_hi).astype(BF16)


def _dense_attention(q, kc, vc, kl, vl, heads, outs, name):
    b, nq, qw = q.shape
    tq = ROW_TILE
    n_ctx, kw = kc.shape[1], kc.shape[2]
    vw = vc.shape[2]
    ow = len(outs) * LANE
    n_lat_chunks = 0 if kl is None else kl.shape[1] // KEY_CHUNK
    in_specs = [pl.BlockSpec((1, tq, qw), lambda i, j: (i, j, 0)),
                pl.BlockSpec((1, n_ctx, kw), lambda i, j: (i, 0, 0)),
                pl.BlockSpec((1, n_ctx, vw), lambda i, j: (i, 0, 0))]
    args = [q, kc, vc]
    if kl is not None:
        n_lat = kl.shape[1]
        in_specs += [pl.BlockSpec((1, n_lat, kw), lambda i, j: (i, 0, 0)),
                     pl.BlockSpec((1, n_lat, vw), lambda i, j: (i, 0, 0))]
        args += [kl, vl]
    nh = len(heads)
    return pl.pallas_call(
        functools.partial(_dense_kernel, heads=heads, outs=outs, n_lat_chunks=n_lat_chunks),
        grid=(b, nq // tq),
        in_specs=in_specs,
        out_specs=pl.BlockSpec((1, tq, ow), lambda i, j: (i, j, 0)),
        out_shape=jax.ShapeDtypeStruct((b, nq, ow), BF16),
        scratch_shapes=[pltpu.VMEM((nh * tq, 1), F32), pltpu.VMEM((nh * tq, 1), F32),
                        pltpu.VMEM((nh * tq, LANE), F32)],
        compiler_params=_cparams(("parallel", "parallel")),
        name=name,
    )(*args)


def _window_kernel(q_ref, kc_ref, vc_ref, sink_ref, *rest, use_window):
    if use_window:
        kl_ref, vl_ref, o_ref = rest
    else:
        (o_ref,) = rest
    tq = q_ref.shape[1]
    span = tq + 2 * WINDOW
    j = pl.program_id(1)
    if use_window:
        n_lat = kl_ref.shape[1]
        t0 = j * tq
        start = pl.multiple_of(jnp.clip(t0 - WINDOW, 0, n_lat - span), LANE)
        qpos = t0 + lax.broadcasted_iota(jnp.int32, (tq, span), 0)
        kpos = start + lax.broadcasted_iota(jnp.int32, (tq, span), 1)
        valid = jnp.abs(qpos - kpos) <= WINDOW
        kw = kl_ref[0, pl.ds(start, span), :]
        vw = vl_ref[0, pl.ds(start, span), :]
    kcx = kc_ref[0]
    vcx = vc_ref[0]
    lane = lax.broadcasted_iota(jnp.int32, (tq, LANE), 1)
    res = []
    for h in range(C_HEADS):
        half = h // (C_HEADS // C_KV_HEADS)
        q = _masked_q(q_ref, h % 2, half)
        sink = sink_ref[0:1, h:h + 1]
        s_c = _qk(q, kcx)
        m = jnp.maximum(jnp.max(s_c, axis=1, keepdims=True), sink)
        if use_window:
            s_w = jnp.where(valid, _qk(q, kw), NEG_INF)
            m = jnp.maximum(m, jnp.max(s_w, axis=1, keepdims=True))
        p_c = jnp.exp(s_c - m)
        l = jnp.sum(p_c, axis=1, keepdims=True) + jnp.exp(sink - m)
        o = jnp.dot(p_c.astype(BF16), vcx, preferred_element_type=F32)
        if use_window:
            p_w = jnp.exp(s_w - m)
            l = l + jnp.sum(p_w, axis=1, keepdims=True)
            o = o + jnp.dot(p_w.astype(BF16), vw, preferred_element_type=F32)
        res.append(o / l)
    for ob in range(2):
        o_ref[0, :, ob * LANE:(ob + 1) * LANE] = jnp.where(lane < 64, res[ob], res[2 + ob]).astype(BF16)


def _window_attention(q, kc, vc, kl, vl, sink, name):
    b, nq, qw = q.shape
    tq = ROW_TILE
    n_ctx = kc.shape[1]
    use_window = kl is not None
    in_specs = [pl.BlockSpec((1, tq, qw), lambda i, j: (i, j, 0)),
                pl.BlockSpec((1, n_ctx, LANE), lambda i, j: (i, 0, 0)),
                pl.BlockSpec((1, n_ctx, LANE), lambda i, j: (i, 0, 0)),
                pl.BlockSpec((8, LANE), lambda i, j: (0, 0))]
    args = [q, kc, vc, sink]
    if use_window:
        n_lat = kl.shape[1]
        in_specs += [pl.BlockSpec((1, n_lat, LANE), lambda i, j: (i, 0, 0)),
                     pl.BlockSpec((1, n_lat, LANE), lambda i, j: (i, 0, 0))]
        args += [kl, vl]
    return pl.pallas_call(
        functools.partial(_window_kernel, use_window=use_window),
        grid=(b, nq // tq),
        in_specs=in_specs,
        out_specs=pl.BlockSpec((1, tq, qw), lambda i, j: (i, j, 0)),
        out_shape=jax.ShapeDtypeStruct((b, nq, qw), BF16),
        compiler_params=_cparams(("parallel", "parallel")),
        name=name,
    )(*args)


def _outproj_kernel(oa_ref, ob_ref, oc_ref, x_ref, mod_ref, g2n_ref, w1_ref, w2_ref, w3_ref, wr_ref,
                    xo_ref, h2_ref, aff_ref):
    mix = (jnp.dot(oa_ref[0], w1_ref[...], preferred_element_type=F32)
           + jnp.dot(ob_ref[0], w2_ref[...], preferred_element_type=F32)
           + jnp.dot(oc_ref[0], w3_ref[...], preferred_element_type=F32))
    x = x_ref[0] + mod_ref[0, 0, 2:3, :] * mix
    xo_ref[0] = x
    y = x * lax.rsqrt(jnp.mean(x * x, axis=-1, keepdims=True) + EPS) * g2n_ref[...]
    h2 = y * (1.0 + mod_ref[0, 0, 4:5, :]) + mod_ref[0, 0, 3:4, :]
    h2_ref[0] = h2.astype(BF16)
    logits = lax.dot_general(wr_ref[...], h2, (((1,), (1,)), ((), ())),
                             preferred_element_type=F32, precision=lax.Precision.HIGHEST)
    z = jnp.exp(logits - jnp.max(logits, axis=0, keepdims=True))
    aff_ref[0] = z / jnp.sum(z, axis=0, keepdims=True)


def _outproj(oa, ob, oc, xs, modv, kind, g2n, w1, w2, w3, wr_t, name):
    b, n, d = xs.shape
    tm = ROW_TILE
    e = wr_t.shape[0]
    const = lambda *shape: pl.BlockSpec(shape, lambda i, j: (0,) * len(shape))
    row = lambda w: pl.BlockSpec((1, tm, w), lambda i, j: (i, j, 0))
    return pl.pallas_call(
        _outproj_kernel,
        grid=(b, n // tm),
        in_specs=[row(384), row(384), row(256), row(d),
                  pl.BlockSpec((1, 1, 8, d), lambda i, j: (i, kind, 0, 0)),
                  const(1, d), const(384, d), const(384, d), const(256, d), const(e, d)],
        out_specs=[row(d), row(d), pl.BlockSpec((1, e, tm), lambda i, j: (i, 0, j))],
        out_shape=[jax.ShapeDtypeStruct((b, n, d), F32), jax.ShapeDtypeStruct((b, n, d), BF16),
                   jax.ShapeDtypeStruct((b, e, n), F32)],
        compiler_params=_cparams(("parallel", "parallel")),
        name=name,
    )(oa, ob, oc, xs, modv, g2n, w1, w2, w3, wr_t)


def _prefix_exclusive(mask_f, tri):
    e, t = mask_f.shape
    ck = tri.shape[0]
    carry = jnp.zeros((e, 1), F32)
    parts, offs = [], [carry]
    for c in range(t // ck):
        blk = mask_f[:, c * ck:(c + 1) * ck]
        parts.append(jnp.dot(blk.astype(BF16), tri, preferred_element_type=F32) + carry)
        carry = carry + jnp.sum(blk, axis=1, keepdims=True)
        offs.append(carry)
    return jnp.concatenate(parts, axis=1) if len(parts) > 1 else parts[0], offs


def _topk_kernel(aff_ref, slot_ref, cnt_ref, *, cap, slot_stride):
    aff = aff_ref[0]
    e, t = aff.shape
    bits = pltpu.bitcast(aff, jnp.int32)
    kf = jnp.float32(cap)

    def search(i, lo):
        cand = lo | (jnp.int32(1) << (30 - i))
        n_ge = jnp.sum((bits >= cand).astype(F32), axis=1, keepdims=True)
        return jnp.where(n_ge >= kf, cand, lo)

    thr = lax.fori_loop(0, 31, search, jnp.zeros((e, 1), jnp.int32))
    gt = bits > thr
    eq = bits == thr
    need = kf - jnp.sum(gt.astype(F32), axis=1, keepdims=True)

    r = lax.broadcasted_iota(jnp.int32, (TOK_CHUNK, TOK_CHUNK), 0)
    c = lax.broadcasted_iota(jnp.int32, (TOK_CHUNK, TOK_CHUNK), 1)
    tri = (r < c).astype(BF16)
    tie_rank, _ = _prefix_exclusive(eq.astype(F32), tri)
    sel = gt | (eq & (tie_rank < need))
    slot, offs = _prefix_exclusive(sel.astype(F32), tri)
    base = pl.program_id(0) * slot_stride
    slot_ref[0] = jnp.where(sel, slot.astype(jnp.int32) + base, -1)
    lane = lax.broadcasted_iota(jnp.int32, (e, LANE), 1)
    cnt = jnp.zeros((e, LANE), jnp.int32)
    for ci, off in enumerate(offs):
        cnt = jnp.where(lane == ci, off.astype(jnp.int32) + base, cnt)
    cnt_ref[0] = cnt


def _topk(aff_t, cap, slot_stride, name):
    b, e, t = aff_t.shape
    return pl.pallas_call(
        functools.partial(_topk_kernel, cap=cap, slot_stride=slot_stride),
        grid=(b,),
        in_specs=[pl.BlockSpec((1, e, t), lambda i: (i, 0, 0))],
        out_specs=[pl.BlockSpec((1, e, t), lambda i: (i, 0, 0)),
                   pl.BlockSpec((1, e, LANE), lambda i: (i, 0, 0))],
        out_shape=[jax.ShapeDtypeStruct((b, e, t), jnp.int32), jax.ShapeDtypeStruct((b, e, LANE), jnp.int32)],
        compiler_params=_cparams(("parallel",)),
        name=name,
    )(aff_t)


def _moe_kernel(cnt_ref, h_ref, slot_ref, gate_ref, wg_ref, wu_ref, wd_ref, acc_ref,
                xe_sc, gs_sc, ye_sc, *, n_chunks, n_sb):
    bi = pl.program_id(0)
    ei = pl.program_id(1)
    n_e = pl.num_programs(1)

    @pl.when(ei == 0)
    def _():
        acc_ref[...] = jnp.zeros_like(acc_ref)

    xe_sc[...] = jnp.zeros_like(xe_sc)
    gs_sc[...] = jnp.zeros_like(gs_sc)
    cbase = (bi * n_e + ei) * 32
    srow = lax.broadcasted_iota(jnp.int32, (SLOT_BLOCK, TOK_CHUNK), 0)

    def overlap(c, sb):
        lo = cnt_ref[cbase + c]
        hi = cnt_ref[cbase + c + 1]
        return (lo < (sb + 1) * SLOT_BLOCK) & (hi > sb * SLOT_BLOCK)

    def onehot(c, sb):
        return slot_ref[0, 0, c:c + 1, :] == (srow + sb * SLOT_BLOCK)

    for c in range(n_chunks):
        for sb in range(n_sb):
            @pl.when(overlap(c, sb))
            def _(c=c, sb=sb):
                oh = onehot(c, sb)
                rows = slice(sb * SLOT_BLOCK, (sb + 1) * SLOT_BLOCK)
                xe_sc[rows] += jnp.dot(oh.astype(BF16), h_ref[0, c * TOK_CHUNK:(c + 1) * TOK_CHUNK, :],
                                       preferred_element_type=F32)
                g = jnp.where(oh, gate_ref[0, 0, c:c + 1, :], 0.0)
                gs_sc[rows] += jnp.sum(g, axis=1, keepdims=True)

    xe = xe_sc[...].astype(BF16)
    f = wg_ref.shape[2]
    fb = 512
    y = None
    for f0 in range(0, f, fb):
        hg = jnp.dot(xe, wg_ref[0, :, f0:f0 + fb], preferred_element_type=F32)
        hu = jnp.dot(xe, wu_ref[0, :, f0:f0 + fb], preferred_element_type=F32)
        hid = (hg * jax.nn.sigmoid(hg) * hu).astype(BF16)
        part = jnp.dot(hid, wd_ref[0, f0:f0 + fb, :], preferred_element_type=F32)
        y = part if y is None else y + part
    ye_sc[...] = (y * gs_sc[...]).astype(BF16)

    for c in range(n_chunks):
        for sb in range(n_sb):
            @pl.when(overlap(c, sb))
            def _(c=c, sb=sb):
                oh_t = onehot(c, sb).astype(F32).T.astype(BF16)
                rows = slice(sb * SLOT_BLOCK, (sb + 1) * SLOT_BLOCK)
                acc_ref[0, c * TOK_CHUNK:(c + 1) * TOK_CHUNK, :] += jnp.dot(
                    oh_t, ye_sc[rows], preferred_element_type=F32)


def _moe(h2, slot, gate, cnt, wg, wu, wd, cap, name):
    b, t, d = h2.shape
    e = slot.shape[1]
    f = wg.shape[2]
    n_chunks = t // TOK_CHUNK
    n_sb = cap // SLOT_BLOCK
    slot4 = slot.reshape(b, e, n_chunks, TOK_CHUNK)
    gate4 = gate.reshape(b, e, n_chunks, TOK_CHUNK)
    cnt_flat = cnt[:, :, :32].reshape(-1)
    grid_spec = pltpu.PrefetchScalarGridSpec(
        num_scalar_prefetch=1,
        grid=(b, e),
        in_specs=[pl.BlockSpec((1, t, d), lambda i, j, s: (i, 0, 0), pipeline_mode=pl.Buffered(1)),
                  pl.BlockSpec((1, 1, n_chunks, TOK_CHUNK), lambda i, j, s: (i, j, 0, 0)),
                  pl.BlockSpec((1, 1, n_chunks, TOK_CHUNK), lambda i, j, s: (i, j, 0, 0)),
                  pl.BlockSpec((1, d, f), lambda i, j, s: (j, 0, 0)),
                  pl.BlockSpec((1, d, f), lambda i, j, s: (j, 0, 0)),
                  pl.BlockSpec((1, f, d), lambda i, j, s: (j, 0, 0))],
        out_specs=pl.BlockSpec((1, t, d), lambda i, j, s: (i, 0, 0), pipeline_mode=pl.Buffered(1)),
        scratch_shapes=[pltpu.VMEM((cap, d), F32), pltpu.VMEM((cap, 1), F32), pltpu.VMEM((cap, d), BF16)],
    )
    return pl.pallas_call(
        functools.partial(_moe_kernel, n_chunks=n_chunks, n_sb=n_sb),
        grid_spec=grid_spec,
        out_shape=jax.ShapeDtypeStruct((b, t, d), F32),
        compiler_params=_cparams(("parallel", "arbitrary")),
        name=name,
    )(cnt_flat, h2, slot4, gate4, wg, wu, wd)


def _resid_kernel(x_ref, y_ref, mod_ref, o_ref):
    o_ref[0] = x_ref[0] + mod_ref[0, 0, 5:6, :] * y_ref[0]


def _resid(xs, ys, modv, kind, name):
    b, n, d = xs.shape
    tm = ROW_TILE
    row = pl.BlockSpec((1, tm, d), lambda i, j: (i, j, 0))
    return pl.pallas_call(
        _resid_kernel,
        grid=(b, n // tm),
        in_specs=[row, row, pl.BlockSpec((1, 1, 8, d), lambda i, j: (i, kind, 0, 0))],
        out_specs=row,
        out_shape=jax.ShapeDtypeStruct((b, n, d), F32),
        compiler_params=_cparams(("parallel", "parallel")),
        name=name,
    )(xs, ys, modv)


def _pad_lanes(v, width):
    return jnp.pad(v, (0, width - v.shape[0]))


def kernel(x, c, ctx, c_ctx, w_ada, b_ada, norm1_g, norm2_g, w_in, a_q_norm, a_k_norm, b_cq_norm, b_ckv_norm, w_uq, w_ukv, b_qn_norm, b_kn_norm, b_qr_norm, b_kr_norm, c_q_norm, c_k_norm, c_sink, w_out, w_router, w_e_gate, w_e_up, w_e_down):
    b, t, d = x.shape
    n_ctx = ctx.shape[1]
    depth = w_ada.shape[0]
    assert n_ctx == ROW_TILE and t % KEY_CHUNK == 0 and t % GRID_W == 0
    cap = max(1, CAP_FACTOR * t // N_EXPERTS)
    cap_c = max(1, CAP_FACTOR * n_ctx // N_EXPERTS)
    assert cap % SLOT_BLOCK == 0 and (b * cap_c) % SLOT_BLOCK == 0

    rows = ((b + 1 + 7) // 8) * 8
    cc = jnp.zeros((rows, d), F32).at[:b].set(c).at[b].set(c_ctx)
    mod = _modulation(cc, w_ada, b_ada)

    rope = _rope_tables(t)
    m64_np, mb_np = _seg_matrices()
    m64, mb = jnp.asarray(m64_np, BF16), jnp.asarray(mb_np, BF16)
    in_perm = _in_perm()
    uq_perm = _uq_perm()
    ukv_pk, ukv_pv = _ukv_perms()
    oa_rows = np.asarray(_pair_cols(0, A_HEADS))
    oc_rows = np.asarray(_pair_cols(0, C_HEADS)) + 768
    sa = HEAD_DIM ** -0.5
    sb = (B_NOPE + B_ROPE) ** -0.5
    z32, z64 = jnp.zeros((32,), F32), jnp.zeros((64,), F32)

    for l in range(depth):
        last = l == depth - 1
        m6 = mod[l].reshape(rows, 6, d)
        m8 = jnp.pad(m6, ((0, 0), (0, 2), (0, 0)))
        modv = jnp.stack([jnp.broadcast_to(m8[b][None], (b, 8, d)), m8[:b]], axis=1)

        w_in_p = _take_cols(w_in[l], in_perm).astype(BF16)
        wuq = jnp.pad(_take_cols(w_uq[l], uq_perm), ((0, 256 - B_Q_RANK), (0, 0))).astype(BF16)
        wukvk = _take_cols(w_ukv[l], ukv_pk).astype(BF16)
        wukvv = jnp.take(w_ukv[l], jnp.asarray(ukv_pv, jnp.int32), axis=1).astype(BF16)
        gains = jnp.stack([
            jnp.tile(a_q_norm[l], 2) * sa, jnp.tile(a_k_norm[l], 2),
            jnp.tile(c_q_norm[l], 2) * sa, jnp.tile(c_k_norm[l], 2),
            jnp.concatenate([b_qn_norm[l] * sb, b_qr_norm[l] * sb, z32]),
            jnp.concatenate([b_kn_norm[l], z64]),
            jnp.concatenate([z64, b_kr_norm[l], z32]),
            jnp.zeros((LANE,), F32)])
        gcq = _pad_lanes(b_cq_norm[l], 256)[None]
        gckv = b_ckv_norm[l][None]
        g1 = norm1_g[l][None]
        g2n = norm2_g[l][None]
        w1 = jnp.take(w_out[l], jnp.asarray(oa_rows, jnp.int32), axis=0).astype(BF16)
        w2 = w_out[l, 384:768].astype(BF16)
        w3 = jnp.take(w_out[l], jnp.asarray(oc_rows, jnp.int32), axis=0).astype(BF16)
        wr_t = w_router[l].T
        sink = jnp.zeros((8, LANE), F32).at[0, :C_HEADS].set(c_sink[l])
        wg = w_e_gate[l].astype(BF16)
        wu = w_e_up[l].astype(BF16)
        wd = w_e_down[l].astype(BF16)

        shared = (g1, w_in_p, gains, gcq, gckv, wuq, wukvk, wukvv, m64, mb)
        qa, ka, va, qb, kb, vb, qc, kc, vc = _inproj(x, modv, 1, *shared, rope)
        qac, kac, vac, qbc, kbc, vbc, qcc, kcc, vcc = _inproj(ctx, modv, 0, *shared, None)

        o_a = _dense_attention(qa, kac, vac, ka, va, _A_HEADS_SPEC, _A_OUT_SPEC, "attn_a")
        o_b = _dense_attention(qb, kbc, vbc, kb, vb, _B_HEADS_SPEC, _B_OUT_SPEC, "attn_b")
        o_c = _window_attention(qc, kcc, vcc, kc, vc, sink, "attn_c")
        x_mid, h2, aff = _outproj(o_a, o_b, o_c, x, modv, 1, g2n, w1, w2, w3, wr_t, "outproj")
        slot, cnt = _topk(aff, cap, 0, "topk")
        acc = _moe(h2, slot, aff, cnt, wg, wu, wd, cap, "moe")
        x = _resid(x_mid, acc, modv, 1, "resid")

        if not last:
            o_ac = _dense_attention(qac, kac, vac, None, None, _A_HEADS_SPEC, _A_OUT_SPEC, "attn_a_ctx")
            o_bc = _dense_attention(qbc, kbc, vbc, None, None, _B_HEADS_SPEC, _B_OUT_SPEC, "attn_b_ctx")
            o_cc = _window_attention(qcc, kcc, vcc, None, None, sink, "attn_c_ctx")
            c_mid, hc2, aff_c = _outproj(o_ac, o_bc, o_cc, ctx, modv, 0, g2n, w1, w2, w3, wr_t, "outproj_ctx")
            slot_c, cnt_c = _topk(aff_c, cap_c, cap_c, "topk_ctx")
            e = N_EXPERTS
            slot_f = jnp.transpose(slot_c, (1, 0, 2)).reshape(1, e, b * n_ctx)
            gate_f = jnp.transpose(aff_c, (1, 0, 2)).reshape(1, e, b * n_ctx)
            starts = jnp.transpose(cnt_c[:, :, 0], (1, 0))
            cnt_f = jnp.concatenate([starts, cnt_c[b - 1, :, 1:2],
                                     jnp.zeros((e, LANE - b - 1), jnp.int32)], axis=1)[None]
            acc_c = _moe(hc2.reshape(1, b * n_ctx, d), slot_f, gate_f, cnt_f, wg, wu, wd, b * cap_c, "moe_ctx")
            ctx = _resid(c_mid, acc_c.reshape(b, n_ctx, d), modv, 0, "resid_ctx")
    return x
```

```python
import functools
import math

import numpy as np
import jax
import jax.numpy as jnp
from jax import lax
from jax.experimental import pallas as pl
from jax.experimental.pallas import tpu as pltpu

F32 = jnp.float32
BF16 = jnp.bfloat16

GRID_W = 64
ROPE_THETA = 10000.0
EPS = 1e-6
NEG_INF = -1e30
HEAD_DIM = 64
A_HEADS, A_KV_HEADS = 6, 2
B_HEADS, B_Q_RANK, B_KV_RANK, B_NOPE, B_ROPE, B_V = 6, 192, 128, 64, 32, 64
C_HEADS, C_KV_HEADS = 4, 2
N_EXPERTS = 16
CAP_FACTOR = 2
WINDOW = 128

LANE = 128
ROW_TILE = 256
Q_TILE = 512
KEY_CHUNK = 512
LOG2E = 1.4426950408889634
SLOT_BLOCK = 128
TOK_CHUNK = 256
VMEM_LIMIT = 56 * 1024 * 1024

_QA, _KA, _VA = 0, 384, 512
_QC, _KC, _VC = 640, 896, 1024
_BCQ, _BCKV, _BKR = 1152, 1408, 1536
IN_PAD = 1664
_ORIG = dict(aq=0, ak=384, av=512, bcq=640, bckv=832, bkr=960, cq=992, ck=1248, cv=1376)


def _cparams(sem, vmem=VMEM_LIMIT):
    return pltpu.CompilerParams(dimension_semantics=sem, vmem_limit_bytes=vmem)


def _pair_cols(base, n_heads):
    half = n_heads // 2
    cols = []
    for i in range(half):
        cols += list(range(base + i * 64, base + (i + 1) * 64))
        cols += list(range(base + (half + i) * 64, base + (half + i + 1) * 64))
    return cols


def _in_perm():
    perm = -np.ones((IN_PAD,), np.int64)
    perm[_QA:_QA + 384] = _pair_cols(_ORIG["aq"], A_HEADS)
    perm[_KA:_KA + 128] = np.arange(_ORIG["ak"], _ORIG["ak"] + 128)
    perm[_VA:_VA + 128] = np.arange(_ORIG["av"], _ORIG["av"] + 128)
    perm[_QC:_QC + 256] = _pair_cols(_ORIG["cq"], C_HEADS)
    perm[_KC:_KC + 128] = np.arange(_ORIG["ck"], _ORIG["ck"] + 128)
    perm[_VC:_VC + 128] = np.arange(_ORIG["cv"], _ORIG["cv"] + 128)
    perm[_BCQ:_BCQ + B_Q_RANK] = np.arange(_ORIG["bcq"], _ORIG["bcq"] + B_Q_RANK)
    perm[_BCKV:_BCKV + 128] = np.arange(_ORIG["bckv"], _ORIG["bckv"] + 128)
    perm[_BKR + 64:_BKR + 96] = np.arange(_ORIG["bkr"], _ORIG["bkr"] + 32)
    return perm


def _take_cols(w, perm):
    safe = np.where(perm >= 0, perm, 0)
    out = jnp.take(w, jnp.asarray(safe, jnp.int32), axis=1)
    return jnp.where(jnp.asarray(perm >= 0)[None, :], out, 0.0)


def _uq_perm():
    perm = -np.ones((B_HEADS * LANE,), np.int64)
    for h in range(B_HEADS):
        perm[h * LANE:h * LANE + 96] = np.arange(h * 96, h * 96 + 96)
    return perm


def _ukv_perms():
    pk = -np.ones((B_HEADS * LANE,), np.int64)
    pv = np.zeros((B_HEADS * 64,), np.int64)
    for h in range(B_HEADS):
        pk[h * LANE:h * LANE + 64] = np.arange(h * 128, h * 128 + 64)
        pv[h * 64:(h + 1) * 64] = np.arange(h * 128 + 64, h * 128 + 128)
    return pk, pv


def _seg_matrices():
    i = np.arange(LANE)
    m64 = (i[:, None] // 64 == i[None, :] // 64).astype(np.float32) / 64.0
    mb = np.zeros((LANE, LANE), np.float32)
    mb[:64, :64] = 1.0 / 64.0
    mb[64:96, 64:96] = 1.0 / 32.0
    return m64, mb


def _rope_tables(length):
    rows = length // GRID_W
    t = jnp.arange(rows * GRID_W)
    row = jnp.repeat(jnp.arange(rows), GRID_W).astype(F32)
    col = (t % GRID_W).astype(F32)

    def tabs(dim):
        axis_dim = dim // 2
        inv = ROPE_THETA ** (-jnp.arange(0, axis_dim, 2, dtype=F32) / axis_dim)
        ar = row[:, None] * inv[None, :]
        ac = col[:, None] * inv[None, :]
        cos = jnp.concatenate([jnp.cos(ar), jnp.cos(ar), jnp.cos(ac), jnp.cos(ac)], axis=-1)
        sin = jnp.concatenate([-jnp.sin(ar), jnp.sin(ar), -jnp.sin(ac), jnp.sin(ac)], axis=-1)
        return cos, sin

    c64, s64 = tabs(HEAD_DIM)
    cb, sb = tabs(B_ROPE)
    n = c64.shape[0]
    ones64, zeros64 = jnp.ones((n, 64), F32), jnp.zeros((n, 64), F32)
    ones32, zeros32 = jnp.ones((n, 32), F32), jnp.zeros((n, 32), F32)
    return jnp.concatenate([c64, c64, s64, s64,
                            ones64, cb, ones32, zeros64, sb, zeros32], axis=-1)


def _mod_kernel(c_ref, w_ref, b_ref, o_ref):
    cv = c_ref[...]
    a = (cv * jax.nn.sigmoid(cv)).astype(BF16)
    o_ref[0] = jnp.dot(a, w_ref[0].astype(BF16), preferred_element_type=F32) + b_ref[0]


def _modulation(cc, w_ada, b_ada):
    n_layers, d, n6 = w_ada.shape
    rows = cc.shape[0]
    tn = 1536
    return pl.pallas_call(
        _mod_kernel,
        grid=(n_layers, n6 // tn),
        in_specs=[pl.BlockSpec((rows, d), lambda l, j: (0, 0)),
                  pl.BlockSpec((1, d, tn), lambda l, j: (l, 0, j)),
                  pl.BlockSpec((1, 1, tn), lambda l, j: (l, 0, j))],
        out_specs=pl.BlockSpec((1, rows, tn), lambda l, j: (l, 0, j)),
        out_shape=jax.ShapeDtypeStruct((n_layers, rows, n6), F32),
        compiler_params=_cparams(("parallel", "parallel")),
        name="adaln_mod",
    )(cc, w_ada, b_ada.reshape(n_layers, 1, n6))


def _seg_mean(xsq, m_ref):
    hi = xsq.astype(BF16)
    lo = (xsq - hi.astype(F32)).astype(BF16)
    m = m_ref[...]
    return jnp.dot(hi, m, preferred_element_type=F32) + jnp.dot(lo, m, preferred_element_type=F32)


def _rope(x, cos, sin, half):
    lane = lax.broadcasted_iota(jnp.int32, x.shape, 1)
    lo = (lane % (2 * half)) < half
    partner = jnp.where(lo, pltpu.roll(x, LANE - half, 1), pltpu.roll(x, half, 1))
    return x * cos + partner * sin


def _inproj_kernel(x_ref, mod_ref, g1_ref, w_ref, gains_ref, gcq_ref, gckv_ref, wuq_ref, wukvk_ref,
                   wukvv_ref, m64_ref, mb_ref, *rest, use_rope):
    if use_rope:
        rope_ref = rest[0]
        outs = rest[1:]
    else:
        rope_ref = None
        outs = rest
    qa_ref, ka_ref, va_ref, qb_ref, kb_ref, vb_ref, qc_ref, kc_ref, vc_ref = outs

    x = x_ref[0]
    y = x * lax.rsqrt(jnp.mean(x * x, axis=-1, keepdims=True) + EPS) * g1_ref[...]
    h = y * (1.0 + mod_ref[0, 0, 1:2, :]) + mod_ref[0, 0, 0:1, :]
    p = jnp.dot(h.astype(BF16), w_ref[...], preferred_element_type=F32)

    def head_block(xb, m_ref, gain_row, rope_cols, half):
        y = xb * lax.rsqrt(_seg_mean(xb * xb, m_ref) + EPS) * gains_ref[gain_row:gain_row + 1, :]
        if rope_ref is not None and rope_cols is not None:
            c0, s0 = rope_cols
            y = _rope(y, rope_ref[:, c0:c0 + LANE], rope_ref[:, s0:s0 + LANE], half)
        return y

    r64 = (0, 128)
    rb = (256, 384)
    for i in range(3):
        qa_ref[0, i] = head_block(
            p[:, _QA + i * LANE:_QA + (i + 1) * LANE], m64_ref, 0, r64, 16).astype(BF16)
    ka_ref[0, 0] = head_block(p[:, _KA:_KA + LANE], m64_ref, 1, r64, 16).astype(BF16)
    va_ref[0, 0] = p[:, _VA:_VA + LANE].astype(BF16)
    for i in range(2):
        qc_ref[0, :, i * LANE:(i + 1) * LANE] = head_block(
            p[:, _QC + i * LANE:_QC + (i + 1) * LANE], m64_ref, 2, r64, 16).astype(BF16)
    kc_ref[0] = head_block(p[:, _KC:_KC + LANE], m64_ref, 3, r64, 16).astype(BF16)
    vc_ref[0] = p[:, _VC:_VC + LANE].astype(BF16)

    cq = p[:, _BCQ:_BCQ + 256]
    cq = cq * lax.rsqrt(jnp.sum(cq * cq, axis=-1, keepdims=True) * (1.0 / B_Q_RANK) + EPS) * gcq_ref[...]
    qb = jnp.dot(cq.astype(BF16), wuq_ref[...], preferred_element_type=F32)
    ckv = p[:, _BCKV:_BCKV + LANE]
    ckv = ckv * lax.rsqrt(jnp.mean(ckv * ckv, axis=-1, keepdims=True) + EPS) * gckv_ref[...]
    ckv = ckv.astype(BF16)
    kbn = jnp.dot(ckv, wukvk_ref[...], preferred_element_type=F32)
    vbv = jnp.dot(ckv, wukvv_ref[...], preferred_element_type=F32).astype(BF16)
    for i in range(B_HEADS // 2):
        vb_ref[0, i] = vbv[:, i * LANE:(i + 1) * LANE]
    kpe = head_block(p[:, _BKR:_BKR + LANE], mb_ref, 6, rb, 8)
    for hh in range(B_HEADS):
        sl = slice(hh * LANE, (hh + 1) * LANE)
        qb_ref[0, hh] = head_block(qb[:, sl], mb_ref, 4, rb, 8).astype(BF16)
        kb_ref[0, hh] = (head_block(kbn[:, sl], mb_ref, 5, None, 8) + kpe).astype(BF16)


def _inproj(xs, modv, kind, g1, w_in_p, gains, gcq, gckv, wuq, wukvk, wukvv, m64, mb, rope):
    b, n, d = xs.shape
    tm = ROW_TILE
    use_rope = rope is not None
    const = lambda *shape: pl.BlockSpec(shape, lambda i, j: (0,) * len(shape))
    in_specs = [pl.BlockSpec((1, tm, d), lambda i, j: (i, j, 0)),
                pl.BlockSpec((1, 1, 8, d), lambda i, j: (i, kind, 0, 0)),
                const(1, d), const(d, IN_PAD), const(8, LANE), const(1, 256), const(1, LANE),
                const(256, 768), const(LANE, 768), const(LANE, 384), const(LANE, LANE), const(LANE, LANE)]
    args = [xs, modv, g1, w_in_p, gains, gcq, gckv, wuq, wukvk, wukvv, m64, mb]
    if use_rope:
        in_specs.append(pl.BlockSpec((tm, 512), lambda i, j: (j, 0)))
        args.append(rope)
    blocks = (3, 1, 1, B_HEADS, B_HEADS, B_HEADS // 2)
    widths = (256, 128, 128)
    out_specs = ([pl.BlockSpec((1, nb, tm, LANE), lambda i, j: (i, 0, j, 0)) for nb in blocks]
                 + [pl.BlockSpec((1, tm, w), lambda i, j: (i, j, 0)) for w in widths])
    out_shape = ([jax.ShapeDtypeStruct((b, nb, n, LANE), BF16) for nb in blocks]
                 + [jax.ShapeDtypeStruct((b, n, w), BF16) for w in widths])
    return pl.pallas_call(
        functools.partial(_inproj_kernel, use_rope=use_rope),
        grid=(b, n // tm),
        in_specs=in_specs,
        out_specs=out_specs,
        out_shape=out_shape,
        compiler_params=_cparams(("parallel", "parallel")),
        name="inproj_rope" if use_rope else "inproj_ctx",
    )(*args)


_A_MAPS = (lambda h: h % 3, lambda h: 0, lambda h: 0)
_B_MAPS = (lambda h: h, lambda h: h, lambda h: h // 2)


def _masked_q(q_ref, qb, half):
    q = q_ref[0, :, qb * LANE:(qb + 1) * LANE]
    if half is None:
        return q
    lane = lax.broadcasted_iota(jnp.int32, q.shape, 1)
    keep = (lane < 64) if half == 0 else (lane >= 64)
    return jnp.where(keep, q, jnp.zeros_like(q))


def _qk(q, k):
    return lax.dot_general(q, k, (((1,), (1,)), ((), ())), preferred_element_type=F32)


def _half_mask(q, half):
    lane = lax.broadcasted_iota(jnp.int32, q.shape, 1)
    return jnp.where((lane // 64) == half, q, jnp.zeros_like(q))


def _score_pass(q, chunks, s_w):
    mx = None
    for k_ref, _, st, sz, off in chunks:
        s = _qk(q, k_ref[0, 0, st:st + sz, :])
        s_w[:, off:off + sz] = s
        for t in range(sz // LANE):
            blk = s[:, t * LANE:(t + 1) * LANE]
            mx = blk if mx is None else jnp.maximum(mx, blk)
    return jnp.broadcast_to(jnp.max(mx, axis=1, keepdims=True), mx.shape)


def _value_pass(chunks, s_r, mb):
    acc = None
    for _, v_ref, st, sz, off in chunks:
        ps = [jnp.exp2(s_r[:, off + t * LANE:off + (t + 1) * LANE] - mb).astype(BF16)
              for t in range(sz // LANE)]
        v_ones = jnp.concatenate([v_ref[0, 0, st:st + sz, :], jnp.ones((sz, LANE), BF16)], axis=1)
        part = jnp.dot(jnp.concatenate(ps, axis=1), v_ones, preferred_element_type=F32)
        acc = part if acc is None else acc + part
    return (acc[:, :LANE] / acc[:, LANE:LANE + 1]).astype(BF16)


def _fused_passes(q, chunks, s_w, s_r, mb_r):
    mx = None
    acc = None
    for k_ref, v_ref, st, sz, off in chunks:
        s = _qk(q, k_ref[0, 0, st:st + sz, :])
        s_w[:, off:off + sz] = s
        for t in range(sz // LANE):
            blk = s[:, t * LANE:(t + 1) * LANE]
            mx = blk if mx is None else jnp.maximum(mx, blk)
        ps = [jnp.exp2(s_r[:, off + t * LANE:off + (t + 1) * LANE] - mb_r).astype(BF16)
              for t in range(sz // LANE)]
        v_ones = jnp.concatenate([v_ref[0, 0, st:st + sz, :], jnp.ones((sz, LANE), BF16)], axis=1)
        part = jnp.dot(jnp.concatenate(ps, axis=1), v_ones, preferred_element_type=F32)
        acc = part if acc is None else acc + part
    mb_new = jnp.broadcast_to(jnp.max(mx, axis=1, keepdims=True), mx.shape)
    return mb_new, (acc[:, :LANE] / acc[:, LANE:LANE + 1]).astype(BF16)


def _key_chunks(kc_ref, vc_ref, kl_ref, vl_ref):
    n_ctx = kc_ref.shape[2]
    chunks = [(kc_ref, vc_ref, 0, n_ctx, 0)]
    if kl_ref is not None:
        for c in range(kl_ref.shape[2] // KEY_CHUNK):
            chunks.append((kl_ref, vl_ref, c * KEY_CHUNK, KEY_CHUNK, n_ctx + c * KEY_CHUNK))
    return chunks


def _dense_kernel(q_ref, kc_ref, kl_ref, vcp_ref, vlp_ref, vcc_ref, vlc_ref, o_ref,
                  sa_sc, sb_sc, mba_sc, mbb_sc, *, mask_q, n_tiles, tiles_per_head):
    step = pl.program_id(1)
    tq = sa_sc.shape[0]

    @pl.when(step == 0)
    def _():
        sa_sc[...] = jnp.zeros(sa_sc.shape, F32)
        mba_sc[...] = jnp.zeros(mba_sc.shape, F32)

    q0 = q_ref[0, 0, 0:tq, :]
    q1 = q_ref[0, 0, tq:2 * tq, :]
    if mask_q:
        half = (jnp.minimum(2 * step, n_tiles - 2) // tiles_per_head) // 3
        q0 = _half_mask(q0, half)
        q1 = _half_mask(q1, half)

    mb_new, o_ref[0, 0:tq, :] = _fused_passes(
        q0, _key_chunks(kc_ref, vcp_ref, kl_ref, vlp_ref), sb_sc, sa_sc, mba_sc[...])
    mbb_sc[...] = mb_new
    mb_new, o_ref[0, tq:2 * tq, :] = _fused_passes(
        q1, _key_chunks(kc_ref, vcc_ref, kl_ref, vlc_ref), sa_sc, sb_sc, mbb_sc[...])
    mba_sc[...] = mb_new


def _dense_attention(q, kc, vc, kl, vl, n_heads, maps, mask_q, name):
    b, _, nq, _ = q.shape
    qmap, kmap, vmap = maps
    n_ctx, n_lat = kc.shape[2], kl.shape[2]
    tq = Q_TILE
    tph = nq // tq
    assert tph % 2 == 0
    n_tiles = n_heads * tph
    head_cur = lambda s: jnp.minimum(2 * s, n_tiles - 2) // tph
    head_prev = lambda s: jnp.maximum(2 * s - 1, 0) // tph
    blk = lambda n, fmap, head: pl.BlockSpec((1, 1, n, LANE), lambda i, s: (i, fmap(head(s)), 0, 0))
    q_spec = pl.BlockSpec((1, 1, 2 * tq, LANE),
                          lambda i, s: (i, qmap(head_cur(s)), (jnp.minimum(2 * s, n_tiles - 2) % tph) // 2, 0))
    out = pl.pallas_call(
        functools.partial(_dense_kernel, mask_q=mask_q, n_tiles=n_tiles, tiles_per_head=tph),
        grid=(b, n_tiles // 2 + 1),
        in_specs=[q_spec, blk(n_ctx, kmap, head_cur), blk(n_lat, kmap, head_cur),
                  blk(n_ctx, vmap, head_prev), blk(n_lat, vmap, head_prev),
                  blk(n_ctx, vmap, head_cur), blk(n_lat, vmap, head_cur)],
        out_specs=pl.BlockSpec((1, 2 * tq, LANE), lambda i, s: (i, s, 0)),
        out_shape=jax.ShapeDtypeStruct((b, (n_tiles + 2) * tq, LANE), BF16),
        scratch_shapes=[pltpu.VMEM((tq, n_ctx + n_lat), F32), pltpu.VMEM((tq, n_ctx + n_lat), F32),
                        pltpu.VMEM((tq, LANE), F32), pltpu.VMEM((tq, LANE), F32)],
        compiler_params=_cparams(("parallel", "arbitrary")),
        name=name,
    )(q, kc, kl, vc, vl, vc, vl)
    return out[:, tq:tq + n_tiles * tq].reshape(b, n_heads, nq, LANE)


def _dense_ctx_kernel(q_ref, kc_ref, vc_ref, o_ref, s_sc, *, mask_q):
    q = q_ref[0, 0]
    if mask_q:
        q = _half_mask(q, pl.program_id(1) // 3)
    chunks = _key_chunks(kc_ref, vc_ref, None, None)
    mb = _score_pass(q, chunks, s_sc)
    o_ref[0, 0] = _value_pass(chunks, s_sc, mb)


def _dense_attention_ctx(q, kc, vc, n_heads, maps, mask_q, name):
    b, _, nq, _ = q.shape
    qmap, kmap, vmap = maps
    blk = lambda fmap: pl.BlockSpec((1, 1, nq, LANE), lambda i, h: (i, fmap(h), 0, 0))
    return pl.pallas_call(
        functools.partial(_dense_ctx_kernel, mask_q=mask_q),
        grid=(b, n_heads),
        in_specs=[blk(qmap), blk(kmap), blk(vmap)],
        out_specs=pl.BlockSpec((1, 1, nq, LANE), lambda i, h: (i, h, 0, 0)),
        out_shape=jax.ShapeDtypeStruct((b, n_heads, nq, LANE), BF16),
        scratch_shapes=[pltpu.VMEM((nq, kc.shape[2]), F32)],
        compiler_params=_cparams(("parallel", "parallel")),
        name=name,
    )(q, kc, vc)


def _window_kernel(q_ref, kc_ref, vc_ref, sink_ref, *rest, use_window):
    if use_window:
        kl_ref, vl_ref, o_ref = rest
    else:
        (o_ref,) = rest
    tq = q_ref.shape[1]
    span = tq + 2 * WINDOW
    j = pl.program_id(1)
    if use_window:
        n_lat = kl_ref.shape[1]
        t0 = j * tq
        start = pl.multiple_of(jnp.clip(t0 - WINDOW, 0, n_lat - span), LANE)
        qpos = t0 + lax.broadcasted_iota(jnp.int32, (tq, span), 0)
        kpos = start + lax.broadcasted_iota(jnp.int32, (tq, span), 1)
        valid = jnp.abs(qpos - kpos) <= WINDOW
        kw = kl_ref[0, pl.ds(start, span), :]
        vw = vl_ref[0, pl.ds(start, span), :]
    kcx = kc_ref[0]
    vcx = vc_ref[0]
    lane = lax.broadcasted_iota(jnp.int32, (tq, LANE), 1)
    res = []
    for h in range(C_HEADS):
        half = h // (C_HEADS // C_KV_HEADS)
        q = _masked_q(q_ref, h % 2, half)
        sink = sink_ref[0:1, h:h + 1]
        s_c = _qk(q, kcx)
        m = jnp.maximum(jnp.max(s_c, axis=1, keepdims=True), sink)
        if use_window:
            s_w = jnp.where(valid, _qk(q, kw), NEG_INF)
            m = jnp.maximum(m, jnp.max(s_w, axis=1, keepdims=True))
        p_c = jnp.exp(s_c - m)
        l = jnp.sum(p_c, axis=1, keepdims=True) + jnp.exp(sink - m)
        o = jnp.dot(p_c.astype(BF16), vcx, preferred_element_type=F32)
        if use_window:
            p_w = jnp.exp(s_w - m)
            l = l + jnp.sum(p_w, axis=1, keepdims=True)
            o = o + jnp.dot(p_w.astype(BF16), vw, preferred_element_type=F32)
        res.append(o / l)
    for ob in range(2):
        o_ref[0, :, ob * LANE:(ob + 1) * LANE] = jnp.where(lane < 64, res[ob], res[2 + ob]).astype(BF16)


def _window_attention(q, kc, vc, kl, vl, sink, name):
    b, nq, qw = q.shape
    tq = ROW_TILE
    n_ctx = kc.shape[1]
    use_window = kl is not None
    in_specs = [pl.BlockSpec((1, tq, qw), lambda i, j: (i, j, 0)),
                pl.BlockSpec((1, n_ctx, LANE), lambda i, j: (i, 0, 0)),
                pl.BlockSpec((1, n_ctx, LANE), lambda i, j: (i, 0, 0)),
                pl.BlockSpec((8, LANE), lambda i, j: (0, 0))]
    args = [q, kc, vc, sink]
    if use_window:
        n_lat = kl.shape[1]
        in_specs += [pl.BlockSpec((1, n_lat, LANE), lambda i, j: (i, 0, 0)),
                     pl.BlockSpec((1, n_lat, LANE), lambda i, j: (i, 0, 0))]
        args += [kl, vl]
    return pl.pallas_call(
        functools.partial(_window_kernel, use_window=use_window),
        grid=(b, nq // tq),
        in_specs=in_specs,
        out_specs=pl.BlockSpec((1, tq, qw), lambda i, j: (i, j, 0)),
        out_shape=jax.ShapeDtypeStruct((b, nq, qw), BF16),
        compiler_params=_cparams(("parallel", "parallel")),
        name=name,
    )(*args)


def _outproj_kernel(oa_ref, ob_ref, oc_ref, x_ref, mod_ref, g2n_ref, w_ref, wr_ref,
                    xo_ref, h2_ref, aff_ref):
    tm = x_ref.shape[1]
    low = lax.broadcasted_iota(jnp.int32, (tm, LANE), 1) < 64
    parts = [jnp.where(low, oa_ref[0, i], oa_ref[0, 3 + i]) for i in range(3)]
    parts += [jnp.where(low, ob_ref[0, 2 * i], ob_ref[0, 2 * i + 1]) for i in range(3)]
    parts.append(oc_ref[0])
    mix = jnp.dot(jnp.concatenate(parts, axis=1), w_ref[...], preferred_element_type=F32)
    x = x_ref[0] + mod_ref[0, 0, 2:3, :] * mix
    xo_ref[0] = x
    y = x * lax.rsqrt(jnp.mean(x * x, axis=-1, keepdims=True) + EPS) * g2n_ref[...]
    h2 = y * (1.0 + mod_ref[0, 0, 4:5, :]) + mod_ref[0, 0, 3:4, :]
    h2_ref[0] = h2.astype(BF16)
    logits = lax.dot_general(wr_ref[...], h2, (((1,), (1,)), ((), ())),
                             preferred_element_type=F32, precision=lax.Precision.HIGHEST)
    z = jnp.exp(logits - jnp.max(logits, axis=0, keepdims=True))
    aff_ref[0] = z / jnp.sum(z, axis=0, keepdims=True)


def _outproj(oa, ob, oc, xs, modv, kind, g2n, w_mix, wr_t, name):
    b, n, d = xs.shape
    tm = ROW_TILE
    e = wr_t.shape[0]
    const = lambda *shape: pl.BlockSpec(shape, lambda i, j: (0,) * len(shape))
    row = lambda w: pl.BlockSpec((1, tm, w), lambda i, j: (i, j, 0))
    heads = lambda nh: pl.BlockSpec((1, nh, tm, LANE), lambda i, j: (i, 0, j, 0))
    return pl.pallas_call(
        _outproj_kernel,
        grid=(b, n // tm),
        in_specs=[heads(A_HEADS), heads(B_HEADS), row(256), row(d),
                  pl.BlockSpec((1, 1, 8, d), lambda i, j: (i, kind, 0, 0)),
                  const(1, d), const(d, d), const(e, d)],
        out_specs=[row(d), row(d), pl.BlockSpec((1, e, tm), lambda i, j: (i, 0, j))],
        out_shape=[jax.ShapeDtypeStruct((b, n, d), F32), jax.ShapeDtypeStruct((b, n, d), BF16),
                   jax.ShapeDtypeStruct((b, e, n), F32)],
        compiler_params=_cparams(("parallel", "parallel")),
        name=name,
    )(oa, ob, oc, xs, modv, g2n, w_mix, wr_t)


def _prefix_exclusive(mask_f, tri):
    e, t = mask_f.shape
    ck = tri.shape[0]
    carry = jnp.zeros((e, 1), F32)
    parts, offs = [], [carry]
    for c in range(t // ck):
        blk = mask_f[:, c * ck:(c + 1) * ck]
        parts.append(jnp.dot(blk.astype(BF16), tri, preferred_element_type=F32) + carry)
        carry = carry + jnp.sum(blk, axis=1, keepdims=True)
        offs.append(carry)
    return jnp.concatenate(parts, axis=1) if len(parts) > 1 else parts[0], offs


def _topk_kernel(aff_ref, slot_ref, cnt_ref, *, cap, slot_stride):
    aff = aff_ref[0]
    e, t = aff.shape
    bits = pltpu.bitcast(aff, jnp.int32)
    kf = jnp.float32(cap)

    def search(i, lo):
        cand = lo | (jnp.int32(1) << (30 - i))
        n_ge = jnp.sum((bits >= cand).astype(F32), axis=1, keepdims=True)
        return jnp.where(n_ge >= kf, cand, lo)

    thr = lax.fori_loop(0, 31, search, jnp.zeros((e, 1), jnp.int32))
    gt = bits > thr
    eq = bits == thr
    need = kf - jnp.sum(gt.astype(F32), axis=1, keepdims=True)

    r = lax.broadcasted_iota(jnp.int32, (TOK_CHUNK, TOK_CHUNK), 0)
    c = lax.broadcasted_iota(jnp.int32, (TOK_CHUNK, TOK_CHUNK), 1)
    tri = (r < c).astype(BF16)
    tie_rank, _ = _prefix_exclusive(eq.astype(F32), tri)
    sel = gt | (eq & (tie_rank < need))
    slot, offs = _prefix_exclusive(sel.astype(F32), tri)
    base = pl.program_id(0) * slot_stride
    slot_ref[0] = jnp.where(sel, slot.astype(jnp.int32) + base, -1)
    lane = lax.broadcasted_iota(jnp.int32, (e, LANE), 1)
    cnt = jnp.zeros((e, LANE), jnp.int32)
    for ci, off in enumerate(offs):
        cnt = jnp.where(lane == ci, off.astype(jnp.int32) + base, cnt)
    cnt_ref[0] = cnt


def _topk(aff_t, cap, slot_stride, name):
    b, e, t = aff_t.shape
    return pl.pallas_call(
        functools.partial(_topk_kernel, cap=cap, slot_stride=slot_stride),
        grid=(b,),
        in_specs=[pl.BlockSpec((1, e, t), lambda i: (i, 0, 0))],
        out_specs=[pl.BlockSpec((1, e, t), lambda i: (i, 0, 0)),
                   pl.BlockSpec((1, e, LANE), lambda i: (i, 0, 0))],
        out_shape=[jax.ShapeDtypeStruct((b, e, t), jnp.int32), jax.ShapeDtypeStruct((b, e, LANE), jnp.int32)],
        compiler_params=_cparams(("parallel",)),
        name=name,
    )(aff_t)


def _moe_kernel(cnt_ref, h_ref, slot_ref, gate_ref, wg_ref, wu_ref, wd_ref, acc_ref,
                xe_sc, gs_sc, ye_sc, *, n_chunks, n_sb):
    bi = pl.program_id(0)
    ei = pl.program_id(1)
    n_e = pl.num_programs(1)

    @pl.when(ei == 0)
    def _():
        acc_ref[...] = jnp.zeros_like(acc_ref)

    xe_sc[...] = jnp.zeros_like(xe_sc)
    gs_sc[...] = jnp.zeros_like(gs_sc)
    cbase = (bi * n_e + ei) * 32
    srow = lax.broadcasted_iota(jnp.int32, (SLOT_BLOCK, TOK_CHUNK), 0)

    def overlap(c, sb):
        lo = cnt_ref[cbase + c]
        hi = cnt_ref[cbase + c + 1]
        return (lo < (sb + 1) * SLOT_BLOCK) & (hi > sb * SLOT_BLOCK)

    def onehot(c, sb):
        return slot_ref[0, 0, c:c + 1, :] == (srow + sb * SLOT_BLOCK)

    for c in range(n_chunks):
        for sb in range(n_sb):
            @pl.when(overlap(c, sb))
            def _(c=c, sb=sb):
                oh = onehot(c, sb)
                rows = slice(sb * SLOT_BLOCK, (sb + 1) * SLOT_BLOCK)
                xe_sc[rows] += jnp.dot(oh.astype(BF16), h_ref[0, c * TOK_CHUNK:(c + 1) * TOK_CHUNK, :],
                                       preferred_element_type=F32)
                g = jnp.where(oh, gate_ref[0, 0, c:c + 1, :], 0.0)
                gs_sc[rows] += jnp.sum(g, axis=1, keepdims=True)

    xe = xe_sc[...].astype(BF16)
    f = wg_ref.shape[2]
    fb = 512
    y = None
    for f0 in range(0, f, fb):
        hg = jnp.dot(xe, wg_ref[0, :, f0:f0 + fb], preferred_element_type=F32)
        hu = jnp.dot(xe, wu_ref[0, :, f0:f0 + fb], preferred_element_type=F32)
        hid = (hg * jax.nn.sigmoid(hg) * hu).astype(BF16)
        part = jnp.dot(hid, wd_ref[0, f0:f0 + fb, :], preferred_element_type=F32)
        y = part if y is None else y + part
    ye_sc[...] = (y * gs_sc[...]).astype(BF16)

    for c in range(n_chunks):
        for sb in range(n_sb):
            @pl.when(overlap(c, sb))
            def _(c=c, sb=sb):
                oh_t = onehot(c, sb).astype(F32).T.astype(BF16)
                rows = slice(sb * SLOT_BLOCK, (sb + 1) * SLOT_BLOCK)
                acc_ref[0, c * TOK_CHUNK:(c + 1) * TOK_CHUNK, :] += jnp.dot(
                    oh_t, ye_sc[rows], preferred_element_type=F32)


def _moe(h2, slot, gate, cnt, wg, wu, wd, cap, name):
    b, t, d = h2.shape
    e = slot.shape[1]
    f = wg.shape[2]
    n_chunks = t // TOK_CHUNK
    n_sb = cap // SLOT_BLOCK
    slot4 = slot.reshape(b, e, n_chunks, TOK_CHUNK)
    gate4 = gate.reshape(b, e, n_chunks, TOK_CHUNK)
    cnt_flat = cnt[:, :, :32].reshape(-1)
    grid_spec = pltpu.PrefetchScalarGridSpec(
        num_scalar_prefetch=1,
        grid=(b, e),
        in_specs=[pl.BlockSpec((1, t, d), lambda i, j, s: (i, 0, 0), pipeline_mode=pl.Buffered(1)),
                  pl.BlockSpec((1, 1, n_chunks, TOK_CHUNK), lambda i, j, s: (i, j, 0, 0)),
                  pl.BlockSpec((1, 1, n_chunks, TOK_CHUNK), lambda i, j, s: (i, j, 0, 0)),
                  pl.BlockSpec((1, d, f), lambda i, j, s: (j, 0, 0)),
                  pl.BlockSpec((1, d, f), lambda i, j, s: (j, 0, 0)),
                  pl.BlockSpec((1, f, d), lambda i, j, s: (j, 0, 0))],
        out_specs=pl.BlockSpec((1, t, d), lambda i, j, s: (i, 0, 0), pipeline_mode=pl.Buffered(1)),
        scratch_shapes=[pltpu.VMEM((cap, d), F32), pltpu.VMEM((cap, 1), F32), pltpu.VMEM((cap, d), BF16)],
    )
    return pl.pallas_call(
        functools.partial(_moe_kernel, n_chunks=n_chunks, n_sb=n_sb),
        grid_spec=grid_spec,
        out_shape=jax.ShapeDtypeStruct((b, t, d), F32),
        compiler_params=_cparams(("parallel", "arbitrary")),
        name=name,
    )(cnt_flat, h2, slot4, gate4, wg, wu, wd)


def _resid_kernel(x_ref, y_ref, mod_ref, o_ref):
    o_ref[0] = x_ref[0] + mod_ref[0, 0, 5:6, :] * y_ref[0]


def _resid(xs, ys, modv, kind, name):
    b, n, d = xs.shape
    tm = ROW_TILE
    row = pl.BlockSpec((1, tm, d), lambda i, j: (i, j, 0))
    return pl.pallas_call(
        _resid_kernel,
        grid=(b, n // tm),
        in_specs=[row, row, pl.BlockSpec((1, 1, 8, d), lambda i, j: (i, kind, 0, 0))],
        out_specs=row,
        out_shape=jax.ShapeDtypeStruct((b, n, d), F32),
        compiler_params=_cparams(("parallel", "parallel")),
        name=name,
    )(xs, ys, modv)


def _pad_lanes(v, width):
    return jnp.pad(v, (0, width - v.shape[0]))


def kernel(x, c, ctx, c_ctx, w_ada, b_ada, norm1_g, norm2_g, w_in, a_q_norm, a_k_norm, b_cq_norm, b_ckv_norm, w_uq, w_ukv, b_qn_norm, b_kn_norm, b_qr_norm, b_kr_norm, c_q_norm, c_k_norm, c_sink, w_out, w_router, w_e_gate, w_e_up, w_e_down):
    b, t, d = x.shape
    n_ctx = ctx.shape[1]
    depth = w_ada.shape[0]
    assert n_ctx == ROW_TILE and t % KEY_CHUNK == 0 and t % GRID_W == 0
    cap = max(1, CAP_FACTOR * t // N_EXPERTS)
    cap_c = max(1, CAP_FACTOR * n_ctx // N_EXPERTS)
    assert cap % SLOT_BLOCK == 0 and (b * cap_c) % SLOT_BLOCK == 0

    rows = ((b + 1 + 7) // 8) * 8
    cc = jnp.zeros((rows, d), F32).at[:b].set(c).at[b].set(c_ctx)
    mod = _modulation(cc, w_ada, b_ada)

    rope = _rope_tables(t)
    m64_np, mb_np = _seg_matrices()
    m64, mb = jnp.asarray(m64_np, BF16), jnp.asarray(mb_np, BF16)
    in_perm = _in_perm()
    uq_perm = _uq_perm()
    ukv_pk, ukv_pv = _ukv_perms()
    oa_rows = np.asarray(_pair_cols(0, A_HEADS))
    oc_rows = np.asarray(_pair_cols(0, C_HEADS)) + 768
    sa = HEAD_DIM ** -0.5
    sb = (B_NOPE + B_ROPE) ** -0.5
    z32, z64 = jnp.zeros((32,), F32), jnp.zeros((64,), F32)

    for l in range(depth):
        last = l == depth - 1
        m6 = mod[l].reshape(rows, 6, d)
        m8 = jnp.pad(m6, ((0, 0), (0, 2), (0, 0)))
        modv = jnp.stack([jnp.broadcast_to(m8[b][None], (b, 8, d)), m8[:b]], axis=1)

        w_in_p = _take_cols(w_in[l], in_perm).astype(BF16)
        wuq = jnp.pad(_take_cols(w_uq[l], uq_perm), ((0, 256 - B_Q_RANK), (0, 0))).astype(BF16)
        wukvk = _take_cols(w_ukv[l], ukv_pk).astype(BF16)
        wukvv = jnp.take(w_ukv[l], jnp.asarray(ukv_pv, jnp.int32), axis=1).astype(BF16)
        gains = jnp.stack([
            jnp.tile(a_q_norm[l], 2) * (sa * LOG2E), jnp.tile(a_k_norm[l], 2),
            jnp.tile(c_q_norm[l], 2) * sa, jnp.tile(c_k_norm[l], 2),
            jnp.concatenate([b_qn_norm[l] * (sb * LOG2E), b_qr_norm[l] * (sb * LOG2E), z32]),
            jnp.concatenate([b_kn_norm[l], z64]),
            jnp.concatenate([z64, b_kr_norm[l], z32]),
            jnp.zeros((LANE,), F32)])
        gcq = _pad_lanes(b_cq_norm[l], 256)[None]
        gckv = b_ckv_norm[l][None]
        g1 = norm1_g[l][None]
        g2n = norm2_g[l][None]
        mix_rows = np.concatenate([oa_rows, np.arange(384, 768), oc_rows])
        w_mix = jnp.take(w_out[l], jnp.asarray(mix_rows, jnp.int32), axis=0).astype(BF16)
        wr_t = w_router[l].T
        sink = jnp.zeros((8, LANE), F32).at[0, :C_HEADS].set(c_sink[l])
        wg = w_e_gate[l].astype(BF16)
        wu = w_e_up[l].astype(BF16)
        wd = w_e_down[l].astype(BF16)

        shared = (g1, w_in_p, gains, gcq, gckv, wuq, wukvk, wukvv, m64, mb)
        qa, ka, va, qb, kb, vb, qc, kc, vc = _inproj(x, modv, 1, *shared, rope)
        qac, kac, vac, qbc, kbc, vbc, qcc, kcc, vcc = _inproj(ctx, modv, 0, *shared, None)

        o_a = _dense_attention(qa, kac, vac, ka, va, A_HEADS, _A_MAPS, True, "attn_a")
        o_b = _dense_attention(qb, kbc, vbc, kb, vb, B_HEADS, _B_MAPS, False, "attn_b")
        o_c = _window_attention(qc, kcc, vcc, kc, vc, sink, "attn_c")
        x_mid, h2, aff = _outproj(o_a, o_b, o_c, x, modv, 1, g2n, w_mix, wr_t, "outproj")
        slot, cnt = _topk(aff, cap, 0, "topk")
        acc = _moe(h2, slot, aff, cnt, wg, wu, wd, cap, "moe")
        x = _resid(x_mid, acc, modv, 1, "resid")

        if not last:
            o_ac = _dense_attention_ctx(qac, kac, vac, A_HEADS, _A_MAPS, True, "attn_a_ctx")
            o_bc = _dense_attention_ctx(qbc, kbc, vbc, B_HEADS, _B_MAPS, False, "attn_b_ctx")
            o_cc = _window_attention(qcc, kcc, vcc, None, None, sink, "attn_c_ctx")
            c_mid, hc2, aff_c = _outproj(o_ac, o_bc, o_cc, ctx, modv, 0, g2n, w_mix, wr_t, "outproj_ctx")
            slot_c, cnt_c = _topk(aff_c, cap_c, cap_c, "topk_ctx")
            e = N_EXPERTS
            slot_f = jnp.transpose(slot_c, (1, 0, 2)).reshape(1, e, b * n_ctx)
            gate_f = jnp.transpose(aff_c, (1, 0, 2)).reshape(1, e, b * n_ctx)
            starts = jnp.transpose(cnt_c[:, :, 0], (1, 0))
            cnt_f = jnp.concatenate([starts, cnt_c[b - 1, :, 1:2],
                                     jnp.zeros((e, LANE - b - 1), jnp.int32)], axis=1)[None]
            acc_c = _moe(hc2.reshape(1, b * n_ctx, d), slot_f, gate_f, cnt_f, wg, wu, wd, b * cap_c, "moe_ctx")
            ctx = _resid(c_mid, acc_c.reshape(b, n_ctx, d), modv, 0, "resid_ctx")
    return x
```

```python
import functools
import math

import numpy as np
import jax
import jax.numpy as jnp
from jax import lax
from jax.experimental import pallas as pl
from jax.experimental.pallas import tpu as pltpu

F32 = jnp.float32
BF16 = jnp.bfloat16

GRID_W = 64
ROPE_THETA = 10000.0
EPS = 1e-6
NEG_INF = -1e30
HEAD_DIM = 64
A_HEADS, A_KV_HEADS = 6, 2
B_HEADS, B_Q_RANK, B_KV_RANK, B_NOPE, B_ROPE, B_V = 6, 192, 128, 64, 32, 64
C_HEADS, C_KV_HEADS = 4, 2
N_EXPERTS = 16
CAP_FACTOR = 2
WINDOW = 128

LANE = 128
ROW_TILE = 256
Q_TILE = 512
KEY_CHUNK = 512
LOG2E = 1.4426950408889634
SLOT_BLOCK = 128
TOK_CHUNK = 256
VMEM_LIMIT = 56 * 1024 * 1024

_QA, _KA, _VA = 0, 384, 512
_QC, _KC, _VC = 640, 896, 1024
_BCQ, _BCKV, _BKR = 1152, 1408, 1536
IN_PAD = 1664
_ORIG = dict(aq=0, ak=384, av=512, bcq=640, bckv=832, bkr=960, cq=992, ck=1248, cv=1376)


def _cparams(sem, vmem=VMEM_LIMIT):
    return pltpu.CompilerParams(dimension_semantics=sem, vmem_limit_bytes=vmem)


def _pair_cols(base, n_heads):
    half = n_heads // 2
    cols = []
    for i in range(half):
        cols += list(range(base + i * 64, base + (i + 1) * 64))
        cols += list(range(base + (half + i) * 64, base + (half + i + 1) * 64))
    return cols


def _in_perm():
    perm = -np.ones((IN_PAD,), np.int64)
    perm[_QA:_QA + 384] = _pair_cols(_ORIG["aq"], A_HEADS)
    perm[_KA:_KA + 128] = np.arange(_ORIG["ak"], _ORIG["ak"] + 128)
    perm[_VA:_VA + 128] = np.arange(_ORIG["av"], _ORIG["av"] + 128)
    perm[_QC:_QC + 256] = _pair_cols(_ORIG["cq"], C_HEADS)
    perm[_KC:_KC + 128] = np.arange(_ORIG["ck"], _ORIG["ck"] + 128)
    perm[_VC:_VC + 128] = np.arange(_ORIG["cv"], _ORIG["cv"] + 128)
    perm[_BCQ:_BCQ + B_Q_RANK] = np.arange(_ORIG["bcq"], _ORIG["bcq"] + B_Q_RANK)
    perm[_BCKV:_BCKV + 128] = np.arange(_ORIG["bckv"], _ORIG["bckv"] + 128)
    perm[_BKR + 64:_BKR + 96] = np.arange(_ORIG["bkr"], _ORIG["bkr"] + 32)
    return perm


def _take_cols(w, perm):
    safe = np.where(perm >= 0, perm, 0)
    out = jnp.take(w, jnp.asarray(safe, jnp.int32), axis=1)
    return jnp.where(jnp.asarray(perm >= 0)[None, :], out, 0.0)


def _uq_perm():
    perm = -np.ones((B_HEADS * LANE,), np.int64)
    for h in range(B_HEADS):
        perm[h * LANE:h * LANE + 96] = np.arange(h * 96, h * 96 + 96)
    return perm


def _ukv_perms():
    pk = -np.ones((B_HEADS * LANE,), np.int64)
    pv = np.zeros((B_HEADS * 64,), np.int64)
    for h in range(B_HEADS):
        pk[h * LANE:h * LANE + 64] = np.arange(h * 128, h * 128 + 64)
        pv[h * 64:(h + 1) * 64] = np.arange(h * 128 + 64, h * 128 + 128)
    return pk, pv


def _seg_matrices():
    i = np.arange(LANE)
    m64 = (i[:, None] // 64 == i[None, :] // 64).astype(np.float32) / 64.0
    mb = np.zeros((LANE, LANE), np.float32)
    mb[:64, :64] = 1.0 / 64.0
    mb[64:96, 64:96] = 1.0 / 32.0
    return m64, mb


def _rope_tables(length):
    rows = length // GRID_W
    t = jnp.arange(rows * GRID_W)
    row = jnp.repeat(jnp.arange(rows), GRID_W).astype(F32)
    col = (t % GRID_W).astype(F32)

    def tabs(dim):
        axis_dim = dim // 2
        inv = ROPE_THETA ** (-jnp.arange(0, axis_dim, 2, dtype=F32) / axis_dim)
        ar = row[:, None] * inv[None, :]
        ac = col[:, None] * inv[None, :]
        cos = jnp.concatenate([jnp.cos(ar), jnp.cos(ar), jnp.cos(ac), jnp.cos(ac)], axis=-1)
        sin = jnp.concatenate([-jnp.sin(ar), jnp.sin(ar), -jnp.sin(ac), jnp.sin(ac)], axis=-1)
        return cos, sin

    c64, s64 = tabs(HEAD_DIM)
    cb, sb = tabs(B_ROPE)
    n = c64.shape[0]
    ones64, zeros64 = jnp.ones((n, 64), F32), jnp.zeros((n, 64), F32)
    ones32, zeros32 = jnp.ones((n, 32), F32), jnp.zeros((n, 32), F32)
    return jnp.concatenate([c64, c64, s64, s64,
                            ones64, cb, ones32, zeros64, sb, zeros32], axis=-1)


def _mod_kernel(c_ref, w_ref, b_ref, o_ref):
    cv = c_ref[...]
    a = (cv * jax.nn.sigmoid(cv)).astype(BF16)
    o_ref[0] = jnp.dot(a, w_ref[0].astype(BF16), preferred_element_type=F32) + b_ref[0]


def _modulation(cc, w_ada, b_ada):
    n_layers, d, n6 = w_ada.shape
    rows = cc.shape[0]
    tn = 1536
    return pl.pallas_call(
        _mod_kernel,
        grid=(n_layers, n6 // tn),
        in_specs=[pl.BlockSpec((rows, d), lambda l, j: (0, 0)),
                  pl.BlockSpec((1, d, tn), lambda l, j: (l, 0, j)),
                  pl.BlockSpec((1, 1, tn), lambda l, j: (l, 0, j))],
        out_specs=pl.BlockSpec((1, rows, tn), lambda l, j: (l, 0, j)),
        out_shape=jax.ShapeDtypeStruct((n_layers, rows, n6), F32),
        compiler_params=_cparams(("parallel", "parallel")),
        name="adaln_mod",
    )(cc, w_ada, b_ada.reshape(n_layers, 1, n6))


def _seg_mean(xsq, m_ref):
    hi = xsq.astype(BF16)
    lo = (xsq - hi.astype(F32)).astype(BF16)
    m = m_ref[...]
    return jnp.dot(hi, m, preferred_element_type=F32) + jnp.dot(lo, m, preferred_element_type=F32)


def _rope(x, cos, sin, half):
    lane = lax.broadcasted_iota(jnp.int32, x.shape, 1)
    lo = (lane % (2 * half)) < half
    partner = jnp.where(lo, pltpu.roll(x, LANE - half, 1), pltpu.roll(x, half, 1))
    return x * cos + partner * sin


def _inproj_kernel(x_ref, mod_ref, g1_ref, w_ref, gains_ref, gcq_ref, gckv_ref, wuq_ref, wukvk_ref,
                   wukvv_ref, m64_ref, mb_ref, *rest, use_rope):
    if use_rope:
        rope_ref = rest[0]
        outs = rest[1:]
    else:
        rope_ref = None
        outs = rest
    qa_ref, ka_ref, va_ref, qb_ref, kb_ref, vb_ref, qc_ref, kc_ref, vc_ref = outs

    x = x_ref[0]
    y = x * lax.rsqrt(jnp.mean(x * x, axis=-1, keepdims=True) + EPS) * g1_ref[...]
    h = y * (1.0 + mod_ref[0, 0, 1:2, :]) + mod_ref[0, 0, 0:1, :]
    p = jnp.dot(h.astype(BF16), w_ref[...], preferred_element_type=F32)

    def head_block(xb, m_ref, gain_row, rope_cols, half):
        y = xb * lax.rsqrt(_seg_mean(xb * xb, m_ref) + EPS) * gains_ref[gain_row:gain_row + 1, :]
        if rope_ref is not None and rope_cols is not None:
            c0, s0 = rope_cols
            y = _rope(y, rope_ref[:, c0:c0 + LANE], rope_ref[:, s0:s0 + LANE], half)
        return y

    r64 = (0, 128)
    rb = (256, 384)
    for i in range(3):
        qa_ref[0, i] = head_block(
            p[:, _QA + i * LANE:_QA + (i + 1) * LANE], m64_ref, 0, r64, 16).astype(BF16)
    ka_ref[0, 0] = head_block(p[:, _KA:_KA + LANE], m64_ref, 1, r64, 16).astype(BF16)
    va_ref[0, 0] = p[:, _VA:_VA + LANE].astype(BF16)
    for i in range(2):
        qc_ref[0, :, i * LANE:(i + 1) * LANE] = head_block(
            p[:, _QC + i * LANE:_QC + (i + 1) * LANE], m64_ref, 2, r64, 16).astype(BF16)
    kc_ref[0] = head_block(p[:, _KC:_KC + LANE], m64_ref, 3, r64, 16).astype(BF16)
    vc_ref[0] = p[:, _VC:_VC + LANE].astype(BF16)

    cq = p[:, _BCQ:_BCQ + 256]
    cq = cq * lax.rsqrt(jnp.sum(cq * cq, axis=-1, keepdims=True) * (1.0 / B_Q_RANK) + EPS) * gcq_ref[...]
    qb = jnp.dot(cq.astype(BF16), wuq_ref[...], preferred_element_type=F32)
    ckv = p[:, _BCKV:_BCKV + LANE]
    ckv = ckv * lax.rsqrt(jnp.mean(ckv * ckv, axis=-1, keepdims=True) + EPS) * gckv_ref[...]
    ckv = ckv.astype(BF16)
    kbn = jnp.dot(ckv, wukvk_ref[...], preferred_element_type=F32)
    vbv = jnp.dot(ckv, wukvv_ref[...], preferred_element_type=F32).astype(BF16)
    for i in range(B_HEADS // 2):
        vb_ref[0, i] = vbv[:, i * LANE:(i + 1) * LANE]
    kpe = head_block(p[:, _BKR:_BKR + LANE], mb_ref, 6, rb, 8)
    for hh in range(B_HEADS):
        sl = slice(hh * LANE, (hh + 1) * LANE)
        qb_ref[0, hh] = head_block(qb[:, sl], mb_ref, 4, rb, 8).astype(BF16)
        kb_ref[0, hh] = (head_block(kbn[:, sl], mb_ref, 5, None, 8) + kpe).astype(BF16)


def _inproj(xs, modv, kind, g1, w_in_p, gains, gcq, gckv, wuq, wukvk, wukvv, m64, mb, rope):
    b, n, d = xs.shape
    tm = ROW_TILE
    use_rope = rope is not None
    const = lambda *shape: pl.BlockSpec(shape, lambda i, j: (0,) * len(shape))
    in_specs = [pl.BlockSpec((1, tm, d), lambda i, j: (i, j, 0)),
                pl.BlockSpec((1, 1, 8, d), lambda i, j: (i, kind, 0, 0)),
                const(1, d), const(d, IN_PAD), const(8, LANE), const(1, 256), const(1, LANE),
                const(256, 768), const(LANE, 768), const(LANE, 384), const(LANE, LANE), const(LANE, LANE)]
    args = [xs, modv, g1, w_in_p, gains, gcq, gckv, wuq, wukvk, wukvv, m64, mb]
    if use_rope:
        in_specs.append(pl.BlockSpec((tm, 512), lambda i, j: (j, 0)))
        args.append(rope)
    blocks = (3, 1, 1, B_HEADS, B_HEADS, B_HEADS // 2)
    widths = (256, 128, 128)
    out_specs = ([pl.BlockSpec((1, nb, tm, LANE), lambda i, j: (i, 0, j, 0)) for nb in blocks]
                 + [pl.BlockSpec((1, tm, w), lambda i, j: (i, j, 0)) for w in widths])
    out_shape = ([jax.ShapeDtypeStruct((b, nb, n, LANE), BF16) for nb in blocks]
                 + [jax.ShapeDtypeStruct((b, n, w), BF16) for w in widths])
    return pl.pallas_call(
        functools.partial(_inproj_kernel, use_rope=use_rope),
        grid=(b, n // tm),
        in_specs=in_specs,
        out_specs=out_specs,
        out_shape=out_shape,
        compiler_params=_cparams(("parallel", "parallel")),
        name="inproj_rope" if use_rope else "inproj_ctx",
    )(*args)


_A_MAPS = (lambda h: h % 3, lambda h: 0, lambda h: 0)
_B_MAPS = (lambda h: h, lambda h: h, lambda h: h // 2)


def _masked_q(q_ref, qb, half):
    q = q_ref[0, :, qb * LANE:(qb + 1) * LANE]
    if half is None:
        return q
    lane = lax.broadcasted_iota(jnp.int32, q.shape, 1)
    keep = (lane < 64) if half == 0 else (lane >= 64)
    return jnp.where(keep, q, jnp.zeros_like(q))


def _qk(q, k):
    return lax.dot_general(q, k, (((1,), (1,)), ((), ())), preferred_element_type=F32)


def _half_mask(q, half):
    lane = lax.broadcasted_iota(jnp.int32, q.shape, 1)
    return jnp.where((lane // 64) == half, q, jnp.zeros_like(q))


def _score_pass(q, chunks, s_w):
    mx = None
    for k_ref, _, st, sz, off in chunks:
        s = _qk(q, k_ref[0, 0, st:st + sz, :])
        s_w[:, off:off + sz] = s
        for t in range(sz // LANE):
            blk = s[:, t * LANE:(t + 1) * LANE]
            mx = blk if mx is None else jnp.maximum(mx, blk)
    return jnp.broadcast_to(jnp.max(mx, axis=1, keepdims=True), mx.shape)


def _value_pass(chunks, s_r, mb):
    acc = None
    for _, v_ref, st, sz, off in chunks:
        ps = [jnp.exp2(s_r[:, off + t * LANE:off + (t + 1) * LANE] - mb).astype(BF16)
              for t in range(sz // LANE)]
        v_ones = jnp.concatenate([v_ref[0, 0, st:st + sz, :], jnp.ones((sz, LANE), BF16)], axis=1)
        part = jnp.dot(jnp.concatenate(ps, axis=1), v_ones, preferred_element_type=F32)
        acc = part if acc is None else acc + part
    return (acc[:, :LANE] / acc[:, LANE:LANE + 1]).astype(BF16)


def _fused_passes(q, chunks, s_w, s_r, mb_r):
    mx = None
    acc = None
    for k_ref, v_ref, st, sz, off in chunks:
        s = _qk(q, k_ref[0, 0, st:st + sz, :])
        s_w[:, off:off + sz] = s
        for t in range(sz // LANE):
            blk = s[:, t * LANE:(t + 1) * LANE]
            mx = blk if mx is None else jnp.maximum(mx, blk)
        ps = [jnp.exp2(s_r[:, off + t * LANE:off + (t + 1) * LANE] - mb_r).astype(BF16)
              for t in range(sz // LANE)]
        v_ones = jnp.concatenate([v_ref[0, 0, st:st + sz, :], jnp.ones((sz, LANE), BF16)], axis=1)
        part = jnp.dot(jnp.concatenate(ps, axis=1), v_ones, preferred_element_type=F32)
        acc = part if acc is None else acc + part
    mb_new = jnp.broadcast_to(jnp.max(mx, axis=1, keepdims=True), mx.shape)
    return mb_new, (acc[:, :LANE] / acc[:, LANE:LANE + 1]).astype(BF16)


def _key_chunks(kc_ref, vc_ref, kl_ref, vl_ref):
    n_ctx = kc_ref.shape[2]
    chunks = [(kc_ref, vc_ref, 0, n_ctx, 0)]
    if kl_ref is not None:
        for c in range(kl_ref.shape[2] // KEY_CHUNK):
            chunks.append((kl_ref, vl_ref, c * KEY_CHUNK, KEY_CHUNK, n_ctx + c * KEY_CHUNK))
    return chunks


def _dense_kernel(q_ref, kc_ref, kl_ref, vcp_ref, vlp_ref, vcc_ref, vlc_ref, o_ref,
                  sa_sc, sb_sc, mba_sc, mbb_sc, *, mask_q, n_tiles, tiles_per_head):
    step = pl.program_id(1)
    tq = sa_sc.shape[0]

    @pl.when(step == 0)
    def _():
        sa_sc[...] = jnp.zeros(sa_sc.shape, F32)
        mba_sc[...] = jnp.zeros(mba_sc.shape, F32)

    q0 = q_ref[0, 0, 0:tq, :]
    q1 = q_ref[0, 0, tq:2 * tq, :]
    if mask_q:
        half = (jnp.minimum(2 * step, n_tiles - 2) // tiles_per_head) // 3
        q0 = _half_mask(q0, half)
        q1 = _half_mask(q1, half)

    mb_new, o_ref[0, 0:tq, :] = _fused_passes(
        q0, _key_chunks(kc_ref, vcp_ref, kl_ref, vlp_ref), sb_sc, sa_sc, mba_sc[...])
    mbb_sc[...] = mb_new
    mb_new, o_ref[0, tq:2 * tq, :] = _fused_passes(
        q1, _key_chunks(kc_ref, vcc_ref, kl_ref, vlc_ref), sa_sc, sb_sc, mbb_sc[...])
    mba_sc[...] = mb_new


def _dense_attention(q, kc, vc, kl, vl, n_heads, maps, mask_q, name):
    b, _, nq, _ = q.shape
    qmap, kmap, vmap = maps
    n_ctx, n_lat = kc.shape[2], kl.shape[2]
    tq = Q_TILE
    tph = nq // tq
    assert tph % 2 == 0
    n_tiles = n_heads * tph
    head_cur = lambda s: jnp.minimum(2 * s, n_tiles - 2) // tph
    head_prev = lambda s: jnp.maximum(2 * s - 1, 0) // tph
    blk = lambda n, fmap, head: pl.BlockSpec((1, 1, n, LANE), lambda i, s: (i, fmap(head(s)), 0, 0))
    q_spec = pl.BlockSpec((1, 1, 2 * tq, LANE),
                          lambda i, s: (i, qmap(head_cur(s)), (jnp.minimum(2 * s, n_tiles - 2) % tph) // 2, 0))
    out = pl.pallas_call(
        functools.partial(_dense_kernel, mask_q=mask_q, n_tiles=n_tiles, tiles_per_head=tph),
        grid=(b, n_tiles // 2 + 1),
        in_specs=[q_spec, blk(n_ctx, kmap, head_cur), blk(n_lat, kmap, head_cur),
                  blk(n_ctx, vmap, head_prev), blk(n_lat, vmap, head_prev),
                  blk(n_ctx, vmap, head_cur), blk(n_lat, vmap, head_cur)],
        out_specs=pl.BlockSpec((1, 2 * tq, LANE), lambda i, s: (i, s, 0)),
        out_shape=jax.ShapeDtypeStruct((b, (n_tiles + 2) * tq, LANE), BF16),
        scratch_shapes=[pltpu.VMEM((tq, n_ctx + n_lat), F32), pltpu.VMEM((tq, n_ctx + n_lat), F32),
                        pltpu.VMEM((tq, LANE), F32), pltpu.VMEM((tq, LANE), F32)],
        compiler_params=_cparams(("parallel", "arbitrary")),
        name=name,
    )(q, kc, kl, vc, vl, vc, vl)
    return out[:, tq:tq + n_tiles * tq].reshape(b, n_heads, nq, LANE)


def _dense_ctx_kernel(q_ref, kc_ref, vc_ref, o_ref, s_sc, *, mask_q):
    q = q_ref[0, 0]
    if mask_q:
        q = _half_mask(q, pl.program_id(1) // 3)
    chunks = _key_chunks(kc_ref, vc_ref, None, None)
    mb = _score_pass(q, chunks, s_sc)
    o_ref[0, 0] = _value_pass(chunks, s_sc, mb)


def _dense_attention_ctx(q, kc, vc, n_heads, maps, mask_q, name):
    b, _, nq, _ = q.shape
    qmap, kmap, vmap = maps
    blk = lambda fmap: pl.BlockSpec((1, 1, nq, LANE), lambda i, h: (i, fmap(h), 0, 0))
    return pl.pallas_call(
        functools.partial(_dense_ctx_kernel, mask_q=mask_q),
        grid=(b, n_heads),
        in_specs=[blk(qmap), blk(kmap), blk(vmap)],
        out_specs=pl.BlockSpec((1, 1, nq, LANE), lambda i, h: (i, h, 0, 0)),
        out_shape=jax.ShapeDtypeStruct((b, n_heads, nq, LANE), BF16),
        scratch_shapes=[pltpu.VMEM((nq, kc.shape[2]), F32)],
        compiler_params=_cparams(("parallel", "parallel")),
        name=name,
    )(q, kc, vc)


def _window_kernel(q_ref, kc_ref, vc_ref, sink_ref, *rest, use_window):
    if use_window:
        kl_ref, vl_ref, o_ref = rest
    else:
        (o_ref,) = rest
    tq = q_ref.shape[1]
    span = tq + 2 * WINDOW
    j = pl.program_id(1)
    if use_window:
        n_lat = kl_ref.shape[1]
        t0 = j * tq
        start = pl.multiple_of(jnp.clip(t0 - WINDOW, 0, n_lat - span), LANE)
        qpos = t0 + lax.broadcasted_iota(jnp.int32, (tq, span), 0)
        kpos = start + lax.broadcasted_iota(jnp.int32, (tq, span), 1)
        valid = jnp.abs(qpos - kpos) <= WINDOW
        kw = kl_ref[0, pl.ds(start, span), :]
        vw = vl_ref[0, pl.ds(start, span), :]
    kcx = kc_ref[0]
    vcx = vc_ref[0]
    lane = lax.broadcasted_iota(jnp.int32, (tq, LANE), 1)
    res = []
    for h in range(C_HEADS):
        half = h // (C_HEADS // C_KV_HEADS)
        q = _masked_q(q_ref, h % 2, half)
        sink = sink_ref[0:1, h:h + 1]
        s_c = _qk(q, kcx)
        m = jnp.maximum(jnp.max(s_c, axis=1, keepdims=True), sink)
        if use_window:
            s_w = jnp.where(valid, _qk(q, kw), NEG_INF)
            m = jnp.maximum(m, jnp.max(s_w, axis=1, keepdims=True))
        p_c = jnp.exp(s_c - m)
        l = jnp.sum(p_c, axis=1, keepdims=True) + jnp.exp(sink - m)
        o = jnp.dot(p_c.astype(BF16), vcx, preferred_element_type=F32)
        if use_window:
            p_w = jnp.exp(s_w - m)
            l = l + jnp.sum(p_w, axis=1, keepdims=True)
            o = o + jnp.dot(p_w.astype(BF16), vw, preferred_element_type=F32)
        res.append(o / l)
    for ob in range(2):
        o_ref[0, :, ob * LANE:(ob + 1) * LANE] = jnp.where(lane < 64, res[ob], res[2 + ob]).astype(BF16)


def _window_attention(q, kc, vc, kl, vl, sink, name):
    b, nq, qw = q.shape
    tq = ROW_TILE
    n_ctx = kc.shape[1]
    use_window = kl is not None
    in_specs = [pl.BlockSpec((1, tq, qw), lambda i, j: (i, j, 0)),
                pl.BlockSpec((1, n_ctx, LANE), lambda i, j: (i, 0, 0)),
                pl.BlockSpec((1, n_ctx, LANE), lambda i, j: (i, 0, 0)),
                pl.BlockSpec((8, LANE), lambda i, j: (0, 0))]
    args = [q, kc, vc, sink]
    if use_window:
        n_lat = kl.shape[1]
        in_specs += [pl.BlockSpec((1, n_lat, LANE), lambda i, j: (i, 0, 0)),
                     pl.BlockSpec((1, n_lat, LANE), lambda i, j: (i, 0, 0))]
        args += [kl, vl]
    return pl.pallas_call(
        functools.partial(_window_kernel, use_window=use_window),
        grid=(b, nq // tq),
        in_specs=in_specs,
        out_specs=pl.BlockSpec((1, tq, qw), lambda i, j: (i, j, 0)),
        out_shape=jax.ShapeDtypeStruct((b, nq, qw), BF16),
        compiler_params=_cparams(("parallel", "parallel")),
        name=name,
    )(*args)


def _outproj_kernel(oa_ref, ob_ref, oc_ref, x_ref, mod_ref, g2n_ref, w_ref, wr_ref,
                    xo_ref, h2_ref, aff_ref):
    tm = x_ref.shape[1]
    low = lax.broadcasted_iota(jnp.int32, (tm, LANE), 1) < 64
    parts = [jnp.where(low, oa_ref[0, i], oa_ref[0, 3 + i]) for i in range(3)]
    parts += [jnp.where(low, ob_ref[0, 2 * i], ob_ref[0, 2 * i + 1]) for i in range(3)]
    parts.append(oc_ref[0])
    mix = jnp.dot(jnp.concatenate(parts, axis=1), w_ref[...], preferred_element_type=F32)
    x = x_ref[0] + mod_ref[0, 0, 2:3, :] * mix
    xo_ref[0] = x
    y = x * lax.rsqrt(jnp.mean(x * x, axis=-1, keepdims=True) + EPS) * g2n_ref[...]
    h2 = y * (1.0 + mod_ref[0, 0, 4:5, :]) + mod_ref[0, 0, 3:4, :]
    h2_ref[0] = h2.astype(BF16)
    logits = lax.dot_general(wr_ref[...], h2, (((1,), (1,)), ((), ())),
                             preferred_element_type=F32, precision=lax.Precision.HIGHEST)
    z = jnp.exp(logits - jnp.max(logits, axis=0, keepdims=True))
    aff_ref[0] = z / jnp.sum(z, axis=0, keepdims=True)


def _outproj(oa, ob, oc, xs, modv, kind, g2n, w_mix, wr_t, name):
    b, n, d = xs.shape
    tm = ROW_TILE
    e = wr_t.shape[0]
    const = lambda *shape: pl.BlockSpec(shape, lambda i, j: (0,) * len(shape))
    row = lambda w: pl.BlockSpec((1, tm, w), lambda i, j: (i, j, 0))
    heads = lambda nh: pl.BlockSpec((1, nh, tm, LANE), lambda i, j: (i, 0, j, 0))
    return pl.pallas_call(
        _outproj_kernel,
        grid=(b, n // tm),
        in_specs=[heads(A_HEADS), heads(B_HEADS), row(256), row(d),
                  pl.BlockSpec((1, 1, 8, d), lambda i, j: (i, kind, 0, 0)),
                  const(1, d), const(d, d), const(e, d)],
        out_specs=[row(d), row(d), pl.BlockSpec((1, e, tm), lambda i, j: (i, 0, j))],
        out_shape=[jax.ShapeDtypeStruct((b, n, d), F32), jax.ShapeDtypeStruct((b, n, d), BF16),
                   jax.ShapeDtypeStruct((b, e, n), F32)],
        compiler_params=_cparams(("parallel", "parallel")),
        name=name,
    )(oa, ob, oc, xs, modv, g2n, w_mix, wr_t)


def _prefix_exclusive(mask_f, tri):
    e, t = mask_f.shape
    ck = tri.shape[0]
    carry = jnp.zeros((e, 1), F32)
    parts, offs = [], [carry]
    for c in range(t // ck):
        blk = mask_f[:, c * ck:(c + 1) * ck]
        parts.append(jnp.dot(blk.astype(BF16), tri, preferred_element_type=F32) + carry)
        carry = carry + jnp.sum(blk, axis=1, keepdims=True)
        offs.append(carry)
    return jnp.concatenate(parts, axis=1) if len(parts) > 1 else parts[0], offs


def _topk_kernel(aff_ref, slot_ref, cnt_ref, *, cap, slot_stride):
    aff = aff_ref[0]
    e, t = aff.shape
    bits = pltpu.bitcast(aff, jnp.int32)
    kf = jnp.float32(cap)

    def search(i, lo):
        cand = lo | (jnp.int32(1) << (30 - i))
        n_ge = jnp.sum((bits >= cand).astype(F32), axis=1, keepdims=True)
        return jnp.where(n_ge >= kf, cand, lo)

    thr = lax.fori_loop(0, 31, search, jnp.zeros((e, 1), jnp.int32))
    gt = bits > thr
    eq = bits == thr
    need = kf - jnp.sum(gt.astype(F32), axis=1, keepdims=True)

    r = lax.broadcasted_iota(jnp.int32, (TOK_CHUNK, TOK_CHUNK), 0)
    c = lax.broadcasted_iota(jnp.int32, (TOK_CHUNK, TOK_CHUNK), 1)
    tri = (r < c).astype(BF16)
    tie_rank, _ = _prefix_exclusive(eq.astype(F32), tri)
    sel = gt | (eq & (tie_rank < need))
    slot, offs = _prefix_exclusive(sel.astype(F32), tri)
    base = pl.program_id(0) * slot_stride
    slot_ref[0] = jnp.where(sel, slot.astype(jnp.int32) + base, -1)
    lane = lax.broadcasted_iota(jnp.int32, (e, LANE), 1)
    cnt = jnp.zeros((e, LANE), jnp.int32)
    for ci, off in enumerate(offs):
        cnt = jnp.where(lane == ci, off.astype(jnp.int32) + base, cnt)
    cnt_ref[0] = cnt


def _topk(aff_t, cap, slot_stride, name):
    b, e, t = aff_t.shape
    return pl.pallas_call(
        functools.partial(_topk_kernel, cap=cap, slot_stride=slot_stride),
        grid=(b,),
        in_specs=[pl.BlockSpec((1, e, t), lambda i: (i, 0, 0))],
        out_specs=[pl.BlockSpec((1, e, t), lambda i: (i, 0, 0)),
                   pl.BlockSpec((1, e, LANE), lambda i: (i, 0, 0))],
        out_shape=[jax.ShapeDtypeStruct((b, e, t), jnp.int32), jax.ShapeDtypeStruct((b, e, LANE), jnp.int32)],
        compiler_params=_cparams(("parallel",)),
        name=name,
    )(aff_t)


def _moe_kernel(cnt_ref, h_ref, slot_ref, gate_ref, wg_ref, wu_ref, wd_ref, acc_ref,
                xe_sc, gs_sc, ye_sc, *, n_chunks, n_sb):
    bi = pl.program_id(0)
    ei = pl.program_id(1)
    n_e = pl.num_programs(1)

    @pl.when(ei == 0)
    def _():
        acc_ref[...] = jnp.zeros_like(acc_ref)

    cap = n_sb * SLOT_BLOCK
    cbase = (bi * n_e + ei) * 32
    counts = [cnt_ref[cbase + c] for c in range(n_chunks + 1)]
    window = 2 * SLOT_BLOCK

    def ffn():
        xe = xe_sc[0:cap, :].astype(BF16)
        f = wg_ref.shape[2]
        fb = 512
        y = None
        for f0 in range(0, f, fb):
            hg = jnp.dot(xe, wg_ref[0, :, f0:f0 + fb], preferred_element_type=F32)
            hu = jnp.dot(xe, wu_ref[0, :, f0:f0 + fb], preferred_element_type=F32)
            hid = (hg * jax.nn.sigmoid(hg) * hu).astype(BF16)
            part = jnp.dot(hid, wd_ref[0, f0:f0 + fb, :], preferred_element_type=F32)
            y = part if y is None else y + part
        return y

    def tok(c):
        return slice(c * TOK_CHUNK, (c + 1) * TOK_CHUNK)

    narrow = counts[1] - counts[0] <= SLOT_BLOCK
    for c in range(1, n_chunks):
        narrow = narrow & (counts[c + 1] - counts[c] <= SLOT_BLOCK)

    @pl.when(narrow)
    def _():
        xe_sc[...] = jnp.zeros_like(xe_sc)
        starts = [pl.multiple_of(jnp.minimum((counts[c] // SLOT_BLOCK) * SLOT_BLOCK, cap - SLOT_BLOCK),
                                 SLOT_BLOCK) for c in range(n_chunks)]
        srow = lax.broadcasted_iota(jnp.int32, (window, TOK_CHUNK), 0)
        for c in range(n_chunks):
            oh = (slot_ref[0, 0, c:c + 1, :] - starts[c]) == srow
            xe_sc[pl.ds(starts[c], window), :] += jnp.dot(
                oh.astype(BF16), h_ref[0, tok(c), :], preferred_element_type=F32)
        ye_sc[0:cap, :] = ffn().astype(BF16)
        ye_sc[cap:cap + SLOT_BLOCK, :] = jnp.zeros((SLOT_BLOCK, ye_sc.shape[1]), BF16)
        pad = jnp.zeros((LANE - n_chunks, TOK_CHUNK), F32)
        slot_t = jnp.concatenate([slot_ref[0, 0].astype(F32), pad], axis=0).T
        gate_t = jnp.concatenate([gate_ref[0, 0], pad], axis=0).T
        scol = lax.broadcasted_iota(jnp.int32, (TOK_CHUNK, window), 1).astype(F32)
        for c in range(n_chunks):
            oh_t = (slot_t[:, c:c + 1] - starts[c].astype(F32)) == scol
            contrib = jnp.dot(oh_t.astype(BF16), ye_sc[pl.ds(starts[c], window), :],
                              preferred_element_type=F32)
            acc_ref[0, tok(c), :] += gate_t[:, c:c + 1] * contrib

    @pl.when(jnp.logical_not(narrow))
    def _():
        xe_sc[...] = jnp.zeros_like(xe_sc)
        gs_sc[...] = jnp.zeros_like(gs_sc)
        srow = lax.broadcasted_iota(jnp.int32, (SLOT_BLOCK, TOK_CHUNK), 0)

        def overlap(c, sb):
            return (counts[c] < (sb + 1) * SLOT_BLOCK) & (counts[c + 1] > sb * SLOT_BLOCK)

        def onehot(c, sb):
            return slot_ref[0, 0, c:c + 1, :] == (srow + sb * SLOT_BLOCK)

        for c in range(n_chunks):
            for sb in range(n_sb):
                @pl.when(overlap(c, sb))
                def _(c=c, sb=sb):
                    oh = onehot(c, sb)
                    rows = slice(sb * SLOT_BLOCK, (sb + 1) * SLOT_BLOCK)
                    xe_sc[rows] += jnp.dot(oh.astype(BF16), h_ref[0, tok(c), :], preferred_element_type=F32)
                    g = jnp.where(oh, gate_ref[0, 0, c:c + 1, :], 0.0)
                    gs_sc[rows] += jnp.sum(g, axis=1, keepdims=True)

        ye_sc[0:cap, :] = (ffn() * gs_sc[...]).astype(BF16)

        for c in range(n_chunks):
            for sb in range(n_sb):
                @pl.when(overlap(c, sb))
                def _(c=c, sb=sb):
                    oh_t = onehot(c, sb).astype(F32).T.astype(BF16)
                    rows = slice(sb * SLOT_BLOCK, (sb + 1) * SLOT_BLOCK)
                    acc_ref[0, tok(c), :] += jnp.dot(oh_t, ye_sc[rows], preferred_element_type=F32)


def _moe(h2, slot, gate, cnt, wg, wu, wd, cap, name):
    b, t, d = h2.shape
    e = slot.shape[1]
    f = wg.shape[2]
    n_chunks = t // TOK_CHUNK
    n_sb = cap // SLOT_BLOCK
    slot4 = slot.reshape(b, e, n_chunks, TOK_CHUNK)
    gate4 = gate.reshape(b, e, n_chunks, TOK_CHUNK)
    cnt_flat = cnt[:, :, :32].reshape(-1)
    grid_spec = pltpu.PrefetchScalarGridSpec(
        num_scalar_prefetch=1,
        grid=(b, e),
        in_specs=[pl.BlockSpec((1, t, d), lambda i, j, s: (i, 0, 0), pipeline_mode=pl.Buffered(1)),
                  pl.BlockSpec((1, 1, n_chunks, TOK_CHUNK), lambda i, j, s: (i, j, 0, 0)),
                  pl.BlockSpec((1, 1, n_chunks, TOK_CHUNK), lambda i, j, s: (i, j, 0, 0)),
                  pl.BlockSpec((1, d, f), lambda i, j, s: (j, 0, 0)),
                  pl.BlockSpec((1, d, f), lambda i, j, s: (j, 0, 0)),
                  pl.BlockSpec((1, f, d), lambda i, j, s: (j, 0, 0))],
        out_specs=pl.BlockSpec((1, t, d), lambda i, j, s: (i, 0, 0), pipeline_mode=pl.Buffered(1)),
        scratch_shapes=[pltpu.VMEM((cap + SLOT_BLOCK, d), F32), pltpu.VMEM((cap, 1), F32),
                        pltpu.VMEM((cap + SLOT_BLOCK, d), BF16)],
    )
    return pl.pallas_call(
        functools.partial(_moe_kernel, n_chunks=n_chunks, n_sb=n_sb),
        grid_spec=grid_spec,
        out_shape=jax.ShapeDtypeStruct((b, t, d), F32),
        compiler_params=_cparams(("parallel", "arbitrary")),
        name=name,
    )(cnt_flat, h2, slot4, gate4, wg, wu, wd)


def _resid_kernel(x_ref, y_ref, mod_ref, o_ref):
    o_ref[0] = x_ref[0] + mod_ref[0, 0, 5:6, :] * y_ref[0]


def _resid(xs, ys, modv, kind, name):
    b, n, d = xs.shape
    tm = ROW_TILE
    row = pl.BlockSpec((1, tm, d), lambda i, j: (i, j, 0))
    return pl.pallas_call(
        _resid_kernel,
        grid=(b, n // tm),
        in_specs=[row, row, pl.BlockSpec((1, 1, 8, d), lambda i, j: (i, kind, 0, 0))],
        out_specs=row,
        out_shape=jax.ShapeDtypeStruct((b, n, d), F32),
        compiler_params=_cparams(("parallel", "parallel")),
        name=name,
    )(xs, ys, modv)


def _pad_lanes(v, width):
    return jnp.pad(v, (0, width - v.shape[0]))


def kernel(x, c, ctx, c_ctx, w_ada, b_ada, norm1_g, norm2_g, w_in, a_q_norm, a_k_norm, b_cq_norm, b_ckv_norm, w_uq, w_ukv, b_qn_norm, b_kn_norm, b_qr_norm, b_kr_norm, c_q_norm, c_k_norm, c_sink, w_out, w_router, w_e_gate, w_e_up, w_e_down):
    b, t, d = x.shape
    n_ctx = ctx.shape[1]
    depth = w_ada.shape[0]
    assert n_ctx == ROW_TILE and t % KEY_CHUNK == 0 and t % GRID_W == 0
    cap = max(1, CAP_FACTOR * t // N_EXPERTS)
    cap_c = max(1, CAP_FACTOR * n_ctx // N_EXPERTS)
    assert cap % SLOT_BLOCK == 0 and (b * cap_c) % SLOT_BLOCK == 0

    rows = ((b + 1 + 7) // 8) * 8
    cc = jnp.zeros((rows, d), F32).at[:b].set(c).at[b].set(c_ctx)
    mod = _modulation(cc, w_ada, b_ada)

    rope = _rope_tables(t)
    m64_np, mb_np = _seg_matrices()
    m64, mb = jnp.asarray(m64_np, BF16), jnp.asarray(mb_np, BF16)
    in_perm = _in_perm()
    uq_perm = _uq_perm()
    ukv_pk, ukv_pv = _ukv_perms()
    oa_rows = np.asarray(_pair_cols(0, A_HEADS))
    oc_rows = np.asarray(_pair_cols(0, C_HEADS)) + 768
    sa = HEAD_DIM ** -0.5
    sb = (B_NOPE + B_ROPE) ** -0.5
    z32, z64 = jnp.zeros((32,), F32), jnp.zeros((64,), F32)

    for l in range(depth):
        last = l == depth - 1
        m6 = mod[l].reshape(rows, 6, d)
        m8 = jnp.pad(m6, ((0, 0), (0, 2), (0, 0)))
        modv = jnp.stack([jnp.broadcast_to(m8[b][None], (b, 8, d)), m8[:b]], axis=1)

        w_in_p = _take_cols(w_in[l], in_perm).astype(BF16)
        wuq = jnp.pad(_take_cols(w_uq[l], uq_perm), ((0, 256 - B_Q_RANK), (0, 0))).astype(BF16)
        wukvk = _take_cols(w_ukv[l], ukv_pk).astype(BF16)
        wukvv = jnp.take(w_ukv[l], jnp.asarray(ukv_pv, jnp.int32), axis=1).astype(BF16)
        gains = jnp.stack([
            jnp.tile(a_q_norm[l], 2) * (sa * LOG2E), jnp.tile(a_k_norm[l], 2),
            jnp.tile(c_q_norm[l], 2) * sa, jnp.tile(c_k_norm[l], 2),
            jnp.concatenate([b_qn_norm[l] * (sb * LOG2E), b_qr_norm[l] * (sb * LOG2E), z32]),
            jnp.concatenate([b_kn_norm[l], z64]),
            jnp.concatenate([z64, b_kr_norm[l], z32]),
            jnp.zeros((LANE,), F32)])
        gcq = _pad_lanes(b_cq_norm[l], 256)[None]
        gckv = b_ckv_norm[l][None]
        g1 = norm1_g[l][None]
        g2n = norm2_g[l][None]
        mix_rows = np.concatenate([oa_rows, np.arange(384, 768), oc_rows])
        w_mix = jnp.take(w_out[l], jnp.asarray(mix_rows, jnp.int32), axis=0).astype(BF16)
        wr_t = w_router[l].T
        sink = jnp.zeros((8, LANE), F32).at[0, :C_HEADS].set(c_sink[l])
        wg = w_e_gate[l].astype(BF16)
        wu = w_e_up[l].astype(BF16)
        wd = w_e_down[l].astype(BF16)

        shared = (g1, w_in_p, gains, gcq, gckv, wuq, wukvk, wukvv, m64, mb)
        qa, ka, va, qb, kb, vb, qc, kc, vc = _inproj(x, modv, 1, *shared, rope)
        qac, kac, vac, qbc, kbc, vbc, qcc, kcc, vcc = _inproj(ctx, modv, 0, *shared, None)

        o_a = _dense_attention(qa, kac, vac, ka, va, A_HEADS, _A_MAPS, True, "attn_a")
        o_b = _dense_attention(qb, kbc, vbc, kb, vb, B_HEADS, _B_MAPS, False, "attn_b")
        o_c = _window_attention(qc, kcc, vcc, kc, vc, sink, "attn_c")
        x_mid, h2, aff = _outproj(o_a, o_b, o_c, x, modv, 1, g2n, w_mix, wr_t, "outproj")
        slot, cnt = _topk(aff, cap, 0, "topk")
        acc = _moe(h2, slot, aff, cnt, wg, wu, wd, cap, "moe")
        x = _resid(x_mid, acc, modv, 1, "resid")

        if not last:
            o_ac = _dense_attention_ctx(qac, kac, vac, A_HEADS, _A_MAPS, True, "attn_a_ctx")
            o_bc = _dense_attention_ctx(qbc, kbc, vbc, B_HEADS, _B_MAPS, False, "attn_b_ctx")
            o_cc = _window_attention(qcc, kcc, vcc, None, None, sink, "attn_c_ctx")
            c_mid, hc2, aff_c = _outproj(o_ac, o_bc, o_cc, ctx, modv, 0, g2n, w_mix, wr_t, "outproj_ctx")
            slot_c, cnt_c = _topk(aff_c, cap_c, cap_c, "topk_ctx")
            e = N_EXPERTS
            slot_f = jnp.transpose(slot_c, (1, 0, 2)).reshape(1, e, b * n_ctx)
            gate_f = jnp.transpose(aff_c, (1, 0, 2)).reshape(1, e, b * n_ctx)
            starts = jnp.transpose(cnt_c[:, :, 0], (1, 0))
            cnt_f = jnp.concatenate([starts, cnt_c[b - 1, :, 1:2],
                                     jnp.zeros((e, LANE - b - 1), jnp.int32)], axis=1)[None]
            acc_c = _moe(hc2.reshape(1, b * n_ctx, d), slot_f, gate_f, cnt_f, wg, wu, wd, b * cap_c, "moe_ctx")
            ctx = _resid(c_mid, acc_c.reshape(b, n_ctx, d), modv, 0, "resid_ctx")
    return x
```

```python
import functools
import math

import numpy as np
import jax
import jax.numpy as jnp
from jax import lax
from jax.experimental import pallas as pl
from jax.experimental.pallas import tpu as pltpu

F32 = jnp.float32
BF16 = jnp.bfloat16

GRID_W = 64
ROPE_THETA = 10000.0
EPS = 1e-6
NEG_INF = -1e30
HEAD_DIM = 64
A_HEADS, A_KV_HEADS = 6, 2
B_HEADS, B_Q_RANK, B_KV_RANK, B_NOPE, B_ROPE, B_V = 6, 192, 128, 64, 32, 64
C_HEADS, C_KV_HEADS = 4, 2
N_EXPERTS = 16
CAP_FACTOR = 2
WINDOW = 128

LANE = 128
ROW_TILE = 256
Q_TILE = 512
KEY_CHUNK = 512
LOG2E = 1.4426950408889634
SLOT_BLOCK = 128
TOK_CHUNK = 256
GATHER_WINDOW = 144
VMEM_LIMIT = 56 * 1024 * 1024

_QA, _KA, _VA = 0, 384, 512
_QC, _KC, _VC = 640, 896, 1024
_BCQ, _BCKV, _BKR = 1152, 1408, 1536
IN_PAD = 1664
_ORIG = dict(aq=0, ak=384, av=512, bcq=640, bckv=832, bkr=960, cq=992, ck=1248, cv=1376)


def _cparams(sem, vmem=VMEM_LIMIT):
    return pltpu.CompilerParams(dimension_semantics=sem, vmem_limit_bytes=vmem)


def _pair_cols(base, n_heads):
    half = n_heads // 2
    cols = []
    for i in range(half):
        cols += list(range(base + i * 64, base + (i + 1) * 64))
        cols += list(range(base + (half + i) * 64, base + (half + i + 1) * 64))
    return cols


def _in_perm():
    perm = -np.ones((IN_PAD,), np.int64)
    perm[_QA:_QA + 384] = _pair_cols(_ORIG["aq"], A_HEADS)
    perm[_KA:_KA + 128] = np.arange(_ORIG["ak"], _ORIG["ak"] + 128)
    perm[_VA:_VA + 128] = np.arange(_ORIG["av"], _ORIG["av"] + 128)
    perm[_QC:_QC + 256] = _pair_cols(_ORIG["cq"], C_HEADS)
    perm[_KC:_KC + 128] = np.arange(_ORIG["ck"], _ORIG["ck"] + 128)
    perm[_VC:_VC + 128] = np.arange(_ORIG["cv"], _ORIG["cv"] + 128)
    perm[_BCQ:_BCQ + B_Q_RANK] = np.arange(_ORIG["bcq"], _ORIG["bcq"] + B_Q_RANK)
    perm[_BCKV:_BCKV + 128] = np.arange(_ORIG["bckv"], _ORIG["bckv"] + 128)
    perm[_BKR + 64:_BKR + 96] = np.arange(_ORIG["bkr"], _ORIG["bkr"] + 32)
    return perm


def _take_cols(w, perm):
    safe = np.where(perm >= 0, perm, 0)
    out = jnp.take(w, jnp.asarray(safe, jnp.int32), axis=1)
    return jnp.where(jnp.asarray(perm >= 0)[None, :], out, 0.0)


def _uq_perm():
    perm = -np.ones((B_HEADS * LANE,), np.int64)
    for h in range(B_HEADS):
        perm[h * LANE:h * LANE + 96] = np.arange(h * 96, h * 96 + 96)
    return perm


def _ukv_perms():
    pk = -np.ones((B_HEADS * LANE,), np.int64)
    pv = np.zeros((B_HEADS * 64,), np.int64)
    for h in range(B_HEADS):
        pk[h * LANE:h * LANE + 64] = np.arange(h * 128, h * 128 + 64)
        pv[h * 64:(h + 1) * 64] = np.arange(h * 128 + 64, h * 128 + 128)
    return pk, pv


def _seg_matrices():
    i = np.arange(LANE)
    m64 = (i[:, None] // 64 == i[None, :] // 64).astype(np.float32) / 64.0
    mb = np.zeros((LANE, LANE), np.float32)
    mb[:64, :64] = 1.0 / 64.0
    mb[64:96, 64:96] = 1.0 / 32.0
    return m64, mb


def _rope_tables(length):
    rows = length // GRID_W
    t = jnp.arange(rows * GRID_W)
    row = jnp.repeat(jnp.arange(rows), GRID_W).astype(F32)
    col = (t % GRID_W).astype(F32)

    def tabs(dim):
        axis_dim = dim // 2
        inv = ROPE_THETA ** (-jnp.arange(0, axis_dim, 2, dtype=F32) / axis_dim)
        ar = row[:, None] * inv[None, :]
        ac = col[:, None] * inv[None, :]
        cos = jnp.concatenate([jnp.cos(ar), jnp.cos(ar), jnp.cos(ac), jnp.cos(ac)], axis=-1)
        sin = jnp.concatenate([-jnp.sin(ar), jnp.sin(ar), -jnp.sin(ac), jnp.sin(ac)], axis=-1)
        return cos, sin

    c64, s64 = tabs(HEAD_DIM)
    cb, sb = tabs(B_ROPE)
    n = c64.shape[0]
    ones64, zeros64 = jnp.ones((n, 64), F32), jnp.zeros((n, 64), F32)
    ones32, zeros32 = jnp.ones((n, 32), F32), jnp.zeros((n, 32), F32)
    return jnp.concatenate([c64, c64, s64, s64,
                            ones64, cb, ones32, zeros64, sb, zeros32], axis=-1)


def _mod_kernel(c_ref, w_ref, b_ref, o_ref):
    cv = c_ref[...]
    a = (cv * jax.nn.sigmoid(cv)).astype(BF16)
    o_ref[0] = jnp.dot(a, w_ref[0].astype(BF16), preferred_element_type=F32) + b_ref[0]


def _modulation(cc, w_ada, b_ada):
    n_layers, d, n6 = w_ada.shape
    rows = cc.shape[0]
    tn = 1536
    return pl.pallas_call(
        _mod_kernel,
        grid=(n_layers, n6 // tn),
        in_specs=[pl.BlockSpec((rows, d), lambda l, j: (0, 0)),
                  pl.BlockSpec((1, d, tn), lambda l, j: (l, 0, j)),
                  pl.BlockSpec((1, 1, tn), lambda l, j: (l, 0, j))],
        out_specs=pl.BlockSpec((1, rows, tn), lambda l, j: (l, 0, j)),
        out_shape=jax.ShapeDtypeStruct((n_layers, rows, n6), F32),
        compiler_params=_cparams(("parallel", "parallel")),
        name="adaln_mod",
    )(cc, w_ada, b_ada.reshape(n_layers, 1, n6))


def _seg_mean(xsq, m_ref):
    hi = xsq.astype(BF16)
    lo = (xsq - hi.astype(F32)).astype(BF16)
    m = m_ref[...]
    return jnp.dot(hi, m, preferred_element_type=F32) + jnp.dot(lo, m, preferred_element_type=F32)


def _rope(x, cos, sin, half):
    lane = lax.broadcasted_iota(jnp.int32, x.shape, 1)
    lo = (lane % (2 * half)) < half
    partner = jnp.where(lo, pltpu.roll(x, LANE - half, 1), pltpu.roll(x, half, 1))
    return x * cos + partner * sin


def _inproj_kernel(x_ref, mod_ref, g1_ref, w_ref, gains_ref, gcq_ref, gckv_ref, wuq_ref, wukvk_ref,
                   wukvv_ref, m64_ref, mb_ref, *rest, use_rope):
    if use_rope:
        rope_ref = rest[0]
        outs = rest[1:]
    else:
        rope_ref = None
        outs = rest
    qa_ref, ka_ref, va_ref, qb_ref, kb_ref, vb_ref, qc_ref, kc_ref, vc_ref = outs

    x = x_ref[0]
    y = x * lax.rsqrt(jnp.mean(x * x, axis=-1, keepdims=True) + EPS) * g1_ref[...]
    h = y * (1.0 + mod_ref[0, 0, 1:2, :]) + mod_ref[0, 0, 0:1, :]
    p = jnp.dot(h.astype(BF16), w_ref[...], preferred_element_type=F32)

    def head_block(xb, m_ref, gain_row, rope_cols, half):
        y = xb * lax.rsqrt(_seg_mean(xb * xb, m_ref) + EPS) * gains_ref[gain_row:gain_row + 1, :]
        if rope_ref is not None and rope_cols is not None:
            c0, s0 = rope_cols
            y = _rope(y, rope_ref[:, c0:c0 + LANE], rope_ref[:, s0:s0 + LANE], half)
        return y

    r64 = (0, 128)
    rb = (256, 384)
    for i in range(3):
        qa_ref[0, i] = head_block(
            p[:, _QA + i * LANE:_QA + (i + 1) * LANE], m64_ref, 0, r64, 16).astype(BF16)
    ka_ref[0, 0] = head_block(p[:, _KA:_KA + LANE], m64_ref, 1, r64, 16).astype(BF16)
    va_ref[0, 0] = p[:, _VA:_VA + LANE].astype(BF16)
    for i in range(2):
        qc_ref[0, :, i * LANE:(i + 1) * LANE] = head_block(
            p[:, _QC + i * LANE:_QC + (i + 1) * LANE], m64_ref, 2, r64, 16).astype(BF16)
    kc_ref[0] = head_block(p[:, _KC:_KC + LANE], m64_ref, 3, r64, 16).astype(BF16)
    vc_ref[0] = p[:, _VC:_VC + LANE].astype(BF16)

    cq = p[:, _BCQ:_BCQ + 256]
    cq = cq * lax.rsqrt(jnp.sum(cq * cq, axis=-1, keepdims=True) * (1.0 / B_Q_RANK) + EPS) * gcq_ref[...]
    qb = jnp.dot(cq.astype(BF16), wuq_ref[...], preferred_element_type=F32)
    ckv = p[:, _BCKV:_BCKV + LANE]
    ckv = ckv * lax.rsqrt(jnp.mean(ckv * ckv, axis=-1, keepdims=True) + EPS) * gckv_ref[...]
    ckv = ckv.astype(BF16)
    kbn = jnp.dot(ckv, wukvk_ref[...], preferred_element_type=F32)
    vbv = jnp.dot(ckv, wukvv_ref[...], preferred_element_type=F32).astype(BF16)
    for i in range(B_HEADS // 2):
        vb_ref[0, i] = vbv[:, i * LANE:(i + 1) * LANE]
    kpe = head_block(p[:, _BKR:_BKR + LANE], mb_ref, 6, rb, 8)
    for hh in range(B_HEADS):
        sl = slice(hh * LANE, (hh + 1) * LANE)
        qb_ref[0, hh] = head_block(qb[:, sl], mb_ref, 4, rb, 8).astype(BF16)
        kb_ref[0, hh] = (head_block(kbn[:, sl], mb_ref, 5, None, 8) + kpe).astype(BF16)


def _inproj(xs, modv, kind, g1, w_in_p, gains, gcq, gckv, wuq, wukvk, wukvv, m64, mb, rope):
    b, n, d = xs.shape
    tm = ROW_TILE
    use_rope = rope is not None
    const = lambda *shape: pl.BlockSpec(shape, lambda i, j: (0,) * len(shape))
    in_specs = [pl.BlockSpec((1, tm, d), lambda i, j: (i, j, 0)),
                pl.BlockSpec((1, 1, 8, d), lambda i, j: (i, kind, 0, 0)),
                const(1, d), const(d, IN_PAD), const(8, LANE), const(1, 256), const(1, LANE),
                const(256, 768), const(LANE, 768), const(LANE, 384), const(LANE, LANE), const(LANE, LANE)]
    args = [xs, modv, g1, w_in_p, gains, gcq, gckv, wuq, wukvk, wukvv, m64, mb]
    if use_rope:
        in_specs.append(pl.BlockSpec((tm, 512), lambda i, j: (j, 0)))
        args.append(rope)
    blocks = (3, 1, 1, B_HEADS, B_HEADS, B_HEADS // 2)
    widths = (256, 128, 128)
    out_specs = ([pl.BlockSpec((1, nb, tm, LANE), lambda i, j: (i, 0, j, 0)) for nb in blocks]
                 + [pl.BlockSpec((1, tm, w), lambda i, j: (i, j, 0)) for w in widths])
    out_shape = ([jax.ShapeDtypeStruct((b, nb, n, LANE), BF16) for nb in blocks]
                 + [jax.ShapeDtypeStruct((b, n, w), BF16) for w in widths])
    return pl.pallas_call(
        functools.partial(_inproj_kernel, use_rope=use_rope),
        grid=(b, n // tm),
        in_specs=in_specs,
        out_specs=out_specs,
        out_shape=out_shape,
        compiler_params=_cparams(("parallel", "parallel")),
        name="inproj_rope" if use_rope else "inproj_ctx",
    )(*args)


_A_MAPS = (lambda h: h % 3, lambda h: 0, lambda h: 0)
_B_MAPS = (lambda h: h, lambda h: h, lambda h: h // 2)


def _masked_q(q_ref, qb, half):
    q = q_ref[0, :, qb * LANE:(qb + 1) * LANE]
    if half is None:
        return q
    lane = lax.broadcasted_iota(jnp.int32, q.shape, 1)
    keep = (lane < 64) if half == 0 else (lane >= 64)
    return jnp.where(keep, q, jnp.zeros_like(q))


def _qk(q, k):
    return lax.dot_general(q, k, (((1,), (1,)), ((), ())), preferred_element_type=F32)


def _half_mask(q, half):
    lane = lax.broadcasted_iota(jnp.int32, q.shape, 1)
    return jnp.where((lane // 64) == half, q, jnp.zeros_like(q))


def _score_pass(q, chunks, s_w):
    mx = None
    for k_ref, _, st, sz, off in chunks:
        s = _qk(q, k_ref[0, 0, st:st + sz, :])
        s_w[:, off:off + sz] = s
        for t in range(sz // LANE):
            blk = s[:, t * LANE:(t + 1) * LANE]
            mx = blk if mx is None else jnp.maximum(mx, blk)
    return jnp.broadcast_to(jnp.max(mx, axis=1, keepdims=True), mx.shape)


def _value_pass(chunks, s_r, mb):
    acc = None
    for _, v_ref, st, sz, off in chunks:
        ps = [jnp.exp2(s_r[:, off + t * LANE:off + (t + 1) * LANE] - mb).astype(BF16)
              for t in range(sz // LANE)]
        v_ones = jnp.concatenate([v_ref[0, 0, st:st + sz, :], jnp.ones((sz, LANE), BF16)], axis=1)
        part = jnp.dot(jnp.concatenate(ps, axis=1), v_ones, preferred_element_type=F32)
        acc = part if acc is None else acc + part
    return (acc[:, :LANE] / acc[:, LANE:LANE + 1]).astype(BF16)


def _fused_passes(q, chunks, s_w, s_r, mb_r):
    mx = None
    acc = None
    for k_ref, v_ref, st, sz, off in chunks:
        s = _qk(q, k_ref[0, 0, st:st + sz, :])
        s_w[:, off:off + sz] = s
        for t in range(sz // LANE):
            blk = s[:, t * LANE:(t + 1) * LANE]
            mx = blk if mx is None else jnp.maximum(mx, blk)
        ps = [jnp.exp2(s_r[:, off + t * LANE:off + (t + 1) * LANE] - mb_r).astype(BF16)
              for t in range(sz // LANE)]
        v_ones = jnp.concatenate([v_ref[0, 0, st:st + sz, :], jnp.ones((sz, LANE), BF16)], axis=1)
        part = jnp.dot(jnp.concatenate(ps, axis=1), v_ones, preferred_element_type=F32)
        acc = part if acc is None else acc + part
    mb_new = jnp.broadcast_to(jnp.max(mx, axis=1, keepdims=True), mx.shape)
    return mb_new, (acc[:, :LANE] / acc[:, LANE:LANE + 1]).astype(BF16)


def _key_chunks(kc_ref, vc_ref, kl_ref, vl_ref):
    n_ctx = kc_ref.shape[2]
    chunks = [(kc_ref, vc_ref, 0, n_ctx, 0)]
    if kl_ref is not None:
        for c in range(kl_ref.shape[2] // KEY_CHUNK):
            chunks.append((kl_ref, vl_ref, c * KEY_CHUNK, KEY_CHUNK, n_ctx + c * KEY_CHUNK))
    return chunks


def _dense_kernel(q_ref, kc_ref, kl_ref, vcp_ref, vlp_ref, vcc_ref, vlc_ref, o_ref,
                  sa_sc, sb_sc, mba_sc, mbb_sc, *, mask_q, n_tiles, tiles_per_head):
    step = pl.program_id(1)
    tq = sa_sc.shape[0]

    @pl.when(step == 0)
    def _():
        sa_sc[...] = jnp.zeros(sa_sc.shape, F32)
        mba_sc[...] = jnp.zeros(mba_sc.shape, F32)

    q0 = q_ref[0, 0, 0:tq, :]
    q1 = q_ref[0, 0, tq:2 * tq, :]
    if mask_q:
        half = (jnp.minimum(2 * step, n_tiles - 2) // tiles_per_head) // 3
        q0 = _half_mask(q0, half)
        q1 = _half_mask(q1, half)

    mb_new, o_ref[0, 0:tq, :] = _fused_passes(
        q0, _key_chunks(kc_ref, vcp_ref, kl_ref, vlp_ref), sb_sc, sa_sc, mba_sc[...])
    mbb_sc[...] = mb_new
    mb_new, o_ref[0, tq:2 * tq, :] = _fused_passes(
        q1, _key_chunks(kc_ref, vcc_ref, kl_ref, vlc_ref), sa_sc, sb_sc, mbb_sc[...])
    mba_sc[...] = mb_new


def _dense_attention(q, kc, vc, kl, vl, n_heads, maps, mask_q, name):
    b, _, nq, _ = q.shape
    qmap, kmap, vmap = maps
    n_ctx, n_lat = kc.shape[2], kl.shape[2]
    tq = Q_TILE
    tph = nq // tq
    assert tph % 2 == 0
    n_tiles = n_heads * tph
    head_cur = lambda s: jnp.minimum(2 * s, n_tiles - 2) // tph
    head_prev = lambda s: jnp.maximum(2 * s - 1, 0) // tph
    blk = lambda n, fmap, head: pl.BlockSpec((1, 1, n, LANE), lambda i, s: (i, fmap(head(s)), 0, 0))
    q_spec = pl.BlockSpec((1, 1, 2 * tq, LANE),
                          lambda i, s: (i, qmap(head_cur(s)), (jnp.minimum(2 * s, n_tiles - 2) % tph) // 2, 0))
    out = pl.pallas_call(
        functools.partial(_dense_kernel, mask_q=mask_q, n_tiles=n_tiles, tiles_per_head=tph),
        grid=(b, n_tiles // 2 + 1),
        in_specs=[q_spec, blk(n_ctx, kmap, head_cur), blk(n_lat, kmap, head_cur),
                  blk(n_ctx, vmap, head_prev), blk(n_lat, vmap, head_prev),
                  blk(n_ctx, vmap, head_cur), blk(n_lat, vmap, head_cur)],
        out_specs=pl.BlockSpec((1, 2 * tq, LANE), lambda i, s: (i, s, 0)),
        out_shape=jax.ShapeDtypeStruct((b, (n_tiles + 2) * tq, LANE), BF16),
        scratch_shapes=[pltpu.VMEM((tq, n_ctx + n_lat), F32), pltpu.VMEM((tq, n_ctx + n_lat), F32),
                        pltpu.VMEM((tq, LANE), F32), pltpu.VMEM((tq, LANE), F32)],
        compiler_params=_cparams(("parallel", "arbitrary")),
        name=name,
    )(q, kc, kl, vc, vl, vc, vl)
    return out


def _dense_ctx_kernel(q_ref, kc_ref, vc_ref, o_ref, s_sc, *, mask_q):
    q = q_ref[0, 0]
    if mask_q:
        q = _half_mask(q, pl.program_id(1) // 3)
    chunks = _key_chunks(kc_ref, vc_ref, None, None)
    mb = _score_pass(q, chunks, s_sc)
    o_ref[0, 0] = _value_pass(chunks, s_sc, mb)


def _dense_attention_ctx(q, kc, vc, n_heads, maps, mask_q, name):
    b, _, nq, _ = q.shape
    qmap, kmap, vmap = maps
    blk = lambda fmap: pl.BlockSpec((1, 1, nq, LANE), lambda i, h: (i, fmap(h), 0, 0))
    return pl.pallas_call(
        functools.partial(_dense_ctx_kernel, mask_q=mask_q),
        grid=(b, n_heads),
        in_specs=[blk(qmap), blk(kmap), blk(vmap)],
        out_specs=pl.BlockSpec((1, 1, nq, LANE), lambda i, h: (i, h, 0, 0)),
        out_shape=jax.ShapeDtypeStruct((b, n_heads, nq, LANE), BF16),
        scratch_shapes=[pltpu.VMEM((nq, kc.shape[2]), F32)],
        compiler_params=_cparams(("parallel", "parallel")),
        name=name,
    )(q, kc, vc)


def _window_kernel(q_ref, kc_ref, vc_ref, sink_ref, *rest, use_window):
    if use_window:
        kl_ref, vl_ref, o_ref = rest
    else:
        (o_ref,) = rest
    tq = q_ref.shape[1]
    span = tq + 2 * WINDOW
    j = pl.program_id(1)
    if use_window:
        n_lat = kl_ref.shape[1]
        t0 = j * tq
        start = pl.multiple_of(jnp.clip(t0 - WINDOW, 0, n_lat - span), LANE)
        qpos = t0 + lax.broadcasted_iota(jnp.int32, (tq, span), 0)
        kpos = start + lax.broadcasted_iota(jnp.int32, (tq, span), 1)
        valid = jnp.abs(qpos - kpos) <= WINDOW
        kw = kl_ref[0, pl.ds(start, span), :]
        vw = vl_ref[0, pl.ds(start, span), :]
    kcx = kc_ref[0]
    vcx = vc_ref[0]
    lane = lax.broadcasted_iota(jnp.int32, (tq, LANE), 1)
    res = []
    for h in range(C_HEADS):
        half = h // (C_HEADS // C_KV_HEADS)
        q = _masked_q(q_ref, h % 2, half)
        sink = sink_ref[0:1, h:h + 1]
        s_c = _qk(q, kcx)
        m = jnp.maximum(jnp.max(s_c, axis=1, keepdims=True), sink)
        if use_window:
            s_w = jnp.where(valid, _qk(q, kw), NEG_INF)
            m = jnp.maximum(m, jnp.max(s_w, axis=1, keepdims=True))
        p_c = jnp.exp(s_c - m)
        l = jnp.sum(p_c, axis=1, keepdims=True) + jnp.exp(sink - m)
        o = jnp.dot(p_c.astype(BF16), vcx, preferred_element_type=F32)
        if use_window:
            p_w = jnp.exp(s_w - m)
            l = l + jnp.sum(p_w, axis=1, keepdims=True)
            o = o + jnp.dot(p_w.astype(BF16), vw, preferred_element_type=F32)
        res.append(o / l)
    for ob in range(2):
        o_ref[0, :, ob * LANE:(ob + 1) * LANE] = jnp.where(lane < 64, res[ob], res[2 + ob]).astype(BF16)


def _window_attention(q, kc, vc, kl, vl, sink, name):
    b, nq, qw = q.shape
    tq = ROW_TILE
    n_ctx = kc.shape[1]
    use_window = kl is not None
    in_specs = [pl.BlockSpec((1, tq, qw), lambda i, j: (i, j, 0)),
                pl.BlockSpec((1, n_ctx, LANE), lambda i, j: (i, 0, 0)),
                pl.BlockSpec((1, n_ctx, LANE), lambda i, j: (i, 0, 0)),
                pl.BlockSpec((8, LANE), lambda i, j: (0, 0))]
    args = [q, kc, vc, sink]
    if use_window:
        n_lat = kl.shape[1]
        in_specs += [pl.BlockSpec((1, n_lat, LANE), lambda i, j: (i, 0, 0)),
                     pl.BlockSpec((1, n_lat, LANE), lambda i, j: (i, 0, 0))]
        args += [kl, vl]
    return pl.pallas_call(
        functools.partial(_window_kernel, use_window=use_window),
        grid=(b, nq // tq),
        in_specs=in_specs,
        out_specs=pl.BlockSpec((1, tq, qw), lambda i, j: (i, j, 0)),
        out_shape=jax.ShapeDtypeStruct((b, nq, qw), BF16),
        compiler_params=_cparams(("parallel", "parallel")),
        name=name,
    )(*args)


def _outproj_kernel(*refs):
    oa = refs[0:A_HEADS]
    ob = refs[A_HEADS:A_HEADS + B_HEADS]
    oc_ref, x_ref, mod_ref, g2n_ref, w_ref, wr_ref, xo_ref, h2_ref, aff_ref = refs[A_HEADS + B_HEADS:]
    tm = x_ref.shape[1]
    low = lax.broadcasted_iota(jnp.int32, (tm, LANE), 1) < 64
    parts = [jnp.where(low, oa[i][0], oa[3 + i][0]) for i in range(3)]
    parts += [jnp.where(low, ob[2 * i][0], ob[2 * i + 1][0]) for i in range(3)]
    parts.append(oc_ref[0])
    mix = jnp.dot(jnp.concatenate(parts, axis=1), w_ref[...], preferred_element_type=F32)
    x = x_ref[0] + mod_ref[0, 0, 2:3, :] * mix
    xo_ref[0] = x
    y = x * lax.rsqrt(jnp.mean(x * x, axis=-1, keepdims=True) + EPS) * g2n_ref[...]
    h2 = y * (1.0 + mod_ref[0, 0, 4:5, :]) + mod_ref[0, 0, 3:4, :]
    h2_ref[0] = h2.astype(BF16)
    logits = lax.dot_general(wr_ref[...], h2, (((1,), (1,)), ((), ())),
                             preferred_element_type=F32, precision=lax.Precision.HIGHEST)
    z = jnp.exp(logits - jnp.max(logits, axis=0, keepdims=True))
    aff_ref[0] = z / jnp.sum(z, axis=0, keepdims=True)


def _outproj(oa, ob, oc, xs, modv, kind, g2n, w_mix, wr_t, pad_rows, name):
    b, n, d = xs.shape
    tm = ROW_TILE
    e = wr_t.shape[0]
    const = lambda *shape: pl.BlockSpec(shape, lambda i, j: (0,) * len(shape))
    row = lambda w: pl.BlockSpec((1, tm, w), lambda i, j: (i, j, 0))
    head = lambda h: pl.BlockSpec((1, tm, LANE), lambda i, j: (i, (pad_rows + h * n) // tm + j, 0))
    return pl.pallas_call(
        _outproj_kernel,
        grid=(b, n // tm),
        in_specs=[head(h) for h in range(A_HEADS)] + [head(h) for h in range(B_HEADS)] + [
                  row(256), row(d),
                  pl.BlockSpec((1, 1, 8, d), lambda i, j: (i, kind, 0, 0)),
                  const(1, d), const(d, d), const(e, d)],
        out_specs=[row(d), row(d), pl.BlockSpec((1, e, tm), lambda i, j: (i, 0, j))],
        out_shape=[jax.ShapeDtypeStruct((b, n, d), F32), jax.ShapeDtypeStruct((b, n, d), BF16),
                   jax.ShapeDtypeStruct((b, e, n), F32)],
        compiler_params=_cparams(("parallel", "parallel")),
        name=name,
    )(*([oa] * A_HEADS), *([ob] * B_HEADS), oc, xs, modv, g2n, w_mix, wr_t)


def _prefix_exclusive(mask_f, tri):
    e, t = mask_f.shape
    ck = tri.shape[0]
    carry = jnp.zeros((e, 1), F32)
    parts, offs = [], [carry]
    for c in range(t // ck):
        blk = mask_f[:, c * ck:(c + 1) * ck]
        parts.append(jnp.dot(blk.astype(BF16), tri, preferred_element_type=F32) + carry)
        carry = carry + jnp.sum(blk, axis=1, keepdims=True)
        offs.append(carry)
    return jnp.concatenate(parts, axis=1) if len(parts) > 1 else parts[0], offs


def _topk_kernel(aff_ref, slot_ref, cnt_ref, *, cap, slot_stride):
    aff = aff_ref[0]
    e, t = aff.shape
    bits = pltpu.bitcast(aff, jnp.int32)
    kf = jnp.float32(cap)

    def search(i, lo):
        cand = lo | (jnp.int32(1) << (30 - i))
        n_ge = jnp.sum((bits >= cand).astype(F32), axis=1, keepdims=True)
        return jnp.where(n_ge >= kf, cand, lo)

    thr = lax.fori_loop(0, 31, search, jnp.zeros((e, 1), jnp.int32))
    gt = bits > thr
    eq = bits == thr
    need = kf - jnp.sum(gt.astype(F32), axis=1, keepdims=True)

    r = lax.broadcasted_iota(jnp.int32, (TOK_CHUNK, TOK_CHUNK), 0)
    c = lax.broadcasted_iota(jnp.int32, (TOK_CHUNK, TOK_CHUNK), 1)
    tri = (r < c).astype(BF16)
    tie_rank, _ = _prefix_exclusive(eq.astype(F32), tri)
    sel = gt | (eq & (tie_rank < need))
    slot, offs = _prefix_exclusive(sel.astype(F32), tri)
    base = pl.program_id(0) * slot_stride
    slot_ref[0] = jnp.where(sel, slot.astype(jnp.int32) + base, -1)
    lane = lax.broadcasted_iota(jnp.int32, (e, LANE), 1)
    cnt = jnp.zeros((e, LANE), jnp.int32)
    for ci, off in enumerate(offs):
        cnt = jnp.where(lane == ci, off.astype(jnp.int32) + base, cnt)
    cnt_ref[0] = cnt


def _topk(aff_t, cap, slot_stride, name):
    b, e, t = aff_t.shape
    return pl.pallas_call(
        functools.partial(_topk_kernel, cap=cap, slot_stride=slot_stride),
        grid=(b,),
        in_specs=[pl.BlockSpec((1, e, t), lambda i: (i, 0, 0))],
        out_specs=[pl.BlockSpec((1, e, t), lambda i: (i, 0, 0)),
                   pl.BlockSpec((1, e, LANE), lambda i: (i, 0, 0))],
        out_shape=[jax.ShapeDtypeStruct((b, e, t), jnp.int32), jax.ShapeDtypeStruct((b, e, LANE), jnp.int32)],
        compiler_params=_cparams(("parallel",)),
        name=name,
    )(aff_t)


def _moe_kernel(cnt_ref, h_ref, slot_ref, gate_ref, wg_ref, wu_ref, wd_ref, mod_ref, x_hbm, acc_ref,
                xe_sc, gs_sc, ye_sc, x_sem, *, n_chunks, n_sb):
    bi = pl.program_id(0)
    ei = pl.program_id(1)
    n_e = pl.num_programs(1)
    x_copy = pltpu.make_async_copy(x_hbm.at[bi], acc_ref.at[0], x_sem)

    @pl.when(ei == 0)
    def _():
        x_copy.start()

    def wait_for_x():
        @pl.when(ei == 0)
        def _():
            x_copy.wait()

    g2 = mod_ref[0, 0, 5:6, :]
    cap = n_sb * SLOT_BLOCK
    cbase = (bi * n_e + ei) * 32
    counts = [cnt_ref[cbase + c] for c in range(n_chunks + 1)]
    window = 2 * SLOT_BLOCK

    def ffn():
        xe = xe_sc[0:cap, :].astype(BF16)
        f = wg_ref.shape[2]
        fb = 512
        y = None
        for f0 in range(0, f, fb):
            hg = jnp.dot(xe, wg_ref[0, :, f0:f0 + fb], preferred_element_type=F32)
            hu = jnp.dot(xe, wu_ref[0, :, f0:f0 + fb], preferred_element_type=F32)
            hid = (hg * jax.nn.sigmoid(hg) * hu).astype(BF16)
            part = jnp.dot(hid, wd_ref[0, f0:f0 + fb, :], preferred_element_type=F32)
            y = part if y is None else y + part
        return y

    def tok(c):
        return slice(c * TOK_CHUNK, (c + 1) * TOK_CHUNK)

    narrow = counts[1] - counts[0] <= SLOT_BLOCK
    for c in range(1, n_chunks):
        narrow = narrow & (counts[c + 1] - counts[c] <= SLOT_BLOCK)

    @pl.when(narrow)
    def _():
        xe_sc[...] = jnp.zeros_like(xe_sc)
        starts = [pl.multiple_of(jnp.minimum((counts[c] // SLOT_BLOCK) * SLOT_BLOCK, cap - SLOT_BLOCK),
                                 SLOT_BLOCK) for c in range(n_chunks)]
        srow = lax.broadcasted_iota(jnp.int32, (GATHER_WINDOW, TOK_CHUNK), 0)
        for c in range(n_chunks):
            g0 = pl.multiple_of((counts[c] // 8) * 8, 8)
            oh = (slot_ref[0, 0, c:c + 1, :] - g0) == srow
            xe_sc[pl.ds(g0, GATHER_WINDOW), :] += jnp.dot(
                oh.astype(BF16), h_ref[0, tok(c), :], preferred_element_type=F32)
        ye_sc[0:cap, :] = ffn().astype(BF16)
        ye_sc[cap:cap + SLOT_BLOCK, :] = jnp.zeros((SLOT_BLOCK, ye_sc.shape[1]), BF16)
        wait_for_x()
        pad = jnp.zeros((LANE - n_chunks, TOK_CHUNK), F32)
        slot_t = jnp.concatenate([slot_ref[0, 0].astype(F32), pad], axis=0).T
        gate_t = jnp.concatenate([gate_ref[0, 0], pad], axis=0).T
        scol = lax.broadcasted_iota(jnp.int32, (TOK_CHUNK, window), 1).astype(F32)
        for c in range(n_chunks):
            oh_t = (slot_t[:, c:c + 1] - starts[c].astype(F32)) == scol
            contrib = jnp.dot(oh_t.astype(BF16), ye_sc[pl.ds(starts[c], window), :],
                              preferred_element_type=F32)
            acc_ref[0, tok(c), :] += (gate_t[:, c:c + 1] * g2) * contrib

    @pl.when(jnp.logical_not(narrow))
    def _():
        xe_sc[...] = jnp.zeros_like(xe_sc)
        gs_sc[...] = jnp.zeros_like(gs_sc)
        srow = lax.broadcasted_iota(jnp.int32, (SLOT_BLOCK, TOK_CHUNK), 0)

        def overlap(c, sb):
            return (counts[c] < (sb + 1) * SLOT_BLOCK) & (counts[c + 1] > sb * SLOT_BLOCK)

        def onehot(c, sb):
            return slot_ref[0, 0, c:c + 1, :] == (srow + sb * SLOT_BLOCK)

        for c in range(n_chunks):
            for sb in range(n_sb):
                @pl.when(overlap(c, sb))
                def _(c=c, sb=sb):
                    oh = onehot(c, sb)
                    rows = slice(sb * SLOT_BLOCK, (sb + 1) * SLOT_BLOCK)
                    xe_sc[rows] += jnp.dot(oh.astype(BF16), h_ref[0, tok(c), :], preferred_element_type=F32)
                    g = jnp.where(oh, gate_ref[0, 0, c:c + 1, :], 0.0)
                    gs_sc[rows] += jnp.sum(g, axis=1, keepdims=True)

        ye_sc[0:cap, :] = (ffn() * gs_sc[...]).astype(BF16)
        wait_for_x()

        for c in range(n_chunks):
            for sb in range(n_sb):
                @pl.when(overlap(c, sb))
                def _(c=c, sb=sb):
                    oh_t = onehot(c, sb).astype(F32).T.astype(BF16)
                    rows = slice(sb * SLOT_BLOCK, (sb + 1) * SLOT_BLOCK)
                    acc_ref[0, tok(c), :] += g2 * jnp.dot(oh_t, ye_sc[rows], preferred_element_type=F32)


def _moe(h2, slot, gate, cnt, wg, wu, wd, xs, modv, kind, cap, name):
    b, t, d = h2.shape
    e = slot.shape[1]
    f = wg.shape[2]
    n_chunks = t // TOK_CHUNK
    n_sb = cap // SLOT_BLOCK
    slot4 = slot.reshape(b, e, n_chunks, TOK_CHUNK)
    gate4 = gate.reshape(b, e, n_chunks, TOK_CHUNK)
    cnt_flat = cnt[:, :, :32].reshape(-1)
    grid_spec = pltpu.PrefetchScalarGridSpec(
        num_scalar_prefetch=1,
        grid=(b, e),
        in_specs=[pl.BlockSpec((1, t, d), lambda i, j, s: (i, 0, 0), pipeline_mode=pl.Buffered(1)),
                  pl.BlockSpec((1, 1, n_chunks, TOK_CHUNK), lambda i, j, s: (i, j, 0, 0)),
                  pl.BlockSpec((1, 1, n_chunks, TOK_CHUNK), lambda i, j, s: (i, j, 0, 0)),
                  pl.BlockSpec((1, d, f), lambda i, j, s: (j, 0, 0)),
                  pl.BlockSpec((1, d, f), lambda i, j, s: (j, 0, 0)),
                  pl.BlockSpec((1, f, d), lambda i, j, s: (j, 0, 0)),
                  pl.BlockSpec((1, 1, 8, d), lambda i, j, s: (i, kind, 0, 0)),
                  pl.BlockSpec(memory_space=pl.ANY)],
        out_specs=pl.BlockSpec((1, t, d), lambda i, j, s: (i, 0, 0), pipeline_mode=pl.Buffered(1)),
        scratch_shapes=[pltpu.VMEM((cap + 2 * SLOT_BLOCK, d), F32), pltpu.VMEM((cap, 1), F32),
                        pltpu.VMEM((cap + SLOT_BLOCK, d), BF16), pltpu.SemaphoreType.DMA(())],
    )
    return pl.pallas_call(
        functools.partial(_moe_kernel, n_chunks=n_chunks, n_sb=n_sb),
        grid_spec=grid_spec,
        out_shape=jax.ShapeDtypeStruct((b, t, d), F32),
        compiler_params=_cparams(("parallel", "arbitrary")),
        name=name,
    )(cnt_flat, h2, slot4, gate4, wg, wu, wd, modv, xs)


def _pad_lanes(v, width):
    return jnp.pad(v, (0, width - v.shape[0]))


def kernel(x, c, ctx, c_ctx, w_ada, b_ada, norm1_g, norm2_g, w_in, a_q_norm, a_k_norm, b_cq_norm, b_ckv_norm, w_uq, w_ukv, b_qn_norm, b_kn_norm, b_qr_norm, b_kr_norm, c_q_norm, c_k_norm, c_sink, w_out, w_router, w_e_gate, w_e_up, w_e_down):
    b, t, d = x.shape
    n_ctx = ctx.shape[1]
    depth = w_ada.shape[0]
    assert n_ctx == ROW_TILE and t % KEY_CHUNK == 0 and t % GRID_W == 0
    cap = max(1, CAP_FACTOR * t // N_EXPERTS)
    cap_c = max(1, CAP_FACTOR * n_ctx // N_EXPERTS)
    assert cap % SLOT_BLOCK == 0 and (b * cap_c) % SLOT_BLOCK == 0

    rows = ((b + 1 + 7) // 8) * 8
    cc = jnp.zeros((rows, d), F32).at[:b].set(c).at[b].set(c_ctx)
    mod = _modulation(cc, w_ada, b_ada)

    rope = _rope_tables(t)
    m64_np, mb_np = _seg_matrices()
    m64, mb = jnp.asarray(m64_np, BF16), jnp.asarray(mb_np, BF16)
    in_perm = _in_perm()
    uq_perm = _uq_perm()
    ukv_pk, ukv_pv = _ukv_perms()
    oa_rows = np.asarray(_pair_cols(0, A_HEADS))
    oc_rows = np.asarray(_pair_cols(0, C_HEADS)) + 768
    sa = HEAD_DIM ** -0.5
    sb = (B_NOPE + B_ROPE) ** -0.5
    z32, z64 = jnp.zeros((32,), F32), jnp.zeros((64,), F32)

    for l in range(depth):
        last = l == depth - 1
        m6 = mod[l].reshape(rows, 6, d)
        m8 = jnp.pad(m6, ((0, 0), (0, 2), (0, 0)))
        modv = jnp.stack([jnp.broadcast_to(m8[b][None], (b, 8, d)), m8[:b]], axis=1)

        w_in_p = _take_cols(w_in[l], in_perm).astype(BF16)
        wuq = jnp.pad(_take_cols(w_uq[l], uq_perm), ((0, 256 - B_Q_RANK), (0, 0))).astype(BF16)
        wukvk = _take_cols(w_ukv[l], ukv_pk).astype(BF16)
        wukvv = jnp.take(w_ukv[l], jnp.asarray(ukv_pv, jnp.int32), axis=1).astype(BF16)
        gains = jnp.stack([
            jnp.tile(a_q_norm[l], 2) * (sa * LOG2E), jnp.tile(a_k_norm[l], 2),
            jnp.tile(c_q_norm[l], 2) * sa, jnp.tile(c_k_norm[l], 2),
            jnp.concatenate([b_qn_norm[l] * (sb * LOG2E), b_qr_norm[l] * (sb * LOG2E), z32]),
            jnp.concatenate([b_kn_norm[l], z64]),
            jnp.concatenate([z64, b_kr_norm[l], z32]),
            jnp.zeros((LANE,), F32)])
        gcq = _pad_lanes(b_cq_norm[l], 256)[None]
        gckv = b_ckv_norm[l][None]
        g1 = norm1_g[l][None]
        g2n = norm2_g[l][None]
        mix_rows = np.concatenate([oa_rows, np.arange(384, 768), oc_rows])
        w_mix = jnp.take(w_out[l], jnp.asarray(mix_rows, jnp.int32), axis=0).astype(BF16)
        wr_t = w_router[l].T
        sink = jnp.zeros((8, LANE), F32).at[0, :C_HEADS].set(c_sink[l])
        wg = w_e_gate[l].astype(BF16)
        wu = w_e_up[l].astype(BF16)
        wd = w_e_down[l].astype(BF16)

        shared = (g1, w_in_p, gains, gcq, gckv, wuq, wukvk, wukvv, m64, mb)
        qa, ka, va, qb, kb, vb, qc, kc, vc = _inproj(x, modv, 1, *shared, rope)
        qac, kac, vac, qbc, kbc, vbc, qcc, kcc, vcc = _inproj(ctx, modv, 0, *shared, None)

        o_a = _dense_attention(qa, kac, vac, ka, va, A_HEADS, _A_MAPS, True, "attn_a")
        o_b = _dense_attention(qb, kbc, vbc, kb, vb, B_HEADS, _B_MAPS, False, "attn_b")
        o_c = _window_attention(qc, kcc, vcc, kc, vc, sink, "attn_c")
        x_mid, h2, aff = _outproj(o_a, o_b, o_c, x, modv, 1, g2n, w_mix, wr_t, Q_TILE, "outproj")
        slot, cnt = _topk(aff, cap, 0, "topk")
        x = _moe(h2, slot, aff, cnt, wg, wu, wd, x_mid, modv, 1, cap, "moe")

        if not last:
            o_ac = _dense_attention_ctx(qac, kac, vac, A_HEADS, _A_MAPS, True, "attn_a_ctx")
            o_bc = _dense_attention_ctx(qbc, kbc, vbc, B_HEADS, _B_MAPS, False, "attn_b_ctx")
            o_cc = _window_attention(qcc, kcc, vcc, None, None, sink, "attn_c_ctx")
            flat = lambda o: o.reshape(b, -1, LANE)
            c_mid, hc2, aff_c = _outproj(flat(o_ac), flat(o_bc), o_cc, ctx, modv, 0, g2n, w_mix, wr_t, 0,
                                         "outproj_ctx")
            slot_c, cnt_c = _topk(aff_c, cap_c, cap_c, "topk_ctx")
            e = N_EXPERTS
            slot_f = jnp.transpose(slot_c, (1, 0, 2)).reshape(1, e, b * n_ctx)
            gate_f = jnp.transpose(aff_c, (1, 0, 2)).reshape(1, e, b * n_ctx)
            starts = jnp.transpose(cnt_c[:, :, 0], (1, 0))
            cnt_f = jnp.concatenate([starts, cnt_c[b - 1, :, 1:2],
                                     jnp.zeros((e, LANE - b - 1), jnp.int32)], axis=1)[None]
            ctx = _moe(hc2.reshape(1, b * n_ctx, d), slot_f, gate_f, cnt_f, wg, wu, wd,
                       c_mid.reshape(1, b * n_ctx, d), modv, 0, b * cap_c, "moe_ctx").reshape(b, n_ctx, d)
    return x
```

```python
import functools
import math

import numpy as np
import jax
import jax.numpy as jnp
from jax import lax
from jax.experimental import pallas as pl
from jax.experimental.pallas import tpu as pltpu

F32 = jnp.float32
BF16 = jnp.bfloat16

GRID_W = 64
ROPE_THETA = 10000.0
EPS = 1e-6
NEG_INF = -1e30
HEAD_DIM = 64
A_HEADS, A_KV_HEADS = 6, 2
B_HEADS, B_Q_RANK, B_KV_RANK, B_NOPE, B_ROPE, B_V = 6, 192, 128, 64, 32, 64
C_HEADS, C_KV_HEADS = 4, 2
N_EXPERTS = 16
CAP_FACTOR = 2
WINDOW = 128

LANE = 128
ROW_TILE = 256
PROJ_TILE = 512
Q_TILE = 512
KEY_CHUNK = 512
LOG2E = 1.4426950408889634
SLOT_BLOCK = 128
TOK_CHUNK = 256
GATHER_WINDOW = 144
VMEM_LIMIT = 56 * 1024 * 1024

_QA, _KA, _VA = 0, 384, 512
_QC, _KC, _VC = 640, 896, 1024
_BCQ, _BCKV, _BKR = 1152, 1408, 1536
IN_PAD = 1664
_ORIG = dict(aq=0, ak=384, av=512, bcq=640, bckv=832, bkr=960, cq=992, ck=1248, cv=1376)


def _cparams(sem, vmem=VMEM_LIMIT):
    return pltpu.CompilerParams(dimension_semantics=sem, vmem_limit_bytes=vmem)


def _pair_cols(base, n_heads):
    half = n_heads // 2
    cols = []
    for i in range(half):
        cols += list(range(base + i * 64, base + (i + 1) * 64))
        cols += list(range(base + (half + i) * 64, base + (half + i + 1) * 64))
    return cols


def _in_perm():
    perm = -np.ones((IN_PAD,), np.int64)
    perm[_QA:_QA + 384] = _pair_cols(_ORIG["aq"], A_HEADS)
    perm[_KA:_KA + 128] = np.arange(_ORIG["ak"], _ORIG["ak"] + 128)
    perm[_VA:_VA + 128] = np.arange(_ORIG["av"], _ORIG["av"] + 128)
    perm[_QC:_QC + 256] = _pair_cols(_ORIG["cq"], C_HEADS)
    perm[_KC:_KC + 128] = np.arange(_ORIG["ck"], _ORIG["ck"] + 128)
    perm[_VC:_VC + 128] = np.arange(_ORIG["cv"], _ORIG["cv"] + 128)
    perm[_BCQ:_BCQ + B_Q_RANK] = np.arange(_ORIG["bcq"], _ORIG["bcq"] + B_Q_RANK)
    perm[_BCKV:_BCKV + 128] = np.arange(_ORIG["bckv"], _ORIG["bckv"] + 128)
    perm[_BKR + 64:_BKR + 96] = np.arange(_ORIG["bkr"], _ORIG["bkr"] + 32)
    return perm


def _take_cols(w, perm):
    safe = np.where(perm >= 0, perm, 0)
    out = jnp.take(w, jnp.asarray(safe, jnp.int32), axis=1)
    return jnp.where(jnp.asarray(perm >= 0)[None, :], out, 0.0)


def _uq_perm():
    perm = -np.ones((B_HEADS * LANE,), np.int64)
    for h in range(B_HEADS):
        perm[h * LANE:h * LANE + 96] = np.arange(h * 96, h * 96 + 96)
    return perm


def _ukv_perms():
    pk = -np.ones((B_HEADS * LANE,), np.int64)
    pv = np.zeros((B_HEADS * 64,), np.int64)
    for h in range(B_HEADS):
        pk[h * LANE:h * LANE + 64] = np.arange(h * 128, h * 128 + 64)
        pv[h * 64:(h + 1) * 64] = np.arange(h * 128 + 64, h * 128 + 128)
    return pk, pv


def _seg_matrices():
    i = np.arange(LANE)
    m64 = (i[:, None] // 64 == i[None, :] // 64).astype(np.float32) / 64.0
    mb = np.zeros((LANE, LANE), np.float32)
    mb[:64, :64] = 1.0 / 64.0
    mb[64:96, 64:96] = 1.0 / 32.0
    def pair(a, b):
        out = np.zeros((2 * LANE, 2 * LANE), np.float32)
        out[:LANE, :LANE] = a
        out[LANE:, LANE:] = b
        return out

    return pair(m64, m64), pair(m64, mb), pair(mb, mb)


def _rope_tables(length):
    rows = length // GRID_W
    t = np.arange(rows * GRID_W)
    row = np.repeat(np.arange(rows), GRID_W).astype(np.float32)
    col = (t % GRID_W).astype(np.float32)

    def tabs(dim):
        axis_dim = dim // 2
        inv = (np.float32(ROPE_THETA) ** (-np.arange(0, axis_dim, 2, dtype=np.float32) / axis_dim)).astype(np.float32)
        ar = row[:, None] * inv[None, :]
        ac = col[:, None] * inv[None, :]
        cos = np.concatenate([np.cos(ar), np.cos(ar), np.cos(ac), np.cos(ac)], axis=-1)
        sin = np.concatenate([-np.sin(ar), np.sin(ar), -np.sin(ac), np.sin(ac)], axis=-1)
        return cos.astype(np.float32), sin.astype(np.float32)

    c64, s64 = tabs(HEAD_DIM)
    cb, sb = tabs(B_ROPE)
    n = c64.shape[0]
    ones64, zeros64 = np.ones((n, 64), np.float32), np.zeros((n, 64), np.float32)
    ones32, zeros32 = np.ones((n, 32), np.float32), np.zeros((n, 32), np.float32)
    return jnp.asarray(np.concatenate([c64, c64, s64, s64,
                                       ones64, cb, ones32, zeros64, sb, zeros32], axis=-1))


def _mod_kernel(c_ref, w_ref, b_ref, o_ref):
    cv = c_ref[...]
    a = (cv * jax.nn.sigmoid(cv)).astype(BF16)
    o_ref[0] = jnp.dot(a, w_ref[0].astype(BF16), preferred_element_type=F32) + b_ref[0]


def _modulation(cc, w_ada, b_ada):
    n_layers, d, n6 = w_ada.shape
    rows = cc.shape[0]
    tn = 1536
    return pl.pallas_call(
        _mod_kernel,
        grid=(n_layers, n6 // tn),
        in_specs=[pl.BlockSpec((rows, d), lambda l, j: (0, 0)),
                  pl.BlockSpec((1, d, tn), lambda l, j: (l, 0, j)),
                  pl.BlockSpec((1, 1, tn), lambda l, j: (l, 0, j))],
        out_specs=pl.BlockSpec((1, rows, tn), lambda l, j: (l, 0, j)),
        out_shape=jax.ShapeDtypeStruct((n_layers, rows, n6), F32),
        compiler_params=_cparams(("parallel", "parallel")),
        name="adaln_mod",
    )(cc, w_ada, b_ada.reshape(n_layers, 1, n6))


def _seg_rsqrt(x2, m_ref):
    return lax.rsqrt(jnp.dot((x2 * x2).astype(BF16), m_ref[...], preferred_element_type=F32) + EPS)


def _rope(x, cos, sin, half):
    lane = lax.broadcasted_iota(jnp.int32, x.shape, 1)
    lo = (lane % (2 * half)) < half
    partner = jnp.where(lo, pltpu.roll(x, LANE - half, 1), pltpu.roll(x, half, 1))
    return x * cos + partner * sin


def _inproj_kernel(x_ref, mod_ref, g1_ref, w_ref, gains_ref, gcq_ref, gckv_ref, wuq_ref, wukvk_ref,
                   wukvv_ref, mm_ref, mx_ref, mbb_ref, *rest, use_rope):
    if use_rope:
        rope_ref = rest[0]
        outs = rest[1:]
    else:
        rope_ref = None
        outs = rest
    qa_ref, ka_ref, va_ref, qb_ref, kb_ref, vb_ref, qc_ref, kc_ref, vc_ref = outs

    x = x_ref[0]
    y = x * lax.rsqrt(jnp.mean(x * x, axis=-1, keepdims=True) + EPS) * g1_ref[...]
    h = y * (1.0 + mod_ref[0, 0, 1:2, :]) + mod_ref[0, 0, 0:1, :]
    p = jnp.dot(h.astype(BF16), w_ref[...], preferred_element_type=F32)

    def head_pair(x2, m_ref, gain_rows, rope_cols, half):
        r = _seg_rsqrt(x2, m_ref)
        out = []
        for i in range(2):
            sl = slice(i * LANE, (i + 1) * LANE)
            y = x2[:, sl] * r[:, sl] * gains_ref[gain_rows[i]:gain_rows[i] + 1, :]
            if rope_ref is not None and rope_cols[i] is not None:
                c0, s0 = rope_cols[i]
                y = _rope(y, rope_ref[:, c0:c0 + LANE], rope_ref[:, s0:s0 + LANE], half[i])
            out.append(y)
        return out

    r64 = (0, 128)
    rb = (256, 384)
    qa0, qa1 = head_pair(p[:, _QA:_QA + 256], mm_ref, (0, 0), (r64, r64), (16, 16))
    qa2, ka = head_pair(p[:, _QA + 256:_QA + 512], mm_ref, (0, 1), (r64, r64), (16, 16))
    for i, blk in enumerate((qa0, qa1, qa2)):
        qa_ref[0, i] = blk.astype(BF16)
    ka_ref[0, 0] = ka.astype(BF16)
    va_ref[0, 0] = p[:, _VA:_VA + LANE].astype(BF16)
    qc0, qc1 = head_pair(p[:, _QC:_QC + 256], mm_ref, (2, 2), (r64, r64), (16, 16))
    qc_ref[0, :, 0:LANE] = qc0.astype(BF16)
    qc_ref[0, :, LANE:2 * LANE] = qc1.astype(BF16)
    kc, kpe = head_pair(jnp.concatenate([p[:, _KC:_KC + LANE], p[:, _BKR:_BKR + LANE]], axis=1),
                        mx_ref, (3, 6), (r64, rb), (16, 8))
    kc_ref[0] = kc.astype(BF16)
    vc_ref[0] = p[:, _VC:_VC + LANE].astype(BF16)

    cq = p[:, _BCQ:_BCQ + 256]
    cq = cq * lax.rsqrt(jnp.sum(cq * cq, axis=-1, keepdims=True) * (1.0 / B_Q_RANK) + EPS) * gcq_ref[...]
    qb = jnp.dot(cq.astype(BF16), wuq_ref[...], preferred_element_type=F32)
    ckv = p[:, _BCKV:_BCKV + LANE]
    ckv = ckv * lax.rsqrt(jnp.mean(ckv * ckv, axis=-1, keepdims=True) + EPS) * gckv_ref[...]
    ckv = ckv.astype(BF16)
    kbn = jnp.dot(ckv, wukvk_ref[...], preferred_element_type=F32)
    vbv = jnp.dot(ckv, wukvv_ref[...], preferred_element_type=F32).astype(BF16)
    for i in range(B_HEADS // 2):
        vb_ref[0, i] = vbv[:, i * LANE:(i + 1) * LANE]
    for i in range(B_HEADS // 2):
        sl = slice(2 * i * LANE, (2 * i + 2) * LANE)
        q0, q1 = head_pair(qb[:, sl], mbb_ref, (4, 4), (rb, rb), (8, 8))
        k0, k1 = head_pair(kbn[:, sl], mbb_ref, (5, 5), (None, None), (8, 8))
        qb_ref[0, 2 * i] = q0.astype(BF16)
        qb_ref[0, 2 * i + 1] = q1.astype(BF16)
        kb_ref[0, 2 * i] = (k0 + kpe).astype(BF16)
        kb_ref[0, 2 * i + 1] = (k1 + kpe).astype(BF16)


def _inproj(xs, modv, kind, g1, w_in_p, gains, gcq, gckv, wuq, wukvk, wukvv, mm, mx, mbb, rope):
    b, n, d = xs.shape
    tm = min(PROJ_TILE, n)
    use_rope = rope is not None
    const = lambda *shape: pl.BlockSpec(shape, lambda i, j: (0,) * len(shape))
    in_specs = [pl.BlockSpec((1, tm, d), lambda i, j: (i, j, 0)),
                pl.BlockSpec((1, 1, 8, d), lambda i, j: (i, kind, 0, 0)),
                const(1, d), const(d, IN_PAD), const(8, LANE), const(1, 256), const(1, LANE),
                const(256, 768), const(LANE, 768), const(LANE, 384),
                const(256, 256), const(256, 256), const(256, 256)]
    args = [xs, modv, g1, w_in_p, gains, gcq, gckv, wuq, wukvk, wukvv, mm, mx, mbb]
    if use_rope:
        in_specs.append(pl.BlockSpec((tm, 512), lambda i, j: (j, 0)))
        args.append(rope)
    blocks = (3, 1, 1, B_HEADS, B_HEADS, B_HEADS // 2)
    widths = (256, 128, 128)
    out_specs = ([pl.BlockSpec((1, nb, tm, LANE), lambda i, j: (i, 0, j, 0)) for nb in blocks]
                 + [pl.BlockSpec((1, tm, w), lambda i, j: (i, j, 0)) for w in widths])
    out_shape = ([jax.ShapeDtypeStruct((b, nb, n, LANE), BF16) for nb in blocks]
                 + [jax.ShapeDtypeStruct((b, n, w), BF16) for w in widths])
    return pl.pallas_call(
        functools.partial(_inproj_kernel, use_rope=use_rope),
        grid=(b, n // tm),
        in_specs=in_specs,
        out_specs=out_specs,
        out_shape=out_shape,
        compiler_params=_cparams(("parallel", "parallel")),
        name="inproj_rope" if use_rope else "inproj_ctx",
    )(*args)


_A_MAPS = (lambda h: h % 3, lambda h: 0, lambda h: 0)
_B_MAPS = (lambda h: h, lambda h: h, lambda h: h // 2)


def _masked_q(q_ref, qb, half):
    q = q_ref[0, :, qb * LANE:(qb + 1) * LANE]
    if half is None:
        return q
    lane = lax.broadcasted_iota(jnp.int32, q.shape, 1)
    keep = (lane < 64) if half == 0 else (lane >= 64)
    return jnp.where(keep, q, jnp.zeros_like(q))


def _qk(q, k):
    return lax.dot_general(q, k, (((1,), (1,)), ((), ())), preferred_element_type=F32)


def _half_mask(q, half):
    lane = lax.broadcasted_iota(jnp.int32, q.shape, 1)
    return jnp.where((lane // 64) == half, q, jnp.zeros_like(q))


def _score_pass(q, chunks, s_w):
    mx = None
    for k_ref, _, st, sz, off in chunks:
        s = _qk(q, k_ref[0, 0, st:st + sz, :])
        s_w[:, off:off + sz] = s
        for t in range(sz // LANE):
            blk = s[:, t * LANE:(t + 1) * LANE]
            mx = blk if mx is None else jnp.maximum(mx, blk)
    return jnp.broadcast_to(jnp.max(mx, axis=1, keepdims=True), mx.shape)


def _value_pass(chunks, s_r, mb):
    acc = None
    for _, v_ref, st, sz, off in chunks:
        ps = [jnp.exp2(s_r[:, off + t * LANE:off + (t + 1) * LANE] - mb).astype(BF16)
              for t in range(sz // LANE)]
        v_ones = jnp.concatenate([v_ref[0, 0, st:st + sz, :], jnp.ones((sz, LANE), BF16)], axis=1)
        part = jnp.dot(jnp.concatenate(ps, axis=1), v_ones, preferred_element_type=F32)
        acc = part if acc is None else acc + part
    return (acc[:, :LANE] / acc[:, LANE:LANE + 1]).astype(BF16)


def _fused_passes(q, chunks, s_w, s_r, mb_r):
    mx = None
    acc = None
    for k_ref, v_ref, st, sz, off in chunks:
        s = _qk(q, k_ref[0, 0, st:st + sz, :])
        s_w[:, off:off + sz] = s
        for t in range(sz // LANE):
            blk = s[:, t * LANE:(t + 1) * LANE]
            mx = blk if mx is None else jnp.maximum(mx, blk)
        ps = [jnp.exp2(s_r[:, off + t * LANE:off + (t + 1) * LANE] - mb_r).astype(BF16)
              for t in range(sz // LANE)]
        v_ones = jnp.concatenate([v_ref[0, 0, st:st + sz, :], jnp.ones((sz, LANE), BF16)], axis=1)
        part = jnp.dot(jnp.concatenate(ps, axis=1), v_ones, preferred_element_type=F32)
        acc = part if acc is None else acc + part
    mb_new = jnp.broadcast_to(jnp.max(mx, axis=1, keepdims=True), mx.shape)
    return mb_new, (acc[:, :LANE] / acc[:, LANE:LANE + 1]).astype(BF16)


def _key_chunks(kc_ref, vc_ref, kl_ref, vl_ref):
    n_ctx = kc_ref.shape[2]
    chunks = [(kc_ref, vc_ref, 0, n_ctx, 0)]
    if kl_ref is not None:
        for c in range(kl_ref.shape[2] // KEY_CHUNK):
            chunks.append((kl_ref, vl_ref, c * KEY_CHUNK, KEY_CHUNK, n_ctx + c * KEY_CHUNK))
    return chunks


def _dense_kernel(q_ref, kc_ref, kl_ref, vcp_ref, vlp_ref, vcc_ref, vlc_ref, o_ref,
                  sa_sc, sb_sc, mba_sc, mbb_sc, *, mask_q, n_tiles, tiles_per_head, n_sub):
    step = pl.program_id(1)
    tq = sa_sc.shape[0]

    @pl.when(step == 0)
    def _():
        sa_sc[...] = jnp.zeros(sa_sc.shape, F32)
        mba_sc[...] = jnp.zeros(mba_sc.shape, F32)

    if mask_q:
        half = (jnp.minimum(n_sub * step, n_tiles - n_sub) // tiles_per_head) // 3
    for k in range(n_sub):
        rows = slice(k * tq, (k + 1) * tq)
        q = q_ref[0, 0, rows, :]
        if mask_q:
            q = _half_mask(q, half)
        v_refs = (vcp_ref, vlp_ref) if k == 0 else (vcc_ref, vlc_ref)
        chunks = _key_chunks(kc_ref, v_refs[0], kl_ref, v_refs[1])
        s_w, s_r, mb_w, mb_r = ((sb_sc, sa_sc, mbb_sc, mba_sc) if k % 2 == 0
                                else (sa_sc, sb_sc, mba_sc, mbb_sc))
        mb_new, o_ref[0, rows, :] = _fused_passes(q, chunks, s_w, s_r, mb_r[...])
        mb_w[...] = mb_new


def _dense_attention(q, kc, vc, kl, vl, n_heads, maps, mask_q, name):
    b, _, nq, _ = q.shape
    qmap, kmap, vmap = maps
    n_ctx, n_lat = kc.shape[2], kl.shape[2]
    tq = Q_TILE
    tph = nq // tq
    n_sub = 4 if tph % 4 == 0 else 2
    assert tph % n_sub == 0
    n_tiles = n_heads * tph
    first = lambda s: jnp.minimum(n_sub * s, n_tiles - n_sub)
    head_cur = lambda s: first(s) // tph
    head_prev = lambda s: jnp.maximum(n_sub * s - 1, 0) // tph
    blk = lambda n, fmap, head: pl.BlockSpec((1, 1, n, LANE), lambda i, s: (i, fmap(head(s)), 0, 0))
    q_spec = pl.BlockSpec((1, 1, n_sub * tq, LANE),
                          lambda i, s: (i, qmap(head_cur(s)), (first(s) % tph) // n_sub, 0))
    out = pl.pallas_call(
        functools.partial(_dense_kernel, mask_q=mask_q, n_tiles=n_tiles, tiles_per_head=tph, n_sub=n_sub),
        grid=(b, n_tiles // n_sub + 1),
        in_specs=[q_spec, blk(n_ctx, kmap, head_cur), blk(n_lat, kmap, head_cur),
                  blk(n_ctx, vmap, head_prev), blk(n_lat, vmap, head_prev),
                  blk(n_ctx, vmap, head_cur), blk(n_lat, vmap, head_cur)],
        out_specs=pl.BlockSpec((1, n_sub * tq, LANE), lambda i, s: (i, s, 0)),
        out_shape=jax.ShapeDtypeStruct((b, (n_tiles + n_sub) * tq, LANE), BF16),
        scratch_shapes=[pltpu.VMEM((tq, n_ctx + n_lat), F32), pltpu.VMEM((tq, n_ctx + n_lat), F32),
                        pltpu.VMEM((tq, LANE), F32), pltpu.VMEM((tq, LANE), F32)],
        compiler_params=_cparams(("parallel", "arbitrary")),
        name=name,
    )(q, kc, kl, vc, vl, vc, vl)
    return out


def _dense_ctx_kernel(q_ref, kc_ref, vc_ref, o_ref, s_sc, *, mask_q):
    q = q_ref[0, 0]
    if mask_q:
        q = _half_mask(q, pl.program_id(1) // 3)
    chunks = _key_chunks(kc_ref, vc_ref, None, None)
    mb = _score_pass(q, chunks, s_sc)
    o_ref[0, 0] = _value_pass(chunks, s_sc, mb)


def _dense_attention_ctx(q, kc, vc, n_heads, maps, mask_q, name):
    b, _, nq, _ = q.shape
    qmap, kmap, vmap = maps
    blk = lambda fmap: pl.BlockSpec((1, 1, nq, LANE), lambda i, h: (i, fmap(h), 0, 0))
    return pl.pallas_call(
        functools.partial(_dense_ctx_kernel, mask_q=mask_q),
        grid=(b, n_heads),
        in_specs=[blk(qmap), blk(kmap), blk(vmap)],
        out_specs=pl.BlockSpec((1, 1, nq, LANE), lambda i, h: (i, h, 0, 0)),
        out_shape=jax.ShapeDtypeStruct((b, n_heads, nq, LANE), BF16),
        scratch_shapes=[pltpu.VMEM((nq, kc.shape[2]), F32)],
        compiler_params=_cparams(("parallel", "parallel")),
        name=name,
    )(q, kc, vc)


def _window_kernel(q_ref, kc_ref, vc_ref, sink_ref, *rest, use_window):
    if use_window:
        kl_ref, vl_ref, o_ref = rest
    else:
        (o_ref,) = rest
    tq = q_ref.shape[1]
    span = tq + 2 * WINDOW
    j = pl.program_id(1)
    if use_window:
        n_lat = kl_ref.shape[1]
        t0 = j * tq
        start = pl.multiple_of(jnp.clip(t0 - WINDOW, 0, n_lat - span), LANE)
        qpos = t0 + lax.broadcasted_iota(jnp.int32, (tq, span), 0)
        kpos = start + lax.broadcasted_iota(jnp.int32, (tq, span), 1)
        valid = jnp.abs(qpos - kpos) <= WINDOW
        kw = kl_ref[0, pl.ds(start, span), :]
        vw = vl_ref[0, pl.ds(start, span), :]
    kcx = kc_ref[0]
    vcx = vc_ref[0]
    lane = lax.broadcasted_iota(jnp.int32, (tq, LANE), 1)
    res = []
    for h in range(C_HEADS):
        half = h // (C_HEADS // C_KV_HEADS)
        q = _masked_q(q_ref, h % 2, half)
        sink = sink_ref[0:1, h:h + 1]
        s_c = _qk(q, kcx)
        m = jnp.maximum(jnp.max(s_c, axis=1, keepdims=True), sink)
        if use_window:
            s_w = jnp.where(valid, _qk(q, kw), NEG_INF)
            m = jnp.maximum(m, jnp.max(s_w, axis=1, keepdims=True))
        p_c = jnp.exp(s_c - m)
        l = jnp.sum(p_c, axis=1, keepdims=True) + jnp.exp(sink - m)
        o = jnp.dot(p_c.astype(BF16), vcx, preferred_element_type=F32)
        if use_window:
            p_w = jnp.exp(s_w - m)
            l = l + jnp.sum(p_w, axis=1, keepdims=True)
            o = o + jnp.dot(p_w.astype(BF16), vw, preferred_element_type=F32)
        res.append(o / l)
    for ob in range(2):
        o_ref[0, :, ob * LANE:(ob + 1) * LANE] = jnp.where(lane < 64, res[ob], res[2 + ob]).astype(BF16)


def _window_attention(q, kc, vc, kl, vl, sink, name):
    b, nq, qw = q.shape
    tq = ROW_TILE
    n_ctx = kc.shape[1]
    use_window = kl is not None
    in_specs = [pl.BlockSpec((1, tq, qw), lambda i, j: (i, j, 0)),
                pl.BlockSpec((1, n_ctx, LANE), lambda i, j: (i, 0, 0)),
                pl.BlockSpec((1, n_ctx, LANE), lambda i, j: (i, 0, 0)),
                pl.BlockSpec((8, LANE), lambda i, j: (0, 0))]
    args = [q, kc, vc, sink]
    if use_window:
        n_lat = kl.shape[1]
        in_specs += [pl.BlockSpec((1, n_lat, LANE), lambda i, j: (i, 0, 0)),
                     pl.BlockSpec((1, n_lat, LANE), lambda i, j: (i, 0, 0))]
        args += [kl, vl]
    return pl.pallas_call(
        functools.partial(_window_kernel, use_window=use_window),
        grid=(b, nq // tq),
        in_specs=in_specs,
        out_specs=pl.BlockSpec((1, tq, qw), lambda i, j: (i, j, 0)),
        out_shape=jax.ShapeDtypeStruct((b, nq, qw), BF16),
        compiler_params=_cparams(("parallel", "parallel")),
        name=name,
    )(*args)


def _outproj_kernel(*refs):
    oa = refs[0:A_HEADS]
    ob = refs[A_HEADS:A_HEADS + B_HEADS]
    oc_ref, x_ref, mod_ref, g2n_ref, w_ref, wr_ref, xo_ref, h2_ref, aff_ref = refs[A_HEADS + B_HEADS:]
    tm = x_ref.shape[1]
    low = lax.broadcasted_iota(jnp.int32, (tm, LANE), 1) < 64
    parts = [jnp.where(low, oa[i][0], oa[3 + i][0]) for i in range(3)]
    parts += [jnp.where(low, ob[2 * i][0], ob[2 * i + 1][0]) for i in range(3)]
    parts.append(oc_ref[0])
    mix = jnp.dot(jnp.concatenate(parts, axis=1), w_ref[...], preferred_element_type=F32)
    x = x_ref[0] + mod_ref[0, 0, 2:3, :] * mix
    xo_ref[0] = x
    y = x * lax.rsqrt(jnp.mean(x * x, axis=-1, keepdims=True) + EPS) * g2n_ref[...]
    h2 = y * (1.0 + mod_ref[0, 0, 4:5, :]) + mod_ref[0, 0, 3:4, :]
    h2_ref[0] = h2.astype(BF16)
    wr = wr_ref[...]
    wr_hi = wr.astype(BF16)
    wr_lo = (wr - wr_hi.astype(F32)).astype(BF16)
    h2_hi = h2.astype(BF16)
    h2_lo = (h2 - h2_hi.astype(F32)).astype(BF16)
    logits = (jnp.dot(h2_hi, wr_hi, preferred_element_type=F32)
              + jnp.dot(h2_lo, wr_hi, preferred_element_type=F32)
              + jnp.dot(h2_hi, wr_lo, preferred_element_type=F32))
    logits = logits.T[0:aff_ref.shape[1], :]
    z = jnp.exp(logits - jnp.max(logits, axis=0, keepdims=True))
    aff_ref[0] = z / jnp.sum(z, axis=0, keepdims=True)


def _outproj(oa, ob, oc, xs, modv, kind, g2n, w_mix, wr_t, pad_rows, name):
    b, n, d = xs.shape
    tm = min(PROJ_TILE, n)
    e = N_EXPERTS
    const = lambda *shape: pl.BlockSpec(shape, lambda i, j: (0,) * len(shape))
    row = lambda w: pl.BlockSpec((1, tm, w), lambda i, j: (i, j, 0))
    head = lambda h: pl.BlockSpec((1, tm, LANE), lambda i, j: (i, (pad_rows + h * n) // tm + j, 0))
    return pl.pallas_call(
        _outproj_kernel,
        grid=(b, n // tm),
        in_specs=[head(h) for h in range(A_HEADS)] + [head(h) for h in range(B_HEADS)] + [
                  row(256), row(d),
                  pl.BlockSpec((1, 1, 8, d), lambda i, j: (i, kind, 0, 0)),
                  const(1, d), const(d, d), const(d, LANE)],
        out_specs=[row(d), row(d), pl.BlockSpec((1, e, tm), lambda i, j: (i, 0, j))],
        out_shape=[jax.ShapeDtypeStruct((b, n, d), F32), jax.ShapeDtypeStruct((b, n, d), BF16),
                   jax.ShapeDtypeStruct((b, e, n), F32)],
        compiler_params=_cparams(("parallel", "parallel")),
        name=name,
    )(*([oa] * A_HEADS), *([ob] * B_HEADS), oc, xs, modv, g2n, w_mix, wr_t)


def _prefix_exclusive(mask_f, tri):
    e, t = mask_f.shape
    ck = tri.shape[0]
    carry = jnp.zeros((e, 1), F32)
    parts, offs = [], [carry]
    for c in range(t // ck):
        blk = mask_f[:, c * ck:(c + 1) * ck]
        parts.append(jnp.dot(blk.astype(BF16), tri, preferred_element_type=F32) + carry)
        carry = carry + jnp.sum(blk, axis=1, keepdims=True)
        offs.append(carry)
    return jnp.concatenate(parts, axis=1) if len(parts) > 1 else parts[0], offs


def _topk_kernel(aff_ref, slot_ref, cnt_ref, *, cap, slot_stride):
    aff = aff_ref[0]
    e, t = aff.shape
    bits = pltpu.bitcast(aff, jnp.int32)
    kf = jnp.float32(cap)

    def search(i, lo):
        cand = lo | (jnp.int32(1) << (30 - i))
        n_ge = jnp.sum((bits >= cand).astype(F32), axis=1, keepdims=True)
        return jnp.where(n_ge >= kf, cand, lo)

    thr = lax.fori_loop(0, 31, search, jnp.zeros((e, 1), jnp.int32))
    gt = bits > thr
    eq = bits == thr
    need = kf - jnp.sum(gt.astype(F32), axis=1, keepdims=True)

    r = lax.broadcasted_iota(jnp.int32, (TOK_CHUNK, TOK_CHUNK), 0)
    c = lax.broadcasted_iota(jnp.int32, (TOK_CHUNK, TOK_CHUNK), 1)
    tri = (r < c).astype(BF16)
    tie_rank, _ = _prefix_exclusive(eq.astype(F32), tri)
    sel = gt | (eq & (tie_rank < need))
    slot, offs = _prefix_exclusive(sel.astype(F32), tri)
    base = pl.program_id(0) * slot_stride
    slot_ref[0] = jnp.where(sel, slot.astype(jnp.int32) + base, -1)
    lane = lax.broadcasted_iota(jnp.int32, (e, LANE), 1)
    cnt = jnp.zeros((e, LANE), jnp.int32)
    for ci, off in enumerate(offs):
        cnt = jnp.where(lane == ci, off.astype(jnp.int32) + base, cnt)
    cnt_ref[0] = cnt


def _topk(aff_t, cap, slot_stride, name):
    b, e, t = aff_t.shape
    return pl.pallas_call(
        functools.partial(_topk_kernel, cap=cap, slot_stride=slot_stride),
        grid=(b,),
        in_specs=[pl.BlockSpec((1, e, t), lambda i: (i, 0, 0))],
        out_specs=[pl.BlockSpec((1, e, t), lambda i: (i, 0, 0)),
                   pl.BlockSpec((1, e, LANE), lambda i: (i, 0, 0))],
        out_shape=[jax.ShapeDtypeStruct((b, e, t), jnp.int32), jax.ShapeDtypeStruct((b, e, LANE), jnp.int32)],
        compiler_params=_cparams(("parallel",)),
        name=name,
    )(aff_t)


def _moe_kernel(cnt_ref, h_ref, slot_ref, gate_ref, wg_ref, wu_ref, wd_ref, mod_ref, x_hbm, acc_ref,
                xe_sc, gs_sc, ye_sc, x_sem, *, n_chunks, n_sb):
    bi = pl.program_id(0)
    ei = pl.program_id(1)
    n_e = pl.num_programs(1)
    x_copy = pltpu.make_async_copy(x_hbm.at[bi], acc_ref.at[0], x_sem)

    @pl.when(ei == 0)
    def _():
        x_copy.start()

    def wait_for_x():
        @pl.when(ei == 0)
        def _():
            x_copy.wait()

    g2 = mod_ref[0, 0, 5:6, :]
    cap = n_sb * SLOT_BLOCK
    cbase = (bi * n_e + ei) * 32
    counts = [cnt_ref[cbase + c] for c in range(n_chunks + 1)]
    window = 2 * SLOT_BLOCK

    def ffn():
        xe = xe_sc[0:cap, :].astype(BF16)
        f = wg_ref.shape[2]
        fb = 512
        y = None
        for f0 in range(0, f, fb):
            hg = jnp.dot(xe, wg_ref[0, :, f0:f0 + fb], preferred_element_type=F32)
            hu = jnp.dot(xe, wu_ref[0, :, f0:f0 + fb], preferred_element_type=F32)
            hid = (hg * jax.nn.sigmoid(hg) * hu).astype(BF16)
            part = jnp.dot(hid, wd_ref[0, f0:f0 + fb, :], preferred_element_type=F32)
            y = part if y is None else y + part
        return y

    def tok(c):
        return slice(c * TOK_CHUNK, (c + 1) * TOK_CHUNK)

    narrow = counts[1] - counts[0] <= SLOT_BLOCK
    for c in range(1, n_chunks):
        narrow = narrow & (counts[c + 1] - counts[c] <= SLOT_BLOCK)

    @pl.when(narrow)
    def _():
        xe_sc[...] = jnp.zeros_like(xe_sc)
        starts = [pl.multiple_of(jnp.minimum((counts[c] // SLOT_BLOCK) * SLOT_BLOCK, cap - SLOT_BLOCK),
                                 SLOT_BLOCK) for c in range(n_chunks)]
        srow = lax.broadcasted_iota(jnp.int32, (GATHER_WINDOW, TOK_CHUNK), 0)
        for c in range(n_chunks):
            g0 = pl.multiple_of((counts[c] // 8) * 8, 8)
            oh = (slot_ref[0, 0, c:c + 1, :] - g0) == srow
            xe_sc[pl.ds(g0, GATHER_WINDOW), :] += jnp.dot(
                oh.astype(BF16), h_ref[0, tok(c), :], preferred_element_type=F32)
        ye_sc[0:cap, :] = ffn().astype(BF16)
        ye_sc[cap:cap + SLOT_BLOCK, :] = jnp.zeros((SLOT_BLOCK, ye_sc.shape[1]), BF16)
        wait_for_x()
        pad = jnp.zeros((LANE - n_chunks, TOK_CHUNK), F32)
        slot_t = jnp.concatenate([slot_ref[0, 0].astype(F32), pad], axis=0).T
        gate_t = jnp.concatenate([gate_ref[0, 0], pad], axis=0).T
        scol = lax.broadcasted_iota(jnp.int32, (TOK_CHUNK, window), 1).astype(F32)
        for c in range(n_chunks):
            oh_t = (slot_t[:, c:c + 1] - starts[c].astype(F32)) == scol
            contrib = jnp.dot(oh_t.astype(BF16), ye_sc[pl.ds(starts[c], window), :],
                              preferred_element_type=F32)
            acc_ref[0, tok(c), :] += (gate_t[:, c:c + 1] * g2) * contrib

    @pl.when(jnp.logical_not(narrow))
    def _():
        xe_sc[...] = jnp.zeros_like(xe_sc)
        gs_sc[...] = jnp.zeros_like(gs_sc)
        srow = lax.broadcasted_iota(jnp.int32, (SLOT_BLOCK, TOK_CHUNK), 0)

        def overlap(c, sb):
            return (counts[c] < (sb + 1) * SLOT_BLOCK) & (counts[c + 1] > sb * SLOT_BLOCK)

        def onehot(c, sb):
            return slot_ref[0, 0, c:c + 1, :] == (srow + sb * SLOT_BLOCK)

        for c in range(n_chunks):
            for sb in range(n_sb):
                @pl.when(overlap(c, sb))
                def _(c=c, sb=sb):
                    oh = onehot(c, sb)
                    rows = slice(sb * SLOT_BLOCK, (sb + 1) * SLOT_BLOCK)
                    xe_sc[rows] += jnp.dot(oh.astype(BF16), h_ref[0, tok(c), :], preferred_element_type=F32)
                    g = jnp.where(oh, gate_ref[0, 0, c:c + 1, :], 0.0)
                    gs_sc[rows] += jnp.sum(g, axis=1, keepdims=True)

        ye_sc[0:cap, :] = (ffn() * gs_sc[...]).astype(BF16)
        wait_for_x()

        for c in range(n_chunks):
            for sb in range(n_sb):
                @pl.when(overlap(c, sb))
                def _(c=c, sb=sb):
                    oh_t = onehot(c, sb).astype(F32).T.astype(BF16)
                    rows = slice(sb * SLOT_BLOCK, (sb + 1) * SLOT_BLOCK)
                    acc_ref[0, tok(c), :] += g2 * jnp.dot(oh_t, ye_sc[rows], preferred_element_type=F32)


def _moe(h2, slot, gate, cnt, wg, wu, wd, xs, modv, kind, cap, name):
    b, t, d = h2.shape
    e = slot.shape[1]
    f = wg.shape[2]
    n_chunks = t // TOK_CHUNK
    n_sb = cap // SLOT_BLOCK
    slot4 = slot.reshape(b, e, n_chunks, TOK_CHUNK)
    gate4 = gate.reshape(b, e, n_chunks, TOK_CHUNK)
    cnt_flat = cnt[:, :, :32].reshape(-1)
    grid_spec = pltpu.PrefetchScalarGridSpec(
        num_scalar_prefetch=1,
        grid=(b, e),
        in_specs=[pl.BlockSpec((1, t, d), lambda i, j, s: (i, 0, 0), pipeline_mode=pl.Buffered(1)),
                  pl.BlockSpec((1, 1, n_chunks, TOK_CHUNK), lambda i, j, s: (i, j, 0, 0)),
                  pl.BlockSpec((1, 1, n_chunks, TOK_CHUNK), lambda i, j, s: (i, j, 0, 0)),
                  pl.BlockSpec((1, d, f), lambda i, j, s: (j, 0, 0)),
                  pl.BlockSpec((1, d, f), lambda i, j, s: (j, 0, 0)),
                  pl.BlockSpec((1, f, d), lambda i, j, s: (j, 0, 0)),
                  pl.BlockSpec((1, 1, 8, d), lambda i, j, s: (i, kind, 0, 0)),
                  pl.BlockSpec(memory_space=pl.ANY)],
        out_specs=pl.BlockSpec((1, t, d), lambda i, j, s: (i, 0, 0), pipeline_mode=pl.Buffered(1)),
        scratch_shapes=[pltpu.VMEM((cap + 2 * SLOT_BLOCK, d), F32), pltpu.VMEM((cap, 1), F32),
                        pltpu.VMEM((cap + SLOT_BLOCK, d), BF16), pltpu.SemaphoreType.DMA(())],
    )
    return pl.pallas_call(
        functools.partial(_moe_kernel, n_chunks=n_chunks, n_sb=n_sb),
        grid_spec=grid_spec,
        out_shape=jax.ShapeDtypeStruct((b, t, d), F32),
        compiler_params=_cparams(("parallel", "arbitrary")),
        name=name,
    )(cnt_flat, h2, slot4, gate4, wg, wu, wd, modv, xs)


def _pad_lanes(v, width):
    return jnp.pad(v, (0, width - v.shape[0]))


def kernel(x, c, ctx, c_ctx, w_ada, b_ada, norm1_g, norm2_g, w_in, a_q_norm, a_k_norm, b_cq_norm, b_ckv_norm, w_uq, w_ukv, b_qn_norm, b_kn_norm, b_qr_norm, b_kr_norm, c_q_norm, c_k_norm, c_sink, w_out, w_router, w_e_gate, w_e_up, w_e_down):
    b, t, d = x.shape
    n_ctx = ctx.shape[1]
    depth = w_ada.shape[0]
    assert n_ctx == ROW_TILE and t % KEY_CHUNK == 0 and t % GRID_W == 0
    cap = max(1, CAP_FACTOR * t // N_EXPERTS)
    cap_c = max(1, CAP_FACTOR * n_ctx // N_EXPERTS)
    assert cap % SLOT_BLOCK == 0 and (b * cap_c) % SLOT_BLOCK == 0

    rows = ((b + 1 + 7) // 8) * 8
    cc = jnp.zeros((rows, d), F32).at[:b].set(c).at[b].set(c_ctx)
    mod = _modulation(cc, w_ada, b_ada)

    rope = _rope_tables(t)
    mm, mx, mbb = (jnp.asarray(m, BF16) for m in _seg_matrices())
    in_perm = _in_perm()
    uq_perm = _uq_perm()
    ukv_pk, ukv_pv = _ukv_perms()
    oa_rows = np.asarray(_pair_cols(0, A_HEADS))
    oc_rows = np.asarray(_pair_cols(0, C_HEADS)) + 768
    sa = HEAD_DIM ** -0.5
    sb = (B_NOPE + B_ROPE) ** -0.5
    z32, z64 = jnp.zeros((32,), F32), jnp.zeros((64,), F32)

    for l in range(depth):
        last = l == depth - 1
        m6 = mod[l].reshape(rows, 6, d)
        m8 = jnp.pad(m6, ((0, 0), (0, 2), (0, 0)))
        modv = jnp.stack([jnp.broadcast_to(m8[b][None], (b, 8, d)), m8[:b]], axis=1)

        w_in_p = _take_cols(w_in[l], in_perm).astype(BF16)
        wuq = jnp.pad(_take_cols(w_uq[l], uq_perm), ((0, 256 - B_Q_RANK), (0, 0))).astype(BF16)
        wukvk = _take_cols(w_ukv[l], ukv_pk).astype(BF16)
        wukvv = jnp.take(w_ukv[l], jnp.asarray(ukv_pv, jnp.int32), axis=1).astype(BF16)
        gains = jnp.stack([
            jnp.tile(a_q_norm[l], 2) * (sa * LOG2E), jnp.tile(a_k_norm[l], 2),
            jnp.tile(c_q_norm[l], 2) * sa, jnp.tile(c_k_norm[l], 2),
            jnp.concatenate([b_qn_norm[l] * (sb * LOG2E), b_qr_norm[l] * (sb * LOG2E), z32]),
            jnp.concatenate([b_kn_norm[l], z64]),
            jnp.concatenate([z64, b_kr_norm[l], z32]),
            jnp.zeros((LANE,), F32)])
        gcq = _pad_lanes(b_cq_norm[l], 256)[None]
        gckv = b_ckv_norm[l][None]
        g1 = norm1_g[l][None]
        g2n = norm2_g[l][None]
        mix_rows = np.concatenate([oa_rows, np.arange(384, 768), oc_rows])
        w_mix = jnp.take(w_out[l], jnp.asarray(mix_rows, jnp.int32), axis=0).astype(BF16)
        wr_t = jnp.pad(w_router[l], ((0, 0), (0, LANE - N_EXPERTS)))
        sink = jnp.zeros((8, LANE), F32).at[0, :C_HEADS].set(c_sink[l])
        wg = w_e_gate[l].astype(BF16)
        wu = w_e_up[l].astype(BF16)
        wd = w_e_down[l].astype(BF16)

        shared = (g1, w_in_p, gains, gcq, gckv, wuq, wukvk, wukvv, mm, mx, mbb)
        qa, ka, va, qb, kb, vb, qc, kc, vc = _inproj(x, modv, 1, *shared, rope)
        qac, kac, vac, qbc, kbc, vbc, qcc, kcc, vcc = _inproj(ctx, modv, 0, *shared, None)

        o_a = _dense_attention(qa, kac, vac, ka, va, A_HEADS, _A_MAPS, True, "attn_a")
        o_b = _dense_attention(qb, kbc, vbc, kb, vb, B_HEADS, _B_MAPS, False, "attn_b")
        o_c = _window_attention(qc, kcc, vcc, kc, vc, sink, "attn_c")
        x_mid, h2, aff = _outproj(o_a, o_b, o_c, x, modv, 1, g2n, w_mix, wr_t, Q_TILE, "outproj")
        slot, cnt = _topk(aff, cap, 0, "topk")
        x = _moe(h2, slot, aff, cnt, wg, wu, wd, x_mid, modv, 1, cap, "moe")

        if not last:
            o_ac = _dense_attention_ctx(qac, kac, vac, A_HEADS, _A_MAPS, True, "attn_a_ctx")
            o_bc = _dense_attention_ctx(qbc, kbc, vbc, B_HEADS, _B_MAPS, False, "attn_b_ctx")
            o_cc = _window_attention(qcc, kcc, vcc, None, None, sink, "attn_c_ctx")
            flat = lambda o: o.reshape(b, -1, LANE)
            c_mid, hc2, aff_c = _outproj(flat(o_ac), flat(o_bc), o_cc, ctx, modv, 0, g2n, w_mix, wr_t, 0,
                                         "outproj_ctx")
            slot_c, cnt_c = _topk(aff_c, cap_c, cap_c, "topk_ctx")
            e = N_EXPERTS
            slot_f = jnp.transpose(slot_c, (1, 0, 2)).reshape(1, e, b * n_ctx)
            gate_f = jnp.transpose(aff_c, (1, 0, 2)).reshape(1, e, b * n_ctx)
            starts = jnp.transpose(cnt_c[:, :, 0], (1, 0))
            cnt_f = jnp.concatenate([starts, cnt_c[b - 1, :, 1:2],
                                     jnp.zeros((e, LANE - b - 1), jnp.int32)], axis=1)[None]
            ctx = _moe(hc2.reshape(1, b * n_ctx, d), slot_f, gate_f, cnt_f, wg, wu, wd,
                       c_mid.reshape(1, b * n_ctx, d), modv, 0, b * cap_c, "moe_ctx").reshape(b, n_ctx, d)
    return x
```

```python
import functools
import math

import numpy as np
import jax
import jax.numpy as jnp
from jax import lax
from jax.experimental import pallas as pl
from jax.experimental.pallas import tpu as pltpu

F32 = jnp.float32
BF16 = jnp.bfloat16

GRID_W = 64
ROPE_THETA = 10000.0
EPS = 1e-6
NEG_INF = -1e30
HEAD_DIM = 64
A_HEADS, A_KV_HEADS = 6, 2
B_HEADS, B_Q_RANK, B_KV_RANK, B_NOPE, B_ROPE, B_V = 6, 192, 128, 64, 32, 64
C_HEADS, C_KV_HEADS = 4, 2
N_EXPERTS = 16
CAP_FACTOR = 2
WINDOW = 128

LANE = 128
ROW_TILE = 256
PROJ_TILE = 512
Q_TILE = 512
KEY_CHUNK = 512
LOG2E = 1.4426950408889634
SLOT_BLOCK = 128
TOK_CHUNK = 256
GATHER_WINDOW = 144
VMEM_LIMIT = 56 * 1024 * 1024

_QA, _KA, _VA = 0, 384, 512
_QC, _KC, _VC = 640, 896, 1024
_BCQ, _BCKV, _BKR = 1152, 1408, 1536
IN_PAD = 1664
_ORIG = dict(aq=0, ak=384, av=512, bcq=640, bckv=832, bkr=960, cq=992, ck=1248, cv=1376)


def _cparams(sem, vmem=VMEM_LIMIT):
    return pltpu.CompilerParams(dimension_semantics=sem, vmem_limit_bytes=vmem)


def _pair_cols(base, n_heads):
    half = n_heads // 2
    cols = []
    for i in range(half):
        cols += list(range(base + i * 64, base + (i + 1) * 64))
        cols += list(range(base + (half + i) * 64, base + (half + i + 1) * 64))
    return cols


def _in_perm():
    perm = -np.ones((IN_PAD,), np.int64)
    perm[_QA:_QA + 384] = _pair_cols(_ORIG["aq"], A_HEADS)
    perm[_KA:_KA + 128] = np.arange(_ORIG["ak"], _ORIG["ak"] + 128)
    perm[_VA:_VA + 128] = np.arange(_ORIG["av"], _ORIG["av"] + 128)
    perm[_QC:_QC + 256] = _pair_cols(_ORIG["cq"], C_HEADS)
    perm[_KC:_KC + 128] = np.arange(_ORIG["ck"], _ORIG["ck"] + 128)
    perm[_VC:_VC + 128] = np.arange(_ORIG["cv"], _ORIG["cv"] + 128)
    perm[_BCQ:_BCQ + B_Q_RANK] = np.arange(_ORIG["bcq"], _ORIG["bcq"] + B_Q_RANK)
    perm[_BCKV:_BCKV + 128] = np.arange(_ORIG["bckv"], _ORIG["bckv"] + 128)
    perm[_BKR + 64:_BKR + 96] = np.arange(_ORIG["bkr"], _ORIG["bkr"] + 32)
    return perm


def _take_cols(w, perm):
    safe = np.where(perm >= 0, perm, 0)
    out = jnp.take(w, jnp.asarray(safe, jnp.int32), axis=1)
    return jnp.where(jnp.asarray(perm >= 0)[None, :], out, 0.0)


def _uq_perm():
    perm = -np.ones((B_HEADS * LANE,), np.int64)
    for h in range(B_HEADS):
        perm[h * LANE:h * LANE + 96] = np.arange(h * 96, h * 96 + 96)
    return perm


def _ukv_perms():
    pk = -np.ones((B_HEADS * LANE,), np.int64)
    pv = np.zeros((B_HEADS * 64,), np.int64)
    for h in range(B_HEADS):
        pk[h * LANE:h * LANE + 64] = np.arange(h * 128, h * 128 + 64)
        pv[h * 64:(h + 1) * 64] = np.arange(h * 128 + 64, h * 128 + 128)
    return pk, pv


def _seg_matrices():
    i = np.arange(LANE)
    m64 = (i[:, None] // 64 == i[None, :] // 64).astype(np.float32) / 64.0
    mb = np.zeros((LANE, LANE), np.float32)
    mb[:64, :64] = 1.0 / 64.0
    mb[64:96, 64:96] = 1.0 / 32.0
    def pair(a, b):
        out = np.zeros((2 * LANE, 2 * LANE), np.float32)
        out[:LANE, :LANE] = a
        out[LANE:, LANE:] = b
        return out

    return pair(m64, m64), pair(m64, mb), pair(mb, mb)


def _rope_tables(length):
    rows = length // GRID_W
    t = np.arange(rows * GRID_W)
    row = np.repeat(np.arange(rows), GRID_W).astype(np.float32)
    col = (t % GRID_W).astype(np.float32)

    def tabs(dim):
        axis_dim = dim // 2
        inv = (np.float32(ROPE_THETA) ** (-np.arange(0, axis_dim, 2, dtype=np.float32) / axis_dim)).astype(np.float32)
        ar = row[:, None] * inv[None, :]
        ac = col[:, None] * inv[None, :]
        cos = np.concatenate([np.cos(ar), np.cos(ar), np.cos(ac), np.cos(ac)], axis=-1)
        sin = np.concatenate([-np.sin(ar), np.sin(ar), -np.sin(ac), np.sin(ac)], axis=-1)
        return cos.astype(np.float32), sin.astype(np.float32)

    c64, s64 = tabs(HEAD_DIM)
    cb, sb = tabs(B_ROPE)
    n = c64.shape[0]
    ones64, zeros64 = np.ones((n, 64), np.float32), np.zeros((n, 64), np.float32)
    ones32, zeros32 = np.ones((n, 32), np.float32), np.zeros((n, 32), np.float32)
    return jnp.asarray(np.concatenate([c64, c64, s64, s64,
                                       ones64, cb, ones32, zeros64, sb, zeros32], axis=-1))


def _mod_kernel(c_ref, w_ref, b_ref, o_ref):
    cv = c_ref[...]
    a = (cv * jax.nn.sigmoid(cv)).astype(BF16)
    o_ref[0] = jnp.dot(a, w_ref[0].astype(BF16), preferred_element_type=F32) + b_ref[0]


def _modulation(cc, w_ada, b_ada):
    n_layers, d, n6 = w_ada.shape
    rows = cc.shape[0]
    tn = 1536
    return pl.pallas_call(
        _mod_kernel,
        grid=(n_layers, n6 // tn),
        in_specs=[pl.BlockSpec((rows, d), lambda l, j: (0, 0)),
                  pl.BlockSpec((1, d, tn), lambda l, j: (l, 0, j)),
                  pl.BlockSpec((1, 1, tn), lambda l, j: (l, 0, j))],
        out_specs=pl.BlockSpec((1, rows, tn), lambda l, j: (l, 0, j)),
        out_shape=jax.ShapeDtypeStruct((n_layers, rows, n6), F32),
        compiler_params=_cparams(("parallel", "parallel")),
        name="adaln_mod",
    )(cc, w_ada, b_ada.reshape(n_layers, 1, n6))


def _seg_rsqrt(x2, m_ref):
    return lax.rsqrt(jnp.dot((x2 * x2).astype(BF16), m_ref[...], preferred_element_type=F32) + EPS)


def _rope(x, cos, sin, half):
    lane = lax.broadcasted_iota(jnp.int32, x.shape, 1)
    lo = (lane % (2 * half)) < half
    partner = jnp.where(lo, pltpu.roll(x, LANE - half, 1), pltpu.roll(x, half, 1))
    return x * cos + partner * sin


def _inproj_kernel(x_ref, mod_ref, g1_ref, w_ref, gains_ref, gcq_ref, gckv_ref, wuq_ref, wukvk_ref,
                   wukvv_ref, mm_ref, mx_ref, mbb_ref, *rest, use_rope):
    if use_rope:
        rope_ref = rest[0]
        outs = rest[1:]
    else:
        rope_ref = None
        outs = rest
    qa_ref, ka_ref, va_ref, qb_ref, kb_ref, vb_ref, qc_ref, kc_ref, vc_ref = outs

    x = x_ref[0]
    y = x * lax.rsqrt(jnp.mean(x * x, axis=-1, keepdims=True) + EPS) * g1_ref[...]
    h = y * (1.0 + mod_ref[0, 0, 1:2, :]) + mod_ref[0, 0, 0:1, :]
    p = jnp.dot(h.astype(BF16), w_ref[...], preferred_element_type=F32)

    def head_pair(x2, m_ref, gain_rows, rope_cols, half):
        r = _seg_rsqrt(x2, m_ref)
        out = []
        for i in range(2):
            sl = slice(i * LANE, (i + 1) * LANE)
            y = x2[:, sl] * r[:, sl] * gains_ref[gain_rows[i]:gain_rows[i] + 1, :]
            if rope_ref is not None and rope_cols[i] is not None:
                c0, s0 = rope_cols[i]
                y = _rope(y, rope_ref[:, c0:c0 + LANE], rope_ref[:, s0:s0 + LANE], half[i])
            out.append(y)
        return out

    r64 = (0, 128)
    rb = (256, 384)
    qa0, qa1 = head_pair(p[:, _QA:_QA + 256], mm_ref, (0, 0), (r64, r64), (16, 16))
    qa2, ka = head_pair(p[:, _QA + 256:_QA + 512], mm_ref, (0, 1), (r64, r64), (16, 16))
    for i, blk in enumerate((qa0, qa1, qa2)):
        qa_ref[0, i] = blk.astype(BF16)
    ka_ref[0, 0] = ka.astype(BF16)
    va_ref[0, 0] = p[:, _VA:_VA + LANE].astype(BF16)
    qc0, qc1 = head_pair(p[:, _QC:_QC + 256], mm_ref, (2, 2), (r64, r64), (16, 16))
    qc_ref[0, :, 0:LANE] = qc0.astype(BF16)
    qc_ref[0, :, LANE:2 * LANE] = qc1.astype(BF16)
    kc, kpe = head_pair(jnp.concatenate([p[:, _KC:_KC + LANE], p[:, _BKR:_BKR + LANE]], axis=1),
                        mx_ref, (3, 6), (r64, rb), (16, 8))
    kc_ref[0] = kc.astype(BF16)
    vc_ref[0] = p[:, _VC:_VC + LANE].astype(BF16)

    cq = p[:, _BCQ:_BCQ + 256]
    cq = cq * lax.rsqrt(jnp.sum(cq * cq, axis=-1, keepdims=True) * (1.0 / B_Q_RANK) + EPS) * gcq_ref[...]
    qb = jnp.dot(cq.astype(BF16), wuq_ref[...], preferred_element_type=F32)
    ckv = p[:, _BCKV:_BCKV + LANE]
    ckv = ckv * lax.rsqrt(jnp.mean(ckv * ckv, axis=-1, keepdims=True) + EPS) * gckv_ref[...]
    ckv = ckv.astype(BF16)
    kbn = jnp.dot(ckv, wukvk_ref[...], preferred_element_type=F32)
    vbv = jnp.dot(ckv, wukvv_ref[...], preferred_element_type=F32).astype(BF16)
    for i in range(B_HEADS // 2):
        vb_ref[0, i] = vbv[:, i * LANE:(i + 1) * LANE]
    for i in range(B_HEADS // 2):
        sl = slice(2 * i * LANE, (2 * i + 2) * LANE)
        q0, q1 = head_pair(qb[:, sl], mbb_ref, (4, 4), (rb, rb), (8, 8))
        k0, k1 = head_pair(kbn[:, sl], mbb_ref, (5, 5), (None, None), (8, 8))
        qb_ref[0, 2 * i] = q0.astype(BF16)
        qb_ref[0, 2 * i + 1] = q1.astype(BF16)
        kb_ref[0, 2 * i] = (k0 + kpe).astype(BF16)
        kb_ref[0, 2 * i + 1] = (k1 + kpe).astype(BF16)


def _inproj(xs, modv, kind, g1, w_in_p, gains, gcq, gckv, wuq, wukvk, wukvv, mm, mx, mbb, rope):
    b, n, d = xs.shape
    tm = min(PROJ_TILE, n)
    use_rope = rope is not None
    const = lambda *shape: pl.BlockSpec(shape, lambda i, j: (0,) * len(shape))
    in_specs = [pl.BlockSpec((1, tm, d), lambda i, j: (i, j, 0)),
                pl.BlockSpec((1, 1, 8, d), lambda i, j: (i, kind, 0, 0)),
                const(1, d), const(d, IN_PAD), const(8, LANE), const(1, 256), const(1, LANE),
                const(256, 768), const(LANE, 768), const(LANE, 384),
                const(256, 256), const(256, 256), const(256, 256)]
    args = [xs, modv, g1, w_in_p, gains, gcq, gckv, wuq, wukvk, wukvv, mm, mx, mbb]
    if use_rope:
        in_specs.append(pl.BlockSpec((tm, 512), lambda i, j: (j, 0)))
        args.append(rope)
    blocks = (3, 1, 1, B_HEADS, B_HEADS, B_HEADS // 2)
    widths = (256, 128, 128)
    out_specs = ([pl.BlockSpec((1, nb, tm, LANE), lambda i, j: (i, 0, j, 0)) for nb in blocks]
                 + [pl.BlockSpec((1, tm, w), lambda i, j: (i, j, 0)) for w in widths])
    out_shape = ([jax.ShapeDtypeStruct((b, nb, n, LANE), BF16) for nb in blocks]
                 + [jax.ShapeDtypeStruct((b, n, w), BF16) for w in widths])
    return pl.pallas_call(
        functools.partial(_inproj_kernel, use_rope=use_rope),
        grid=(b, n // tm),
        in_specs=in_specs,
        out_specs=out_specs,
        out_shape=out_shape,
        compiler_params=_cparams(("parallel", "parallel")),
        name="inproj_rope" if use_rope else "inproj_ctx",
    )(*args)


_A_MAPS = (lambda h: h % 3, lambda h: 0, lambda h: 0)
_B_MAPS = (lambda h: h, lambda h: h, lambda h: h // 2)


def _qk(q, k):
    return lax.dot_general(q, k, (((1,), (1,)), ((), ())), preferred_element_type=F32)


def _half_mask(q, half):
    lane = lax.broadcasted_iota(jnp.int32, q.shape, 1)
    return jnp.where((lane // 64) == half, q, jnp.zeros_like(q))


def _score_pass(q, chunks, s_w):
    mx = None
    for k_ref, _, st, sz, off in chunks:
        s = _qk(q, k_ref[0, 0, st:st + sz, :])
        s_w[:, off:off + sz] = s
        for t in range(sz // LANE):
            blk = s[:, t * LANE:(t + 1) * LANE]
            mx = blk if mx is None else jnp.maximum(mx, blk)
    return jnp.broadcast_to(jnp.max(mx, axis=1, keepdims=True), mx.shape)


def _value_pass(chunks, s_r, mb):
    acc = None
    for _, v_ref, st, sz, off in chunks:
        ps = [jnp.exp2(s_r[:, off + t * LANE:off + (t + 1) * LANE] - mb).astype(BF16)
              for t in range(sz // LANE)]
        v_ones = jnp.concatenate([v_ref[0, 0, st:st + sz, :], jnp.ones((sz, LANE), BF16)], axis=1)
        part = jnp.dot(jnp.concatenate(ps, axis=1), v_ones, preferred_element_type=F32)
        acc = part if acc is None else acc + part
    return (acc[:, :LANE] / acc[:, LANE:LANE + 1]).astype(BF16)


def _fused_passes(q, chunks, s_w, s_r, mb_r):
    mx = None
    acc = None
    for k_ref, v_ref, st, sz, off in chunks:
        s = _qk(q, k_ref[0, 0, st:st + sz, :])
        s_w[:, off:off + sz] = s
        for t in range(sz // LANE):
            blk = s[:, t * LANE:(t + 1) * LANE]
            mx = blk if mx is None else jnp.maximum(mx, blk)
        ps = [jnp.exp2(s_r[:, off + t * LANE:off + (t + 1) * LANE] - mb_r).astype(BF16)
              for t in range(sz // LANE)]
        v_ones = jnp.concatenate([v_ref[0, 0, st:st + sz, :], jnp.ones((sz, LANE), BF16)], axis=1)
        part = jnp.dot(jnp.concatenate(ps, axis=1), v_ones, preferred_element_type=F32)
        acc = part if acc is None else acc + part
    mb_new = jnp.broadcast_to(jnp.max(mx, axis=1, keepdims=True), mx.shape)
    return mb_new, (acc[:, :LANE] / acc[:, LANE:LANE + 1]).astype(BF16)


def _key_chunks(kc_ref, vc_ref, kl_ref, vl_ref):
    n_ctx = kc_ref.shape[2]
    chunks = [(kc_ref, vc_ref, 0, n_ctx, 0)]
    if kl_ref is not None:
        for c in range(kl_ref.shape[2] // KEY_CHUNK):
            chunks.append((kl_ref, vl_ref, c * KEY_CHUNK, KEY_CHUNK, n_ctx + c * KEY_CHUNK))
    return chunks


def _dense_kernel(q_ref, kc_ref, kl_ref, vcp_ref, vlp_ref, vcc_ref, vlc_ref, o_ref,
                  sa_sc, sb_sc, mba_sc, mbb_sc, *, mask_q, n_tiles, tiles_per_sample, tiles_per_head, n_sub):
    step = pl.program_id(0)
    tq = sa_sc.shape[0]

    @pl.when(step == 0)
    def _():
        sa_sc[...] = jnp.zeros(sa_sc.shape, F32)
        mba_sc[...] = jnp.zeros(mba_sc.shape, F32)

    if mask_q:
        first = jnp.minimum(n_sub * step, n_tiles - n_sub)
        half = ((first % tiles_per_sample) // tiles_per_head) // 3
    for k in range(n_sub):
        rows = slice(k * tq, (k + 1) * tq)
        q = q_ref[0, 0, rows, :]
        if mask_q:
            q = _half_mask(q, half)
        v_refs = (vcp_ref, vlp_ref) if k == 0 else (vcc_ref, vlc_ref)
        chunks = _key_chunks(kc_ref, v_refs[0], kl_ref, v_refs[1])
        s_w, s_r, mb_w, mb_r = ((sb_sc, sa_sc, mbb_sc, mba_sc) if k % 2 == 0
                                else (sa_sc, sb_sc, mba_sc, mbb_sc))
        mb_new, o_ref[rows, :] = _fused_passes(q, chunks, s_w, s_r, mb_r[...])
        mb_w[...] = mb_new


def _dense_attention(q, kc, vc, kl, vl, n_heads, maps, mask_q, name):
    b, _, nq, _ = q.shape
    qmap, kmap, vmap = maps
    n_ctx, n_lat = kc.shape[2], kl.shape[2]
    tq = Q_TILE
    tph = nq // tq
    n_sub = 4 if tph % 4 == 0 else 2
    assert tph % n_sub == 0
    tps = n_heads * tph
    n_tiles = b * tps
    first = lambda s: jnp.minimum(n_sub * s, n_tiles - n_sub)
    before = lambda s: jnp.maximum(n_sub * s - 1, 0)
    kv_blk = lambda n, fmap, tile: pl.BlockSpec(
        (1, 1, n, LANE), lambda s: (tile(s) // tps, fmap((tile(s) % tps) // tph), 0, 0))
    q_spec = pl.BlockSpec(
        (1, 1, n_sub * tq, LANE),
        lambda s: (first(s) // tps, qmap((first(s) % tps) // tph), (first(s) % tph) // n_sub, 0))
    return pl.pallas_call(
        functools.partial(_dense_kernel, mask_q=mask_q, n_tiles=n_tiles, tiles_per_sample=tps,
                          tiles_per_head=tph, n_sub=n_sub),
        grid=(n_tiles // n_sub + 1,),
        in_specs=[q_spec, kv_blk(n_ctx, kmap, first), kv_blk(n_lat, kmap, first),
                  kv_blk(n_ctx, vmap, before), kv_blk(n_lat, vmap, before),
                  kv_blk(n_ctx, vmap, first), kv_blk(n_lat, vmap, first)],
        out_specs=pl.BlockSpec((n_sub * tq, LANE), lambda s: (s, 0)),
        out_shape=jax.ShapeDtypeStruct(((n_tiles + n_sub) * tq, LANE), BF16),
        scratch_shapes=[pltpu.VMEM((tq, n_ctx + n_lat), F32), pltpu.VMEM((tq, n_ctx + n_lat), F32),
                        pltpu.VMEM((tq, LANE), F32), pltpu.VMEM((tq, LANE), F32)],
        compiler_params=_cparams(("arbitrary",)),
        name=name,
    )(q, kc, kl, vc, vl, vc, vl)


def _dense_ctx_kernel(q_ref, kc_ref, vc_ref, o_ref, s_sc, *, mask_q):
    q = q_ref[0, 0]
    if mask_q:
        q = _half_mask(q, pl.program_id(1) // 3)
    chunks = _key_chunks(kc_ref, vc_ref, None, None)
    mb = _score_pass(q, chunks, s_sc)
    o_ref[0, 0] = _value_pass(chunks, s_sc, mb)


def _dense_attention_ctx(q, kc, vc, n_heads, maps, mask_q, name):
    b, _, nq, _ = q.shape
    qmap, kmap, vmap = maps
    blk = lambda fmap: pl.BlockSpec((1, 1, nq, LANE), lambda i, h: (i, fmap(h), 0, 0))
    return pl.pallas_call(
        functools.partial(_dense_ctx_kernel, mask_q=mask_q),
        grid=(b, n_heads),
        in_specs=[blk(qmap), blk(kmap), blk(vmap)],
        out_specs=pl.BlockSpec((1, 1, nq, LANE), lambda i, h: (i, h, 0, 0)),
        out_shape=jax.ShapeDtypeStruct((b, n_heads, nq, LANE), BF16),
        scratch_shapes=[pltpu.VMEM((nq, kc.shape[2]), F32)],
        compiler_params=_cparams(("parallel", "parallel")),
        name=name,
    )(q, kc, vc)


def _window_kernel(q_ref, kc_ref, vc_ref, sink_ref, *rest, use_window):
    if use_window:
        kl_ref, vl_ref, o_ref = rest
    else:
        (o_ref,) = rest
    tq = ROW_TILE
    span = tq + 2 * WINDOW
    kcx = kc_ref[0]
    ones_c = jnp.ones((kcx.shape[0], LANE), BF16)
    v_aug = jnp.concatenate([vc_ref[0], ones_c], axis=1)
    lane = lax.broadcasted_iota(jnp.int32, (tq, LANE), 1)
    for ti in range(q_ref.shape[1] // tq):
        rows = slice(ti * tq, (ti + 1) * tq)
        if use_window:
            n_lat = kl_ref.shape[1]
            t0 = (pl.program_id(1) * (q_ref.shape[1] // tq) + ti) * tq
            start = pl.multiple_of(jnp.clip(t0 - WINDOW, 0, n_lat - span), LANE)
            qpos = t0 + lax.broadcasted_iota(jnp.int32, (tq, span), 0)
            kpos = start + lax.broadcasted_iota(jnp.int32, (tq, span), 1)
            valid = jnp.abs(qpos - kpos) <= WINDOW
            kw = kl_ref[0, pl.ds(start, span), :]
            v_all = jnp.concatenate(
                [v_aug, jnp.concatenate([vl_ref[0, pl.ds(start, span), :], jnp.ones((span, LANE), BF16)], axis=1)],
                axis=0)
        else:
            v_all = v_aug
        res = []
        for h in range(C_HEADS):
            half = h // (C_HEADS // C_KV_HEADS)
            q = q_ref[0, rows, (h % 2) * LANE:(h % 2 + 1) * LANE]
            q = jnp.where((lane < 64) if half == 0 else (lane >= 64), q, jnp.zeros_like(q))
            sink = sink_ref[0:1, h:h + 1]
            s = _qk(q, kcx)
            if use_window:
                s = jnp.concatenate([s, jnp.where(valid, _qk(q, kw), NEG_INF)], axis=1)
            mx = s[:, 0:LANE]
            for t in range(1, s.shape[1] // LANE):
                mx = jnp.maximum(mx, s[:, t * LANE:(t + 1) * LANE])
            m = jnp.maximum(jnp.max(mx, axis=1, keepdims=True), sink)
            mb = jnp.broadcast_to(m, (tq, LANE))
            p = jnp.concatenate([jnp.exp2(s[:, t * LANE:(t + 1) * LANE] - mb).astype(BF16)
                                 for t in range(s.shape[1] // LANE)], axis=1)
            o = jnp.dot(p, v_all, preferred_element_type=F32)
            res.append(o[:, :LANE] / (o[:, LANE:LANE + 1] + jnp.exp2(sink - m)))
        for ob in range(2):
            o_ref[0, rows, ob * LANE:(ob + 1) * LANE] = jnp.where(lane < 64, res[ob], res[2 + ob]).astype(BF16)


def _window_attention(q, kc, vc, kl, vl, sink, name):
    b, nq, qw = q.shape
    tq = min(2 * ROW_TILE, nq)
    n_ctx = kc.shape[1]
    use_window = kl is not None
    in_specs = [pl.BlockSpec((1, tq, qw), lambda i, j: (i, j, 0)),
                pl.BlockSpec((1, n_ctx, LANE), lambda i, j: (i, 0, 0)),
                pl.BlockSpec((1, n_ctx, LANE), lambda i, j: (i, 0, 0)),
                pl.BlockSpec((8, LANE), lambda i, j: (0, 0))]
    args = [q, kc, vc, sink]
    if use_window:
        n_lat = kl.shape[1]
        in_specs += [pl.BlockSpec((1, n_lat, LANE), lambda i, j: (i, 0, 0)),
                     pl.BlockSpec((1, n_lat, LANE), lambda i, j: (i, 0, 0))]
        args += [kl, vl]
    return pl.pallas_call(
        functools.partial(_window_kernel, use_window=use_window),
        grid=(b, nq // tq),
        in_specs=in_specs,
        out_specs=pl.BlockSpec((1, tq, qw), lambda i, j: (i, j, 0)),
        out_shape=jax.ShapeDtypeStruct((b, nq, qw), BF16),
        compiler_params=_cparams(("parallel", "parallel")),
        name=name,
    )(*args)


def _outproj_kernel(*refs):
    oa = refs[0:A_HEADS]
    ob = refs[A_HEADS:A_HEADS + B_HEADS]
    oc_ref, x_ref, mod_ref, g2n_ref, w_ref, wr_ref, xo_ref, h2_ref, aff_ref = refs[A_HEADS + B_HEADS:]
    tm = x_ref.shape[1]
    low = lax.broadcasted_iota(jnp.int32, (tm, LANE), 1) < 64
    parts = [jnp.where(low, oa[i][...], oa[3 + i][...]) for i in range(3)]
    parts += [jnp.where(low, ob[2 * i][...], ob[2 * i + 1][...]) for i in range(3)]
    parts.append(oc_ref[0])
    mix = jnp.dot(jnp.concatenate(parts, axis=1), w_ref[...], preferred_element_type=F32)
    x = x_ref[0] + mod_ref[0, 0, 2:3, :] * mix
    xo_ref[0] = x
    y = x * lax.rsqrt(jnp.mean(x * x, axis=-1, keepdims=True) + EPS) * g2n_ref[...]
    h2 = y * (1.0 + mod_ref[0, 0, 4:5, :]) + mod_ref[0, 0, 3:4, :]
    h2_ref[0] = h2.astype(BF16)
    wr = wr_ref[...]
    wr_hi = wr.astype(BF16)
    wr_lo = (wr - wr_hi.astype(F32)).astype(BF16)
    h2_hi = h2.astype(BF16)
    h2_lo = (h2 - h2_hi.astype(F32)).astype(BF16)
    logits = (jnp.dot(h2_hi, wr_hi, preferred_element_type=F32)
              + jnp.dot(h2_lo, wr_hi, preferred_element_type=F32)
              + jnp.dot(h2_hi, wr_lo, preferred_element_type=F32))
    logits = logits.T[0:aff_ref.shape[1], :]
    z = jnp.exp(logits - jnp.max(logits, axis=0, keepdims=True))
    aff_ref[0] = z / jnp.sum(z, axis=0, keepdims=True)


def _outproj(oa, ob, oc, xs, modv, kind, g2n, w_mix, wr_t, pad_rows, name):
    b, n, d = xs.shape
    tm = min(PROJ_TILE, n)
    e = N_EXPERTS
    const = lambda *shape: pl.BlockSpec(shape, lambda i, j: (0,) * len(shape))
    row = lambda w: pl.BlockSpec((1, tm, w), lambda i, j: (i, j, 0))
    n_heads = A_HEADS
    head = lambda h: pl.BlockSpec((tm, LANE), lambda i, j: ((pad_rows + (i * n_heads + h) * n) // tm + j, 0))
    return pl.pallas_call(
        _outproj_kernel,
        grid=(b, n // tm),
        in_specs=[head(h) for h in range(A_HEADS)] + [head(h) for h in range(B_HEADS)] + [
                  row(256), row(d),
                  pl.BlockSpec((1, 1, 8, d), lambda i, j: (i, kind, 0, 0)),
                  const(1, d), const(d, d), const(d, LANE)],
        out_specs=[row(d), row(d), pl.BlockSpec((1, e, tm), lambda i, j: (i, 0, j))],
        out_shape=[jax.ShapeDtypeStruct((b, n, d), F32), jax.ShapeDtypeStruct((b, n, d), BF16),
                   jax.ShapeDtypeStruct((b, e, n), F32)],
        compiler_params=_cparams(("parallel", "parallel")),
        name=name,
    )(*([oa] * A_HEADS), *([ob] * B_HEADS), oc, xs, modv, g2n, w_mix, wr_t)


def _prefix_exclusive(mask_f, tri):
    e, t = mask_f.shape
    ck = tri.shape[0]
    carry = jnp.zeros((e, 1), F32)
    parts, offs = [], [carry]
    for c in range(t // ck):
        blk = mask_f[:, c * ck:(c + 1) * ck]
        parts.append(jnp.dot(blk.astype(BF16), tri, preferred_element_type=F32) + carry)
        carry = carry + jnp.sum(blk, axis=1, keepdims=True)
        offs.append(carry)
    return jnp.concatenate(parts, axis=1) if len(parts) > 1 else parts[0], offs


def _topk_kernel(aff_ref, slot_ref, cnt_ref, *, cap, slot_stride):
    aff = aff_ref[0]
    e, t = aff.shape
    bits = pltpu.bitcast(aff, jnp.int32)
    kf = jnp.float32(cap)

    def search(i, lo):
        cand = lo | (jnp.int32(1) << (30 - i))
        n_ge = jnp.sum((bits >= cand).astype(F32), axis=1, keepdims=True)
        return jnp.where(n_ge >= kf, cand, lo)

    thr = lax.fori_loop(0, 31, search, jnp.zeros((e, 1), jnp.int32))
    gt = bits > thr
    eq = bits == thr
    need = kf - jnp.sum(gt.astype(F32), axis=1, keepdims=True)

    r = lax.broadcasted_iota(jnp.int32, (TOK_CHUNK, TOK_CHUNK), 0)
    c = lax.broadcasted_iota(jnp.int32, (TOK_CHUNK, TOK_CHUNK), 1)
    tri = (r < c).astype(BF16)
    tie_rank, _ = _prefix_exclusive(eq.astype(F32), tri)
    sel = gt | (eq & (tie_rank < need))
    slot, offs = _prefix_exclusive(sel.astype(F32), tri)
    base = pl.program_id(0) * slot_stride
    slot_ref[0] = jnp.where(sel, slot.astype(jnp.int32) + base, -1)
    lane = lax.broadcasted_iota(jnp.int32, (e, LANE), 1)
    cnt = jnp.zeros((e, LANE), jnp.int32)
    for ci, off in enumerate(offs):
        cnt = jnp.where(lane == ci, off.astype(jnp.int32) + base, cnt)
    cnt_ref[0] = cnt


def _topk(aff_t, cap, slot_stride, name):
    b, e, t = aff_t.shape
    return pl.pallas_call(
        functools.partial(_topk_kernel, cap=cap, slot_stride=slot_stride),
        grid=(b,),
        in_specs=[pl.BlockSpec((1, e, t), lambda i: (i, 0, 0))],
        out_specs=[pl.BlockSpec((1, e, t), lambda i: (i, 0, 0)),
                   pl.BlockSpec((1, e, LANE), lambda i: (i, 0, 0))],
        out_shape=[jax.ShapeDtypeStruct((b, e, t), jnp.int32), jax.ShapeDtypeStruct((b, e, LANE), jnp.int32)],
        compiler_params=_cparams(("parallel",)),
        name=name,
    )(aff_t)


def _moe_kernel(cnt_ref, h_ref, slot_ref, gate_ref, wg_ref, wu_ref, wd_ref, mod_ref, x_hbm, acc_ref,
                xe_sc, gs_sc, ye_sc, x_sem, *, n_chunks, n_sb):
    bi = pl.program_id(0)
    ei = pl.program_id(1)
    n_e = pl.num_programs(1)
    x_copy = pltpu.make_async_copy(x_hbm.at[bi], acc_ref.at[0], x_sem)

    @pl.when(ei == 0)
    def _():
        x_copy.start()

    def wait_for_x():
        @pl.when(ei == 0)
        def _():
            x_copy.wait()

    g2 = mod_ref[0, 0, 5:6, :]
    cap = n_sb * SLOT_BLOCK
    cbase = (bi * n_e + ei) * 32
    counts = [cnt_ref[cbase + c] for c in range(n_chunks + 1)]
    window = 2 * SLOT_BLOCK

    def ffn():
        xe = xe_sc[0:cap, :].astype(BF16)
        f = wg_ref.shape[2]
        fb = 512
        y = None
        for f0 in range(0, f, fb):
            hg = jnp.dot(xe, wg_ref[0, :, f0:f0 + fb], preferred_element_type=F32)
            hu = jnp.dot(xe, wu_ref[0, :, f0:f0 + fb], preferred_element_type=F32)
            hid = (hg * jax.nn.sigmoid(hg) * hu).astype(BF16)
            part = jnp.dot(hid, wd_ref[0, f0:f0 + fb, :], preferred_element_type=F32)
            y = part if y is None else y + part
        return y

    def tok(c):
        return slice(c * TOK_CHUNK, (c + 1) * TOK_CHUNK)

    narrow = counts[1] - counts[0] <= SLOT_BLOCK
    for c in range(1, n_chunks):
        narrow = narrow & (counts[c + 1] - counts[c] <= SLOT_BLOCK)

    @pl.when(narrow)
    def _():
        xe_sc[...] = jnp.zeros_like(xe_sc)
        starts = [pl.multiple_of(jnp.minimum((counts[c] // SLOT_BLOCK) * SLOT_BLOCK, cap - SLOT_BLOCK),
                                 SLOT_BLOCK) for c in range(n_chunks)]
        srow = lax.broadcasted_iota(jnp.int32, (GATHER_WINDOW, TOK_CHUNK), 0)
        for c in range(n_chunks):
            g0 = pl.multiple_of((counts[c] // 8) * 8, 8)
            oh = (slot_ref[0, 0, c:c + 1, :] - g0) == srow
            xe_sc[pl.ds(g0, GATHER_WINDOW), :] += jnp.dot(
                oh.astype(BF16), h_ref[0, tok(c), :], preferred_element_type=F32)
        ye_sc[0:cap, :] = ffn().astype(BF16)
        ye_sc[cap:cap + SLOT_BLOCK, :] = jnp.zeros((SLOT_BLOCK, ye_sc.shape[1]), BF16)
        wait_for_x()
        pad = jnp.zeros((LANE - n_chunks, TOK_CHUNK), F32)
        slot_t = jnp.concatenate([slot_ref[0, 0].astype(F32), pad], axis=0).T
        gate_t = jnp.concatenate([gate_ref[0, 0], pad], axis=0).T
        scol = lax.broadcasted_iota(jnp.int32, (TOK_CHUNK, window), 1).astype(F32)
        for c in range(n_chunks):
            oh_t = (slot_t[:, c:c + 1] - starts[c].astype(F32)) == scol
            contrib = jnp.dot(oh_t.astype(BF16), ye_sc[pl.ds(starts[c], window), :],
                              preferred_element_type=F32)
            acc_ref[0, tok(c), :] += (gate_t[:, c:c + 1] * g2) * contrib

    @pl.when(jnp.logical_not(narrow))
    def _():
        xe_sc[...] = jnp.zeros_like(xe_sc)
        gs_sc[...] = jnp.zeros_like(gs_sc)
        srow = lax.broadcasted_iota(jnp.int32, (SLOT_BLOCK, TOK_CHUNK), 0)

        def overlap(c, sb):
            return (counts[c] < (sb + 1) * SLOT_BLOCK) & (counts[c + 1] > sb * SLOT_BLOCK)

        def onehot(c, sb):
            return slot_ref[0, 0, c:c + 1, :] == (srow + sb * SLOT_BLOCK)

        for c in range(n_chunks):
            for sb in range(n_sb):
                @pl.when(overlap(c, sb))
                def _(c=c, sb=sb):
                    oh = onehot(c, sb)
                    rows = slice(sb * SLOT_BLOCK, (sb + 1) * SLOT_BLOCK)
                    xe_sc[rows] += jnp.dot(oh.astype(BF16), h_ref[0, tok(c), :], preferred_element_type=F32)
                    g = jnp.where(oh, gate_ref[0, 0, c:c + 1, :], 0.0)
                    gs_sc[rows] += jnp.sum(g, axis=1, keepdims=True)

        ye_sc[0:cap, :] = (ffn() * gs_sc[...]).astype(BF16)
        wait_for_x()

        for c in range(n_chunks):
            for sb in range(n_sb):
                @pl.when(overlap(c, sb))
                def _(c=c, sb=sb):
                    oh_t = onehot(c, sb).astype(F32).T.astype(BF16)
                    rows = slice(sb * SLOT_BLOCK, (sb + 1) * SLOT_BLOCK)
                    acc_ref[0, tok(c), :] += g2 * jnp.dot(oh_t, ye_sc[rows], preferred_element_type=F32)


def _moe(h2, slot, gate, cnt, wg, wu, wd, xs, modv, kind, cap, name):
    b, t, d = h2.shape
    e = slot.shape[1]
    f = wg.shape[2]
    n_chunks = t // TOK_CHUNK
    n_sb = cap // SLOT_BLOCK
    slot4 = slot.reshape(b, e, n_chunks, TOK_CHUNK)
    gate4 = gate.reshape(b, e, n_chunks, TOK_CHUNK)
    cnt_flat = cnt[:, :, :32].reshape(-1)
    grid_spec = pltpu.PrefetchScalarGridSpec(
        num_scalar_prefetch=1,
        grid=(b, e),
        in_specs=[pl.BlockSpec((1, t, d), lambda i, j, s: (i, 0, 0), pipeline_mode=pl.Buffered(1)),
                  pl.BlockSpec((1, 1, n_chunks, TOK_CHUNK), lambda i, j, s: (i, j, 0, 0)),
                  pl.BlockSpec((1, 1, n_chunks, TOK_CHUNK), lambda i, j, s: (i, j, 0, 0)),
                  pl.BlockSpec((1, d, f), lambda i, j, s: (j, 0, 0)),
                  pl.BlockSpec((1, d, f), lambda i, j, s: (j, 0, 0)),
                  pl.BlockSpec((1, f, d), lambda i, j, s: (j, 0, 0)),
                  pl.BlockSpec((1, 1, 8, d), lambda i, j, s: (i, kind, 0, 0)),
                  pl.BlockSpec(memory_space=pl.ANY)],
        out_specs=pl.BlockSpec((1, t, d), lambda i, j, s: (i, 0, 0), pipeline_mode=pl.Buffered(1)),
        scratch_shapes=[pltpu.VMEM((cap + 2 * SLOT_BLOCK, d), F32), pltpu.VMEM((cap, 1), F32),
                        pltpu.VMEM((cap + SLOT_BLOCK, d), BF16), pltpu.SemaphoreType.DMA(())],
    )
    return pl.pallas_call(
        functools.partial(_moe_kernel, n_chunks=n_chunks, n_sb=n_sb),
        grid_spec=grid_spec,
        out_shape=jax.ShapeDtypeStruct((b, t, d), F32),
        compiler_params=_cparams(("parallel", "arbitrary")),
        name=name,
    )(cnt_flat, h2, slot4, gate4, wg, wu, wd, modv, xs)


def _cast_kernel(w_ref, o_ref):
    o_ref[...] = w_ref[0].astype(BF16)


def _layer_bf16(w, layer):
    _, e, r, c = w.shape
    return pl.pallas_call(
        _cast_kernel,
        grid=(e,),
        in_specs=[pl.BlockSpec((1, 1, r, c), lambda i: (layer, i, 0, 0))],
        out_specs=pl.BlockSpec((1, r, c), lambda i: (i, 0, 0)),
        out_shape=jax.ShapeDtypeStruct((e, r, c), BF16),
        compiler_params=_cparams(("parallel",)),
        name="expert_weights_bf16",
    )(w)


def _pad_lanes(v, width):
    return jnp.pad(v, (0, width - v.shape[0]))


def kernel(x, c, ctx, c_ctx, w_ada, b_ada, norm1_g, norm2_g, w_in, a_q_norm, a_k_norm, b_cq_norm, b_ckv_norm, w_uq, w_ukv, b_qn_norm, b_kn_norm, b_qr_norm, b_kr_norm, c_q_norm, c_k_norm, c_sink, w_out, w_router, w_e_gate, w_e_up, w_e_down):
    b, t, d = x.shape
    n_ctx = ctx.shape[1]
    depth = w_ada.shape[0]
    assert n_ctx == ROW_TILE and t % KEY_CHUNK == 0 and t % GRID_W == 0
    cap = max(1, CAP_FACTOR * t // N_EXPERTS)
    cap_c = max(1, CAP_FACTOR * n_ctx // N_EXPERTS)
    assert cap % SLOT_BLOCK == 0 and (b * cap_c) % SLOT_BLOCK == 0

    rows = ((b + 1 + 7) // 8) * 8
    cc = jnp.zeros((rows, d), F32).at[:b].set(c).at[b].set(c_ctx)
    mod = _modulation(cc, w_ada, b_ada)

    rope = _rope_tables(t)
    mm, mx, mbb = (jnp.asarray(m, BF16) for m in _seg_matrices())
    in_perm = _in_perm()
    uq_perm = _uq_perm()
    ukv_pk, ukv_pv = _ukv_perms()
    oa_rows = np.asarray(_pair_cols(0, A_HEADS))
    oc_rows = np.asarray(_pair_cols(0, C_HEADS)) + 768
    sa = HEAD_DIM ** -0.5
    sb = (B_NOPE + B_ROPE) ** -0.5
    z32, z64 = jnp.zeros((32,), F32), jnp.zeros((64,), F32)

    for l in range(depth):
        last = l == depth - 1
        m6 = mod[l].reshape(rows, 6, d)
        m8 = jnp.pad(m6, ((0, 0), (0, 2), (0, 0)))
        modv = jnp.stack([jnp.broadcast_to(m8[b][None], (b, 8, d)), m8[:b]], axis=1)

        w_in_p = _take_cols(w_in[l], in_perm).astype(BF16)
        wuq = jnp.pad(_take_cols(w_uq[l], uq_perm), ((0, 256 - B_Q_RANK), (0, 0))).astype(BF16)
        wukvk = _take_cols(w_ukv[l], ukv_pk).astype(BF16)
        wukvv = jnp.take(w_ukv[l], jnp.asarray(ukv_pv, jnp.int32), axis=1).astype(BF16)
        gains = jnp.stack([
            jnp.tile(a_q_norm[l], 2) * (sa * LOG2E), jnp.tile(a_k_norm[l], 2),
            jnp.tile(c_q_norm[l], 2) * (sa * LOG2E), jnp.tile(c_k_norm[l], 2),
            jnp.concatenate([b_qn_norm[l] * (sb * LOG2E), b_qr_norm[l] * (sb * LOG2E), z32]),
            jnp.concatenate([b_kn_norm[l], z64]),
            jnp.concatenate([z64, b_kr_norm[l], z32]),
            jnp.zeros((LANE,), F32)])
        gcq = _pad_lanes(b_cq_norm[l], 256)[None]
        gckv = b_ckv_norm[l][None]
        g1 = norm1_g[l][None]
        g2n = norm2_g[l][None]
        mix_rows = np.concatenate([oa_rows, np.arange(384, 768), oc_rows])
        w_mix = jnp.take(w_out[l], jnp.asarray(mix_rows, jnp.int32), axis=0).astype(BF16)
        wr_t = jnp.pad(w_router[l], ((0, 0), (0, LANE - N_EXPERTS)))
        sink = jnp.zeros((8, LANE), F32).at[0, :C_HEADS].set(c_sink[l] * LOG2E)
        wg = _layer_bf16(w_e_gate, l)
        wu = _layer_bf16(w_e_up, l)
        wd = _layer_bf16(w_e_down, l)

        shared = (g1, w_in_p, gains, gcq, gckv, wuq, wukvk, wukvv, mm, mx, mbb)
        qa, ka, va, qb, kb, vb, qc, kc, vc = _inproj(x, modv, 1, *shared, rope)
        qac, kac, vac, qbc, kbc, vbc, qcc, kcc, vcc = _inproj(ctx, modv, 0, *shared, None)

        o_a = _dense_attention(qa, kac, vac, ka, va, A_HEADS, _A_MAPS, True, "attn_a")
        o_b = _dense_attention(qb, kbc, vbc, kb, vb, B_HEADS, _B_MAPS, False, "attn_b")
        o_c = _window_attention(qc, kcc, vcc, kc, vc, sink, "attn_c")
        x_mid, h2, aff = _outproj(o_a, o_b, o_c, x, modv, 1, g2n, w_mix, wr_t, Q_TILE, "outproj")
        slot, cnt = _topk(aff, cap, 0, "topk")
        x = _moe(h2, slot, aff, cnt, wg, wu, wd, x_mid, modv, 1, cap, "moe")

        if not last:
            o_ac = _dense_attention_ctx(qac, kac, vac, A_HEADS, _A_MAPS, True, "attn_a_ctx")
            o_bc = _dense_attention_ctx(qbc, kbc, vbc, B_HEADS, _B_MAPS, False, "attn_b_ctx")
            o_cc = _window_attention(qcc, kcc, vcc, None, None, sink, "attn_c_ctx")
            flat = lambda o: o.reshape(-1, LANE)
            c_mid, hc2, aff_c = _outproj(flat(o_ac), flat(o_bc), o_cc, ctx, modv, 0, g2n, w_mix, wr_t, 0,
                                         "outproj_ctx")
            slot_c, cnt_c = _topk(aff_c, cap_c, cap_c, "topk_ctx")
            e = N_EXPERTS
            slot_f = jnp.transpose(slot_c, (1, 0, 2)).reshape(1, e, b * n_ctx)
            gate_f = jnp.transpose(aff_c, (1, 0, 2)).reshape(1, e, b * n_ctx)
            starts = jnp.transpose(cnt_c[:, :, 0], (1, 0))
            cnt_f = jnp.concatenate([starts, cnt_c[b - 1, :, 1:2],
                                     jnp.zeros((e, LANE - b - 1), jnp.int32)], axis=1)[None]
            ctx = _moe(hc2.reshape(1, b * n_ctx, d), slot_f, gate_f, cnt_f, wg, wu, wd,
                       c_mid.reshape(1, b * n_ctx, d), modv, 0, b * cap_c, "moe_ctx").reshape(b, n_ctx, d)
    return x
```

```python
import functools
import math

import numpy as np
import jax
import jax.numpy as jnp
from jax import lax
from jax.experimental import pallas as pl
from jax.experimental.pallas import tpu as pltpu

F32 = jnp.float32
BF16 = jnp.bfloat16

GRID_W = 64
ROPE_THETA = 10000.0
EPS = 1e-6
NEG_INF = -1e30
HEAD_DIM = 64
A_HEADS, A_KV_HEADS = 6, 2
B_HEADS, B_Q_RANK, B_KV_RANK, B_NOPE, B_ROPE, B_V = 6, 192, 128, 64, 32, 64
C_HEADS, C_KV_HEADS = 4, 2
N_EXPERTS = 16
CAP_FACTOR = 2
WINDOW = 128

LANE = 128
ROW_TILE = 256
PROJ_TILE = 512
Q_TILE = 512
KEY_CHUNK = 512
LOG2E = 1.4426950408889634
SLOT_BLOCK = 128
TOK_CHUNK = 256
GATHER_WINDOW = 144
PAIR_LIMIT = SLOT_BLOCK - 16
VMEM_LIMIT = 56 * 1024 * 1024

_QA, _KA, _VA = 0, 384, 512
_QC, _KC, _VC = 640, 896, 1024
_BCQ, _BCKV, _BKR = 1152, 1408, 1536
IN_PAD = 1664
_ORIG = dict(aq=0, ak=384, av=512, bcq=640, bckv=832, bkr=960, cq=992, ck=1248, cv=1376)


def _cparams(sem, vmem=VMEM_LIMIT):
    return pltpu.CompilerParams(dimension_semantics=sem, vmem_limit_bytes=vmem)


def _pair_cols(base, n_heads):
    half = n_heads // 2
    cols = []
    for i in range(half):
        cols += list(range(base + i * 64, base + (i + 1) * 64))
        cols += list(range(base + (half + i) * 64, base + (half + i + 1) * 64))
    return cols


def _in_perm():
    perm = -np.ones((IN_PAD,), np.int64)
    perm[_QA:_QA + 384] = _pair_cols(_ORIG["aq"], A_HEADS)
    perm[_KA:_KA + 128] = np.arange(_ORIG["ak"], _ORIG["ak"] + 128)
    perm[_VA:_VA + 128] = np.arange(_ORIG["av"], _ORIG["av"] + 128)
    perm[_QC:_QC + 256] = _pair_cols(_ORIG["cq"], C_HEADS)
    perm[_KC:_KC + 128] = np.arange(_ORIG["ck"], _ORIG["ck"] + 128)
    perm[_VC:_VC + 128] = np.arange(_ORIG["cv"], _ORIG["cv"] + 128)
    perm[_BCQ:_BCQ + B_Q_RANK] = np.arange(_ORIG["bcq"], _ORIG["bcq"] + B_Q_RANK)
    perm[_BCKV:_BCKV + 128] = np.arange(_ORIG["bckv"], _ORIG["bckv"] + 128)
    perm[_BKR + 64:_BKR + 96] = np.arange(_ORIG["bkr"], _ORIG["bkr"] + 32)
    return perm


def _take_cols(w, perm):
    safe = np.where(perm >= 0, perm, 0)
    out = jnp.take(w, jnp.asarray(safe, jnp.int32), axis=1)
    return jnp.where(jnp.asarray(perm >= 0)[None, :], out, 0.0)


def _uq_perm():
    perm = -np.ones((B_HEADS * LANE,), np.int64)
    for h in range(B_HEADS):
        perm[h * LANE:h * LANE + 96] = np.arange(h * 96, h * 96 + 96)
    return perm


def _ukv_perms():
    pk = -np.ones((B_HEADS * LANE,), np.int64)
    pv = np.zeros((B_HEADS * 64,), np.int64)
    for h in range(B_HEADS):
        pk[h * LANE:h * LANE + 64] = np.arange(h * 128, h * 128 + 64)
        pv[h * 64:(h + 1) * 64] = np.arange(h * 128 + 64, h * 128 + 128)
    return pk, pv


def _seg_matrices():
    i = np.arange(LANE)
    m64 = (i[:, None] // 64 == i[None, :] // 64).astype(np.float32) / 64.0
    mb = np.zeros((LANE, LANE), np.float32)
    mb[:64, :64] = 1.0 / 64.0
    mb[64:96, 64:96] = 1.0 / 32.0
    def pair(a, b):
        out = np.zeros((2 * LANE, 2 * LANE), np.float32)
        out[:LANE, :LANE] = a
        out[LANE:, LANE:] = b
        return out

    return pair(m64, m64), pair(m64, mb), pair(mb, mb)


def _rope_tables(length):
    rows = length // GRID_W
    t = np.arange(rows * GRID_W)
    row = np.repeat(np.arange(rows), GRID_W).astype(np.float32)
    col = (t % GRID_W).astype(np.float32)

    def tabs(dim):
        axis_dim = dim // 2
        inv = (np.float32(ROPE_THETA) ** (-np.arange(0, axis_dim, 2, dtype=np.float32) / axis_dim)).astype(np.float32)
        ar = row[:, None] * inv[None, :]
        ac = col[:, None] * inv[None, :]
        cos = np.concatenate([np.cos(ar), np.cos(ar), np.cos(ac), np.cos(ac)], axis=-1)
        sin = np.concatenate([-np.sin(ar), np.sin(ar), -np.sin(ac), np.sin(ac)], axis=-1)
        return cos.astype(np.float32), sin.astype(np.float32)

    c64, s64 = tabs(HEAD_DIM)
    cb, sb = tabs(B_ROPE)
    n = c64.shape[0]
    ones64, zeros64 = np.ones((n, 64), np.float32), np.zeros((n, 64), np.float32)
    ones32, zeros32 = np.ones((n, 32), np.float32), np.zeros((n, 32), np.float32)
    return jnp.asarray(np.concatenate([c64, c64, s64, s64,
                                       ones64, cb, ones32, zeros64, sb, zeros32], axis=-1))


def _mod_kernel(c_ref, w_ref, b_ref, o_ref):
    cv = c_ref[...]
    a = (cv * jax.nn.sigmoid(cv)).astype(BF16)
    o_ref[0] = jnp.dot(a, w_ref[0].astype(BF16), preferred_element_type=F32) + b_ref[0]


def _modulation(cc, w_ada, b_ada):
    n_layers, d, n6 = w_ada.shape
    rows = cc.shape[0]
    tn = 1536
    return pl.pallas_call(
        _mod_kernel,
        grid=(n_layers, n6 // tn),
        in_specs=[pl.BlockSpec((rows, d), lambda l, j: (0, 0)),
                  pl.BlockSpec((1, d, tn), lambda l, j: (l, 0, j)),
                  pl.BlockSpec((1, 1, tn), lambda l, j: (l, 0, j))],
        out_specs=pl.BlockSpec((1, rows, tn), lambda l, j: (l, 0, j)),
        out_shape=jax.ShapeDtypeStruct((n_layers, rows, n6), F32),
        compiler_params=_cparams(("parallel", "parallel")),
        name="adaln_mod",
    )(cc, w_ada, b_ada.reshape(n_layers, 1, n6))


def _seg_rsqrt(x2, m_ref):
    return lax.rsqrt(jnp.dot((x2 * x2).astype(BF16), m_ref[...], preferred_element_type=F32) + EPS)


def _rope(x, cos, sin, half):
    lane = lax.broadcasted_iota(jnp.int32, x.shape, 1)
    lo = (lane % (2 * half)) < half
    partner = jnp.where(lo, pltpu.roll(x, LANE - half, 1), pltpu.roll(x, half, 1))
    return x * cos + partner * sin


def _inproj_kernel(x_ref, mod_ref, g1_ref, w_ref, gains_ref, gcq_ref, gckv_ref, wuq_ref, wukvk_ref,
                   wukvv_ref, mm_ref, mx_ref, mbb_ref, *rest, use_rope):
    if use_rope:
        rope_ref = rest[0]
        outs = rest[1:]
    else:
        rope_ref = None
        outs = rest
    qa_ref, ka_ref, va_ref, qb_ref, kb_ref, vb_ref, qc_ref, kc_ref, vc_ref = outs

    x = x_ref[0]
    y = x * lax.rsqrt(jnp.mean(x * x, axis=-1, keepdims=True) + EPS) * g1_ref[...]
    h = y * (1.0 + mod_ref[0, 0, 1:2, :]) + mod_ref[0, 0, 0:1, :]
    p = jnp.dot(h.astype(BF16), w_ref[...], preferred_element_type=F32)

    def head_pair(x2, m_ref, gain_rows, rope_cols, half):
        r = _seg_rsqrt(x2, m_ref)
        out = []
        for i in range(2):
            sl = slice(i * LANE, (i + 1) * LANE)
            y = x2[:, sl] * r[:, sl] * gains_ref[gain_rows[i]:gain_rows[i] + 1, :]
            if rope_ref is not None and rope_cols[i] is not None:
                c0, s0 = rope_cols[i]
                y = _rope(y, rope_ref[:, c0:c0 + LANE], rope_ref[:, s0:s0 + LANE], half[i])
            out.append(y)
        return out

    r64 = (0, 128)
    rb = (256, 384)
    qa0, qa1 = head_pair(p[:, _QA:_QA + 256], mm_ref, (0, 0), (r64, r64), (16, 16))
    qa2, ka = head_pair(p[:, _QA + 256:_QA + 512], mm_ref, (0, 1), (r64, r64), (16, 16))
    for i, blk in enumerate((qa0, qa1, qa2)):
        qa_ref[0, i] = blk.astype(BF16)
    ka_ref[0, 0] = ka.astype(BF16)
    va_ref[0, 0] = p[:, _VA:_VA + LANE].astype(BF16)
    qc0, qc1 = head_pair(p[:, _QC:_QC + 256], mm_ref, (2, 2), (r64, r64), (16, 16))
    qc_ref[0, :, 0:LANE] = qc0.astype(BF16)
    qc_ref[0, :, LANE:2 * LANE] = qc1.astype(BF16)
    kc, kpe = head_pair(jnp.concatenate([p[:, _KC:_KC + LANE], p[:, _BKR:_BKR + LANE]], axis=1),
                        mx_ref, (3, 6), (r64, rb), (16, 8))
    kc_ref[0] = kc.astype(BF16)
    vc_ref[0] = p[:, _VC:_VC + LANE].astype(BF16)

    cq = p[:, _BCQ:_BCQ + 256]
    cq = cq * lax.rsqrt(jnp.sum(cq * cq, axis=-1, keepdims=True) * (1.0 / B_Q_RANK) + EPS) * gcq_ref[...]
    qb = jnp.dot(cq.astype(BF16), wuq_ref[...], preferred_element_type=F32)
    ckv = p[:, _BCKV:_BCKV + LANE]
    ckv = ckv * lax.rsqrt(jnp.mean(ckv * ckv, axis=-1, keepdims=True) + EPS) * gckv_ref[...]
    ckv = ckv.astype(BF16)
    kbn = jnp.dot(ckv, wukvk_ref[...], preferred_element_type=F32)
    vbv = jnp.dot(ckv, wukvv_ref[...], preferred_element_type=F32).astype(BF16)
    for i in range(B_HEADS // 2):
        vb_ref[0, i] = vbv[:, i * LANE:(i + 1) * LANE]
    for i in range(B_HEADS // 2):
        sl = slice(2 * i * LANE, (2 * i + 2) * LANE)
        q0, q1 = head_pair(qb[:, sl], mbb_ref, (4, 4), (rb, rb), (8, 8))
        k0, k1 = head_pair(kbn[:, sl], mbb_ref, (5, 5), (None, None), (8, 8))
        qb_ref[0, 2 * i] = q0.astype(BF16)
        qb_ref[0, 2 * i + 1] = q1.astype(BF16)
        kb_ref[0, 2 * i] = (k0 + kpe).astype(BF16)
        kb_ref[0, 2 * i + 1] = (k1 + kpe).astype(BF16)


def _inproj(xs, modv, kind, g1, w_in_p, gains, gcq, gckv, wuq, wukvk, wukvv, mm, mx, mbb, rope):
    b, n, d = xs.shape
    tm = min(PROJ_TILE, n)
    use_rope = rope is not None
    const = lambda *shape: pl.BlockSpec(shape, lambda i, j: (0,) * len(shape))
    in_specs = [pl.BlockSpec((1, tm, d), lambda i, j: (i, j, 0)),
                pl.BlockSpec((1, 1, 8, d), lambda i, j: (i, kind, 0, 0)),
                const(1, d), const(d, IN_PAD), const(8, LANE), const(1, 256), const(1, LANE),
                const(256, 768), const(LANE, 768), const(LANE, 384),
                const(256, 256), const(256, 256), const(256, 256)]
    args = [xs, modv, g1, w_in_p, gains, gcq, gckv, wuq, wukvk, wukvv, mm, mx, mbb]
    if use_rope:
        in_specs.append(pl.BlockSpec((tm, 512), lambda i, j: (j, 0)))
        args.append(rope)
    blocks = (3, 1, 1, B_HEADS, B_HEADS, B_HEADS // 2)
    widths = (256, 128, 128)
    out_specs = ([pl.BlockSpec((1, nb, tm, LANE), lambda i, j: (i, 0, j, 0)) for nb in blocks]
                 + [pl.BlockSpec((1, tm, w), lambda i, j: (i, j, 0)) for w in widths])
    out_shape = ([jax.ShapeDtypeStruct((b, nb, n, LANE), BF16) for nb in blocks]
                 + [jax.ShapeDtypeStruct((b, n, w), BF16) for w in widths])
    return pl.pallas_call(
        functools.partial(_inproj_kernel, use_rope=use_rope),
        grid=(b, n // tm),
        in_specs=in_specs,
        out_specs=out_specs,
        out_shape=out_shape,
        compiler_params=_cparams(("parallel", "parallel")),
        name="inproj_rope" if use_rope else "inproj_ctx",
    )(*args)


_A_MAPS = (lambda h: h % 3, lambda h: 0, lambda h: 0)
_B_MAPS = (lambda h: h, lambda h: h, lambda h: h // 2)


def _qk(q, k):
    return lax.dot_general(q, k, (((1,), (1,)), ((), ())), preferred_element_type=F32)


def _half_mask(q, half):
    lane = lax.broadcasted_iota(jnp.int32, q.shape, 1)
    return jnp.where((lane // 64) == half, q, jnp.zeros_like(q))


def _score_pass(q, chunks, s_w):
    mx = None
    for k_ref, _, st, sz, off in chunks:
        s = _qk(q, k_ref[0, 0, st:st + sz, :])
        s_w[:, off:off + sz] = s
        for t in range(sz // LANE):
            blk = s[:, t * LANE:(t + 1) * LANE]
            mx = blk if mx is None else jnp.maximum(mx, blk)
    return jnp.broadcast_to(jnp.max(mx, axis=1, keepdims=True), mx.shape)


def _value_pass(chunks, s_r, mb):
    acc = None
    for _, v_ref, st, sz, off in chunks:
        ps = [jnp.exp2(s_r[:, off + t * LANE:off + (t + 1) * LANE] - mb).astype(BF16)
              for t in range(sz // LANE)]
        v_ones = jnp.concatenate([v_ref[0, 0, st:st + sz, :], jnp.ones((sz, LANE), BF16)], axis=1)
        part = jnp.dot(jnp.concatenate(ps, axis=1), v_ones, preferred_element_type=F32)
        acc = part if acc is None else acc + part
    return (acc[:, :LANE] / acc[:, LANE:LANE + 1]).astype(BF16)


def _fused_passes(q, chunks, s_w, s_r, mb_r):
    mx = None
    acc = None
    for k_ref, v_ref, st, sz, off in chunks:
        s = _qk(q, k_ref[0, 0, st:st + sz, :])
        s_w[:, off:off + sz] = s
        for t in range(sz // LANE):
            blk = s[:, t * LANE:(t + 1) * LANE]
            mx = blk if mx is None else jnp.maximum(mx, blk)
        ps = [jnp.exp2(s_r[:, off + t * LANE:off + (t + 1) * LANE] - mb_r).astype(BF16)
              for t in range(sz // LANE)]
        v_ones = jnp.concatenate([v_ref[0, 0, st:st + sz, :], jnp.ones((sz, LANE), BF16)], axis=1)
        part = jnp.dot(jnp.concatenate(ps, axis=1), v_ones, preferred_element_type=F32)
        acc = part if acc is None else acc + part
    mb_new = jnp.broadcast_to(jnp.max(mx, axis=1, keepdims=True), mx.shape)
    return mb_new, (acc[:, :LANE] / acc[:, LANE:LANE + 1]).astype(BF16)


def _key_chunks(kc_ref, vc_ref, kl_ref, vl_ref):
    n_ctx = kc_ref.shape[2]
    chunks = [(kc_ref, vc_ref, 0, n_ctx, 0)]
    if kl_ref is not None:
        for c in range(kl_ref.shape[2] // KEY_CHUNK):
            chunks.append((kl_ref, vl_ref, c * KEY_CHUNK, KEY_CHUNK, n_ctx + c * KEY_CHUNK))
    return chunks


def _dense_kernel(q_ref, kc_ref, kl_ref, vcp_ref, vlp_ref, vcc_ref, vlc_ref, o_ref,
                  sa_sc, sb_sc, mba_sc, mbb_sc, *, mask_q, n_tiles, tiles_per_sample, tiles_per_head, n_sub):
    step = pl.program_id(0)
    tq = sa_sc.shape[0]

    @pl.when(step == 0)
    def _():
        sa_sc[...] = jnp.zeros(sa_sc.shape, F32)
        mba_sc[...] = jnp.zeros(mba_sc.shape, F32)

    if mask_q:
        first = jnp.minimum(n_sub * step, n_tiles - n_sub)
        half = ((first % tiles_per_sample) // tiles_per_head) // 3
    for k in range(n_sub):
        rows = slice(k * tq, (k + 1) * tq)
        q = q_ref[0, 0, rows, :]
        if mask_q:
            q = _half_mask(q, half)
        v_refs = (vcp_ref, vlp_ref) if k == 0 else (vcc_ref, vlc_ref)
        chunks = _key_chunks(kc_ref, v_refs[0], kl_ref, v_refs[1])
        s_w, s_r, mb_w, mb_r = ((sb_sc, sa_sc, mbb_sc, mba_sc) if k % 2 == 0
                                else (sa_sc, sb_sc, mba_sc, mbb_sc))
        mb_new, o_ref[rows, :] = _fused_passes(q, chunks, s_w, s_r, mb_r[...])
        mb_w[...] = mb_new


def _dense_attention(q, kc, vc, kl, vl, n_heads, maps, mask_q, name):
    b, _, nq, _ = q.shape
    qmap, kmap, vmap = maps
    n_ctx, n_lat = kc.shape[2], kl.shape[2]
    tq = Q_TILE
    tph = nq // tq
    n_sub = 4 if tph % 4 == 0 else 2
    assert tph % n_sub == 0
    tps = n_heads * tph
    n_tiles = b * tps
    first = lambda s: jnp.minimum(n_sub * s, n_tiles - n_sub)
    before = lambda s: jnp.maximum(n_sub * s - 1, 0)
    kv_blk = lambda n, fmap, tile: pl.BlockSpec(
        (1, 1, n, LANE), lambda s: (tile(s) // tps, fmap((tile(s) % tps) // tph), 0, 0))
    q_spec = pl.BlockSpec(
        (1, 1, n_sub * tq, LANE),
        lambda s: (first(s) // tps, qmap((first(s) % tps) // tph), (first(s) % tph) // n_sub, 0))
    return pl.pallas_call(
        functools.partial(_dense_kernel, mask_q=mask_q, n_tiles=n_tiles, tiles_per_sample=tps,
                          tiles_per_head=tph, n_sub=n_sub),
        grid=(n_tiles // n_sub + 1,),
        in_specs=[q_spec, kv_blk(n_ctx, kmap, first), kv_blk(n_lat, kmap, first),
                  kv_blk(n_ctx, vmap, before), kv_blk(n_lat, vmap, before),
                  kv_blk(n_ctx, vmap, first), kv_blk(n_lat, vmap, first)],
        out_specs=pl.BlockSpec((n_sub * tq, LANE), lambda s: (s, 0)),
        out_shape=jax.ShapeDtypeStruct(((n_tiles + n_sub) * tq, LANE), BF16),
        scratch_shapes=[pltpu.VMEM((tq, n_ctx + n_lat), F32), pltpu.VMEM((tq, n_ctx + n_lat), F32),
                        pltpu.VMEM((tq, LANE), F32), pltpu.VMEM((tq, LANE), F32)],
        compiler_params=_cparams(("arbitrary",)),
        name=name,
    )(q, kc, kl, vc, vl, vc, vl)


def _dense_ctx_kernel(q_ref, kc_ref, vc_ref, o_ref, s_sc, *, mask_q):
    q = q_ref[0, 0]
    if mask_q:
        q = _half_mask(q, pl.program_id(1) // 3)
    chunks = _key_chunks(kc_ref, vc_ref, None, None)
    mb = _score_pass(q, chunks, s_sc)
    o_ref[0, 0] = _value_pass(chunks, s_sc, mb)


def _dense_attention_ctx(q, kc, vc, n_heads, maps, mask_q, name):
    b, _, nq, _ = q.shape
    qmap, kmap, vmap = maps
    blk = lambda fmap: pl.BlockSpec((1, 1, nq, LANE), lambda i, h: (i, fmap(h), 0, 0))
    return pl.pallas_call(
        functools.partial(_dense_ctx_kernel, mask_q=mask_q),
        grid=(b, n_heads),
        in_specs=[blk(qmap), blk(kmap), blk(vmap)],
        out_specs=pl.BlockSpec((1, 1, nq, LANE), lambda i, h: (i, h, 0, 0)),
        out_shape=jax.ShapeDtypeStruct((b, n_heads, nq, LANE), BF16),
        scratch_shapes=[pltpu.VMEM((nq, kc.shape[2]), F32)],
        compiler_params=_cparams(("parallel", "parallel")),
        name=name,
    )(q, kc, vc)


def _window_kernel(q_ref, kc_ref, vc_ref, sink_ref, *rest, use_window):
    if use_window:
        kl_ref, vl_ref, o_ref = rest
    else:
        (o_ref,) = rest
    tq = ROW_TILE
    span = tq + 2 * WINDOW
    kcx = kc_ref[0]
    ones_c = jnp.ones((kcx.shape[0], LANE), BF16)
    v_aug = jnp.concatenate([vc_ref[0], ones_c], axis=1)
    lane = lax.broadcasted_iota(jnp.int32, (tq, LANE), 1)
    for ti in range(q_ref.shape[1] // tq):
        rows = slice(ti * tq, (ti + 1) * tq)
        if use_window:
            n_lat = kl_ref.shape[1]
            t0 = (pl.program_id(1) * (q_ref.shape[1] // tq) + ti) * tq
            start = pl.multiple_of(jnp.clip(t0 - WINDOW, 0, n_lat - span), LANE)
            qpos = t0 + lax.broadcasted_iota(jnp.int32, (tq, span), 0)
            kpos = start + lax.broadcasted_iota(jnp.int32, (tq, span), 1)
            valid = jnp.abs(qpos - kpos) <= WINDOW
            kw = kl_ref[0, pl.ds(start, span), :]
            v_all = jnp.concatenate(
                [v_aug, jnp.concatenate([vl_ref[0, pl.ds(start, span), :], jnp.ones((span, LANE), BF16)], axis=1)],
                axis=0)
        else:
            v_all = v_aug
        res = []
        for h in range(C_HEADS):
            half = h // (C_HEADS // C_KV_HEADS)
            q = q_ref[0, rows, (h % 2) * LANE:(h % 2 + 1) * LANE]
            q = jnp.where((lane < 64) if half == 0 else (lane >= 64), q, jnp.zeros_like(q))
            sink = sink_ref[0:1, h:h + 1]
            s = _qk(q, kcx)
            if use_window:
                s = jnp.concatenate([s, jnp.where(valid, _qk(q, kw), NEG_INF)], axis=1)
            mx = s[:, 0:LANE]
            for t in range(1, s.shape[1] // LANE):
                mx = jnp.maximum(mx, s[:, t * LANE:(t + 1) * LANE])
            m = jnp.maximum(jnp.max(mx, axis=1, keepdims=True), sink)
            mb = jnp.broadcast_to(m, (tq, LANE))
            p = jnp.concatenate([jnp.exp2(s[:, t * LANE:(t + 1) * LANE] - mb).astype(BF16)
                                 for t in range(s.shape[1] // LANE)], axis=1)
            o = jnp.dot(p, v_all, preferred_element_type=F32)
            res.append(o[:, :LANE] / (o[:, LANE:LANE + 1] + jnp.exp2(sink - m)))
        for ob in range(2):
            o_ref[0, rows, ob * LANE:(ob + 1) * LANE] = jnp.where(lane < 64, res[ob], res[2 + ob]).astype(BF16)


def _window_attention(q, kc, vc, kl, vl, sink, name):
    b, nq, qw = q.shape
    tq = min(2 * ROW_TILE, nq)
    n_ctx = kc.shape[1]
    use_window = kl is not None
    in_specs = [pl.BlockSpec((1, tq, qw), lambda i, j: (i, j, 0)),
                pl.BlockSpec((1, n_ctx, LANE), lambda i, j: (i, 0, 0)),
                pl.BlockSpec((1, n_ctx, LANE), lambda i, j: (i, 0, 0)),
                pl.BlockSpec((8, LANE), lambda i, j: (0, 0))]
    args = [q, kc, vc, sink]
    if use_window:
        n_lat = kl.shape[1]
        in_specs += [pl.BlockSpec((1, n_lat, LANE), lambda i, j: (i, 0, 0)),
                     pl.BlockSpec((1, n_lat, LANE), lambda i, j: (i, 0, 0))]
        args += [kl, vl]
    return pl.pallas_call(
        functools.partial(_window_kernel, use_window=use_window),
        grid=(b, nq // tq),
        in_specs=in_specs,
        out_specs=pl.BlockSpec((1, tq, qw), lambda i, j: (i, j, 0)),
        out_shape=jax.ShapeDtypeStruct((b, nq, qw), BF16),
        compiler_params=_cparams(("parallel", "parallel")),
        name=name,
    )(*args)


def _outproj_kernel(*refs):
    oa = refs[0:A_HEADS]
    ob = refs[A_HEADS:A_HEADS + B_HEADS]
    oc_ref, x_ref, mod_ref, g2n_ref, w_ref, wr_ref, xo_ref, h2_ref, aff_ref = refs[A_HEADS + B_HEADS:]
    tm = x_ref.shape[1]
    low = lax.broadcasted_iota(jnp.int32, (tm, LANE), 1) < 64
    parts = [jnp.where(low, oa[i][...], oa[3 + i][...]) for i in range(3)]
    parts += [jnp.where(low, ob[2 * i][...], ob[2 * i + 1][...]) for i in range(3)]
    parts.append(oc_ref[0])
    mix = jnp.dot(jnp.concatenate(parts, axis=1), w_ref[...], preferred_element_type=F32)
    x = x_ref[0] + mod_ref[0, 0, 2:3, :] * mix
    xo_ref[0] = x
    y = x * lax.rsqrt(jnp.mean(x * x, axis=-1, keepdims=True) + EPS) * g2n_ref[...]
    h2 = y * (1.0 + mod_ref[0, 0, 4:5, :]) + mod_ref[0, 0, 3:4, :]
    h2_ref[0] = h2.astype(BF16)
    wr = wr_ref[...]
    wr_hi = wr.astype(BF16)
    wr_lo = (wr - wr_hi.astype(F32)).astype(BF16)
    h2_hi = h2.astype(BF16)
    h2_lo = (h2 - h2_hi.astype(F32)).astype(BF16)
    logits = (jnp.dot(h2_hi, wr_hi, preferred_element_type=F32)
              + jnp.dot(h2_lo, wr_hi, preferred_element_type=F32)
              + jnp.dot(h2_hi, wr_lo, preferred_element_type=F32))
    logits = logits.T[0:aff_ref.shape[1], :]
    z = jnp.exp(logits - jnp.max(logits, axis=0, keepdims=True))
    aff_ref[0] = z / jnp.sum(z, axis=0, keepdims=True)


def _outproj(oa, ob, oc, xs, modv, kind, g2n, w_mix, wr_t, pad_rows, name):
    b, n, d = xs.shape
    tm = min(PROJ_TILE, n)
    e = N_EXPERTS
    const = lambda *shape: pl.BlockSpec(shape, lambda i, j: (0,) * len(shape))
    row = lambda w: pl.BlockSpec((1, tm, w), lambda i, j: (i, j, 0))
    n_heads = A_HEADS
    head = lambda h: pl.BlockSpec((tm, LANE), lambda i, j: ((pad_rows + (i * n_heads + h) * n) // tm + j, 0))
    return pl.pallas_call(
        _outproj_kernel,
        grid=(b, n // tm),
        in_specs=[head(h) for h in range(A_HEADS)] + [head(h) for h in range(B_HEADS)] + [
                  row(256), row(d),
                  pl.BlockSpec((1, 1, 8, d), lambda i, j: (i, kind, 0, 0)),
                  const(1, d), const(d, d), const(d, LANE)],
        out_specs=[row(d), row(d), pl.BlockSpec((1, e, tm), lambda i, j: (i, 0, j))],
        out_shape=[jax.ShapeDtypeStruct((b, n, d), F32), jax.ShapeDtypeStruct((b, n, d), BF16),
                   jax.ShapeDtypeStruct((b, e, n), F32)],
        compiler_params=_cparams(("parallel", "parallel")),
        name=name,
    )(*([oa] * A_HEADS), *([ob] * B_HEADS), oc, xs, modv, g2n, w_mix, wr_t)


def _prefix_exclusive(mask_f, tri):
    e, t = mask_f.shape
    ck = tri.shape[0]
    carry = jnp.zeros((e, 1), F32)
    parts, offs = [], [carry]
    for c in range(t // ck):
        blk = mask_f[:, c * ck:(c + 1) * ck]
        parts.append(jnp.dot(blk.astype(BF16), tri, preferred_element_type=F32) + carry)
        carry = carry + jnp.sum(blk, axis=1, keepdims=True)
        offs.append(carry)
    return jnp.concatenate(parts, axis=1) if len(parts) > 1 else parts[0], offs


def _topk_kernel(aff_ref, slot_ref, cnt_ref, *, cap, slot_stride):
    aff = aff_ref[0]
    e, t = aff.shape
    bits = pltpu.bitcast(aff, jnp.int32)
    kf = jnp.float32(cap)

    def search(i, lo):
        cand = lo | (jnp.int32(1) << (30 - i))
        n_ge = jnp.sum((bits >= cand).astype(F32), axis=1, keepdims=True)
        return jnp.where(n_ge >= kf, cand, lo)

    thr = lax.fori_loop(0, 31, search, jnp.zeros((e, 1), jnp.int32))
    gt = bits > thr
    eq = bits == thr
    need = kf - jnp.sum(gt.astype(F32), axis=1, keepdims=True)

    r = lax.broadcasted_iota(jnp.int32, (TOK_CHUNK, TOK_CHUNK), 0)
    c = lax.broadcasted_iota(jnp.int32, (TOK_CHUNK, TOK_CHUNK), 1)
    tri = (r < c).astype(BF16)
    tie_rank, _ = _prefix_exclusive(eq.astype(F32), tri)
    sel = gt | (eq & (tie_rank < need))
    slot, offs = _prefix_exclusive(sel.astype(F32), tri)
    base = pl.program_id(0) * slot_stride
    slot_ref[0] = jnp.where(sel, slot.astype(jnp.int32) + base, -1)
    lane = lax.broadcasted_iota(jnp.int32, (e, LANE), 1)
    cnt = jnp.zeros((e, LANE), jnp.int32)
    for ci, off in enumerate(offs):
        cnt = jnp.where(lane == ci, off.astype(jnp.int32) + base, cnt)
    cnt_ref[0] = cnt


def _topk(aff_t, cap, slot_stride, name):
    b, e, t = aff_t.shape
    return pl.pallas_call(
        functools.partial(_topk_kernel, cap=cap, slot_stride=slot_stride),
        grid=(b,),
        in_specs=[pl.BlockSpec((1, e, t), lambda i: (i, 0, 0))],
        out_specs=[pl.BlockSpec((1, e, t), lambda i: (i, 0, 0)),
                   pl.BlockSpec((1, e, LANE), lambda i: (i, 0, 0))],
        out_shape=[jax.ShapeDtypeStruct((b, e, t), jnp.int32), jax.ShapeDtypeStruct((b, e, LANE), jnp.int32)],
        compiler_params=_cparams(("parallel",)),
        name=name,
    )(aff_t)


def _moe_kernel(cnt_ref, h_ref, slot_ref, slotp_ref, gate_ref, wg_ref, wu_ref, wd_ref, mod_ref, x_hbm, acc_ref,
                xe_sc, gs_sc, ye_sc, x_sem, *, n_chunks, n_sb):
    bi = pl.program_id(0)
    ei = pl.program_id(1)
    n_e = pl.num_programs(1)
    x_copy = pltpu.make_async_copy(x_hbm.at[bi], acc_ref.at[0], x_sem)

    @pl.when(ei == 0)
    def _():
        x_copy.start()

    def wait_for_x(at_expert):
        @pl.when(ei == at_expert)
        def _():
            x_copy.wait()

    g2 = mod_ref[0, 0, 5:6, :]
    cap = n_sb * SLOT_BLOCK

    def counts_of(e):
        base = (bi * n_e + e) * 32
        return [cnt_ref[base + c] for c in range(n_chunks + 1)]

    def all_chunks_at_most(cs, limit):
        ok = cs[1] - cs[0] <= limit
        for c in range(1, n_chunks):
            ok = ok & (cs[c + 1] - cs[c] <= limit)
        return ok

    odd = (ei % 2) == 1
    counts = counts_of(ei)
    counts_other = counts_of(jnp.where(odd, ei - 1, ei + 1))
    pair_narrow = all_chunks_at_most(counts, PAIR_LIMIT) & all_chunks_at_most(counts_other, PAIR_LIMIT)

    def ffn():
        xe = xe_sc[0:cap, :].astype(BF16)
        f = wg_ref.shape[2]
        fb = 512
        y = None
        for f0 in range(0, f, fb):
            hg = jnp.dot(xe, wg_ref[0, :, f0:f0 + fb], preferred_element_type=F32)
            hu = jnp.dot(xe, wu_ref[0, :, f0:f0 + fb], preferred_element_type=F32)
            hid = (hg * jax.nn.sigmoid(hg) * hu).astype(BF16)
            part = jnp.dot(hid, wd_ref[0, f0:f0 + fb, :], preferred_element_type=F32)
            y = part if y is None else y + part
        return y

    def tok(c):
        return slice(c * TOK_CHUNK, (c + 1) * TOK_CHUNK)

    def slot_columns(ref):
        pad = jnp.zeros((LANE - n_chunks, TOK_CHUNK), F32)
        return jnp.concatenate([ref[0, 0].astype(F32), pad], axis=0).T

    @pl.when(pair_narrow)
    def _():
        xe_sc[...] = jnp.zeros_like(xe_sc)
        gs_sc[...] = jnp.zeros_like(gs_sc)
        srow = lax.broadcasted_iota(jnp.int32, (GATHER_WINDOW, TOK_CHUNK), 0)
        for c in range(n_chunks):
            g0 = pl.multiple_of((counts[c] // 8) * 8, 8)
            oh = (slot_ref[0, 0, c:c + 1, :] - g0) == srow
            xe_sc[pl.ds(g0, GATHER_WINDOW), :] += jnp.dot(
                oh.astype(BF16), h_ref[0, tok(c), :], preferred_element_type=F32)
            gs_sc[pl.ds(g0, GATHER_WINDOW), :] += jnp.sum(
                jnp.where(oh, gate_ref[0, 0, c:c + 1, :], 0.0), axis=1, keepdims=True)
        ye = ye_sc.at[ei % 2]
        ye[0:cap, :] = (ffn() * gs_sc[0:cap, :]).astype(BF16)
        ye[cap:cap + SLOT_BLOCK, :] = jnp.zeros((SLOT_BLOCK, ye_sc.shape[2]), BF16)

        @pl.when(odd)
        def _():
            wait_for_x(1)
            cols_prev = slot_columns(slotp_ref)
            cols_cur = slot_columns(slot_ref)
            scol = lax.broadcasted_iota(jnp.int32, (TOK_CHUNK, SLOT_BLOCK), 1).astype(F32)
            for c in range(n_chunks):
                w_prev = pl.multiple_of((counts_other[c] // 16) * 16, 16)
                w_cur = pl.multiple_of((counts[c] // 16) * 16, 16)
                oh_t = jnp.concatenate(
                    [((cols_prev[:, c:c + 1] - w_prev.astype(F32)) == scol).astype(BF16),
                     ((cols_cur[:, c:c + 1] - w_cur.astype(F32)) == scol).astype(BF16)], axis=1)
                rows = jnp.concatenate([ye_sc[0, pl.ds(w_prev, SLOT_BLOCK), :],
                                        ye_sc[1, pl.ds(w_cur, SLOT_BLOCK), :]], axis=0)
                acc_ref[0, tok(c), :] += g2 * jnp.dot(oh_t, rows, preferred_element_type=F32)

    @pl.when(jnp.logical_not(pair_narrow))
    def _():
        xe_sc[...] = jnp.zeros_like(xe_sc)
        gs_sc[...] = jnp.zeros_like(gs_sc)
        srow = lax.broadcasted_iota(jnp.int32, (SLOT_BLOCK, TOK_CHUNK), 0)

        def overlap(c, sb):
            return (counts[c] < (sb + 1) * SLOT_BLOCK) & (counts[c + 1] > sb * SLOT_BLOCK)

        def onehot(c, sb):
            return slot_ref[0, 0, c:c + 1, :] == (srow + sb * SLOT_BLOCK)

        for c in range(n_chunks):
            for sb in range(n_sb):
                @pl.when(overlap(c, sb))
                def _(c=c, sb=sb):
                    oh = onehot(c, sb)
                    rows = slice(sb * SLOT_BLOCK, (sb + 1) * SLOT_BLOCK)
                    xe_sc[rows] += jnp.dot(oh.astype(BF16), h_ref[0, tok(c), :], preferred_element_type=F32)
                    g = jnp.where(oh, gate_ref[0, 0, c:c + 1, :], 0.0)
                    gs_sc[rows] += jnp.sum(g, axis=1, keepdims=True)

        ye_sc[0, 0:cap, :] = (ffn() * gs_sc[0:cap, :]).astype(BF16)
        wait_for_x(0)

        for c in range(n_chunks):
            for sb in range(n_sb):
                @pl.when(overlap(c, sb))
                def _(c=c, sb=sb):
                    oh_t = onehot(c, sb).astype(F32).T.astype(BF16)
                    rows = slice(sb * SLOT_BLOCK, (sb + 1) * SLOT_BLOCK)
                    acc_ref[0, tok(c), :] += g2 * jnp.dot(oh_t, ye_sc[0, rows, :], preferred_element_type=F32)


def _moe(h2, slot, gate, cnt, wg, wu, wd, xs, modv, kind, cap, name):
    b, t, d = h2.shape
    e = slot.shape[1]
    f = wg.shape[2]
    n_chunks = t // TOK_CHUNK
    n_sb = cap // SLOT_BLOCK
    slot4 = slot.reshape(b, e, n_chunks, TOK_CHUNK)
    gate4 = gate.reshape(b, e, n_chunks, TOK_CHUNK)
    cnt_flat = cnt[:, :, :32].reshape(-1)
    grid_spec = pltpu.PrefetchScalarGridSpec(
        num_scalar_prefetch=1,
        grid=(b, e),
        in_specs=[pl.BlockSpec((1, t, d), lambda i, j, s: (i, 0, 0)),
                  pl.BlockSpec((1, 1, n_chunks, TOK_CHUNK), lambda i, j, s: (i, j, 0, 0)),
                  pl.BlockSpec((1, 1, n_chunks, TOK_CHUNK), lambda i, j, s: (i, jnp.maximum(j - 1, 0), 0, 0)),
                  pl.BlockSpec((1, 1, n_chunks, TOK_CHUNK), lambda i, j, s: (i, j, 0, 0)),
                  pl.BlockSpec((1, d, f), lambda i, j, s: (j, 0, 0)),
                  pl.BlockSpec((1, d, f), lambda i, j, s: (j, 0, 0)),
                  pl.BlockSpec((1, f, d), lambda i, j, s: (j, 0, 0)),
                  pl.BlockSpec((1, 1, 8, d), lambda i, j, s: (i, kind, 0, 0)),
                  pl.BlockSpec(memory_space=pl.ANY)],
        out_specs=pl.BlockSpec((1, t, d), lambda i, j, s: (i, 0, 0), pipeline_mode=pl.Buffered(1)),
        scratch_shapes=[pltpu.VMEM((cap + 2 * SLOT_BLOCK, d), F32), pltpu.VMEM((cap + 2 * SLOT_BLOCK, 1), F32),
                        pltpu.VMEM((2, cap + SLOT_BLOCK, d), BF16), pltpu.SemaphoreType.DMA(())],
    )
    return pl.pallas_call(
        functools.partial(_moe_kernel, n_chunks=n_chunks, n_sb=n_sb),
        grid_spec=grid_spec,
        out_shape=jax.ShapeDtypeStruct((b, t, d), F32),
        compiler_params=_cparams(("parallel", "arbitrary")),
        name=name,
    )(cnt_flat, h2, slot4, slot4, gate4, wg, wu, wd, modv, xs)


def _cast_kernel(w_ref, o_ref):
    o_ref[...] = w_ref[0].astype(BF16)


def _layer_bf16(w, layer):
    _, e, r, c = w.shape
    return pl.pallas_call(
        _cast_kernel,
        grid=(e,),
        in_specs=[pl.BlockSpec((1, 1, r, c), lambda i: (layer, i, 0, 0))],
        out_specs=pl.BlockSpec((1, r, c), lambda i: (i, 0, 0)),
        out_shape=jax.ShapeDtypeStruct((e, r, c), BF16),
        compiler_params=_cparams(("parallel",)),
        name="expert_weights_bf16",
    )(w)


def _pad_lanes(v, width):
    return jnp.pad(v, (0, width - v.shape[0]))


def kernel(x, c, ctx, c_ctx, w_ada, b_ada, norm1_g, norm2_g, w_in, a_q_norm, a_k_norm, b_cq_norm, b_ckv_norm, w_uq, w_ukv, b_qn_norm, b_kn_norm, b_qr_norm, b_kr_norm, c_q_norm, c_k_norm, c_sink, w_out, w_router, w_e_gate, w_e_up, w_e_down):
    b, t, d = x.shape
    n_ctx = ctx.shape[1]
    depth = w_ada.shape[0]
    assert n_ctx == ROW_TILE and t % KEY_CHUNK == 0 and t % GRID_W == 0
    cap = max(1, CAP_FACTOR * t // N_EXPERTS)
    cap_c = max(1, CAP_FACTOR * n_ctx // N_EXPERTS)
    assert cap % SLOT_BLOCK == 0 and (b * cap_c) % SLOT_BLOCK == 0

    rows = ((b + 1 + 7) // 8) * 8
    cc = jnp.zeros((rows, d), F32).at[:b].set(c).at[b].set(c_ctx)
    mod = _modulation(cc, w_ada, b_ada)

    rope = _rope_tables(t)
    mm, mx, mbb = (jnp.asarray(m, BF16) for m in _seg_matrices())
    in_perm = _in_perm()
    uq_perm = _uq_perm()
    ukv_pk, ukv_pv = _ukv_perms()
    oa_rows = np.asarray(_pair_cols(0, A_HEADS))
    oc_rows = np.asarray(_pair_cols(0, C_HEADS)) + 768
    sa = HEAD_DIM ** -0.5
    sb = (B_NOPE + B_ROPE) ** -0.5
    z32, z64 = jnp.zeros((32,), F32), jnp.zeros((64,), F32)

    for l in range(depth):
        last = l == depth - 1
        m6 = mod[l].reshape(rows, 6, d)
        m8 = jnp.pad(m6, ((0, 0), (0, 2), (0, 0)))
        modv = jnp.stack([jnp.broadcast_to(m8[b][None], (b, 8, d)), m8[:b]], axis=1)

        w_in_p = _take_cols(w_in[l], in_perm).astype(BF16)
        wuq = jnp.pad(_take_cols(w_uq[l], uq_perm), ((0, 256 - B_Q_RANK), (0, 0))).astype(BF16)
        wukvk = _take_cols(w_ukv[l], ukv_pk).astype(BF16)
        wukvv = jnp.take(w_ukv[l], jnp.asarray(ukv_pv, jnp.int32), axis=1).astype(BF16)
        gains = jnp.stack([
            jnp.tile(a_q_norm[l], 2) * (sa * LOG2E), jnp.tile(a_k_norm[l], 2),
            jnp.tile(c_q_norm[l], 2) * (sa * LOG2E), jnp.tile(c_k_norm[l], 2),
            jnp.concatenate([b_qn_norm[l] * (sb * LOG2E), b_qr_norm[l] * (sb * LOG2E), z32]),
            jnp.concatenate([b_kn_norm[l], z64]),
            jnp.concatenate([z64, b_kr_norm[l], z32]),
            jnp.zeros((LANE,), F32)])
        gcq = _pad_lanes(b_cq_norm[l], 256)[None]
        gckv = b_ckv_norm[l][None]
        g1 = norm1_g[l][None]
        g2n = norm2_g[l][None]
        mix_rows = np.concatenate([oa_rows, np.arange(384, 768), oc_rows])
        w_mix = jnp.take(w_out[l], jnp.asarray(mix_rows, jnp.int32), axis=0).astype(BF16)
        wr_t = jnp.pad(w_router[l], ((0, 0), (0, LANE - N_EXPERTS)))
        sink = jnp.zeros((8, LANE), F32).at[0, :C_HEADS].set(c_sink[l] * LOG2E)
        wg = _layer_bf16(w_e_gate, l)
        wu = _layer_bf16(w_e_up, l)
        wd = _layer_bf16(w_e_down, l)

        shared = (g1, w_in_p, gains, gcq, gckv, wuq, wukvk, wukvv, mm, mx, mbb)
        qa, ka, va, qb, kb, vb, qc, kc, vc = _inproj(x, modv, 1, *shared, rope)
        qac, kac, vac, qbc, kbc, vbc, qcc, kcc, vcc = _inproj(ctx, modv, 0, *shared, None)

        o_a = _dense_attention(qa, kac, vac, ka, va, A_HEADS, _A_MAPS, True, "attn_a")
        o_b = _dense_attention(qb, kbc, vbc, kb, vb, B_HEADS, _B_MAPS, False, "attn_b")
        o_c = _window_attention(qc, kcc, vcc, kc, vc, sink, "attn_c")
        x_mid, h2, aff = _outproj(o_a, o_b, o_c, x, modv, 1, g2n, w_mix, wr_t, Q_TILE, "outproj")
        slot, cnt = _topk(aff, cap, 0, "topk")
        x = _moe(h2, slot, aff, cnt, wg, wu, wd, x_mid, modv, 1, cap, "moe")

        if not last:
            o_ac = _dense_attention_ctx(qac, kac, vac, A_HEADS, _A_MAPS, True, "attn_a_ctx")
            o_bc = _dense_attention_ctx(qbc, kbc, vbc, B_HEADS, _B_MAPS, False, "attn_b_ctx")
            o_cc = _window_attention(qcc, kcc, vcc, None, None, sink, "attn_c_ctx")
            flat = lambda o: o.reshape(-1, LANE)
            c_mid, hc2, aff_c = _outproj(flat(o_ac), flat(o_bc), o_cc, ctx, modv, 0, g2n, w_mix, wr_t, 0,
                                         "outproj_ctx")
            slot_c, cnt_c = _topk(aff_c, cap_c, cap_c, "topk_ctx")
            e = N_EXPERTS
            slot_f = jnp.transpose(slot_c, (1, 0, 2)).reshape(1, e, b * n_ctx)
            gate_f = jnp.transpose(aff_c, (1, 0, 2)).reshape(1, e, b * n_ctx)
            starts = jnp.transpose(cnt_c[:, :, 0], (1, 0))
            cnt_f = jnp.concatenate([starts, cnt_c[b - 1, :, 1:2],
                                     jnp.zeros((e, LANE - b - 1), jnp.int32)], axis=1)[None]
            ctx = _moe(hc2.reshape(1, b * n_ctx, d), slot_f, gate_f, cnt_f, wg, wu, wd,
                       c_mid.reshape(1, b * n_ctx, d), modv, 0, b * cap_c, "moe_ctx").reshape(b, n_ctx, d)
    return x
```

```python
import functools
import math

import numpy as np
import jax
import jax.numpy as jnp
from jax import lax
from jax.experimental import pallas as pl
from jax.experimental.pallas import tpu as pltpu

F32 = jnp.float32
BF16 = jnp.bfloat16

GRID_W = 64
ROPE_THETA = 10000.0
EPS = 1e-6
NEG_INF = -1e30
HEAD_DIM = 64
A_HEADS, A_KV_HEADS = 6, 2
B_HEADS, B_Q_RANK, B_KV_RANK, B_NOPE, B_ROPE, B_V = 6, 192, 128, 64, 32, 64
C_HEADS, C_KV_HEADS = 4, 2
N_EXPERTS = 16
CAP_FACTOR = 2
WINDOW = 128

LANE = 128
ROW_TILE = 256
PROJ_TILE = 512
Q_TILE = 512
KEY_CHUNK = 512
LOG2E = 1.4426950408889634
SLOT_BLOCK = 128
TOK_CHUNK = 256
GATHER_WINDOW = 144
PAIR_LIMIT = SLOT_BLOCK - 16
VMEM_LIMIT = 56 * 1024 * 1024

_QA, _KA, _VA = 0, 384, 512
_QC, _KC, _VC = 640, 896, 1024
_BCQ, _BCKV, _BKR = 1152, 1408, 1536
IN_PAD = 1664
_ORIG = dict(aq=0, ak=384, av=512, bcq=640, bckv=832, bkr=960, cq=992, ck=1248, cv=1376)


def _cparams(sem, vmem=VMEM_LIMIT):
    return pltpu.CompilerParams(dimension_semantics=sem, vmem_limit_bytes=vmem)


def _pair_cols(base, n_heads):
    half = n_heads // 2
    cols = []
    for i in range(half):
        cols += list(range(base + i * 64, base + (i + 1) * 64))
        cols += list(range(base + (half + i) * 64, base + (half + i + 1) * 64))
    return cols


def _in_perm():
    perm = -np.ones((IN_PAD,), np.int64)
    perm[_QA:_QA + 384] = _pair_cols(_ORIG["aq"], A_HEADS)
    perm[_KA:_KA + 128] = np.arange(_ORIG["ak"], _ORIG["ak"] + 128)
    perm[_VA:_VA + 128] = np.arange(_ORIG["av"], _ORIG["av"] + 128)
    perm[_QC:_QC + 256] = _pair_cols(_ORIG["cq"], C_HEADS)
    perm[_KC:_KC + 128] = np.arange(_ORIG["ck"], _ORIG["ck"] + 128)
    perm[_VC:_VC + 128] = np.arange(_ORIG["cv"], _ORIG["cv"] + 128)
    perm[_BCQ:_BCQ + B_Q_RANK] = np.arange(_ORIG["bcq"], _ORIG["bcq"] + B_Q_RANK)
    perm[_BCKV:_BCKV + 128] = np.arange(_ORIG["bckv"], _ORIG["bckv"] + 128)
    perm[_BKR + 64:_BKR + 96] = np.arange(_ORIG["bkr"], _ORIG["bkr"] + 32)
    return perm


def _take_cols(w, perm):
    safe = np.where(perm >= 0, perm, 0)
    out = jnp.take(w, jnp.asarray(safe, jnp.int32), axis=1)
    return jnp.where(jnp.asarray(perm >= 0)[None, :], out, 0.0)


def _uq_perm():
    perm = -np.ones((B_HEADS * LANE,), np.int64)
    for h in range(B_HEADS):
        perm[h * LANE:h * LANE + 96] = np.arange(h * 96, h * 96 + 96)
    return perm


def _ukv_perms():
    pk = -np.ones((B_HEADS * LANE,), np.int64)
    pv = np.zeros((B_HEADS * 64,), np.int64)
    for h in range(B_HEADS):
        pk[h * LANE:h * LANE + 64] = np.arange(h * 128, h * 128 + 64)
        pv[h * 64:(h + 1) * 64] = np.arange(h * 128 + 64, h * 128 + 128)
    return pk, pv


def _seg_matrices():
    i = np.arange(LANE)
    m64 = (i[:, None] // 64 == i[None, :] // 64).astype(np.float32) / 64.0
    mb = np.zeros((LANE, LANE), np.float32)
    mb[:64, :64] = 1.0 / 64.0
    mb[64:96, 64:96] = 1.0 / 32.0
    def pair(a, b):
        out = np.zeros((2 * LANE, 2 * LANE), np.float32)
        out[:LANE, :LANE] = a
        out[LANE:, LANE:] = b
        return out

    return pair(m64, m64), pair(m64, mb), pair(mb, mb)


def _rope_tables(length):
    rows = length // GRID_W
    t = np.arange(rows * GRID_W)
    row = np.repeat(np.arange(rows), GRID_W).astype(np.float32)
    col = (t % GRID_W).astype(np.float32)

    def tabs(dim):
        axis_dim = dim // 2
        inv = (np.float32(ROPE_THETA) ** (-np.arange(0, axis_dim, 2, dtype=np.float32) / axis_dim)).astype(np.float32)
        ar = row[:, None] * inv[None, :]
        ac = col[:, None] * inv[None, :]
        cos = np.concatenate([np.cos(ar), np.cos(ar), np.cos(ac), np.cos(ac)], axis=-1)
        sin = np.concatenate([-np.sin(ar), np.sin(ar), -np.sin(ac), np.sin(ac)], axis=-1)
        return cos.astype(np.float32), sin.astype(np.float32)

    c64, s64 = tabs(HEAD_DIM)
    cb, sb = tabs(B_ROPE)
    n = c64.shape[0]
    ones64, zeros64 = np.ones((n, 64), np.float32), np.zeros((n, 64), np.float32)
    ones32, zeros32 = np.ones((n, 32), np.float32), np.zeros((n, 32), np.float32)
    return jnp.asarray(np.concatenate([c64, c64, s64, s64,
                                       ones64, cb, ones32, zeros64, sb, zeros32], axis=-1))


def _mod_kernel(c_ref, w_ref, b_ref, o_ref):
    cv = c_ref[...]
    a = (cv * jax.nn.sigmoid(cv)).astype(BF16)
    o_ref[0] = jnp.dot(a, w_ref[0].astype(BF16), preferred_element_type=F32) + b_ref[0]


def _modulation(cc, w_ada, b_ada):
    n_layers, d, n6 = w_ada.shape
    rows = cc.shape[0]
    tn = 1536
    return pl.pallas_call(
        _mod_kernel,
        grid=(n_layers, n6 // tn),
        in_specs=[pl.BlockSpec((rows, d), lambda l, j: (0, 0)),
                  pl.BlockSpec((1, d, tn), lambda l, j: (l, 0, j)),
                  pl.BlockSpec((1, 1, tn), lambda l, j: (l, 0, j))],
        out_specs=pl.BlockSpec((1, rows, tn), lambda l, j: (l, 0, j)),
        out_shape=jax.ShapeDtypeStruct((n_layers, rows, n6), F32),
        compiler_params=_cparams(("parallel", "parallel")),
        name="adaln_mod",
    )(cc, w_ada, b_ada.reshape(n_layers, 1, n6))


def _seg_rsqrt(x2, m_ref):
    return lax.rsqrt(jnp.dot((x2 * x2).astype(BF16), m_ref[...], preferred_element_type=F32) + EPS)


def _rope(x, cos, sin, half):
    lane = lax.broadcasted_iota(jnp.int32, x.shape, 1)
    lo = (lane % (2 * half)) < half
    partner = jnp.where(lo, pltpu.roll(x, LANE - half, 1), pltpu.roll(x, half, 1))
    return x * cos + partner * sin


def _inproj_kernel(x_ref, mod_ref, g1_ref, w_ref, gains_ref, gcq_ref, gckv_ref, wuq_ref, wukvk_ref,
                   wukvv_ref, mm_ref, mx_ref, mbb_ref, *rest, use_rope):
    if use_rope:
        rope_ref = rest[0]
        outs = rest[1:]
    else:
        rope_ref = None
        outs = rest
    qa_ref, ka_ref, va_ref, qb_ref, kb_ref, vb_ref, qc_ref, kc_ref, vc_ref = outs

    x = x_ref[0]
    y = x * lax.rsqrt(jnp.mean(x * x, axis=-1, keepdims=True) + EPS) * g1_ref[...]
    h = y * (1.0 + mod_ref[0, 0, 1:2, :]) + mod_ref[0, 0, 0:1, :]
    p = jnp.dot(h.astype(BF16), w_ref[...], preferred_element_type=F32)

    def head_pair(x2, m_ref, gain_rows, rope_cols, half):
        r = _seg_rsqrt(x2, m_ref)
        out = []
        for i in range(2):
            sl = slice(i * LANE, (i + 1) * LANE)
            y = x2[:, sl] * r[:, sl] * gains_ref[gain_rows[i]:gain_rows[i] + 1, :]
            if rope_ref is not None and rope_cols[i] is not None:
                c0, s0 = rope_cols[i]
                y = _rope(y, rope_ref[:, c0:c0 + LANE], rope_ref[:, s0:s0 + LANE], half[i])
            out.append(y)
        return out

    r64 = (0, 128)
    rb = (256, 384)
    qa0, qa1 = head_pair(p[:, _QA:_QA + 256], mm_ref, (0, 0), (r64, r64), (16, 16))
    qa2, ka = head_pair(p[:, _QA + 256:_QA + 512], mm_ref, (0, 1), (r64, r64), (16, 16))
    for i, blk in enumerate((qa0, qa1, qa2)):
        qa_ref[0, i] = blk.astype(BF16)
    ka_ref[0, 0] = ka.astype(BF16)
    va_ref[0, 0] = p[:, _VA:_VA + LANE].astype(BF16)
    qc0, qc1 = head_pair(p[:, _QC:_QC + 256], mm_ref, (2, 2), (r64, r64), (16, 16))
    qc_ref[0, :, 0:LANE] = qc0.astype(BF16)
    qc_ref[0, :, LANE:2 * LANE] = qc1.astype(BF16)
    kc, kpe = head_pair(jnp.concatenate([p[:, _KC:_KC + LANE], p[:, _BKR:_BKR + LANE]], axis=1),
                        mx_ref, (3, 6), (r64, rb), (16, 8))
    kc_ref[0] = kc.astype(BF16)
    vc_ref[0] = p[:, _VC:_VC + LANE].astype(BF16)

    cq = p[:, _BCQ:_BCQ + 256]
    cq = cq * lax.rsqrt(jnp.sum(cq * cq, axis=-1, keepdims=True) * (1.0 / B_Q_RANK) + EPS) * gcq_ref[...]
    qb = jnp.dot(cq.astype(BF16), wuq_ref[...], preferred_element_type=F32)
    ckv = p[:, _BCKV:_BCKV + LANE]
    ckv = ckv * lax.rsqrt(jnp.mean(ckv * ckv, axis=-1, keepdims=True) + EPS) * gckv_ref[...]
    ckv = ckv.astype(BF16)
    kbn = jnp.dot(ckv, wukvk_ref[...], preferred_element_type=F32)
    vbv = jnp.dot(ckv, wukvv_ref[...], preferred_element_type=F32).astype(BF16)
    for i in range(B_HEADS // 2):
        vb_ref[0, i] = vbv[:, i * LANE:(i + 1) * LANE]
    for i in range(B_HEADS // 2):
        sl = slice(2 * i * LANE, (2 * i + 2) * LANE)
        q0, q1 = head_pair(qb[:, sl], mbb_ref, (4, 4), (rb, rb), (8, 8))
        k0, k1 = head_pair(kbn[:, sl], mbb_ref, (5, 5), (None, None), (8, 8))
        qb_ref[0, 2 * i] = q0.astype(BF16)
        qb_ref[0, 2 * i + 1] = q1.astype(BF16)
        kb_ref[0, 2 * i] = (k0 + kpe).astype(BF16)
        kb_ref[0, 2 * i + 1] = (k1 + kpe).astype(BF16)


def _inproj(xs, modv, kind, g1, w_in_p, gains, gcq, gckv, wuq, wukvk, wukvv, mm, mx, mbb, rope):
    b, n, d = xs.shape
    tm = min(PROJ_TILE, n)
    use_rope = rope is not None
    const = lambda *shape: pl.BlockSpec(shape, lambda i, j: (0,) * len(shape))
    in_specs = [pl.BlockSpec((1, tm, d), lambda i, j: (i, j, 0)),
                pl.BlockSpec((1, 1, 8, d), lambda i, j: (i, kind, 0, 0)),
                const(1, d), const(d, IN_PAD), const(8, LANE), const(1, 256), const(1, LANE),
                const(256, 768), const(LANE, 768), const(LANE, 384),
                const(256, 256), const(256, 256), const(256, 256)]
    args = [xs, modv, g1, w_in_p, gains, gcq, gckv, wuq, wukvk, wukvv, mm, mx, mbb]
    if use_rope:
        in_specs.append(pl.BlockSpec((tm, 512), lambda i, j: (j, 0)))
        args.append(rope)
    blocks = (3, 1, 1, B_HEADS, B_HEADS, B_HEADS // 2)
    widths = (256, 128, 128)
    out_specs = ([pl.BlockSpec((1, nb, tm, LANE), lambda i, j: (i, 0, j, 0)) for nb in blocks]
                 + [pl.BlockSpec((1, tm, w), lambda i, j: (i, j, 0)) for w in widths])
    out_shape = ([jax.ShapeDtypeStruct((b, nb, n, LANE), BF16) for nb in blocks]
                 + [jax.ShapeDtypeStruct((b, n, w), BF16) for w in widths])
    return pl.pallas_call(
        functools.partial(_inproj_kernel, use_rope=use_rope),
        grid=(b, n // tm),
        in_specs=in_specs,
        out_specs=out_specs,
        out_shape=out_shape,
        compiler_params=_cparams(("parallel", "parallel")),
        name="inproj_rope" if use_rope else "inproj_ctx",
    )(*args)


_A_MAPS = (lambda h: h % 3, lambda h: 0, lambda h: 0)
_B_MAPS = (lambda h: h, lambda h: h, lambda h: h // 2)


def _qk(q, k):
    return lax.dot_general(q, k, (((1,), (1,)), ((), ())), preferred_element_type=F32)


def _half_mask(q, half):
    lane = lax.broadcasted_iota(jnp.int32, q.shape, 1)
    return jnp.where((lane // 64) == half, q, jnp.zeros_like(q))


def _score_pass(q, chunks, s_w):
    mx = None
    for k_ref, _, st, sz, off in chunks:
        s = _qk(q, k_ref[0, 0, st:st + sz, :])
        s_w[:, off:off + sz] = s
        for t in range(sz // LANE):
            blk = s[:, t * LANE:(t + 1) * LANE]
            mx = blk if mx is None else jnp.maximum(mx, blk)
    return jnp.broadcast_to(jnp.max(mx, axis=1, keepdims=True), mx.shape)


def _value_pass(chunks, s_r, mb):
    acc = None
    for _, v_ref, st, sz, off in chunks:
        ps = [jnp.exp2(s_r[:, off + t * LANE:off + (t + 1) * LANE] - mb).astype(BF16)
              for t in range(sz // LANE)]
        v_ones = jnp.concatenate([v_ref[0, 0, st:st + sz, :], jnp.ones((sz, LANE), BF16)], axis=1)
        part = jnp.dot(jnp.concatenate(ps, axis=1), v_ones, preferred_element_type=F32)
        acc = part if acc is None else acc + part
    return (acc[:, :LANE] / acc[:, LANE:LANE + 1]).astype(BF16)


def _fused_passes(q, chunks, s_w, s_r, mb_r):
    mx = None
    acc = None
    for k_ref, v_ref, st, sz, off in chunks:
        s = _qk(q, k_ref[0, 0, st:st + sz, :])
        s_w[:, off:off + sz] = s
        for t in range(sz // LANE):
            blk = s[:, t * LANE:(t + 1) * LANE]
            mx = blk if mx is None else jnp.maximum(mx, blk)
        ps = [jnp.exp2(s_r[:, off + t * LANE:off + (t + 1) * LANE] - mb_r).astype(BF16)
              for t in range(sz // LANE)]
        v_ones = jnp.concatenate([v_ref[0, 0, st:st + sz, :], jnp.ones((sz, LANE), BF16)], axis=1)
        part = jnp.dot(jnp.concatenate(ps, axis=1), v_ones, preferred_element_type=F32)
        acc = part if acc is None else acc + part
    mb_new = jnp.broadcast_to(jnp.max(mx, axis=1, keepdims=True), mx.shape)
    return mb_new, (acc[:, :LANE] / acc[:, LANE:LANE + 1]).astype(BF16)


def _key_chunks(kc_ref, vc_ref, kl_ref, vl_ref):
    n_ctx = kc_ref.shape[2]
    chunks = [(kc_ref, vc_ref, 0, n_ctx, 0)]
    if kl_ref is not None:
        for c in range(kl_ref.shape[2] // KEY_CHUNK):
            chunks.append((kl_ref, vl_ref, c * KEY_CHUNK, KEY_CHUNK, n_ctx + c * KEY_CHUNK))
    return chunks


def _dense_kernel(q_ref, kc_ref, kl_ref, vcp_ref, vlp_ref, vcc_ref, vlc_ref, o_ref,
                  sa_sc, sb_sc, mba_sc, mbb_sc, *, mask_q, n_tiles, tiles_per_sample, tiles_per_head, n_sub):
    step = pl.program_id(0)
    tq = sa_sc.shape[0]

    @pl.when(step == 0)
    def _():
        sa_sc[...] = jnp.zeros(sa_sc.shape, F32)
        mba_sc[...] = jnp.zeros(mba_sc.shape, F32)

    if mask_q:
        first = jnp.minimum(n_sub * step, n_tiles - n_sub)
        half = ((first % tiles_per_sample) // tiles_per_head) // 3
    for k in range(n_sub):
        rows = slice(k * tq, (k + 1) * tq)
        q = q_ref[0, 0, rows, :]
        if mask_q:
            q = _half_mask(q, half)
        v_refs = (vcp_ref, vlp_ref) if k == 0 else (vcc_ref, vlc_ref)
        chunks = _key_chunks(kc_ref, v_refs[0], kl_ref, v_refs[1])
        s_w, s_r, mb_w, mb_r = ((sb_sc, sa_sc, mbb_sc, mba_sc) if k % 2 == 0
                                else (sa_sc, sb_sc, mba_sc, mbb_sc))
        mb_new, o_ref[rows, :] = _fused_passes(q, chunks, s_w, s_r, mb_r[...])
        mb_w[...] = mb_new


def _dense_attention(q, kc, vc, kl, vl, n_heads, maps, mask_q, name):
    b, _, nq, _ = q.shape
    qmap, kmap, vmap = maps
    n_ctx, n_lat = kc.shape[2], kl.shape[2]
    tq = Q_TILE
    tph = nq // tq
    n_sub = 4 if tph % 4 == 0 else 2
    assert tph % n_sub == 0
    tps = n_heads * tph
    n_tiles = b * tps
    first = lambda s: jnp.minimum(n_sub * s, n_tiles - n_sub)
    before = lambda s: jnp.maximum(n_sub * s - 1, 0)
    kv_blk = lambda n, fmap, tile: pl.BlockSpec(
        (1, 1, n, LANE), lambda s: (tile(s) // tps, fmap((tile(s) % tps) // tph), 0, 0))
    q_spec = pl.BlockSpec(
        (1, 1, n_sub * tq, LANE),
        lambda s: (first(s) // tps, qmap((first(s) % tps) // tph), (first(s) % tph) // n_sub, 0))
    return pl.pallas_call(
        functools.partial(_dense_kernel, mask_q=mask_q, n_tiles=n_tiles, tiles_per_sample=tps,
                          tiles_per_head=tph, n_sub=n_sub),
        grid=(n_tiles // n_sub + 1,),
        in_specs=[q_spec, kv_blk(n_ctx, kmap, first), kv_blk(n_lat, kmap, first),
                  kv_blk(n_ctx, vmap, before), kv_blk(n_lat, vmap, before),
                  kv_blk(n_ctx, vmap, first), kv_blk(n_lat, vmap, first)],
        out_specs=pl.BlockSpec((n_sub * tq, LANE), lambda s: (s, 0)),
        out_shape=jax.ShapeDtypeStruct(((n_tiles + n_sub) * tq, LANE), BF16),
        scratch_shapes=[pltpu.VMEM((tq, n_ctx + n_lat), F32), pltpu.VMEM((tq, n_ctx + n_lat), F32),
                        pltpu.VMEM((tq, LANE), F32), pltpu.VMEM((tq, LANE), F32)],
        compiler_params=_cparams(("arbitrary",)),
        name=name,
    )(q, kc, kl, vc, vl, vc, vl)


def _dense_ctx_kernel(q_ref, kc_ref, vc_ref, o_ref, s_sc, *, mask_q):
    q = q_ref[0, 0]
    if mask_q:
        q = _half_mask(q, pl.program_id(1) // 3)
    chunks = _key_chunks(kc_ref, vc_ref, None, None)
    mb = _score_pass(q, chunks, s_sc)
    o_ref[0, 0] = _value_pass(chunks, s_sc, mb)


def _dense_attention_ctx(q, kc, vc, n_heads, maps, mask_q, name):
    b, _, nq, _ = q.shape
    qmap, kmap, vmap = maps
    blk = lambda fmap: pl.BlockSpec((1, 1, nq, LANE), lambda i, h: (i, fmap(h), 0, 0))
    return pl.pallas_call(
        functools.partial(_dense_ctx_kernel, mask_q=mask_q),
        grid=(b, n_heads),
        in_specs=[blk(qmap), blk(kmap), blk(vmap)],
        out_specs=pl.BlockSpec((1, 1, nq, LANE), lambda i, h: (i, h, 0, 0)),
        out_shape=jax.ShapeDtypeStruct((b, n_heads, nq, LANE), BF16),
        scratch_shapes=[pltpu.VMEM((nq, kc.shape[2]), F32)],
        compiler_params=_cparams(("parallel", "parallel")),
        name=name,
    )(q, kc, vc)


def _window_kernel(q_ref, kc_ref, vc_ref, sink_ref, *rest, use_window):
    if use_window:
        kl_ref, vl_ref, o_ref = rest
    else:
        (o_ref,) = rest
    tq = ROW_TILE
    span = tq + 2 * WINDOW
    kcx = kc_ref[0]
    ones_c = jnp.ones((kcx.shape[0], LANE), BF16)
    v_aug = jnp.concatenate([vc_ref[0], ones_c], axis=1)
    lane = lax.broadcasted_iota(jnp.int32, (tq, LANE), 1)
    for ti in range(q_ref.shape[1] // tq):
        rows = slice(ti * tq, (ti + 1) * tq)
        if use_window:
            n_lat = kl_ref.shape[1]
            t0 = (pl.program_id(1) * (q_ref.shape[1] // tq) + ti) * tq
            start = pl.multiple_of(jnp.clip(t0 - WINDOW, 0, n_lat - span), LANE)
            qpos = t0 + lax.broadcasted_iota(jnp.int32, (tq, span), 0)
            kpos = start + lax.broadcasted_iota(jnp.int32, (tq, span), 1)
            valid = jnp.abs(qpos - kpos) <= WINDOW
            kw = kl_ref[0, pl.ds(start, span), :]
            v_all = jnp.concatenate(
                [v_aug, jnp.concatenate([vl_ref[0, pl.ds(start, span), :], jnp.ones((span, LANE), BF16)], axis=1)],
                axis=0)
        else:
            v_all = v_aug
        res = []
        for h in range(C_HEADS):
            half = h // (C_HEADS // C_KV_HEADS)
            q = q_ref[0, rows, (h % 2) * LANE:(h % 2 + 1) * LANE]
            q = jnp.where((lane < 64) if half == 0 else (lane >= 64), q, jnp.zeros_like(q))
            sink = sink_ref[0:1, h:h + 1]
            s = _qk(q, kcx)
            if use_window:
                s = jnp.concatenate([s, jnp.where(valid, _qk(q, kw), NEG_INF)], axis=1)
            mx = s[:, 0:LANE]
            for t in range(1, s.shape[1] // LANE):
                mx = jnp.maximum(mx, s[:, t * LANE:(t + 1) * LANE])
            m = jnp.maximum(jnp.max(mx, axis=1, keepdims=True), sink)
            mb = jnp.broadcast_to(m, (tq, LANE))
            p = jnp.concatenate([jnp.exp2(s[:, t * LANE:(t + 1) * LANE] - mb).astype(BF16)
                                 for t in range(s.shape[1] // LANE)], axis=1)
            o = jnp.dot(p, v_all, preferred_element_type=F32)
            res.append(o[:, :LANE] / (o[:, LANE:LANE + 1] + jnp.exp2(sink - m)))
        for ob in range(2):
            o_ref[0, rows, ob * LANE:(ob + 1) * LANE] = jnp.where(lane < 64, res[ob], res[2 + ob]).astype(BF16)


def _window_attention(q, kc, vc, kl, vl, sink, name):
    b, nq, qw = q.shape
    tq = min(2 * ROW_TILE, nq)
    n_ctx = kc.shape[1]
    use_window = kl is not None
    in_specs = [pl.BlockSpec((1, tq, qw), lambda i, j: (i, j, 0)),
                pl.BlockSpec((1, n_ctx, LANE), lambda i, j: (i, 0, 0)),
                pl.BlockSpec((1, n_ctx, LANE), lambda i, j: (i, 0, 0)),
                pl.BlockSpec((8, LANE), lambda i, j: (0, 0))]
    args = [q, kc, vc, sink]
    if use_window:
        n_lat = kl.shape[1]
        in_specs += [pl.BlockSpec((1, n_lat, LANE), lambda i, j: (i, 0, 0)),
                     pl.BlockSpec((1, n_lat, LANE), lambda i, j: (i, 0, 0))]
        args += [kl, vl]
    return pl.pallas_call(
        functools.partial(_window_kernel, use_window=use_window),
        grid=(b, nq // tq),
        in_specs=in_specs,
        out_specs=pl.BlockSpec((1, tq, qw), lambda i, j: (i, j, 0)),
        out_shape=jax.ShapeDtypeStruct((b, nq, qw), BF16),
        compiler_params=_cparams(("parallel", "parallel")),
        name=name,
    )(*args)


def _outproj_kernel(*refs):
    oa = refs[0:A_HEADS]
    ob = refs[A_HEADS:A_HEADS + B_HEADS]
    oc_ref, x_ref, mod_ref, g2n_ref, w_ref, wr_ref, xo_ref, h2_ref, aff_ref = refs[A_HEADS + B_HEADS:]
    tm = x_ref.shape[1]
    low = lax.broadcasted_iota(jnp.int32, (tm, LANE), 1) < 64
    parts = [jnp.where(low, oa[i][...], oa[3 + i][...]) for i in range(3)]
    parts += [jnp.where(low, ob[2 * i][...], ob[2 * i + 1][...]) for i in range(3)]
    parts.append(oc_ref[0])
    mix = jnp.dot(jnp.concatenate(parts, axis=1), w_ref[...], preferred_element_type=F32)
    x = x_ref[0] + mod_ref[0, 0, 2:3, :] * mix
    xo_ref[0] = x
    y = x * lax.rsqrt(jnp.mean(x * x, axis=-1, keepdims=True) + EPS) * g2n_ref[...]
    h2 = y * (1.0 + mod_ref[0, 0, 4:5, :]) + mod_ref[0, 0, 3:4, :]
    h2_ref[0] = h2.astype(BF16)
    wr = wr_ref[...]
    wr_hi = wr.astype(BF16)
    wr_lo = (wr - wr_hi.astype(F32)).astype(BF16)
    h2_hi = h2.astype(BF16)
    h2_lo = (h2 - h2_hi.astype(F32)).astype(BF16)
    both = jnp.dot(h2_hi, jnp.concatenate([wr_hi, wr_lo], axis=1), preferred_element_type=F32)
    logits = both[:, :LANE] + both[:, LANE:] + jnp.dot(h2_lo, wr_hi, preferred_element_type=F32)
    logits = logits.T[0:aff_ref.shape[1], :]
    z = jnp.exp(logits - jnp.max(logits, axis=0, keepdims=True))
    aff_ref[0] = z / jnp.sum(z, axis=0, keepdims=True)


def _outproj(oa, ob, oc, xs, modv, kind, g2n, w_mix, wr_t, pad_rows, name):
    b, n, d = xs.shape
    tm = min(PROJ_TILE, n)
    e = N_EXPERTS
    const = lambda *shape: pl.BlockSpec(shape, lambda i, j: (0,) * len(shape))
    row = lambda w: pl.BlockSpec((1, tm, w), lambda i, j: (i, j, 0))
    n_heads = A_HEADS
    head = lambda h: pl.BlockSpec((tm, LANE), lambda i, j: ((pad_rows + (i * n_heads + h) * n) // tm + j, 0))
    return pl.pallas_call(
        _outproj_kernel,
        grid=(b, n // tm),
        in_specs=[head(h) for h in range(A_HEADS)] + [head(h) for h in range(B_HEADS)] + [
                  row(256), row(d),
                  pl.BlockSpec((1, 1, 8, d), lambda i, j: (i, kind, 0, 0)),
                  const(1, d), const(d, d), const(d, LANE)],
        out_specs=[row(d), row(d), pl.BlockSpec((1, e, tm), lambda i, j: (i, 0, j))],
        out_shape=[jax.ShapeDtypeStruct((b, n, d), F32), jax.ShapeDtypeStruct((b, n, d), BF16),
                   jax.ShapeDtypeStruct((b, e, n), F32)],
        compiler_params=_cparams(("parallel", "parallel")),
        name=name,
    )(*([oa] * A_HEADS), *([ob] * B_HEADS), oc, xs, modv, g2n, w_mix, wr_t)


def _prefix_exclusive(mask_f, tri):
    e, t = mask_f.shape
    ck = tri.shape[0]
    carry = jnp.zeros((e, 1), F32)
    parts, offs = [], [carry]
    for c in range(t // ck):
        blk = mask_f[:, c * ck:(c + 1) * ck]
        parts.append(jnp.dot(blk.astype(BF16), tri, preferred_element_type=F32) + carry)
        carry = carry + jnp.sum(blk, axis=1, keepdims=True)
        offs.append(carry)
    return jnp.concatenate(parts, axis=1) if len(parts) > 1 else parts[0], offs


def _topk_kernel(aff_ref, slot_ref, cnt_ref, *, cap, slot_stride):
    aff = aff_ref[0]
    e, t = aff.shape
    bits = pltpu.bitcast(aff, jnp.int32)
    kf = jnp.float32(cap)

    def search(i, lo):
        cand = lo | (jnp.int32(1) << (30 - i))
        n_ge = jnp.sum((bits >= cand).astype(F32), axis=1, keepdims=True)
        return jnp.where(n_ge >= kf, cand, lo)

    thr = lax.fori_loop(0, 31, search, jnp.zeros((e, 1), jnp.int32))
    gt = bits > thr
    eq = bits == thr
    need = kf - jnp.sum(gt.astype(F32), axis=1, keepdims=True)

    r = lax.broadcasted_iota(jnp.int32, (TOK_CHUNK, TOK_CHUNK), 0)
    c = lax.broadcasted_iota(jnp.int32, (TOK_CHUNK, TOK_CHUNK), 1)
    tri = (r < c).astype(BF16)
    tie_rank, _ = _prefix_exclusive(eq.astype(F32), tri)
    sel = gt | (eq & (tie_rank < need))
    slot, offs = _prefix_exclusive(sel.astype(F32), tri)
    base = pl.program_id(0) * slot_stride
    slot_ref[0] = jnp.where(sel, slot.astype(jnp.int32) + base, -1)
    lane = lax.broadcasted_iota(jnp.int32, (e, LANE), 1)
    cnt = jnp.zeros((e, LANE), jnp.int32)
    for ci, off in enumerate(offs):
        cnt = jnp.where(lane == ci, off.astype(jnp.int32) + base, cnt)
    cnt_ref[0] = cnt


def _topk(aff_t, cap, slot_stride, name):
    b, e, t = aff_t.shape
    return pl.pallas_call(
        functools.partial(_topk_kernel, cap=cap, slot_stride=slot_stride),
        grid=(b,),
        in_specs=[pl.BlockSpec((1, e, t), lambda i: (i, 0, 0))],
        out_specs=[pl.BlockSpec((1, e, t), lambda i: (i, 0, 0)),
                   pl.BlockSpec((1, e, LANE), lambda i: (i, 0, 0))],
        out_shape=[jax.ShapeDtypeStruct((b, e, t), jnp.int32), jax.ShapeDtypeStruct((b, e, LANE), jnp.int32)],
        compiler_params=_cparams(("parallel",)),
        name=name,
    )(aff_t)


def _moe_kernel(cnt_ref, h_ref, slot_ref, slotp_ref, gate_ref, wg_ref, wu_ref, wd_ref, mod_ref, x_hbm, acc_ref,
                xe_sc, gs_sc, ye_sc, x_sem, *, n_chunks, n_sb):
    bi = pl.program_id(0)
    ei = pl.program_id(1)
    n_e = pl.num_programs(1)
    x_copy = pltpu.make_async_copy(x_hbm.at[bi], acc_ref.at[0], x_sem)

    @pl.when(ei == 0)
    def _():
        x_copy.start()

    def wait_for_x(at_expert):
        @pl.when(ei == at_expert)
        def _():
            x_copy.wait()

    g2 = mod_ref[0, 0, 5:6, :]
    cap = n_sb * SLOT_BLOCK

    def counts_of(e):
        base = (bi * n_e + e) * 32
        return [cnt_ref[base + c] for c in range(n_chunks + 1)]

    def all_chunks_at_most(cs, limit):
        ok = cs[1] - cs[0] <= limit
        for c in range(1, n_chunks):
            ok = ok & (cs[c + 1] - cs[c] <= limit)
        return ok

    odd = (ei % 2) == 1
    counts = counts_of(ei)
    counts_other = counts_of(jnp.where(odd, ei - 1, ei + 1))
    pair_narrow = all_chunks_at_most(counts, PAIR_LIMIT) & all_chunks_at_most(counts_other, PAIR_LIMIT)

    def ffn():
        xe = xe_sc[0:cap, :].astype(BF16)
        f = wg_ref.shape[2]
        fb = 512
        y = None
        for f0 in range(0, f, fb):
            hg = jnp.dot(xe, wg_ref[0, :, f0:f0 + fb], preferred_element_type=F32)
            hu = jnp.dot(xe, wu_ref[0, :, f0:f0 + fb], preferred_element_type=F32)
            hid = (hg * jax.nn.sigmoid(hg) * hu).astype(BF16)
            part = jnp.dot(hid, wd_ref[0, f0:f0 + fb, :], preferred_element_type=F32)
            y = part if y is None else y + part
        return y

    def tok(c):
        return slice(c * TOK_CHUNK, (c + 1) * TOK_CHUNK)

    def slot_columns(ref):
        pad = jnp.zeros((LANE - n_chunks, TOK_CHUNK), F32)
        return jnp.concatenate([ref[0, 0].astype(F32), pad], axis=0).T

    @pl.when(pair_narrow)
    def _():
        xe_sc[...] = jnp.zeros_like(xe_sc)
        gs_sc[...] = jnp.zeros_like(gs_sc)
        srow = lax.broadcasted_iota(jnp.int32, (GATHER_WINDOW, TOK_CHUNK), 0)
        for c in range(n_chunks):
            g0 = pl.multiple_of((counts[c] // 8) * 8, 8)
            oh = (slot_ref[0, 0, c:c + 1, :] - g0) == srow
            xe_sc[pl.ds(g0, GATHER_WINDOW), :] += jnp.dot(
                oh.astype(BF16), h_ref[0, tok(c), :], preferred_element_type=F32)
            gs_sc[pl.ds(g0, GATHER_WINDOW), :] += jnp.sum(
                jnp.where(oh, gate_ref[0, 0, c:c + 1, :], 0.0), axis=1, keepdims=True)
        ye = ye_sc.at[ei % 2]
        ye[0:cap, :] = (ffn() * gs_sc[0:cap, :]).astype(BF16)
        ye[cap:cap + SLOT_BLOCK, :] = jnp.zeros((SLOT_BLOCK, ye_sc.shape[2]), BF16)

        @pl.when(odd)
        def _():
            wait_for_x(1)
            cols_prev = slot_columns(slotp_ref)
            cols_cur = slot_columns(slot_ref)
            scol = lax.broadcasted_iota(jnp.int32, (TOK_CHUNK, SLOT_BLOCK), 1).astype(F32)
            for c in range(n_chunks):
                w_prev = pl.multiple_of((counts_other[c] // 16) * 16, 16)
                w_cur = pl.multiple_of((counts[c] // 16) * 16, 16)
                oh_t = jnp.concatenate(
                    [((cols_prev[:, c:c + 1] - w_prev.astype(F32)) == scol).astype(BF16),
                     ((cols_cur[:, c:c + 1] - w_cur.astype(F32)) == scol).astype(BF16)], axis=1)
                rows = jnp.concatenate([ye_sc[0, pl.ds(w_prev, SLOT_BLOCK), :],
                                        ye_sc[1, pl.ds(w_cur, SLOT_BLOCK), :]], axis=0)
                acc_ref[0, tok(c), :] += g2 * jnp.dot(oh_t, rows, preferred_element_type=F32)

    @pl.when(jnp.logical_not(pair_narrow))
    def _():
        xe_sc[...] = jnp.zeros_like(xe_sc)
        gs_sc[...] = jnp.zeros_like(gs_sc)
        srow = lax.broadcasted_iota(jnp.int32, (SLOT_BLOCK, TOK_CHUNK), 0)
        base = (bi * n_e + ei) * 32

        def overlap(c, sb):
            return (cnt_ref[base + c] < (sb + 1) * SLOT_BLOCK) & (cnt_ref[base + c + 1] > sb * SLOT_BLOCK)

        def onehot(c, sb):
            return slot_ref[0, 0, pl.ds(c, 1), :] == (srow + sb * SLOT_BLOCK)

        def dyn_tok(c):
            return pl.ds(pl.multiple_of(c * TOK_CHUNK, TOK_CHUNK), TOK_CHUNK)

        def gather_chunk(c, carry):
            for sb in range(n_sb):
                @pl.when(overlap(c, sb))
                def _(sb=sb):
                    oh = onehot(c, sb)
                    rows = slice(sb * SLOT_BLOCK, (sb + 1) * SLOT_BLOCK)
                    xe_sc[rows] += jnp.dot(oh.astype(BF16), h_ref[0, dyn_tok(c), :], preferred_element_type=F32)
                    g = jnp.where(oh, gate_ref[0, 0, pl.ds(c, 1), :], 0.0)
                    gs_sc[rows] += jnp.sum(g, axis=1, keepdims=True)
            return carry

        lax.fori_loop(0, n_chunks, gather_chunk, 0)
        ye_sc[0, 0:cap, :] = (ffn() * gs_sc[0:cap, :]).astype(BF16)
        wait_for_x(0)

        def scatter_chunk(c, carry):
            for sb in range(n_sb):
                @pl.when(overlap(c, sb))
                def _(sb=sb):
                    oh_t = onehot(c, sb).astype(F32).T.astype(BF16)
                    rows = slice(sb * SLOT_BLOCK, (sb + 1) * SLOT_BLOCK)
                    acc_ref[0, dyn_tok(c), :] += g2 * jnp.dot(oh_t, ye_sc[0, rows, :],
                                                              preferred_element_type=F32)
            return carry

        lax.fori_loop(0, n_chunks, scatter_chunk, 0)


def _moe(h2, slot, gate, cnt, wg, wu, wd, xs, modv, kind, cap, name):
    b, t, d = h2.shape
    e = slot.shape[1]
    f = wg.shape[2]
    n_chunks = t // TOK_CHUNK
    n_sb = cap // SLOT_BLOCK
    slot4 = slot.reshape(b, e, n_chunks, TOK_CHUNK)
    gate4 = gate.reshape(b, e, n_chunks, TOK_CHUNK)
    cnt_flat = cnt[:, :, :32].reshape(-1)
    grid_spec = pltpu.PrefetchScalarGridSpec(
        num_scalar_prefetch=1,
        grid=(b, e),
        in_specs=[pl.BlockSpec((1, t, d), lambda i, j, s: (i, 0, 0)),
                  pl.BlockSpec((1, 1, n_chunks, TOK_CHUNK), lambda i, j, s: (i, j, 0, 0)),
                  pl.BlockSpec((1, 1, n_chunks, TOK_CHUNK), lambda i, j, s: (i, jnp.maximum(j - 1, 0), 0, 0)),
                  pl.BlockSpec((1, 1, n_chunks, TOK_CHUNK), lambda i, j, s: (i, j, 0, 0)),
                  pl.BlockSpec((1, d, f), lambda i, j, s: (j, 0, 0)),
                  pl.BlockSpec((1, d, f), lambda i, j, s: (j, 0, 0)),
                  pl.BlockSpec((1, f, d), lambda i, j, s: (j, 0, 0)),
                  pl.BlockSpec((1, 1, 8, d), lambda i, j, s: (i, kind, 0, 0)),
                  pl.BlockSpec(memory_space=pl.ANY)],
        out_specs=pl.BlockSpec((1, t, d), lambda i, j, s: (i, 0, 0), pipeline_mode=pl.Buffered(1)),
        scratch_shapes=[pltpu.VMEM((cap + 2 * SLOT_BLOCK, d), F32), pltpu.VMEM((cap + 2 * SLOT_BLOCK, 1), F32),
                        pltpu.VMEM((2, cap + SLOT_BLOCK, d), BF16), pltpu.SemaphoreType.DMA(())],
    )
    return pl.pallas_call(
        functools.partial(_moe_kernel, n_chunks=n_chunks, n_sb=n_sb),
        grid_spec=grid_spec,
        out_shape=jax.ShapeDtypeStruct((b, t, d), F32),
        compiler_params=_cparams(("parallel", "arbitrary")),
        name=name,
    )(cnt_flat, h2, slot4, slot4, gate4, wg, wu, wd, modv, xs)


def _cast_kernel(w_ref, o_ref):
    o_ref[...] = w_ref[0].astype(BF16)


def _layer_bf16(w, layer):
    _, e, r, c = w.shape
    return pl.pallas_call(
        _cast_kernel,
        grid=(e,),
        in_specs=[pl.BlockSpec((1, 1, r, c), lambda i: (layer, i, 0, 0))],
        out_specs=pl.BlockSpec((1, r, c), lambda i: (i, 0, 0)),
        out_shape=jax.ShapeDtypeStruct((e, r, c), BF16),
        compiler_params=_cparams(("parallel",)),
        name="expert_weights_bf16",
    )(w)


def _pad_lanes(v, width):
    return jnp.pad(v, (0, width - v.shape[0]))


def kernel(x, c, ctx, c_ctx, w_ada, b_ada, norm1_g, norm2_g, w_in, a_q_norm, a_k_norm, b_cq_norm, b_ckv_norm, w_uq, w_ukv, b_qn_norm, b_kn_norm, b_qr_norm, b_kr_norm, c_q_norm, c_k_norm, c_sink, w_out, w_router, w_e_gate, w_e_up, w_e_down):
    b, t, d = x.shape
    n_ctx = ctx.shape[1]
    depth = w_ada.shape[0]
    assert n_ctx == ROW_TILE and t % KEY_CHUNK == 0 and t % GRID_W == 0
    cap = max(1, CAP_FACTOR * t // N_EXPERTS)
    cap_c = max(1, CAP_FACTOR * n_ctx // N_EXPERTS)
    assert cap % SLOT_BLOCK == 0 and (b * cap_c) % SLOT_BLOCK == 0

    rows = ((b + 1 + 7) // 8) * 8
    cc = jnp.zeros((rows, d), F32).at[:b].set(c).at[b].set(c_ctx)
    mod = _modulation(cc, w_ada, b_ada)

    rope = _rope_tables(t)
    mm, mx, mbb = (jnp.asarray(m, BF16) for m in _seg_matrices())
    in_perm = _in_perm()
    uq_perm = _uq_perm()
    ukv_pk, ukv_pv = _ukv_perms()
    oa_rows = np.asarray(_pair_cols(0, A_HEADS))
    oc_rows = np.asarray(_pair_cols(0, C_HEADS)) + 768
    sa = HEAD_DIM ** -0.5
    sb = (B_NOPE + B_ROPE) ** -0.5
    z32, z64 = jnp.zeros((32,), F32), jnp.zeros((64,), F32)

    for l in range(depth):
        last = l == depth - 1
        m6 = mod[l].reshape(rows, 6, d)
        m8 = jnp.pad(m6, ((0, 0), (0, 2), (0, 0)))
        modv = jnp.stack([jnp.broadcast_to(m8[b][None], (b, 8, d)), m8[:b]], axis=1)

        w_in_p = _take_cols(w_in[l], in_perm).astype(BF16)
        wuq = jnp.pad(_take_cols(w_uq[l], uq_perm), ((0, 256 - B_Q_RANK), (0, 0))).astype(BF16)
        wukvk = _take_cols(w_ukv[l], ukv_pk).astype(BF16)
        wukvv = jnp.take(w_ukv[l], jnp.asarray(ukv_pv, jnp.int32), axis=1).astype(BF16)
        gains = jnp.stack([
            jnp.tile(a_q_norm[l], 2) * (sa * LOG2E), jnp.tile(a_k_norm[l], 2),
            jnp.tile(c_q_norm[l], 2) * (sa * LOG2E), jnp.tile(c_k_norm[l], 2),
            jnp.concatenate([b_qn_norm[l] * (sb * LOG2E), b_qr_norm[l] * (sb * LOG2E), z32]),
            jnp.concatenate([b_kn_norm[l], z64]),
            jnp.concatenate([z64, b_kr_norm[l], z32]),
            jnp.zeros((LANE,), F32)])
        gcq = _pad_lanes(b_cq_norm[l], 256)[None]
        gckv = b_ckv_norm[l][None]
        g1 = norm1_g[l][None]
        g2n = norm2_g[l][None]
        mix_rows = np.concatenate([oa_rows, np.arange(384, 768), oc_rows])
        w_mix = jnp.take(w_out[l], jnp.asarray(mix_rows, jnp.int32), axis=0).astype(BF16)
        wr_t = jnp.pad(w_router[l], ((0, 0), (0, LANE - N_EXPERTS)))
        sink = jnp.zeros((8, LANE), F32).at[0, :C_HEADS].set(c_sink[l] * LOG2E)
        wg = _layer_bf16(w_e_gate, l)
        wu = _layer_bf16(w_e_up, l)
        wd = _layer_bf16(w_e_down, l)

        shared = (g1, w_in_p, gains, gcq, gckv, wuq, wukvk, wukvv, mm, mx, mbb)
        qa, ka, va, qb, kb, vb, qc, kc, vc = _inproj(x, modv, 1, *shared, rope)
        qac, kac, vac, qbc, kbc, vbc, qcc, kcc, vcc = _inproj(ctx, modv, 0, *shared, None)

        o_a = _dense_attention(qa, kac, vac, ka, va, A_HEADS, _A_MAPS, True, "attn_a")
        o_b = _dense_attention(qb, kbc, vbc, kb, vb, B_HEADS, _B_MAPS, False, "attn_b")
        o_c = _window_attention(qc, kcc, vcc, kc, vc, sink, "attn_c")
        x_mid, h2, aff = _outproj(o_a, o_b, o_c, x, modv, 1, g2n, w_mix, wr_t, Q_TILE, "outproj")
        slot, cnt = _topk(aff, cap, 0, "topk")
        x = _moe(h2, slot, aff, cnt, wg, wu, wd, x_mid, modv, 1, cap, "moe")

        if not last:
            o_ac = _dense_attention_ctx(qac, kac, vac, A_HEADS, _A_MAPS, True, "attn_a_ctx")
            o_bc = _dense_attention_ctx(qbc, kbc, vbc, B_HEADS, _B_MAPS, False, "attn_b_ctx")
            o_cc = _window_attention(qcc, kcc, vcc, None, None, sink, "attn_c_ctx")
            flat = lambda o: o.reshape(-1, LANE)
            c_mid, hc2, aff_c = _outproj(flat(o_ac), flat(o_bc), o_cc, ctx, modv, 0, g2n, w_mix, wr_t, 0,
                                         "outproj_ctx")
            slot_c, cnt_c = _topk(aff_c, cap_c, cap_c, "topk_ctx")
            e = N_EXPERTS
            slot_f = jnp.transpose(slot_c, (1, 0, 2)).reshape(1, e, b * n_ctx)
            gate_f = jnp.transpose(aff_c, (1, 0, 2)).reshape(1, e, b * n_ctx)
            starts = jnp.transpose(cnt_c[:, :, 0], (1, 0))
            cnt_f = jnp.concatenate([starts, cnt_c[b - 1, :, 1:2],
                                     jnp.zeros((e, LANE - b - 1), jnp.int32)], axis=1)[None]
            ctx = _moe(hc2.reshape(1, b * n_ctx, d), slot_f, gate_f, cnt_f, wg, wu, wd,
                       c_mid.reshape(1, b * n_ctx, d), modv, 0, b * cap_c, "moe_ctx").reshape(b, n_ctx, d)
    return x
```

```python
import functools
import math

import numpy as np
import jax
import jax.numpy as jnp
from jax import lax
from jax.experimental import pallas as pl
from jax.experimental.pallas import tpu as pltpu

F32 = jnp.float32
BF16 = jnp.bfloat16

GRID_W = 64
ROPE_THETA = 10000.0
EPS = 1e-6
NEG_INF = -1e30
HEAD_DIM = 64
A_HEADS, A_KV_HEADS = 6, 2
B_HEADS, B_Q_RANK, B_KV_RANK, B_NOPE, B_ROPE, B_V = 6, 192, 128, 64, 32, 64
C_HEADS, C_KV_HEADS = 4, 2
N_EXPERTS = 16
CAP_FACTOR = 2
WINDOW = 128

LANE = 128
ROW_TILE = 256
PROJ_TILE = 512
Q_TILE = 512
KEY_CHUNK = 512
LOG2E = 1.4426950408889634
SLOT_BLOCK = 128
TOK_CHUNK = 256
GATHER_WINDOW = 144
PAIR_LIMIT = SLOT_BLOCK - 16
VMEM_LIMIT = 56 * 1024 * 1024

_QA, _KA, _VA = 0, 384, 512
_QC, _KC, _VC = 640, 896, 1024
_BCQ, _BCKV, _BKR = 1152, 1408, 1536
IN_PAD = 1664
_ORIG = dict(aq=0, ak=384, av=512, bcq=640, bckv=832, bkr=960, cq=992, ck=1248, cv=1376)


def _cparams(sem, vmem=VMEM_LIMIT):
    return pltpu.CompilerParams(dimension_semantics=sem, vmem_limit_bytes=vmem)


def _pair_cols(base, n_heads):
    half = n_heads // 2
    cols = []
    for i in range(half):
        cols += list(range(base + i * 64, base + (i + 1) * 64))
        cols += list(range(base + (half + i) * 64, base + (half + i + 1) * 64))
    return cols


def _in_perm():
    perm = -np.ones((IN_PAD,), np.int64)
    perm[_QA:_QA + 384] = _pair_cols(_ORIG["aq"], A_HEADS)
    perm[_KA:_KA + 128] = np.arange(_ORIG["ak"], _ORIG["ak"] + 128)
    perm[_VA:_VA + 128] = np.arange(_ORIG["av"], _ORIG["av"] + 128)
    perm[_QC:_QC + 256] = _pair_cols(_ORIG["cq"], C_HEADS)
    perm[_KC:_KC + 128] = np.arange(_ORIG["ck"], _ORIG["ck"] + 128)
    perm[_VC:_VC + 128] = np.arange(_ORIG["cv"], _ORIG["cv"] + 128)
    perm[_BCQ:_BCQ + B_Q_RANK] = np.arange(_ORIG["bcq"], _ORIG["bcq"] + B_Q_RANK)
    perm[_BCKV:_BCKV + 128] = np.arange(_ORIG["bckv"], _ORIG["bckv"] + 128)
    perm[_BKR + 64:_BKR + 96] = np.arange(_ORIG["bkr"], _ORIG["bkr"] + 32)
    return perm


def _take_cols(w, perm):
    safe = np.where(perm >= 0, perm, 0)
    out = jnp.take(w, jnp.asarray(safe, jnp.int32), axis=1)
    return jnp.where(jnp.asarray(perm >= 0)[None, :], out, 0.0)


def _uq_perm():
    perm = -np.ones((B_HEADS * LANE,), np.int64)
    for h in range(B_HEADS):
        perm[h * LANE:h * LANE + 96] = np.arange(h * 96, h * 96 + 96)
    return perm


def _ukv_perms():
    pk = -np.ones((B_HEADS * LANE,), np.int64)
    pv = np.zeros((B_HEADS * 64,), np.int64)
    for h in range(B_HEADS):
        pk[h * LANE:h * LANE + 64] = np.arange(h * 128, h * 128 + 64)
        pv[h * 64:(h + 1) * 64] = np.arange(h * 128 + 64, h * 128 + 128)
    return pk, pv


def _seg_matrices():
    i = np.arange(LANE)
    m64 = (i[:, None] // 64 == i[None, :] // 64).astype(np.float32) / 64.0
    mb = np.zeros((LANE, LANE), np.float32)
    mb[:64, :64] = 1.0 / 64.0
    mb[64:96, 64:96] = 1.0 / 32.0
    def pair(a, b):
        out = np.zeros((2 * LANE, 2 * LANE), np.float32)
        out[:LANE, :LANE] = a
        out[LANE:, LANE:] = b
        return out

    return pair(m64, m64), pair(m64, mb), pair(mb, mb)


def _rope_tables(length):
    rows = length // GRID_W
    t = np.arange(rows * GRID_W)
    row = np.repeat(np.arange(rows), GRID_W).astype(np.float32)
    col = (t % GRID_W).astype(np.float32)

    def tabs(dim):
        axis_dim = dim // 2
        inv = (np.float32(ROPE_THETA) ** (-np.arange(0, axis_dim, 2, dtype=np.float32) / axis_dim)).astype(np.float32)
        ar = row[:, None] * inv[None, :]
        ac = col[:, None] * inv[None, :]
        cos = np.concatenate([np.cos(ar), np.cos(ar), np.cos(ac), np.cos(ac)], axis=-1)
        sin = np.concatenate([-np.sin(ar), np.sin(ar), -np.sin(ac), np.sin(ac)], axis=-1)
        return cos.astype(np.float32), sin.astype(np.float32)

    c64, s64 = tabs(HEAD_DIM)
    cb, sb = tabs(B_ROPE)
    n = c64.shape[0]
    ones64, zeros64 = np.ones((n, 64), np.float32), np.zeros((n, 64), np.float32)
    ones32, zeros32 = np.ones((n, 32), np.float32), np.zeros((n, 32), np.float32)
    return jnp.asarray(np.concatenate([c64, c64, s64, s64,
                                       ones64, cb, ones32, zeros64, sb, zeros32], axis=-1))


def _mod_kernel(c_ref, w_ref, b_ref, o_ref):
    cv = c_ref[...]
    a = (cv * jax.nn.sigmoid(cv)).astype(BF16)
    o_ref[0] = jnp.dot(a, w_ref[0].astype(BF16), preferred_element_type=F32) + b_ref[0]


def _modulation(cc, w_ada, b_ada):
    n_layers, d, n6 = w_ada.shape
    rows = cc.shape[0]
    tn = 1536
    return pl.pallas_call(
        _mod_kernel,
        grid=(n_layers, n6 // tn),
        in_specs=[pl.BlockSpec((rows, d), lambda l, j: (0, 0)),
                  pl.BlockSpec((1, d, tn), lambda l, j: (l, 0, j)),
                  pl.BlockSpec((1, 1, tn), lambda l, j: (l, 0, j))],
        out_specs=pl.BlockSpec((1, rows, tn), lambda l, j: (l, 0, j)),
        out_shape=jax.ShapeDtypeStruct((n_layers, rows, n6), F32),
        compiler_params=_cparams(("parallel", "parallel")),
        name="adaln_mod",
    )(cc, w_ada, b_ada.reshape(n_layers, 1, n6))


def _seg_rsqrt(x2, m_ref):
    return lax.rsqrt(jnp.dot((x2 * x2).astype(BF16), m_ref[...], preferred_element_type=F32) + EPS)


def _rope(x, cos, sin, half):
    lane = lax.broadcasted_iota(jnp.int32, x.shape, 1)
    lo = (lane % (2 * half)) < half
    partner = jnp.where(lo, pltpu.roll(x, LANE - half, 1), pltpu.roll(x, half, 1))
    return x * cos + partner * sin


def _inproj_kernel(x_ref, mod_ref, g1_ref, w_ref, gains_ref, gcq_ref, gckv_ref, wuq_ref, wukvk_ref,
                   wukvv_ref, mm_ref, mx_ref, mbb_ref, *rest, use_rope):
    if use_rope:
        rope_ref = rest[0]
        outs = rest[1:]
    else:
        rope_ref = None
        outs = rest
    qa_ref, ka_ref, va_ref, qb_ref, kb_ref, vb_ref, qc_ref, kc_ref, vc_ref = outs

    x = x_ref[0]
    y = x * lax.rsqrt(jnp.mean(x * x, axis=-1, keepdims=True) + EPS) * g1_ref[...]
    h = y * (1.0 + mod_ref[0, 0, 1:2, :]) + mod_ref[0, 0, 0:1, :]
    p = jnp.dot(h.astype(BF16), w_ref[...], preferred_element_type=F32)

    def head_pair(x2, m_ref, gain_rows, rope_cols, half):
        r = _seg_rsqrt(x2, m_ref)
        out = []
        for i in range(2):
            sl = slice(i * LANE, (i + 1) * LANE)
            y = x2[:, sl] * r[:, sl] * gains_ref[gain_rows[i]:gain_rows[i] + 1, :]
            if rope_ref is not None and rope_cols[i] is not None:
                c0, s0 = rope_cols[i]
                y = _rope(y, rope_ref[:, c0:c0 + LANE], rope_ref[:, s0:s0 + LANE], half[i])
            out.append(y)
        return out

    r64 = (0, 128)
    rb = (256, 384)
    qa0, qa1 = head_pair(p[:, _QA:_QA + 256], mm_ref, (0, 0), (r64, r64), (16, 16))
    qa2, ka = head_pair(p[:, _QA + 256:_QA + 512], mm_ref, (0, 1), (r64, r64), (16, 16))
    for i, blk in enumerate((qa0, qa1, qa2)):
        qa_ref[0, i] = blk.astype(BF16)
    ka_ref[0, 0] = ka.astype(BF16)
    va_ref[0, 0] = p[:, _VA:_VA + LANE].astype(BF16)
    qc0, qc1 = head_pair(p[:, _QC:_QC + 256], mm_ref, (2, 2), (r64, r64), (16, 16))
    qc_ref[0, :, 0:LANE] = qc0.astype(BF16)
    qc_ref[0, :, LANE:2 * LANE] = qc1.astype(BF16)
    kc, kpe = head_pair(jnp.concatenate([p[:, _KC:_KC + LANE], p[:, _BKR:_BKR + LANE]], axis=1),
                        mx_ref, (3, 6), (r64, rb), (16, 8))
    kc_ref[0] = kc.astype(BF16)
    vc_ref[0] = p[:, _VC:_VC + LANE].astype(BF16)

    cq = p[:, _BCQ:_BCQ + 256]
    cq = cq * lax.rsqrt(jnp.sum(cq * cq, axis=-1, keepdims=True) * (1.0 / B_Q_RANK) + EPS) * gcq_ref[...]
    qb = jnp.dot(cq.astype(BF16), wuq_ref[...], preferred_element_type=F32)
    ckv = p[:, _BCKV:_BCKV + LANE]
    ckv = ckv * lax.rsqrt(jnp.mean(ckv * ckv, axis=-1, keepdims=True) + EPS) * gckv_ref[...]
    ckv = ckv.astype(BF16)
    kbn = jnp.dot(ckv, wukvk_ref[...], preferred_element_type=F32)
    vbv = jnp.dot(ckv, wukvv_ref[...], preferred_element_type=F32).astype(BF16)
    for i in range(B_HEADS // 2):
        vb_ref[0, i] = vbv[:, i * LANE:(i + 1) * LANE]
    for i in range(B_HEADS // 2):
        sl = slice(2 * i * LANE, (2 * i + 2) * LANE)
        q0, q1 = head_pair(qb[:, sl], mbb_ref, (4, 4), (rb, rb), (8, 8))
        k0, k1 = head_pair(kbn[:, sl], mbb_ref, (5, 5), (None, None), (8, 8))
        qb_ref[0, 2 * i] = q0.astype(BF16)
        qb_ref[0, 2 * i + 1] = q1.astype(BF16)
        kb_ref[0, 2 * i] = (k0 + kpe).astype(BF16)
        kb_ref[0, 2 * i + 1] = (k1 + kpe).astype(BF16)


def _inproj(xs, modv, kind, g1, w_in_p, gains, gcq, gckv, wuq, wukvk, wukvv, mm, mx, mbb, rope):
    b, n, d = xs.shape
    tm = min(PROJ_TILE, n)
    use_rope = rope is not None
    const = lambda *shape: pl.BlockSpec(shape, lambda i, j: (0,) * len(shape))
    in_specs = [pl.BlockSpec((1, tm, d), lambda i, j: (i, j, 0)),
                pl.BlockSpec((1, 1, 8, d), lambda i, j: (i, kind, 0, 0)),
                const(1, d), const(d, IN_PAD), const(8, LANE), const(1, 256), const(1, LANE),
                const(256, 768), const(LANE, 768), const(LANE, 384),
                const(256, 256), const(256, 256), const(256, 256)]
    args = [xs, modv, g1, w_in_p, gains, gcq, gckv, wuq, wukvk, wukvv, mm, mx, mbb]
    if use_rope:
        in_specs.append(pl.BlockSpec((tm, 512), lambda i, j: (j, 0)))
        args.append(rope)
    blocks = (3, 1, 1, B_HEADS, B_HEADS, B_HEADS // 2)
    widths = (256, 128, 128)
    out_specs = ([pl.BlockSpec((1, nb, tm, LANE), lambda i, j: (i, 0, j, 0)) for nb in blocks]
                 + [pl.BlockSpec((1, tm, w), lambda i, j: (i, j, 0)) for w in widths])
    out_shape = ([jax.ShapeDtypeStruct((b, nb, n, LANE), BF16) for nb in blocks]
                 + [jax.ShapeDtypeStruct((b, n, w), BF16) for w in widths])
    return pl.pallas_call(
        functools.partial(_inproj_kernel, use_rope=use_rope),
        grid=(b, n // tm),
        in_specs=in_specs,
        out_specs=out_specs,
        out_shape=out_shape,
        compiler_params=_cparams(("parallel", "parallel")),
        name="inproj_rope" if use_rope else "inproj_ctx",
    )(*args)


_A_MAPS = (lambda h: h % 3, lambda h: 0, lambda h: 0)
_B_MAPS = (lambda h: h, lambda h: h, lambda h: h // 2)


def _qk(q, k):
    return lax.dot_general(q, k, (((1,), (1,)), ((), ())), preferred_element_type=F32)


def _half_mask(q, half):
    lane = lax.broadcasted_iota(jnp.int32, q.shape, 1)
    return jnp.where((lane // 64) == half, q, jnp.zeros_like(q))


def _score_pass(q, chunks, s_w):
    mx = None
    for k_ref, _, st, sz, off in chunks:
        s = _qk(q, k_ref[0, 0, st:st + sz, :])
        s_w[:, off:off + sz] = s
        for t in range(sz // LANE):
            blk = s[:, t * LANE:(t + 1) * LANE]
            mx = blk if mx is None else jnp.maximum(mx, blk)
    return jnp.broadcast_to(jnp.max(mx, axis=1, keepdims=True), mx.shape)


def _value_pass(chunks, s_r, mb):
    acc = None
    for _, v_ref, st, sz, off in chunks:
        ps = [jnp.exp2(s_r[:, off + t * LANE:off + (t + 1) * LANE] - mb).astype(BF16)
              for t in range(sz // LANE)]
        v_ones = jnp.concatenate([v_ref[0, 0, st:st + sz, :], jnp.ones((sz, LANE), BF16)], axis=1)
        part = jnp.dot(jnp.concatenate(ps, axis=1), v_ones, preferred_element_type=F32)
        acc = part if acc is None else acc + part
    return (acc[:, :LANE] / acc[:, LANE:LANE + 1]).astype(BF16)


def _fused_passes(q, chunks, s_w, s_r, mb_r):
    mx = None
    acc = None
    for k_ref, v_ref, st, sz, off in chunks:
        s = _qk(q, k_ref[0, 0, st:st + sz, :])
        s_w[:, off:off + sz] = s
        for t in range(sz // LANE):
            blk = s[:, t * LANE:(t + 1) * LANE]
            mx = blk if mx is None else jnp.maximum(mx, blk)
        ps = [jnp.exp2(s_r[:, off + t * LANE:off + (t + 1) * LANE] - mb_r).astype(BF16)
              for t in range(sz // LANE)]
        v_ones = jnp.concatenate([v_ref[0, 0, st:st + sz, :], jnp.ones((sz, LANE), BF16)], axis=1)
        part = jnp.dot(jnp.concatenate(ps, axis=1), v_ones, preferred_element_type=F32)
        acc = part if acc is None else acc + part
    mb_new = jnp.broadcast_to(jnp.max(mx, axis=1, keepdims=True), mx.shape)
    return mb_new, (acc[:, :LANE] / acc[:, LANE:LANE + 1]).astype(BF16)


def _key_chunks(kc_ref, vc_ref, kl_ref, vl_ref):
    n_ctx = kc_ref.shape[2]
    chunks = [(kc_ref, vc_ref, 0, n_ctx, 0)]
    if kl_ref is not None:
        for c in range(kl_ref.shape[2] // KEY_CHUNK):
            chunks.append((kl_ref, vl_ref, c * KEY_CHUNK, KEY_CHUNK, n_ctx + c * KEY_CHUNK))
    return chunks


def _dense_kernel(q_ref, kc_ref, kl_ref, vcp_ref, vlp_ref, vcc_ref, vlc_ref, o_ref,
                  sa_sc, sb_sc, mba_sc, mbb_sc, *, mask_q, n_tiles, tiles_per_sample, tiles_per_head, n_sub):
    step = pl.program_id(0)
    tq = sa_sc.shape[0]

    @pl.when(step == 0)
    def _():
        sa_sc[...] = jnp.zeros(sa_sc.shape, F32)
        mba_sc[...] = jnp.zeros(mba_sc.shape, F32)

    if mask_q:
        first = jnp.minimum(n_sub * step, n_tiles - n_sub)
        half = ((first % tiles_per_sample) // tiles_per_head) // 3
    for k in range(n_sub):
        rows = slice(k * tq, (k + 1) * tq)
        q = q_ref[0, 0, rows, :]
        if mask_q:
            q = _half_mask(q, half)
        v_refs = (vcp_ref, vlp_ref) if k == 0 else (vcc_ref, vlc_ref)
        chunks = _key_chunks(kc_ref, v_refs[0], kl_ref, v_refs[1])
        s_w, s_r, mb_w, mb_r = ((sb_sc, sa_sc, mbb_sc, mba_sc) if k % 2 == 0
                                else (sa_sc, sb_sc, mba_sc, mbb_sc))
        mb_new, o_ref[rows, :] = _fused_passes(q, chunks, s_w, s_r, mb_r[...])
        mb_w[...] = mb_new


def _dense_attention(q, kc, vc, kl, vl, n_heads, maps, mask_q, name):
    b, _, nq, _ = q.shape
    qmap, kmap, vmap = maps
    n_ctx, n_lat = kc.shape[2], kl.shape[2]
    tq = Q_TILE
    tph = nq // tq
    n_sub = 4 if tph % 4 == 0 else 2
    assert tph % n_sub == 0
    tps = n_heads * tph
    n_tiles = b * tps
    first = lambda s: jnp.minimum(n_sub * s, n_tiles - n_sub)
    before = lambda s: jnp.maximum(n_sub * s - 1, 0)
    kv_blk = lambda n, fmap, tile: pl.BlockSpec(
        (1, 1, n, LANE), lambda s: (tile(s) // tps, fmap((tile(s) % tps) // tph), 0, 0))
    q_spec = pl.BlockSpec(
        (1, 1, n_sub * tq, LANE),
        lambda s: (first(s) // tps, qmap((first(s) % tps) // tph), (first(s) % tph) // n_sub, 0))
    return pl.pallas_call(
        functools.partial(_dense_kernel, mask_q=mask_q, n_tiles=n_tiles, tiles_per_sample=tps,
                          tiles_per_head=tph, n_sub=n_sub),
        grid=(n_tiles // n_sub + 1,),
        in_specs=[q_spec, kv_blk(n_ctx, kmap, first), kv_blk(n_lat, kmap, first),
                  kv_blk(n_ctx, vmap, before), kv_blk(n_lat, vmap, before),
                  kv_blk(n_ctx, vmap, first), kv_blk(n_lat, vmap, first)],
        out_specs=pl.BlockSpec((n_sub * tq, LANE), lambda s: (s, 0)),
        out_shape=jax.ShapeDtypeStruct(((n_tiles + n_sub) * tq, LANE), BF16),
        scratch_shapes=[pltpu.VMEM((tq, n_ctx + n_lat), F32), pltpu.VMEM((tq, n_ctx + n_lat), F32),
                        pltpu.VMEM((tq, LANE), F32), pltpu.VMEM((tq, LANE), F32)],
        compiler_params=_cparams(("arbitrary",)),
        name=name,
    )(q, kc, kl, vc, vl, vc, vl)


def _dense_ctx_kernel(q_ref, kc_ref, vc_ref, o_ref, s_sc, *, mask_q):
    q = q_ref[0, 0]
    if mask_q:
        q = _half_mask(q, pl.program_id(1) // 3)
    chunks = _key_chunks(kc_ref, vc_ref, None, None)
    mb = _score_pass(q, chunks, s_sc)
    o_ref[0, 0] = _value_pass(chunks, s_sc, mb)


def _dense_attention_ctx(q, kc, vc, n_heads, maps, mask_q, name):
    b, _, nq, _ = q.shape
    qmap, kmap, vmap = maps
    blk = lambda fmap: pl.BlockSpec((1, 1, nq, LANE), lambda i, h: (i, fmap(h), 0, 0))
    return pl.pallas_call(
        functools.partial(_dense_ctx_kernel, mask_q=mask_q),
        grid=(b, n_heads),
        in_specs=[blk(qmap), blk(kmap), blk(vmap)],
        out_specs=pl.BlockSpec((1, 1, nq, LANE), lambda i, h: (i, h, 0, 0)),
        out_shape=jax.ShapeDtypeStruct((b, n_heads, nq, LANE), BF16),
        scratch_shapes=[pltpu.VMEM((nq, kc.shape[2]), F32)],
        compiler_params=_cparams(("parallel", "parallel")),
        name=name,
    )(q, kc, vc)


def _window_kernel(q_ref, kc_ref, vc_ref, sink_ref, *rest, use_window):
    if use_window:
        kl_ref, vl_ref, o_ref = rest
    else:
        (o_ref,) = rest
    tq = ROW_TILE
    span = tq + 2 * WINDOW
    kcx = kc_ref[0]
    ones_c = jnp.ones((kcx.shape[0], LANE), BF16)
    v_aug = jnp.concatenate([vc_ref[0], ones_c], axis=1)
    lane = lax.broadcasted_iota(jnp.int32, (tq, LANE), 1)
    for ti in range(q_ref.shape[1] // tq):
        rows = slice(ti * tq, (ti + 1) * tq)
        if use_window:
            n_lat = kl_ref.shape[1]
            t0 = (pl.program_id(1) * (q_ref.shape[1] // tq) + ti) * tq
            start = pl.multiple_of(jnp.clip(t0 - WINDOW, 0, n_lat - span), LANE)
            qpos = t0 + lax.broadcasted_iota(jnp.int32, (tq, span), 0)
            kpos = start + lax.broadcasted_iota(jnp.int32, (tq, span), 1)
            valid = jnp.abs(qpos - kpos) <= WINDOW
            kw = kl_ref[0, pl.ds(start, span), :]
            v_all = jnp.concatenate(
                [v_aug, jnp.concatenate([vl_ref[0, pl.ds(start, span), :], jnp.ones((span, LANE), BF16)], axis=1)],
                axis=0)
        else:
            v_all = v_aug
        res = []
        for h in range(C_HEADS):
            half = h // (C_HEADS // C_KV_HEADS)
            q = q_ref[0, rows, (h % 2) * LANE:(h % 2 + 1) * LANE]
            q = jnp.where((lane < 64) if half == 0 else (lane >= 64), q, jnp.zeros_like(q))
            sink = sink_ref[0:1, h:h + 1]
            s = _qk(q, kcx)
            if use_window:
                s = jnp.concatenate([s, jnp.where(valid, _qk(q, kw), NEG_INF)], axis=1)
            mx = s[:, 0:LANE]
            for t in range(1, s.shape[1] // LANE):
                mx = jnp.maximum(mx, s[:, t * LANE:(t + 1) * LANE])
            m = jnp.maximum(jnp.max(mx, axis=1, keepdims=True), sink)
            mb = jnp.broadcast_to(m, (tq, LANE))
            p = jnp.concatenate([jnp.exp2(s[:, t * LANE:(t + 1) * LANE] - mb).astype(BF16)
                                 for t in range(s.shape[1] // LANE)], axis=1)
            o = jnp.dot(p, v_all, preferred_element_type=F32)
            res.append(o[:, :LANE] / (o[:, LANE:LANE + 1] + jnp.exp2(sink - m)))
        for ob in range(2):
            o_ref[0, rows, ob * LANE:(ob + 1) * LANE] = jnp.where(lane < 64, res[ob], res[2 + ob]).astype(BF16)


def _window_attention(q, kc, vc, kl, vl, sink, name):
    b, nq, qw = q.shape
    tq = min(2 * ROW_TILE, nq)
    n_ctx = kc.shape[1]
    use_window = kl is not None
    in_specs = [pl.BlockSpec((1, tq, qw), lambda i, j: (i, j, 0)),
                pl.BlockSpec((1, n_ctx, LANE), lambda i, j: (i, 0, 0)),
                pl.BlockSpec((1, n_ctx, LANE), lambda i, j: (i, 0, 0)),
                pl.BlockSpec((8, LANE), lambda i, j: (0, 0))]
    args = [q, kc, vc, sink]
    if use_window:
        n_lat = kl.shape[1]
        in_specs += [pl.BlockSpec((1, n_lat, LANE), lambda i, j: (i, 0, 0)),
                     pl.BlockSpec((1, n_lat, LANE), lambda i, j: (i, 0, 0))]
        args += [kl, vl]
    return pl.pallas_call(
        functools.partial(_window_kernel, use_window=use_window),
        grid=(b, nq // tq),
        in_specs=in_specs,
        out_specs=pl.BlockSpec((1, tq, qw), lambda i, j: (i, j, 0)),
        out_shape=jax.ShapeDtypeStruct((b, nq, qw), BF16),
        compiler_params=_cparams(("parallel", "parallel")),
        name=name,
    )(*args)


def _outproj_kernel(*refs):
    oa = refs[0:A_HEADS]
    ob = refs[A_HEADS:A_HEADS + B_HEADS]
    oc_ref, x_ref, mod_ref, g2n_ref, w_ref, wr_ref, xo_ref, h2_ref, aff_ref = refs[A_HEADS + B_HEADS:]
    tm = x_ref.shape[1]
    low = lax.broadcasted_iota(jnp.int32, (tm, LANE), 1) < 64
    parts = [jnp.where(low, oa[i][...], oa[3 + i][...]) for i in range(3)]
    parts += [jnp.where(low, ob[2 * i][...], ob[2 * i + 1][...]) for i in range(3)]
    parts.append(oc_ref[0])
    mix = jnp.dot(jnp.concatenate(parts, axis=1), w_ref[...], preferred_element_type=F32)
    x = x_ref[0] + mod_ref[0, 0, 2:3, :] * mix
    xo_ref[0] = x
    y = x * lax.rsqrt(jnp.mean(x * x, axis=-1, keepdims=True) + EPS) * g2n_ref[...]
    h2 = y * (1.0 + mod_ref[0, 0, 4:5, :]) + mod_ref[0, 0, 3:4, :]
    h2_ref[0] = h2.astype(BF16)
    wr = wr_ref[...]
    wr_hi = wr.astype(BF16)
    wr_lo = (wr - wr_hi.astype(F32)).astype(BF16)
    h2_hi = h2.astype(BF16)
    h2_lo = (h2 - h2_hi.astype(F32)).astype(BF16)
    both = jnp.dot(h2_hi, jnp.concatenate([wr_hi, wr_lo], axis=1), preferred_element_type=F32)
    logits = both[:, :LANE] + both[:, LANE:] + jnp.dot(h2_lo, wr_hi, preferred_element_type=F32)
    logits = logits.T[0:aff_ref.shape[1], :]
    z = jnp.exp(logits - jnp.max(logits, axis=0, keepdims=True))
    aff_ref[0] = z / jnp.sum(z, axis=0, keepdims=True)


def _outproj(oa, ob, oc, xs, modv, kind, g2n, w_mix, wr_t, pad_rows, name):
    b, n, d = xs.shape
    tm = min(PROJ_TILE, n)
    e = N_EXPERTS
    const = lambda *shape: pl.BlockSpec(shape, lambda i, j: (0,) * len(shape))
    row = lambda w: pl.BlockSpec((1, tm, w), lambda i, j: (i, j, 0))
    n_heads = A_HEADS
    head = lambda h: pl.BlockSpec((tm, LANE), lambda i, j: ((pad_rows + (i * n_heads + h) * n) // tm + j, 0))
    return pl.pallas_call(
        _outproj_kernel,
        grid=(b, n // tm),
        in_specs=[head(h) for h in range(A_HEADS)] + [head(h) for h in range(B_HEADS)] + [
                  row(256), row(d),
                  pl.BlockSpec((1, 1, 8, d), lambda i, j: (i, kind, 0, 0)),
                  const(1, d), const(d, d), const(d, LANE)],
        out_specs=[row(d), row(d), pl.BlockSpec((1, e, tm), lambda i, j: (i, 0, j))],
        out_shape=[jax.ShapeDtypeStruct((b, n, d), F32), jax.ShapeDtypeStruct((b, n, d), BF16),
                   jax.ShapeDtypeStruct((b, e, n), F32)],
        compiler_params=_cparams(("parallel", "parallel")),
        name=name,
    )(*([oa] * A_HEADS), *([ob] * B_HEADS), oc, xs, modv, g2n, w_mix, wr_t)


def _prefix_exclusive(mask_f, tri):
    e, t = mask_f.shape
    ck = tri.shape[0]
    carry = jnp.zeros((e, 1), F32)
    parts, offs = [], [carry]
    for c in range(t // ck):
        blk = mask_f[:, c * ck:(c + 1) * ck]
        parts.append(jnp.dot(blk.astype(BF16), tri, preferred_element_type=F32) + carry)
        carry = carry + jnp.sum(blk, axis=1, keepdims=True)
        offs.append(carry)
    return jnp.concatenate(parts, axis=1) if len(parts) > 1 else parts[0], offs


def _topk_kernel(aff_ref, slot_ref, cnt_ref, *, cap, slot_stride):
    aff = aff_ref[0]
    e, t = aff.shape
    bits = pltpu.bitcast(aff, jnp.int32)
    kf = jnp.float32(cap)

    def search(i, lo):
        cand = lo | (jnp.int32(1) << (30 - i))
        n_ge = jnp.sum((bits >= cand).astype(F32), axis=1, keepdims=True)
        return jnp.where(n_ge >= kf, cand, lo)

    thr = lax.fori_loop(0, 31, search, jnp.zeros((e, 1), jnp.int32))
    gt = bits > thr
    eq = bits == thr
    need = kf - jnp.sum(gt.astype(F32), axis=1, keepdims=True)

    r = lax.broadcasted_iota(jnp.int32, (TOK_CHUNK, TOK_CHUNK), 0)
    c = lax.broadcasted_iota(jnp.int32, (TOK_CHUNK, TOK_CHUNK), 1)
    tri = (r < c).astype(BF16)
    tie_rank, _ = _prefix_exclusive(eq.astype(F32), tri)
    sel = gt | (eq & (tie_rank < need))
    slot, offs = _prefix_exclusive(sel.astype(F32), tri)
    base = pl.program_id(0) * slot_stride
    slot_ref[0] = jnp.where(sel, slot.astype(jnp.int32) + base, -1)
    lane = lax.broadcasted_iota(jnp.int32, (e, LANE), 1)
    cnt = jnp.zeros((e, LANE), jnp.int32)
    for ci, off in enumerate(offs):
        cnt = jnp.where(lane == ci, off.astype(jnp.int32) + base, cnt)
    cnt_ref[0] = cnt


def _topk(aff_t, cap, slot_stride, name):
    b, e, t = aff_t.shape
    return pl.pallas_call(
        functools.partial(_topk_kernel, cap=cap, slot_stride=slot_stride),
        grid=(b,),
        in_specs=[pl.BlockSpec((1, e, t), lambda i: (i, 0, 0))],
        out_specs=[pl.BlockSpec((1, e, t), lambda i: (i, 0, 0)),
                   pl.BlockSpec((1, e, LANE), lambda i: (i, 0, 0))],
        out_shape=[jax.ShapeDtypeStruct((b, e, t), jnp.int32), jax.ShapeDtypeStruct((b, e, LANE), jnp.int32)],
        compiler_params=_cparams(("parallel",)),
        name=name,
    )(aff_t)


def _moe_kernel(cnt_ref, h_ref, slot_ref, slotp_ref, gate_ref, wg_ref, wu_ref, wd_ref, mod_ref, x_hbm, o_hbm,
                acc_sc, xe_sc, gs_sc, ye_sc, in_sem, out_sem, *, n_chunks, n_sb):
    bi = pl.program_id(0)
    ei = pl.program_id(1)
    n_b = pl.num_programs(0)
    n_e = pl.num_programs(1)

    def chunk_rows(c):
        return pl.ds(c * TOK_CHUNK, TOK_CHUNK)

    def load_dma(c):
        return pltpu.make_async_copy(x_hbm.at[bi, chunk_rows(c), :], acc_sc.at[chunk_rows(c), :], in_sem.at[c])

    def store_dma(sample, c):
        return pltpu.make_async_copy(acc_sc.at[chunk_rows(c), :], o_hbm.at[sample, chunk_rows(c), :],
                                     out_sem.at[c])

    def start_x_load():
        @pl.when(ei == 0)
        def _():
            for c in range(n_chunks):
                @pl.when(bi > 0)
                def _(c=c):
                    store_dma(bi - 1, c).wait()
                load_dma(c).start()

    def wait_for_x(at_expert):
        @pl.when(ei == at_expert)
        def _():
            for c in range(n_chunks):
                load_dma(c).wait()

    g2 = mod_ref[0, 0, 5:6, :]
    cap = n_sb * SLOT_BLOCK

    def counts_of(e):
        base = (bi * n_e + e) * 32
        return [cnt_ref[base + c] for c in range(n_chunks + 1)]

    def all_chunks_at_most(cs, limit):
        ok = cs[1] - cs[0] <= limit
        for c in range(1, n_chunks):
            ok = ok & (cs[c + 1] - cs[c] <= limit)
        return ok

    odd = (ei % 2) == 1
    counts = counts_of(ei)
    counts_other = counts_of(jnp.where(odd, ei - 1, ei + 1))
    pair_narrow = all_chunks_at_most(counts, PAIR_LIMIT) & all_chunks_at_most(counts_other, PAIR_LIMIT)

    def ffn():
        xe = xe_sc[0:cap, :].astype(BF16)
        f = wg_ref.shape[2]
        fb = 512
        y = None
        for f0 in range(0, f, fb):
            hg = jnp.dot(xe, wg_ref[0, :, f0:f0 + fb], preferred_element_type=F32)
            hu = jnp.dot(xe, wu_ref[0, :, f0:f0 + fb], preferred_element_type=F32)
            hid = (hg * jax.nn.sigmoid(hg) * hu).astype(BF16)
            part = jnp.dot(hid, wd_ref[0, f0:f0 + fb, :], preferred_element_type=F32)
            y = part if y is None else y + part
        return y

    def tok(c):
        return slice(c * TOK_CHUNK, (c + 1) * TOK_CHUNK)

    def slot_columns(ref):
        pad = jnp.zeros((LANE - n_chunks, TOK_CHUNK), F32)
        return jnp.concatenate([ref[0, 0].astype(F32), pad], axis=0).T

    @pl.when(pair_narrow)
    def _():
        xe_sc[...] = jnp.zeros_like(xe_sc)
        gs_sc[...] = jnp.zeros_like(gs_sc)
        srow = lax.broadcasted_iota(jnp.int32, (GATHER_WINDOW, TOK_CHUNK), 0)
        for c in range(n_chunks):
            g0 = pl.multiple_of((counts[c] // 8) * 8, 8)
            oh = (slot_ref[0, 0, c:c + 1, :] - g0) == srow
            xe_sc[pl.ds(g0, GATHER_WINDOW), :] += jnp.dot(
                oh.astype(BF16), h_ref[0, tok(c), :], preferred_element_type=F32)
            gs_sc[pl.ds(g0, GATHER_WINDOW), :] += jnp.sum(
                jnp.where(oh, gate_ref[0, 0, c:c + 1, :], 0.0), axis=1, keepdims=True)
        ye = ye_sc.at[ei % 2]
        ye[0:cap, :] = (ffn() * gs_sc[0:cap, :]).astype(BF16)
        ye[cap:cap + SLOT_BLOCK, :] = jnp.zeros((SLOT_BLOCK, ye_sc.shape[2]), BF16)
        start_x_load()

        @pl.when(odd)
        def _():
            wait_for_x(1)
            cols_prev = slot_columns(slotp_ref)
            cols_cur = slot_columns(slot_ref)
            scol = lax.broadcasted_iota(jnp.int32, (TOK_CHUNK, SLOT_BLOCK), 1).astype(F32)
            for c in range(n_chunks):
                w_prev = pl.multiple_of((counts_other[c] // 16) * 16, 16)
                w_cur = pl.multiple_of((counts[c] // 16) * 16, 16)
                oh_t = jnp.concatenate(
                    [((cols_prev[:, c:c + 1] - w_prev.astype(F32)) == scol).astype(BF16),
                     ((cols_cur[:, c:c + 1] - w_cur.astype(F32)) == scol).astype(BF16)], axis=1)
                rows = jnp.concatenate([ye_sc[0, pl.ds(w_prev, SLOT_BLOCK), :],
                                        ye_sc[1, pl.ds(w_cur, SLOT_BLOCK), :]], axis=0)
                acc_sc[tok(c), :] += g2 * jnp.dot(oh_t, rows, preferred_element_type=F32)

    @pl.when(jnp.logical_not(pair_narrow))
    def _():
        xe_sc[...] = jnp.zeros_like(xe_sc)
        gs_sc[...] = jnp.zeros_like(gs_sc)
        srow = lax.broadcasted_iota(jnp.int32, (SLOT_BLOCK, TOK_CHUNK), 0)
        base = (bi * n_e + ei) * 32

        def overlap(c, sb):
            return (cnt_ref[base + c] < (sb + 1) * SLOT_BLOCK) & (cnt_ref[base + c + 1] > sb * SLOT_BLOCK)

        def onehot(c, sb):
            return slot_ref[0, 0, pl.ds(c, 1), :] == (srow + sb * SLOT_BLOCK)

        def dyn_tok(c):
            return pl.ds(pl.multiple_of(c * TOK_CHUNK, TOK_CHUNK), TOK_CHUNK)

        def gather_chunk(c, carry):
            for sb in range(n_sb):
                @pl.when(overlap(c, sb))
                def _(sb=sb):
                    oh = onehot(c, sb)
                    rows = slice(sb * SLOT_BLOCK, (sb + 1) * SLOT_BLOCK)
                    xe_sc[rows] += jnp.dot(oh.astype(BF16), h_ref[0, dyn_tok(c), :], preferred_element_type=F32)
                    g = jnp.where(oh, gate_ref[0, 0, pl.ds(c, 1), :], 0.0)
                    gs_sc[rows] += jnp.sum(g, axis=1, keepdims=True)
            return carry

        lax.fori_loop(0, n_chunks, gather_chunk, 0)
        ye_sc[0, 0:cap, :] = (ffn() * gs_sc[0:cap, :]).astype(BF16)
        start_x_load()
        wait_for_x(0)

        def scatter_chunk(c, carry):
            for sb in range(n_sb):
                @pl.when(overlap(c, sb))
                def _(sb=sb):
                    oh_t = onehot(c, sb).astype(F32).T.astype(BF16)
                    rows = slice(sb * SLOT_BLOCK, (sb + 1) * SLOT_BLOCK)
                    acc_sc[dyn_tok(c), :] += g2 * jnp.dot(oh_t, ye_sc[0, rows, :], preferred_element_type=F32)
            return carry

        lax.fori_loop(0, n_chunks, scatter_chunk, 0)

    @pl.when(ei == n_e - 1)
    def _():
        for c in range(n_chunks):
            store_dma(bi, c).start()

    @pl.when((ei == n_e - 1) & (bi == n_b - 1))
    def _():
        for c in range(n_chunks):
            store_dma(bi, c).wait()


def _moe(h2, slot, gate, cnt, wg, wu, wd, xs, modv, kind, cap, name):
    b, t, d = h2.shape
    e = slot.shape[1]
    f = wg.shape[2]
    n_chunks = t // TOK_CHUNK
    n_sb = cap // SLOT_BLOCK
    slot4 = slot.reshape(b, e, n_chunks, TOK_CHUNK)
    gate4 = gate.reshape(b, e, n_chunks, TOK_CHUNK)
    cnt_flat = cnt[:, :, :32].reshape(-1)
    grid_spec = pltpu.PrefetchScalarGridSpec(
        num_scalar_prefetch=1,
        grid=(b, e),
        in_specs=[pl.BlockSpec((1, t, d), lambda i, j, s: (i, 0, 0)),
                  pl.BlockSpec((1, 1, n_chunks, TOK_CHUNK), lambda i, j, s: (i, j, 0, 0)),
                  pl.BlockSpec((1, 1, n_chunks, TOK_CHUNK), lambda i, j, s: (i, jnp.maximum(j - 1, 0), 0, 0)),
                  pl.BlockSpec((1, 1, n_chunks, TOK_CHUNK), lambda i, j, s: (i, j, 0, 0)),
                  pl.BlockSpec((1, d, f), lambda i, j, s: (j, 0, 0)),
                  pl.BlockSpec((1, d, f), lambda i, j, s: (j, 0, 0)),
                  pl.BlockSpec((1, f, d), lambda i, j, s: (j, 0, 0)),
                  pl.BlockSpec((1, 1, 8, d), lambda i, j, s: (i, kind, 0, 0)),
                  pl.BlockSpec(memory_space=pl.ANY)],
        out_specs=pl.BlockSpec(memory_space=pl.ANY),
        scratch_shapes=[pltpu.VMEM((t, d), F32),
                        pltpu.VMEM((cap + 2 * SLOT_BLOCK, d), F32), pltpu.VMEM((cap + 2 * SLOT_BLOCK, 1), F32),
                        pltpu.VMEM((2, cap + SLOT_BLOCK, d), BF16),
                        pltpu.SemaphoreType.DMA((n_chunks,)), pltpu.SemaphoreType.DMA((n_chunks,))],
    )
    return pl.pallas_call(
        functools.partial(_moe_kernel, n_chunks=n_chunks, n_sb=n_sb),
        grid_spec=grid_spec,
        out_shape=jax.ShapeDtypeStruct((b, t, d), F32),
        compiler_params=_cparams(("arbitrary", "arbitrary")),
        name=name,
    )(cnt_flat, h2, slot4, slot4, gate4, wg, wu, wd, modv, xs)


def _cast_kernel(w_ref, o_ref):
    o_ref[...] = w_ref[0].astype(BF16)


def _layer_bf16(w, layer):
    _, e, r, c = w.shape
    return pl.pallas_call(
        _cast_kernel,
        grid=(e,),
        in_specs=[pl.BlockSpec((1, 1, r, c), lambda i: (layer, i, 0, 0))],
        out_specs=pl.BlockSpec((1, r, c), lambda i: (i, 0, 0)),
        out_shape=jax.ShapeDtypeStruct((e, r, c), BF16),
        compiler_params=_cparams(("parallel",)),
        name="expert_weights_bf16",
    )(w)


def _pad_lanes(v, width):
    return jnp.pad(v, (0, width - v.shape[0]))


def kernel(x, c, ctx, c_ctx, w_ada, b_ada, norm1_g, norm2_g, w_in, a_q_norm, a_k_norm, b_cq_norm, b_ckv_norm, w_uq, w_ukv, b_qn_norm, b_kn_norm, b_qr_norm, b_kr_norm, c_q_norm, c_k_norm, c_sink, w_out, w_router, w_e_gate, w_e_up, w_e_down):
    b, t, d = x.shape
    n_ctx = ctx.shape[1]
    depth = w_ada.shape[0]
    assert n_ctx == ROW_TILE and t % KEY_CHUNK == 0 and t % GRID_W == 0
    cap = max(1, CAP_FACTOR * t // N_EXPERTS)
    cap_c = max(1, CAP_FACTOR * n_ctx // N_EXPERTS)
    assert cap % SLOT_BLOCK == 0 and (b * cap_c) % SLOT_BLOCK == 0

    rows = ((b + 1 + 7) // 8) * 8
    cc = jnp.zeros((rows, d), F32).at[:b].set(c).at[b].set(c_ctx)
    mod = _modulation(cc, w_ada, b_ada)

    rope = _rope_tables(t)
    mm, mx, mbb = (jnp.asarray(m, BF16) for m in _seg_matrices())
    in_perm = _in_perm()
    uq_perm = _uq_perm()
    ukv_pk, ukv_pv = _ukv_perms()
    oa_rows = np.asarray(_pair_cols(0, A_HEADS))
    oc_rows = np.asarray(_pair_cols(0, C_HEADS)) + 768
    sa = HEAD_DIM ** -0.5
    sb = (B_NOPE + B_ROPE) ** -0.5
    z32, z64 = jnp.zeros((32,), F32), jnp.zeros((64,), F32)

    for l in range(depth):
        last = l == depth - 1
        m6 = mod[l].reshape(rows, 6, d)
        m8 = jnp.pad(m6, ((0, 0), (0, 2), (0, 0)))
        modv = jnp.stack([jnp.broadcast_to(m8[b][None], (b, 8, d)), m8[:b]], axis=1)

        w_in_p = _take_cols(w_in[l], in_perm).astype(BF16)
        wuq = jnp.pad(_take_cols(w_uq[l], uq_perm), ((0, 256 - B_Q_RANK), (0, 0))).astype(BF16)
        wukvk = _take_cols(w_ukv[l], ukv_pk).astype(BF16)
        wukvv = jnp.take(w_ukv[l], jnp.asarray(ukv_pv, jnp.int32), axis=1).astype(BF16)
        gains = jnp.stack([
            jnp.tile(a_q_norm[l], 2) * (sa * LOG2E), jnp.tile(a_k_norm[l], 2),
            jnp.tile(c_q_norm[l], 2) * (sa * LOG2E), jnp.tile(c_k_norm[l], 2),
            jnp.concatenate([b_qn_norm[l] * (sb * LOG2E), b_qr_norm[l] * (sb * LOG2E), z32]),
            jnp.concatenate([b_kn_norm[l], z64]),
            jnp.concatenate([z64, b_kr_norm[l], z32]),
            jnp.zeros((LANE,), F32)])
        gcq = _pad_lanes(b_cq_norm[l], 256)[None]
        gckv = b_ckv_norm[l][None]
        g1 = norm1_g[l][None]
        g2n = norm2_g[l][None]
        mix_rows = np.concatenate([oa_rows, np.arange(384, 768), oc_rows])
        w_mix = jnp.take(w_out[l], jnp.asarray(mix_rows, jnp.int32), axis=0).astype(BF16)
        wr_t = jnp.pad(w_router[l], ((0, 0), (0, LANE - N_EXPERTS)))
        sink = jnp.zeros((8, LANE), F32).at[0, :C_HEADS].set(c_sink[l] * LOG2E)
        wg = _layer_bf16(w_e_gate, l)
        wu = _layer_bf16(w_e_up, l)
        wd = _layer_bf16(w_e_down, l)

        shared = (g1, w_in_p, gains, gcq, gckv, wuq, wukvk, wukvv, mm, mx, mbb)
        qa, ka, va, qb, kb, vb, qc, kc, vc = _inproj(x, modv, 1, *shared, rope)
        qac, kac, vac, qbc, kbc, vbc, qcc, kcc, vcc = _inproj(ctx, modv, 0, *shared, None)

        o_a = _dense_attention(qa, kac, vac, ka, va, A_HEADS, _A_MAPS, True, "attn_a")
        o_b = _dense_attention(qb, kbc, vbc, kb, vb, B_HEADS, _B_MAPS, False, "attn_b")
        o_c = _window_attention(qc, kcc, vcc, kc, vc, sink, "attn_c")
        x_mid, h2, aff = _outproj(o_a, o_b, o_c, x, modv, 1, g2n, w_mix, wr_t, Q_TILE, "outproj")
        slot, cnt = _topk(aff, cap, 0, "topk")
        x = _moe(h2, slot, aff, cnt, wg, wu, wd, x_mid, modv, 1, cap, "moe")

        if not last:
            o_ac = _dense_attention_ctx(qac, kac, vac, A_HEADS, _A_MAPS, True, "attn_a_ctx")
            o_bc = _dense_attention_ctx(qbc, kbc, vbc, B_HEADS, _B_MAPS, False, "attn_b_ctx")
            o_cc = _window_attention(qcc, kcc, vcc, None, None, sink, "attn_c_ctx")
            flat = lambda o: o.reshape(-1, LANE)
            c_mid, hc2, aff_c = _outproj(flat(o_ac), flat(o_bc), o_cc, ctx, modv, 0, g2n, w_mix, wr_t, 0,
                                         "outproj_ctx")
            slot_c, cnt_c = _topk(aff_c, cap_c, cap_c, "topk_ctx")
            e = N_EXPERTS
            slot_f = jnp.transpose(slot_c, (1, 0, 2)).reshape(1, e, b * n_ctx)
            gate_f = jnp.transpose(aff_c, (1, 0, 2)).reshape(1, e, b * n_ctx)
            starts = jnp.transpose(cnt_c[:, :, 0], (1, 0))
            cnt_f = jnp.concatenate([starts, cnt_c[b - 1, :, 1:2],
                                     jnp.zeros((e, LANE - b - 1), jnp.int32)], axis=1)[None]
            ctx = _moe(hc2.reshape(1, b * n_ctx, d), slot_f, gate_f, cnt_f, wg, wu, wd,
                       c_mid.reshape(1, b * n_ctx, d), modv, 0, b * cap_c, "moe_ctx").reshape(b, n_ctx, d)
    return x
```

```python
import functools
import math

import numpy as np
import jax
import jax.numpy as jnp
from jax import lax
from jax.experimental import pallas as pl
from jax.experimental.pallas import tpu as pltpu

F32 = jnp.float32
BF16 = jnp.bfloat16

GRID_W = 64
ROPE_THETA = 10000.0
EPS = 1e-6
NEG_INF = -1e30
HEAD_DIM = 64
A_HEADS, A_KV_HEADS = 6, 2
B_HEADS, B_Q_RANK, B_KV_RANK, B_NOPE, B_ROPE, B_V = 6, 192, 128, 64, 32, 64
C_HEADS, C_KV_HEADS = 4, 2
N_EXPERTS = 16
CAP_FACTOR = 2
WINDOW = 128

LANE = 128
ROW_TILE = 256
PROJ_TILE = 512
Q_TILE = 512
KEY_CHUNK = 512
LOG2E = 1.4426950408889634
SLOT_BLOCK = 128
TOK_CHUNK = 256
GATHER_WINDOW = 144
PAIR_LIMIT = SLOT_BLOCK - 16
VMEM_LIMIT = 56 * 1024 * 1024

_QA, _KA, _VA = 0, 384, 512
_QC, _KC, _VC = 640, 896, 1024
_BCQ, _BCKV, _BKR = 1152, 1408, 1536
IN_PAD = 1664
_ORIG = dict(aq=0, ak=384, av=512, bcq=640, bckv=832, bkr=960, cq=992, ck=1248, cv=1376)


def _cparams(sem, vmem=VMEM_LIMIT):
    return pltpu.CompilerParams(dimension_semantics=sem, vmem_limit_bytes=vmem)


def _pair_cols(base, n_heads):
    half = n_heads // 2
    cols = []
    for i in range(half):
        cols += list(range(base + i * 64, base + (i + 1) * 64))
        cols += list(range(base + (half + i) * 64, base + (half + i + 1) * 64))
    return cols


def _in_perm():
    perm = -np.ones((IN_PAD,), np.int64)
    perm[_QA:_QA + 384] = _pair_cols(_ORIG["aq"], A_HEADS)
    perm[_KA:_KA + 128] = np.arange(_ORIG["ak"], _ORIG["ak"] + 128)
    perm[_VA:_VA + 128] = np.arange(_ORIG["av"], _ORIG["av"] + 128)
    perm[_QC:_QC + 256] = _pair_cols(_ORIG["cq"], C_HEADS)
    perm[_KC:_KC + 128] = np.arange(_ORIG["ck"], _ORIG["ck"] + 128)
    perm[_VC:_VC + 128] = np.arange(_ORIG["cv"], _ORIG["cv"] + 128)
    perm[_BCQ:_BCQ + B_Q_RANK] = np.arange(_ORIG["bcq"], _ORIG["bcq"] + B_Q_RANK)
    perm[_BCKV:_BCKV + 128] = np.arange(_ORIG["bckv"], _ORIG["bckv"] + 128)
    perm[_BKR + 64:_BKR + 96] = np.arange(_ORIG["bkr"], _ORIG["bkr"] + 32)
    return perm


def _take_cols(w, perm):
    safe = np.where(perm >= 0, perm, 0)
    out = jnp.take(w, jnp.asarray(safe, jnp.int32), axis=1)
    return jnp.where(jnp.asarray(perm >= 0)[None, :], out, 0.0)


def _uq_perm():
    perm = -np.ones((B_HEADS * LANE,), np.int64)
    for h in range(B_HEADS):
        perm[h * LANE:h * LANE + 96] = np.arange(h * 96, h * 96 + 96)
    return perm


def _ukv_perms():
    pk = -np.ones((B_HEADS * LANE,), np.int64)
    pv = np.zeros((B_HEADS * 64,), np.int64)
    for h in range(B_HEADS):
        pk[h * LANE:h * LANE + 64] = np.arange(h * 128, h * 128 + 64)
        pv[h * 64:(h + 1) * 64] = np.arange(h * 128 + 64, h * 128 + 128)
    return pk, pv


def _seg_matrices():
    i = np.arange(LANE)
    m64 = (i[:, None] // 64 == i[None, :] // 64).astype(np.float32) / 64.0
    mb = np.zeros((LANE, LANE), np.float32)
    mb[:64, :64] = 1.0 / 64.0
    mb[64:96, 64:96] = 1.0 / 32.0
    def pair(a, b):
        out = np.zeros((2 * LANE, 2 * LANE), np.float32)
        out[:LANE, :LANE] = a
        out[LANE:, LANE:] = b
        return out

    return pair(m64, m64), pair(m64, mb), pair(mb, mb)


def _rope_tables(length):
    rows = length // GRID_W
    t = np.arange(rows * GRID_W)
    row = np.repeat(np.arange(rows), GRID_W).astype(np.float32)
    col = (t % GRID_W).astype(np.float32)

    def tabs(dim):
        axis_dim = dim // 2
        inv = (np.float32(ROPE_THETA) ** (-np.arange(0, axis_dim, 2, dtype=np.float32) / axis_dim)).astype(np.float32)
        ar = row[:, None] * inv[None, :]
        ac = col[:, None] * inv[None, :]
        cos = np.concatenate([np.cos(ar), np.cos(ar), np.cos(ac), np.cos(ac)], axis=-1)
        sin = np.concatenate([-np.sin(ar), np.sin(ar), -np.sin(ac), np.sin(ac)], axis=-1)
        return cos.astype(np.float32), sin.astype(np.float32)

    c64, s64 = tabs(HEAD_DIM)
    cb, sb = tabs(B_ROPE)
    n = c64.shape[0]
    ones64, zeros64 = np.ones((n, 64), np.float32), np.zeros((n, 64), np.float32)
    ones32, zeros32 = np.ones((n, 32), np.float32), np.zeros((n, 32), np.float32)
    return jnp.asarray(np.concatenate([c64, c64, s64, s64,
                                       ones64, cb, ones32, zeros64, sb, zeros32], axis=-1))


def _mod_kernel(c_ref, w_ref, b_ref, o_ref):
    cv = c_ref[...]
    a = (cv * jax.nn.sigmoid(cv)).astype(BF16)
    o_ref[0] = jnp.dot(a, w_ref[0].astype(BF16), preferred_element_type=F32) + b_ref[0]


def _modulation(cc, w_ada, b_ada):
    n_layers, d, n6 = w_ada.shape
    rows = cc.shape[0]
    tn = 1536
    return pl.pallas_call(
        _mod_kernel,
        grid=(n_layers, n6 // tn),
        in_specs=[pl.BlockSpec((rows, d), lambda l, j: (0, 0)),
                  pl.BlockSpec((1, d, tn), lambda l, j: (l, 0, j)),
                  pl.BlockSpec((1, 1, tn), lambda l, j: (l, 0, j))],
        out_specs=pl.BlockSpec((1, rows, tn), lambda l, j: (l, 0, j)),
        out_shape=jax.ShapeDtypeStruct((n_layers, rows, n6), F32),
        compiler_params=_cparams(("parallel", "parallel")),
        name="adaln_mod",
    )(cc, w_ada, b_ada.reshape(n_layers, 1, n6))


def _seg_rsqrt(x2, m_ref):
    return lax.rsqrt(jnp.dot((x2 * x2).astype(BF16), m_ref[...], preferred_element_type=F32) + EPS)


def _rope(x, cos, sin, half):
    lane = lax.broadcasted_iota(jnp.int32, x.shape, 1)
    lo = (lane % (2 * half)) < half
    partner = jnp.where(lo, pltpu.roll(x, LANE - half, 1), pltpu.roll(x, half, 1))
    return x * cos + partner * sin


def _inproj_kernel(x_ref, mod_ref, g1_ref, w_ref, gains_ref, gcq_ref, gckv_ref, wuq_ref, wukvk_ref,
                   wukvv_ref, mm_ref, mx_ref, mbb_ref, *rest, use_rope):
    if use_rope:
        rope_ref = rest[0]
        outs = rest[1:]
    else:
        rope_ref = None
        outs = rest
    qa_ref, ka_ref, va_ref, qb_ref, kb_ref, vb_ref, qc_ref, kc_ref, vc_ref = outs

    x = x_ref[0]
    y = x * lax.rsqrt(jnp.mean(x * x, axis=-1, keepdims=True) + EPS) * g1_ref[...]
    h = y * (1.0 + mod_ref[0, 0, 1:2, :]) + mod_ref[0, 0, 0:1, :]
    p = jnp.dot(h.astype(BF16), w_ref[...], preferred_element_type=F32)

    def head_pair(x2, m_ref, gain_rows, rope_cols, half):
        r = _seg_rsqrt(x2, m_ref)
        out = []
        for i in range(2):
            sl = slice(i * LANE, (i + 1) * LANE)
            y = x2[:, sl] * r[:, sl] * gains_ref[gain_rows[i]:gain_rows[i] + 1, :]
            if rope_ref is not None and rope_cols[i] is not None:
                c0, s0 = rope_cols[i]
                y = _rope(y, rope_ref[:, c0:c0 + LANE], rope_ref[:, s0:s0 + LANE], half[i])
            out.append(y)
        return out

    r64 = (0, 128)
    rb = (256, 384)
    qa0, qa1 = head_pair(p[:, _QA:_QA + 256], mm_ref, (0, 0), (r64, r64), (16, 16))
    qa2, ka = head_pair(p[:, _QA + 256:_QA + 512], mm_ref, (0, 1), (r64, r64), (16, 16))
    for i, blk in enumerate((qa0, qa1, qa2)):
        qa_ref[0, i] = blk.astype(BF16)
    ka_ref[0, 0] = ka.astype(BF16)
    va_ref[0, 0] = p[:, _VA:_VA + LANE].astype(BF16)
    qc0, qc1 = head_pair(p[:, _QC:_QC + 256], mm_ref, (2, 2), (r64, r64), (16, 16))
    qc_ref[0, :, 0:LANE] = qc0.astype(BF16)
    qc_ref[0, :, LANE:2 * LANE] = qc1.astype(BF16)
    kc, kpe = head_pair(jnp.concatenate([p[:, _KC:_KC + LANE], p[:, _BKR:_BKR + LANE]], axis=1),
                        mx_ref, (3, 6), (r64, rb), (16, 8))
    kc_ref[0] = kc.astype(BF16)
    vc_ref[0] = p[:, _VC:_VC + LANE].astype(BF16)

    cq = p[:, _BCQ:_BCQ + 256]
    cq = cq * lax.rsqrt(jnp.sum(cq * cq, axis=-1, keepdims=True) * (1.0 / B_Q_RANK) + EPS) * gcq_ref[...]
    qb = jnp.dot(cq.astype(BF16), wuq_ref[...], preferred_element_type=F32)
    ckv = p[:, _BCKV:_BCKV + LANE]
    ckv = ckv * lax.rsqrt(jnp.mean(ckv * ckv, axis=-1, keepdims=True) + EPS) * gckv_ref[...]
    ckv = ckv.astype(BF16)
    kbn = jnp.dot(ckv, wukvk_ref[...], preferred_element_type=F32)
    vbv = jnp.dot(ckv, wukvv_ref[...], preferred_element_type=F32).astype(BF16)
    for i in range(B_HEADS // 2):
        vb_ref[0, i] = vbv[:, i * LANE:(i + 1) * LANE]
    for i in range(B_HEADS // 2):
        sl = slice(2 * i * LANE, (2 * i + 2) * LANE)
        q0, q1 = head_pair(qb[:, sl], mbb_ref, (4, 4), (rb, rb), (8, 8))
        k0, k1 = head_pair(kbn[:, sl], mbb_ref, (5, 5), (None, None), (8, 8))
        qb_ref[0, 2 * i] = q0.astype(BF16)
        qb_ref[0, 2 * i + 1] = q1.astype(BF16)
        kb_ref[0, 2 * i] = (k0 + kpe).astype(BF16)
        kb_ref[0, 2 * i + 1] = (k1 + kpe).astype(BF16)


def _inproj(xs, modv, kind, g1, w_in_p, gains, gcq, gckv, wuq, wukvk, wukvv, mm, mx, mbb, rope):
    b, n, d = xs.shape
    tm = min(PROJ_TILE, n)
    use_rope = rope is not None
    const = lambda *shape: pl.BlockSpec(shape, lambda i, j: (0,) * len(shape))
    in_specs = [pl.BlockSpec((1, tm, d), lambda i, j: (i, j, 0)),
                pl.BlockSpec((1, 1, 8, d), lambda i, j: (i, kind, 0, 0)),
                const(1, d), const(d, IN_PAD), const(8, LANE), const(1, 256), const(1, LANE),
                const(256, 768), const(LANE, 768), const(LANE, 384),
                const(256, 256), const(256, 256), const(256, 256)]
    args = [xs, modv, g1, w_in_p, gains, gcq, gckv, wuq, wukvk, wukvv, mm, mx, mbb]
    if use_rope:
        in_specs.append(pl.BlockSpec((tm, 512), lambda i, j: (j, 0)))
        args.append(rope)
    blocks = (3, 1, 1, B_HEADS, B_HEADS, B_HEADS // 2)
    widths = (256, 128, 128)
    out_specs = ([pl.BlockSpec((1, nb, tm, LANE), lambda i, j: (i, 0, j, 0)) for nb in blocks]
                 + [pl.BlockSpec((1, tm, w), lambda i, j: (i, j, 0)) for w in widths])
    out_shape = ([jax.ShapeDtypeStruct((b, nb, n, LANE), BF16) for nb in blocks]
                 + [jax.ShapeDtypeStruct((b, n, w), BF16) for w in widths])
    return pl.pallas_call(
        functools.partial(_inproj_kernel, use_rope=use_rope),
        grid=(b, n // tm),
        in_specs=in_specs,
        out_specs=out_specs,
        out_shape=out_shape,
        compiler_params=_cparams(("parallel", "parallel")),
        name="inproj_rope" if use_rope else "inproj_ctx",
    )(*args)


_A_MAPS = (lambda h: h % 3, lambda h: 0, lambda h: 0)
_B_MAPS = (lambda h: h, lambda h: h, lambda h: h // 2)


def _qk(q, k):
    return lax.dot_general(q, k, (((1,), (1,)), ((), ())), preferred_element_type=F32)


def _half_mask(q, half):
    lane = lax.broadcasted_iota(jnp.int32, q.shape, 1)
    return jnp.where((lane // 64) == half, q, jnp.zeros_like(q))


def _fused_passes(q, chunks, s_w, s_r, mb_r):
    mx = None
    acc = None
    for k_ref, v_ref, st, sz, off in chunks:
        s = _qk(q, k_ref[0, 0, st:st + sz, :])
        s_w[:, off:off + sz] = s
        for t in range(sz // LANE):
            blk = s[:, t * LANE:(t + 1) * LANE]
            mx = blk if mx is None else jnp.maximum(mx, blk)
        ps = [jnp.exp2(s_r[:, off + t * LANE:off + (t + 1) * LANE] - mb_r).astype(BF16)
              for t in range(sz // LANE)]
        v_ones = jnp.concatenate([v_ref[0, 0, st:st + sz, :], jnp.ones((sz, LANE), BF16)], axis=1)
        part = jnp.dot(jnp.concatenate(ps, axis=1), v_ones, preferred_element_type=F32)
        acc = part if acc is None else acc + part
    mb_new = jnp.broadcast_to(jnp.max(mx, axis=1, keepdims=True), mx.shape)
    return mb_new, (acc[:, :LANE] / acc[:, LANE:LANE + 1]).astype(BF16)


def _key_chunks(kc_ref, vc_ref, kl_ref, vl_ref):
    n_ctx = kc_ref.shape[2]
    chunks = [(kc_ref, vc_ref, 0, n_ctx, 0)]
    if kl_ref is not None:
        for c in range(kl_ref.shape[2] // KEY_CHUNK):
            chunks.append((kl_ref, vl_ref, c * KEY_CHUNK, KEY_CHUNK, n_ctx + c * KEY_CHUNK))
    return chunks


def _dense_kernel(q_ref, kc_ref, kl_ref, vcp_ref, vlp_ref, vcc_ref, vlc_ref, o_ref,
                  sa_sc, sb_sc, mba_sc, mbb_sc, *, mask_q, n_tiles, tiles_per_sample, tiles_per_head, n_sub):
    step = pl.program_id(0)
    tq = sa_sc.shape[0]

    @pl.when(step == 0)
    def _():
        sa_sc[...] = jnp.zeros(sa_sc.shape, F32)
        mba_sc[...] = jnp.zeros(mba_sc.shape, F32)

    if mask_q:
        first = jnp.minimum(n_sub * step, n_tiles - n_sub)
        half = ((first % tiles_per_sample) // tiles_per_head) // 3
    for k in range(n_sub):
        rows = slice(k * tq, (k + 1) * tq)
        q = q_ref[0, 0, rows, :]
        if mask_q:
            q = _half_mask(q, half)
        v_refs = (vcp_ref, vlp_ref) if k == 0 else (vcc_ref, vlc_ref)
        chunks = _key_chunks(kc_ref, v_refs[0], kl_ref, v_refs[1])
        s_w, s_r, mb_w, mb_r = ((sb_sc, sa_sc, mbb_sc, mba_sc) if k % 2 == 0
                                else (sa_sc, sb_sc, mba_sc, mbb_sc))
        mb_new, o_ref[rows, :] = _fused_passes(q, chunks, s_w, s_r, mb_r[...])
        mb_w[...] = mb_new


def _dense_attention(q, kc, vc, kl, vl, n_heads, maps, mask_q, name):
    b, _, nq, _ = q.shape
    qmap, kmap, vmap = maps
    n_ctx, n_lat = kc.shape[2], kl.shape[2]
    tq = Q_TILE
    tph = nq // tq
    n_sub = next(n for n in (8, 4, 2) if tph % n == 0)
    assert tph % n_sub == 0
    tps = n_heads * tph
    n_tiles = b * tps
    first = lambda s: jnp.minimum(n_sub * s, n_tiles - n_sub)
    before = lambda s: jnp.maximum(n_sub * s - 1, 0)
    kv_blk = lambda n, fmap, tile: pl.BlockSpec(
        (1, 1, n, LANE), lambda s: (tile(s) // tps, fmap((tile(s) % tps) // tph), 0, 0))
    q_spec = pl.BlockSpec(
        (1, 1, n_sub * tq, LANE),
        lambda s: (first(s) // tps, qmap((first(s) % tps) // tph), (first(s) % tph) // n_sub, 0))
    return pl.pallas_call(
        functools.partial(_dense_kernel, mask_q=mask_q, n_tiles=n_tiles, tiles_per_sample=tps,
                          tiles_per_head=tph, n_sub=n_sub),
        grid=(n_tiles // n_sub + 1,),
        in_specs=[q_spec, kv_blk(n_ctx, kmap, first), kv_blk(n_lat, kmap, first),
                  kv_blk(n_ctx, vmap, before), kv_blk(n_lat, vmap, before),
                  kv_blk(n_ctx, vmap, first), kv_blk(n_lat, vmap, first)],
        out_specs=pl.BlockSpec((n_sub * tq, LANE), lambda s: (s, 0)),
        out_shape=jax.ShapeDtypeStruct(((n_tiles + n_sub) * tq, LANE), BF16),
        scratch_shapes=[pltpu.VMEM((tq, n_ctx + n_lat), F32), pltpu.VMEM((tq, n_ctx + n_lat), F32),
                        pltpu.VMEM((tq, LANE), F32), pltpu.VMEM((tq, LANE), F32)],
        compiler_params=_cparams(("arbitrary",)),
        name=name,
    )(q, kc, kl, vc, vl, vc, vl)


def _dense_ctx_kernel(q_ref, k_ref, v_ref, o_ref, *, n_heads, maps, mask_q):
    qmap, kmap, vmap = maps
    n_keys = k_ref.shape[2]
    ones = jnp.ones((n_keys, LANE), BF16)
    for h in range(n_heads):
        q = q_ref[0, qmap(h)]
        if mask_q:
            q = _half_mask(q, h // 3)
        s = _qk(q, k_ref[0, kmap(h)])
        tiles = [s[:, t * LANE:(t + 1) * LANE] for t in range(n_keys // LANE)]
        mx = functools.reduce(jnp.maximum, tiles)
        mb = jnp.broadcast_to(jnp.max(mx, axis=1, keepdims=True), mx.shape)
        p = jnp.concatenate([jnp.exp2(t - mb).astype(BF16) for t in tiles], axis=1)
        o = jnp.dot(p, jnp.concatenate([v_ref[0, vmap(h)], ones], axis=1), preferred_element_type=F32)
        o_ref[0, h] = (o[:, :LANE] / o[:, LANE:LANE + 1]).astype(BF16)


def _dense_attention_ctx(q, kc, vc, n_heads, maps, mask_q, name):
    b, nqb, nq, _ = q.shape
    whole = lambda a: pl.BlockSpec((1,) + a.shape[1:], lambda i: (i, 0, 0, 0))
    return pl.pallas_call(
        functools.partial(_dense_ctx_kernel, n_heads=n_heads, maps=maps, mask_q=mask_q),
        grid=(b,),
        in_specs=[whole(q), whole(kc), whole(vc)],
        out_specs=pl.BlockSpec((1, n_heads, nq, LANE), lambda i: (i, 0, 0, 0)),
        out_shape=jax.ShapeDtypeStruct((b, n_heads, nq, LANE), BF16),
        compiler_params=_cparams(("parallel",)),
        name=name,
    )(q, kc, vc)


def _window_kernel(q_ref, kc_ref, vc_ref, sink_ref, *rest, use_window):
    if use_window:
        kl_ref, vl_ref, o_ref = rest
    else:
        (o_ref,) = rest
    tq = ROW_TILE
    span = tq + 2 * WINDOW
    kcx = kc_ref[0]
    ones_c = jnp.ones((kcx.shape[0], LANE), BF16)
    v_aug = jnp.concatenate([vc_ref[0], ones_c], axis=1)
    lane = lax.broadcasted_iota(jnp.int32, (tq, LANE), 1)
    for ti in range(q_ref.shape[1] // tq):
        rows = slice(ti * tq, (ti + 1) * tq)
        if use_window:
            n_lat = kl_ref.shape[1]
            t0 = (pl.program_id(1) * (q_ref.shape[1] // tq) + ti) * tq
            start = pl.multiple_of(jnp.clip(t0 - WINDOW, 0, n_lat - span), LANE)
            qpos = t0 + lax.broadcasted_iota(jnp.int32, (tq, span), 0)
            kpos = start + lax.broadcasted_iota(jnp.int32, (tq, span), 1)
            valid = jnp.abs(qpos - kpos) <= WINDOW
            kw = kl_ref[0, pl.ds(start, span), :]
            v_all = jnp.concatenate(
                [v_aug, jnp.concatenate([vl_ref[0, pl.ds(start, span), :], jnp.ones((span, LANE), BF16)], axis=1)],
                axis=0)
        else:
            v_all = v_aug
        res = []
        for h in range(C_HEADS):
            half = h // (C_HEADS // C_KV_HEADS)
            q = q_ref[0, rows, (h % 2) * LANE:(h % 2 + 1) * LANE]
            q = jnp.where((lane < 64) if half == 0 else (lane >= 64), q, jnp.zeros_like(q))
            sink = sink_ref[0:1, h:h + 1]
            s = _qk(q, kcx)
            if use_window:
                s = jnp.concatenate([s, jnp.where(valid, _qk(q, kw), NEG_INF)], axis=1)
            mx = s[:, 0:LANE]
            for t in range(1, s.shape[1] // LANE):
                mx = jnp.maximum(mx, s[:, t * LANE:(t + 1) * LANE])
            m = jnp.maximum(jnp.max(mx, axis=1, keepdims=True), sink)
            mb = jnp.broadcast_to(m, (tq, LANE))
            p = jnp.concatenate([jnp.exp2(s[:, t * LANE:(t + 1) * LANE] - mb).astype(BF16)
                                 for t in range(s.shape[1] // LANE)], axis=1)
            o = jnp.dot(p, v_all, preferred_element_type=F32)
            res.append(o[:, :LANE] / (o[:, LANE:LANE + 1] + jnp.exp2(sink - m)))
        for ob in range(2):
            o_ref[0, rows, ob * LANE:(ob + 1) * LANE] = jnp.where(lane < 64, res[ob], res[2 + ob]).astype(BF16)


def _window_attention(q, kc, vc, kl, vl, sink, name):
    b, nq, qw = q.shape
    tq = min(2 * ROW_TILE, nq)
    n_ctx = kc.shape[1]
    use_window = kl is not None
    in_specs = [pl.BlockSpec((1, tq, qw), lambda i, j: (i, j, 0)),
                pl.BlockSpec((1, n_ctx, LANE), lambda i, j: (i, 0, 0)),
                pl.BlockSpec((1, n_ctx, LANE), lambda i, j: (i, 0, 0)),
                pl.BlockSpec((8, LANE), lambda i, j: (0, 0))]
    args = [q, kc, vc, sink]
    if use_window:
        n_lat = kl.shape[1]
        in_specs += [pl.BlockSpec((1, n_lat, LANE), lambda i, j: (i, 0, 0)),
                     pl.BlockSpec((1, n_lat, LANE), lambda i, j: (i, 0, 0))]
        args += [kl, vl]
    return pl.pallas_call(
        functools.partial(_window_kernel, use_window=use_window),
        grid=(b, nq // tq),
        in_specs=in_specs,
        out_specs=pl.BlockSpec((1, tq, qw), lambda i, j: (i, j, 0)),
        out_shape=jax.ShapeDtypeStruct((b, nq, qw), BF16),
        compiler_params=_cparams(("parallel", "parallel")),
        name=name,
    )(*args)


def _outproj_kernel(*refs):
    oa = refs[0:A_HEADS]
    ob = refs[A_HEADS:A_HEADS + B_HEADS]
    oc_ref, x_ref, mod_ref, g2n_ref, w_ref, wr_ref, xo_ref, h2_ref, aff_ref = refs[A_HEADS + B_HEADS:]
    tm = x_ref.shape[1]
    low = lax.broadcasted_iota(jnp.int32, (tm, LANE), 1) < 64
    parts = [jnp.where(low, oa[i][...], oa[3 + i][...]) for i in range(3)]
    parts += [jnp.where(low, ob[2 * i][...], ob[2 * i + 1][...]) for i in range(3)]
    parts.append(oc_ref[0])
    mix = jnp.dot(jnp.concatenate(parts, axis=1), w_ref[...], preferred_element_type=F32)
    x = x_ref[0] + mod_ref[0, 0, 2:3, :] * mix
    xo_ref[0] = x
    y = x * lax.rsqrt(jnp.mean(x * x, axis=-1, keepdims=True) + EPS) * g2n_ref[...]
    h2 = y * (1.0 + mod_ref[0, 0, 4:5, :]) + mod_ref[0, 0, 3:4, :]
    h2_ref[0] = h2.astype(BF16)
    wr = wr_ref[...]
    wr_hi = wr.astype(BF16)
    wr_lo = (wr - wr_hi.astype(F32)).astype(BF16)
    h2_hi = h2.astype(BF16)
    h2_lo = (h2 - h2_hi.astype(F32)).astype(BF16)
    both = jnp.dot(h2_hi, jnp.concatenate([wr_hi, wr_lo], axis=1), preferred_element_type=F32)
    logits = both[:, :LANE] + both[:, LANE:] + jnp.dot(h2_lo, wr_hi, preferred_element_type=F32)
    logits = logits.T[0:aff_ref.shape[1], :]
    z = jnp.exp(logits - jnp.max(logits, axis=0, keepdims=True))
    aff_ref[0] = z / jnp.sum(z, axis=0, keepdims=True)


def _outproj(oa, ob, oc, xs, modv, kind, g2n, w_mix, wr_t, pad_rows, name):
    b, n, d = xs.shape
    tm = min(PROJ_TILE, n)
    e = N_EXPERTS
    const = lambda *shape: pl.BlockSpec(shape, lambda i, j: (0,) * len(shape))
    row = lambda w: pl.BlockSpec((1, tm, w), lambda i, j: (i, j, 0))
    n_heads = A_HEADS
    head = lambda h: pl.BlockSpec((tm, LANE), lambda i, j: ((pad_rows + (i * n_heads + h) * n) // tm + j, 0))
    return pl.pallas_call(
        _outproj_kernel,
        grid=(b, n // tm),
        in_specs=[head(h) for h in range(A_HEADS)] + [head(h) for h in range(B_HEADS)] + [
                  row(256), row(d),
                  pl.BlockSpec((1, 1, 8, d), lambda i, j: (i, kind, 0, 0)),
                  const(1, d), const(d, d), const(d, LANE)],
        out_specs=[row(d), row(d), pl.BlockSpec((1, e, tm), lambda i, j: (i, 0, j))],
        out_shape=[jax.ShapeDtypeStruct((b, n, d), F32), jax.ShapeDtypeStruct((b, n, d), BF16),
                   jax.ShapeDtypeStruct((b, e, n), F32)],
        compiler_params=_cparams(("parallel", "parallel")),
        name=name,
    )(*([oa] * A_HEADS), *([ob] * B_HEADS), oc, xs, modv, g2n, w_mix, wr_t)


def _prefix_exclusive(mask_f, tri):
    e, t = mask_f.shape
    ck = tri.shape[0]
    carry = jnp.zeros((e, 1), F32)
    parts, offs = [], [carry]
    for c in range(t // ck):
        blk = mask_f[:, c * ck:(c + 1) * ck]
        parts.append(jnp.dot(blk.astype(BF16), tri, preferred_element_type=F32) + carry)
        carry = carry + jnp.sum(blk, axis=1, keepdims=True)
        offs.append(carry)
    return jnp.concatenate(parts, axis=1) if len(parts) > 1 else parts[0], offs


def _topk_kernel(aff_ref, slot_ref, cnt_ref, *, cap, slot_stride):
    aff = aff_ref[0]
    e, t = aff.shape
    bits = pltpu.bitcast(aff, jnp.int32)
    kf = jnp.float32(cap)

    def search(i, lo):
        cand = lo | (jnp.int32(1) << (30 - i))
        n_ge = jnp.sum((bits >= cand).astype(F32), axis=1, keepdims=True)
        return jnp.where(n_ge >= kf, cand, lo)

    thr = lax.fori_loop(0, 31, search, jnp.zeros((e, 1), jnp.int32))
    gt = bits > thr
    eq = bits == thr
    need = kf - jnp.sum(gt.astype(F32), axis=1, keepdims=True)

    r = lax.broadcasted_iota(jnp.int32, (TOK_CHUNK, TOK_CHUNK), 0)
    c = lax.broadcasted_iota(jnp.int32, (TOK_CHUNK, TOK_CHUNK), 1)
    tri = (r < c).astype(BF16)
    tie_rank, _ = _prefix_exclusive(eq.astype(F32), tri)
    sel = gt | (eq & (tie_rank < need))
    slot, offs = _prefix_exclusive(sel.astype(F32), tri)
    base = pl.program_id(0) * slot_stride
    slot_ref[0] = jnp.where(sel, slot.astype(jnp.int32) + base, -1)
    lane = lax.broadcasted_iota(jnp.int32, (e, LANE), 1)
    cnt = jnp.zeros((e, LANE), jnp.int32)
    for ci, off in enumerate(offs):
        cnt = jnp.where(lane == ci, off.astype(jnp.int32) + base, cnt)
    cnt_ref[0] = cnt


def _topk(aff_t, cap, slot_stride, name):
    b, e, t = aff_t.shape
    return pl.pallas_call(
        functools.partial(_topk_kernel, cap=cap, slot_stride=slot_stride),
        grid=(b,),
        in_specs=[pl.BlockSpec((1, e, t), lambda i: (i, 0, 0))],
        out_specs=[pl.BlockSpec((1, e, t), lambda i: (i, 0, 0)),
                   pl.BlockSpec((1, e, LANE), lambda i: (i, 0, 0))],
        out_shape=[jax.ShapeDtypeStruct((b, e, t), jnp.int32), jax.ShapeDtypeStruct((b, e, LANE), jnp.int32)],
        compiler_params=_cparams(("parallel",)),
        name=name,
    )(aff_t)


def _moe_kernel(cnt_ref, h_ref, slot_ref, slotp_ref, gate_ref, wg_ref, wu_ref, wd_ref, mod_ref, x_hbm, o_hbm,
                acc_sc, xe_sc, gs_sc, ye_sc, in_sem, out_sem, *, n_chunks, n_sb):
    bi = pl.program_id(0)
    ei = pl.program_id(1)
    n_b = pl.num_programs(0)
    n_e = pl.num_programs(1)

    def chunk_rows(c):
        return pl.ds(c * TOK_CHUNK, TOK_CHUNK)

    def load_dma(c):
        return pltpu.make_async_copy(x_hbm.at[bi, chunk_rows(c), :], acc_sc.at[chunk_rows(c), :], in_sem.at[c])

    def store_dma(sample, c):
        return pltpu.make_async_copy(acc_sc.at[chunk_rows(c), :], o_hbm.at[sample, chunk_rows(c), :],
                                     out_sem.at[c])

    def start_x_load():
        @pl.when(ei == 0)
        def _():
            for c in range(n_chunks):
                @pl.when(bi > 0)
                def _(c=c):
                    store_dma(bi - 1, c).wait()
                load_dma(c).start()

    def wait_for_x(at_expert):
        @pl.when(ei == at_expert)
        def _():
            for c in range(n_chunks):
                load_dma(c).wait()

    g2 = mod_ref[0, 0, 5:6, :]
    cap = n_sb * SLOT_BLOCK

    def counts_of(e):
        base = (bi * n_e + e) * 32
        return [cnt_ref[base + c] for c in range(n_chunks + 1)]

    def all_chunks_at_most(cs, limit):
        ok = cs[1] - cs[0] <= limit
        for c in range(1, n_chunks):
            ok = ok & (cs[c + 1] - cs[c] <= limit)
        return ok

    odd = (ei % 2) == 1
    counts = counts_of(ei)
    counts_other = counts_of(jnp.where(odd, ei - 1, ei + 1))
    pair_narrow = all_chunks_at_most(counts, PAIR_LIMIT) & all_chunks_at_most(counts_other, PAIR_LIMIT)

    def ffn():
        xe = xe_sc[0:cap, :].astype(BF16)
        f = wg_ref.shape[2]
        fb = 512
        y = None
        for f0 in range(0, f, fb):
            hg = jnp.dot(xe, wg_ref[0, :, f0:f0 + fb], preferred_element_type=F32)
            hu = jnp.dot(xe, wu_ref[0, :, f0:f0 + fb], preferred_element_type=F32)
            hid = (hg * jax.nn.sigmoid(hg) * hu).astype(BF16)
            part = jnp.dot(hid, wd_ref[0, f0:f0 + fb, :], preferred_element_type=F32)
            y = part if y is None else y + part
        return y

    def tok(c):
        return slice(c * TOK_CHUNK, (c + 1) * TOK_CHUNK)

    def slot_columns(ref):
        pad = jnp.zeros((LANE - n_chunks, TOK_CHUNK), F32)
        return jnp.concatenate([ref[0, 0].astype(F32), pad], axis=0).T

    @pl.when(pair_narrow)
    def _():
        xe_sc[...] = jnp.zeros_like(xe_sc)
        gs_sc[...] = jnp.zeros_like(gs_sc)
        srow = lax.broadcasted_iota(jnp.int32, (GATHER_WINDOW, TOK_CHUNK), 0)
        for c in range(n_chunks):
            g0 = pl.multiple_of((counts[c] // 8) * 8, 8)
            oh = (slot_ref[0, 0, c:c + 1, :] - g0) == srow
            xe_sc[pl.ds(g0, GATHER_WINDOW), :] += jnp.dot(
                oh.astype(BF16), h_ref[0, tok(c), :], preferred_element_type=F32)
            gs_sc[pl.ds(g0, GATHER_WINDOW), :] += jnp.sum(
                jnp.where(oh, gate_ref[0, 0, c:c + 1, :], 0.0), axis=1, keepdims=True)
        ye = ye_sc.at[ei % 2]
        ye[0:cap, :] = (ffn() * gs_sc[0:cap, :]).astype(BF16)
        ye[cap:cap + SLOT_BLOCK, :] = jnp.zeros((SLOT_BLOCK, ye_sc.shape[2]), BF16)
        start_x_load()

        @pl.when(odd)
        def _():
            wait_for_x(1)
            cols_prev = slot_columns(slotp_ref)
            cols_cur = slot_columns(slot_ref)
            scol = lax.broadcasted_iota(jnp.int32, (TOK_CHUNK, SLOT_BLOCK), 1).astype(F32)
            for c in range(n_chunks):
                w_prev = pl.multiple_of((counts_other[c] // 16) * 16, 16)
                w_cur = pl.multiple_of((counts[c] // 16) * 16, 16)
                oh_t = jnp.concatenate(
                    [((cols_prev[:, c:c + 1] - w_prev.astype(F32)) == scol).astype(BF16),
                     ((cols_cur[:, c:c + 1] - w_cur.astype(F32)) == scol).astype(BF16)], axis=1)
                rows = jnp.concatenate([ye_sc[0, pl.ds(w_prev, SLOT_BLOCK), :],
                                        ye_sc[1, pl.ds(w_cur, SLOT_BLOCK), :]], axis=0)
                acc_sc[tok(c), :] += g2 * jnp.dot(oh_t, rows, preferred_element_type=F32)

    @pl.when(jnp.logical_not(pair_narrow))
    def _():
        xe_sc[...] = jnp.zeros_like(xe_sc)
        gs_sc[...] = jnp.zeros_like(gs_sc)
        srow = lax.broadcasted_iota(jnp.int32, (SLOT_BLOCK, TOK_CHUNK), 0)
        base = (bi * n_e + ei) * 32

        def overlap(c, sb):
            return (cnt_ref[base + c] < (sb + 1) * SLOT_BLOCK) & (cnt_ref[base + c + 1] > sb * SLOT_BLOCK)

        def onehot(c, sb):
            return slot_ref[0, 0, pl.ds(c, 1), :] == (srow + sb * SLOT_BLOCK)

        def dyn_tok(c):
            return pl.ds(pl.multiple_of(c * TOK_CHUNK, TOK_CHUNK), TOK_CHUNK)

        def gather_chunk(c, carry):
            for sb in range(n_sb):
                @pl.when(overlap(c, sb))
                def _(sb=sb):
                    oh = onehot(c, sb)
                    rows = slice(sb * SLOT_BLOCK, (sb + 1) * SLOT_BLOCK)
                    xe_sc[rows] += jnp.dot(oh.astype(BF16), h_ref[0, dyn_tok(c), :], preferred_element_type=F32)
                    g = jnp.where(oh, gate_ref[0, 0, pl.ds(c, 1), :], 0.0)
                    gs_sc[rows] += jnp.sum(g, axis=1, keepdims=True)
            return carry

        lax.fori_loop(0, n_chunks, gather_chunk, 0)
        ye_sc[0, 0:cap, :] = (ffn() * gs_sc[0:cap, :]).astype(BF16)
        start_x_load()
        wait_for_x(0)

        def scatter_chunk(c, carry):
            for sb in range(n_sb):
                @pl.when(overlap(c, sb))
                def _(sb=sb):
                    oh_t = onehot(c, sb).astype(F32).T.astype(BF16)
                    rows = slice(sb * SLOT_BLOCK, (sb + 1) * SLOT_BLOCK)
                    acc_sc[dyn_tok(c), :] += g2 * jnp.dot(oh_t, ye_sc[0, rows, :], preferred_element_type=F32)
            return carry

        lax.fori_loop(0, n_chunks, scatter_chunk, 0)

    @pl.when(ei == n_e - 1)
    def _():
        for c in range(n_chunks):
            store_dma(bi, c).start()

    @pl.when((ei == n_e - 1) & (bi == n_b - 1))
    def _():
        for c in range(n_chunks):
            store_dma(bi, c).wait()


def _moe(h2, slot, gate, cnt, wg, wu, wd, xs, modv, kind, cap, name):
    b, t, d = h2.shape
    e = slot.shape[1]
    f = wg.shape[2]
    n_chunks = t // TOK_CHUNK
    n_sb = cap // SLOT_BLOCK
    slot4 = slot.reshape(b, e, n_chunks, TOK_CHUNK)
    gate4 = gate.reshape(b, e, n_chunks, TOK_CHUNK)
    cnt_flat = cnt[:, :, :32].reshape(-1)
    grid_spec = pltpu.PrefetchScalarGridSpec(
        num_scalar_prefetch=1,
        grid=(b, e),
        in_specs=[pl.BlockSpec((1, t, d), lambda i, j, s: (i, 0, 0)),
                  pl.BlockSpec((1, 1, n_chunks, TOK_CHUNK), lambda i, j, s: (i, j, 0, 0)),
                  pl.BlockSpec((1, 1, n_chunks, TOK_CHUNK), lambda i, j, s: (i, jnp.maximum(j - 1, 0), 0, 0)),
                  pl.BlockSpec((1, 1, n_chunks, TOK_CHUNK), lambda i, j, s: (i, j, 0, 0)),
                  pl.BlockSpec((1, d, f), lambda i, j, s: (j, 0, 0)),
                  pl.BlockSpec((1, d, f), lambda i, j, s: (j, 0, 0)),
                  pl.BlockSpec((1, f, d), lambda i, j, s: (j, 0, 0)),
                  pl.BlockSpec((1, 1, 8, d), lambda i, j, s: (i, kind, 0, 0)),
                  pl.BlockSpec(memory_space=pl.ANY)],
        out_specs=pl.BlockSpec(memory_space=pl.ANY),
        scratch_shapes=[pltpu.VMEM((t, d), F32),
                        pltpu.VMEM((cap + 2 * SLOT_BLOCK, d), F32), pltpu.VMEM((cap + 2 * SLOT_BLOCK, 1), F32),
                        pltpu.VMEM((2, cap + SLOT_BLOCK, d), BF16),
                        pltpu.SemaphoreType.DMA((n_chunks,)), pltpu.SemaphoreType.DMA((n_chunks,))],
    )
    return pl.pallas_call(
        functools.partial(_moe_kernel, n_chunks=n_chunks, n_sb=n_sb),
        grid_spec=grid_spec,
        out_shape=jax.ShapeDtypeStruct((b, t, d), F32),
        compiler_params=_cparams(("arbitrary", "arbitrary")),
        name=name,
    )(cnt_flat, h2, slot4, slot4, gate4, wg, wu, wd, modv, xs)


def _cast_kernel(w_ref, o_ref):
    o_ref[...] = w_ref[0].astype(BF16)


def _layer_bf16(w, layer):
    _, e, r, c = w.shape
    return pl.pallas_call(
        _cast_kernel,
        grid=(e,),
        in_specs=[pl.BlockSpec((1, 1, r, c), lambda i: (layer, i, 0, 0))],
        out_specs=pl.BlockSpec((1, r, c), lambda i: (i, 0, 0)),
        out_shape=jax.ShapeDtypeStruct((e, r, c), BF16),
        compiler_params=_cparams(("parallel",)),
        name="expert_weights_bf16",
    )(w)


def _pad_lanes(v, width):
    return jnp.pad(v, (0, width - v.shape[0]))


def kernel(x, c, ctx, c_ctx, w_ada, b_ada, norm1_g, norm2_g, w_in, a_q_norm, a_k_norm, b_cq_norm, b_ckv_norm, w_uq, w_ukv, b_qn_norm, b_kn_norm, b_qr_norm, b_kr_norm, c_q_norm, c_k_norm, c_sink, w_out, w_router, w_e_gate, w_e_up, w_e_down):
    b, t, d = x.shape
    n_ctx = ctx.shape[1]
    depth = w_ada.shape[0]
    assert n_ctx == ROW_TILE and t % KEY_CHUNK == 0 and t % GRID_W == 0
    cap = max(1, CAP_FACTOR * t // N_EXPERTS)
    cap_c = max(1, CAP_FACTOR * n_ctx // N_EXPERTS)
    assert cap % SLOT_BLOCK == 0 and (b * cap_c) % SLOT_BLOCK == 0

    rows = ((b + 1 + 7) // 8) * 8
    cc = jnp.zeros((rows, d), F32).at[:b].set(c).at[b].set(c_ctx)
    mod = _modulation(cc, w_ada, b_ada)

    rope = _rope_tables(t)
    mm, mx, mbb = (jnp.asarray(m, BF16) for m in _seg_matrices())
    in_perm = _in_perm()
    uq_perm = _uq_perm()
    ukv_pk, ukv_pv = _ukv_perms()
    oa_rows = np.asarray(_pair_cols(0, A_HEADS))
    oc_rows = np.asarray(_pair_cols(0, C_HEADS)) + 768
    sa = HEAD_DIM ** -0.5
    sb = (B_NOPE + B_ROPE) ** -0.5
    z32, z64 = jnp.zeros((32,), F32), jnp.zeros((64,), F32)

    for l in range(depth):
        last = l == depth - 1
        m6 = mod[l].reshape(rows, 6, d)
        m8 = jnp.pad(m6, ((0, 0), (0, 2), (0, 0)))
        modv = jnp.stack([jnp.broadcast_to(m8[b][None], (b, 8, d)), m8[:b]], axis=1)

        w_in_p = _take_cols(w_in[l], in_perm).astype(BF16)
        wuq = jnp.pad(_take_cols(w_uq[l], uq_perm), ((0, 256 - B_Q_RANK), (0, 0))).astype(BF16)
        wukvk = _take_cols(w_ukv[l], ukv_pk).astype(BF16)
        wukvv = jnp.take(w_ukv[l], jnp.asarray(ukv_pv, jnp.int32), axis=1).astype(BF16)
        gains = jnp.stack([
            jnp.tile(a_q_norm[l], 2) * (sa * LOG2E), jnp.tile(a_k_norm[l], 2),
            jnp.tile(c_q_norm[l], 2) * (sa * LOG2E), jnp.tile(c_k_norm[l], 2),
            jnp.concatenate([b_qn_norm[l] * (sb * LOG2E), b_qr_norm[l] * (sb * LOG2E), z32]),
            jnp.concatenate([b_kn_norm[l], z64]),
            jnp.concatenate([z64, b_kr_norm[l], z32]),
            jnp.zeros((LANE,), F32)])
        gcq = _pad_lanes(b_cq_norm[l], 256)[None]
        gckv = b_ckv_norm[l][None]
        g1 = norm1_g[l][None]
        g2n = norm2_g[l][None]
        mix_rows = np.concatenate([oa_rows, np.arange(384, 768), oc_rows])
        w_mix = jnp.take(w_out[l], jnp.asarray(mix_rows, jnp.int32), axis=0).astype(BF16)
        wr_t = jnp.pad(w_router[l], ((0, 0), (0, LANE - N_EXPERTS)))
        sink = jnp.zeros((8, LANE), F32).at[0, :C_HEADS].set(c_sink[l] * LOG2E)
        wg = _layer_bf16(w_e_gate, l)
        wu = _layer_bf16(w_e_up, l)
        wd = _layer_bf16(w_e_down, l)

        shared = (g1, w_in_p, gains, gcq, gckv, wuq, wukvk, wukvv, mm, mx, mbb)
        qa, ka, va, qb, kb, vb, qc, kc, vc = _inproj(x, modv, 1, *shared, rope)
        qac, kac, vac, qbc, kbc, vbc, qcc, kcc, vcc = _inproj(ctx, modv, 0, *shared, None)

        o_a = _dense_attention(qa, kac, vac, ka, va, A_HEADS, _A_MAPS, True, "attn_a")
        o_b = _dense_attention(qb, kbc, vbc, kb, vb, B_HEADS, _B_MAPS, False, "attn_b")
        o_c = _window_attention(qc, kcc, vcc, kc, vc, sink, "attn_c")
        x_mid, h2, aff = _outproj(o_a, o_b, o_c, x, modv, 1, g2n, w_mix, wr_t, Q_TILE, "outproj")
        slot, cnt = _topk(aff, cap, 0, "topk")
        x = _moe(h2, slot, aff, cnt, wg, wu, wd, x_mid, modv, 1, cap, "moe")

        if not last:
            o_ac = _dense_attention_ctx(qac, kac, vac, A_HEADS, _A_MAPS, True, "attn_a_ctx")
            o_bc = _dense_attention_ctx(qbc, kbc, vbc, B_HEADS, _B_MAPS, False, "attn_b_ctx")
            o_cc = _window_attention(qcc, kcc, vcc, None, None, sink, "attn_c_ctx")
            flat = lambda o: o.reshape(-1, LANE)
            c_mid, hc2, aff_c = _outproj(flat(o_ac), flat(o_bc), o_cc, ctx, modv, 0, g2n, w_mix, wr_t, 0,
                                         "outproj_ctx")
            slot_c, cnt_c = _topk(aff_c, cap_c, cap_c, "topk_ctx")
            e = N_EXPERTS
            slot_f = jnp.transpose(slot_c, (1, 0, 2)).reshape(1, e, b * n_ctx)
            gate_f = jnp.transpose(aff_c, (1, 0, 2)).reshape(1, e, b * n_ctx)
            starts = jnp.transpose(cnt_c[:, :, 0], (1, 0))
            cnt_f = jnp.concatenate([starts, cnt_c[b - 1, :, 1:2],
                                     jnp.zeros((e, LANE - b - 1), jnp.int32)], axis=1)[None]
            ctx = _moe(hc2.reshape(1, b * n_ctx, d), slot_f, gate_f, cnt_f, wg, wu, wd,
                       c_mid.reshape(1, b * n_ctx, d), modv, 0, b * cap_c, "moe_ctx").reshape(b, n_ctx, d)
    return x
```

```python
import functools
import math

import numpy as np
import jax
import jax.numpy as jnp
from jax import lax
from jax.experimental import pallas as pl
from jax.experimental.pallas import tpu as pltpu

F32 = jnp.float32
BF16 = jnp.bfloat16

GRID_W = 64
ROPE_THETA = 10000.0
EPS = 1e-6
NEG_INF = -1e30
HEAD_DIM = 64
A_HEADS, A_KV_HEADS = 6, 2
B_HEADS, B_Q_RANK, B_KV_RANK, B_NOPE, B_ROPE, B_V = 6, 192, 128, 64, 32, 64
C_HEADS, C_KV_HEADS = 4, 2
N_EXPERTS = 16
CAP_FACTOR = 2
WINDOW = 128

LANE = 128
ROW_TILE = 256
PROJ_TILE = 1024
Q_TILE = 512
KEY_CHUNK = 512
LOG2E = 1.4426950408889634
SLOT_BLOCK = 128
TOK_CHUNK = 256
GATHER_WINDOW = 144
PAIR_LIMIT = SLOT_BLOCK - 16
VMEM_LIMIT = 56 * 1024 * 1024

_QA, _KA, _VA = 0, 384, 512
_QC, _KC, _VC = 640, 896, 1024
_BCQ, _BCKV, _BKR = 1152, 1408, 1536
IN_PAD = 1664
_ORIG = dict(aq=0, ak=384, av=512, bcq=640, bckv=832, bkr=960, cq=992, ck=1248, cv=1376)


def _cparams(sem, vmem=VMEM_LIMIT):
    return pltpu.CompilerParams(dimension_semantics=sem, vmem_limit_bytes=vmem)


def _pair_cols(base, n_heads):
    half = n_heads // 2
    cols = []
    for i in range(half):
        cols += list(range(base + i * 64, base + (i + 1) * 64))
        cols += list(range(base + (half + i) * 64, base + (half + i + 1) * 64))
    return cols


def _in_perm():
    perm = -np.ones((IN_PAD,), np.int64)
    perm[_QA:_QA + 384] = _pair_cols(_ORIG["aq"], A_HEADS)
    perm[_KA:_KA + 128] = np.arange(_ORIG["ak"], _ORIG["ak"] + 128)
    perm[_VA:_VA + 128] = np.arange(_ORIG["av"], _ORIG["av"] + 128)
    perm[_QC:_QC + 256] = _pair_cols(_ORIG["cq"], C_HEADS)
    perm[_KC:_KC + 128] = np.arange(_ORIG["ck"], _ORIG["ck"] + 128)
    perm[_VC:_VC + 128] = np.arange(_ORIG["cv"], _ORIG["cv"] + 128)
    perm[_BCQ:_BCQ + B_Q_RANK] = np.arange(_ORIG["bcq"], _ORIG["bcq"] + B_Q_RANK)
    perm[_BCKV:_BCKV + 128] = np.arange(_ORIG["bckv"], _ORIG["bckv"] + 128)
    perm[_BKR + 64:_BKR + 96] = np.arange(_ORIG["bkr"], _ORIG["bkr"] + 32)
    return perm


def _take_cols(w, perm):
    safe = np.where(perm >= 0, perm, 0)
    out = jnp.take(w, jnp.asarray(safe, jnp.int32), axis=1)
    return jnp.where(jnp.asarray(perm >= 0)[None, :], out, 0.0)


def _uq_perm():
    perm = -np.ones((B_HEADS * LANE,), np.int64)
    for h in range(B_HEADS):
        perm[h * LANE:h * LANE + 96] = np.arange(h * 96, h * 96 + 96)
    return perm


def _ukv_perms():
    pk = -np.ones((B_HEADS * LANE,), np.int64)
    pv = np.zeros((B_HEADS * 64,), np.int64)
    for h in range(B_HEADS):
        pk[h * LANE:h * LANE + 64] = np.arange(h * 128, h * 128 + 64)
        pv[h * 64:(h + 1) * 64] = np.arange(h * 128 + 64, h * 128 + 128)
    return pk, pv


def _seg_matrices():
    i = np.arange(LANE)
    m64 = (i[:, None] // 64 == i[None, :] // 64).astype(np.float32) / 64.0
    mb = np.zeros((LANE, LANE), np.float32)
    mb[:64, :64] = 1.0 / 64.0
    mb[64:96, 64:96] = 1.0 / 32.0
    def pair(a, b):
        out = np.zeros((2 * LANE, 2 * LANE), np.float32)
        out[:LANE, :LANE] = a
        out[LANE:, LANE:] = b
        return out

    return pair(m64, m64), pair(m64, mb), pair(mb, mb)


def _rope_tables(length):
    rows = length // GRID_W
    t = np.arange(rows * GRID_W)
    row = np.repeat(np.arange(rows), GRID_W).astype(np.float32)
    col = (t % GRID_W).astype(np.float32)

    def tabs(dim):
        axis_dim = dim // 2
        inv = (np.float32(ROPE_THETA) ** (-np.arange(0, axis_dim, 2, dtype=np.float32) / axis_dim)).astype(np.float32)
        ar = row[:, None] * inv[None, :]
        ac = col[:, None] * inv[None, :]
        cos = np.concatenate([np.cos(ar), np.cos(ar), np.cos(ac), np.cos(ac)], axis=-1)
        sin = np.concatenate([-np.sin(ar), np.sin(ar), -np.sin(ac), np.sin(ac)], axis=-1)
        return cos.astype(np.float32), sin.astype(np.float32)

    c64, s64 = tabs(HEAD_DIM)
    cb, sb = tabs(B_ROPE)
    n = c64.shape[0]
    ones64, zeros64 = np.ones((n, 64), np.float32), np.zeros((n, 64), np.float32)
    ones32, zeros32 = np.ones((n, 32), np.float32), np.zeros((n, 32), np.float32)
    return jnp.asarray(np.concatenate([c64, c64, s64, s64,
                                       ones64, cb, ones32, zeros64, sb, zeros32], axis=-1))


def _mod_kernel(c_ref, w_ref, b_ref, o_ref):
    cv = c_ref[...]
    a = (cv * jax.nn.sigmoid(cv)).astype(BF16)
    o_ref[0] = jnp.dot(a, w_ref[0].astype(BF16), preferred_element_type=F32) + b_ref[0]


def _modulation(cc, w_ada, b_ada):
    n_layers, d, n6 = w_ada.shape
    rows = cc.shape[0]
    tn = 1536
    return pl.pallas_call(
        _mod_kernel,
        grid=(n_layers, n6 // tn),
        in_specs=[pl.BlockSpec((rows, d), lambda l, j: (0, 0)),
                  pl.BlockSpec((1, d, tn), lambda l, j: (l, 0, j)),
                  pl.BlockSpec((1, 1, tn), lambda l, j: (l, 0, j))],
        out_specs=pl.BlockSpec((1, rows, tn), lambda l, j: (l, 0, j)),
        out_shape=jax.ShapeDtypeStruct((n_layers, rows, n6), F32),
        compiler_params=_cparams(("parallel", "parallel")),
        name="adaln_mod",
    )(cc, w_ada, b_ada.reshape(n_layers, 1, n6))


def _seg_rsqrt(x2, m_ref):
    return lax.rsqrt(jnp.dot((x2 * x2).astype(BF16), m_ref[...], preferred_element_type=F32) + EPS)


def _rope(x, cos, sin, half):
    lane = lax.broadcasted_iota(jnp.int32, x.shape, 1)
    lo = (lane % (2 * half)) < half
    partner = jnp.where(lo, pltpu.roll(x, LANE - half, 1), pltpu.roll(x, half, 1))
    return x * cos + partner * sin


def _inproj_kernel(x_ref, mod_ref, g1_ref, w_ref, gains_ref, gcq_ref, gckv_ref, wuq_ref, wukvk_ref,
                   wukvv_ref, mm_ref, mx_ref, mbb_ref, *rest, use_rope):
    if use_rope:
        rope_ref = rest[0]
        outs = rest[1:]
    else:
        rope_ref = None
        outs = rest
    qa_ref, ka_ref, va_ref, qb_ref, kb_ref, vb_ref, qc_ref, kc_ref, vc_ref = outs

    x = x_ref[0]
    y = x * lax.rsqrt(jnp.mean(x * x, axis=-1, keepdims=True) + EPS) * g1_ref[...]
    h = y * (1.0 + mod_ref[0, 0, 1:2, :]) + mod_ref[0, 0, 0:1, :]
    p = jnp.dot(h.astype(BF16), w_ref[...], preferred_element_type=F32)

    def head_pair(x2, m_ref, gain_rows, rope_cols, half):
        r = _seg_rsqrt(x2, m_ref)
        out = []
        for i in range(2):
            sl = slice(i * LANE, (i + 1) * LANE)
            y = x2[:, sl] * r[:, sl] * gains_ref[gain_rows[i]:gain_rows[i] + 1, :]
            if rope_ref is not None and rope_cols[i] is not None:
                c0, s0 = rope_cols[i]
                y = _rope(y, rope_ref[:, c0:c0 + LANE], rope_ref[:, s0:s0 + LANE], half[i])
            out.append(y)
        return out

    r64 = (0, 128)
    rb = (256, 384)
    qa0, qa1 = head_pair(p[:, _QA:_QA + 256], mm_ref, (0, 0), (r64, r64), (16, 16))
    qa2, ka = head_pair(p[:, _QA + 256:_QA + 512], mm_ref, (0, 1), (r64, r64), (16, 16))
    for i, blk in enumerate((qa0, qa1, qa2)):
        qa_ref[0, i] = blk.astype(BF16)
    ka_ref[0, 0] = ka.astype(BF16)
    va_ref[0, 0] = p[:, _VA:_VA + LANE].astype(BF16)
    qc0, qc1 = head_pair(p[:, _QC:_QC + 256], mm_ref, (2, 2), (r64, r64), (16, 16))
    qc_ref[0, :, 0:LANE] = qc0.astype(BF16)
    qc_ref[0, :, LANE:2 * LANE] = qc1.astype(BF16)
    kc, kpe = head_pair(jnp.concatenate([p[:, _KC:_KC + LANE], p[:, _BKR:_BKR + LANE]], axis=1),
                        mx_ref, (3, 6), (r64, rb), (16, 8))
    kc_ref[0] = kc.astype(BF16)
    vc_ref[0] = p[:, _VC:_VC + LANE].astype(BF16)

    cq = p[:, _BCQ:_BCQ + 256]
    cq = cq * lax.rsqrt(jnp.sum(cq * cq, axis=-1, keepdims=True) * (1.0 / B_Q_RANK) + EPS) * gcq_ref[...]
    qb = jnp.dot(cq.astype(BF16), wuq_ref[...], preferred_element_type=F32)
    ckv = p[:, _BCKV:_BCKV + LANE]
    ckv = ckv * lax.rsqrt(jnp.mean(ckv * ckv, axis=-1, keepdims=True) + EPS) * gckv_ref[...]
    ckv = ckv.astype(BF16)
    kbn = jnp.dot(ckv, wukvk_ref[...], preferred_element_type=F32)
    vbv = jnp.dot(ckv, wukvv_ref[...], preferred_element_type=F32).astype(BF16)
    for i in range(B_HEADS // 2):
        vb_ref[0, i] = vbv[:, i * LANE:(i + 1) * LANE]
    for i in range(B_HEADS // 2):
        sl = slice(2 * i * LANE, (2 * i + 2) * LANE)
        q0, q1 = head_pair(qb[:, sl], mbb_ref, (4, 4), (rb, rb), (8, 8))
        k0, k1 = head_pair(kbn[:, sl], mbb_ref, (5, 5), (None, None), (8, 8))
        qb_ref[0, 2 * i] = q0.astype(BF16)
        qb_ref[0, 2 * i + 1] = q1.astype(BF16)
        kb_ref[0, 2 * i] = (k0 + kpe).astype(BF16)
        kb_ref[0, 2 * i + 1] = (k1 + kpe).astype(BF16)


def _inproj(xs, modv, kind, g1, w_in_p, gains, gcq, gckv, wuq, wukvk, wukvv, mm, mx, mbb, rope):
    b, n, d = xs.shape
    tm = min(PROJ_TILE, n)
    use_rope = rope is not None
    const = lambda *shape: pl.BlockSpec(shape, lambda i, j: (0,) * len(shape))
    in_specs = [pl.BlockSpec((1, tm, d), lambda i, j: (i, j, 0)),
                pl.BlockSpec((1, 1, 8, d), lambda i, j: (i, kind, 0, 0)),
                const(1, d), const(d, IN_PAD), const(8, LANE), const(1, 256), const(1, LANE),
                const(256, 768), const(LANE, 768), const(LANE, 384),
                const(256, 256), const(256, 256), const(256, 256)]
    args = [xs, modv, g1, w_in_p, gains, gcq, gckv, wuq, wukvk, wukvv, mm, mx, mbb]
    if use_rope:
        in_specs.append(pl.BlockSpec((tm, 512), lambda i, j: (j, 0)))
        args.append(rope)
    blocks = (3, 1, 1, B_HEADS, B_HEADS, B_HEADS // 2)
    widths = (256, 128, 128)
    out_specs = ([pl.BlockSpec((1, nb, tm, LANE), lambda i, j: (i, 0, j, 0)) for nb in blocks]
                 + [pl.BlockSpec((1, tm, w), lambda i, j: (i, j, 0)) for w in widths])
    out_shape = ([jax.ShapeDtypeStruct((b, nb, n, LANE), BF16) for nb in blocks]
                 + [jax.ShapeDtypeStruct((b, n, w), BF16) for w in widths])
    return pl.pallas_call(
        functools.partial(_inproj_kernel, use_rope=use_rope),
        grid=(b, n // tm),
        in_specs=in_specs,
        out_specs=out_specs,
        out_shape=out_shape,
        compiler_params=_cparams(("parallel", "parallel")),
        name="inproj_rope" if use_rope else "inproj_ctx",
    )(*args)


_A_MAPS = (lambda h: h % 3, lambda h: 0, lambda h: 0)
_B_MAPS = (lambda h: h, lambda h: h, lambda h: h // 2)


def _qk(q, k):
    return lax.dot_general(q, k, (((1,), (1,)), ((), ())), preferred_element_type=F32)


def _half_mask(q, half):
    lane = lax.broadcasted_iota(jnp.int32, q.shape, 1)
    return jnp.where((lane // 64) == half, q, jnp.zeros_like(q))


def _fused_passes(q, chunks, s_w, s_r, mb_r):
    mx = None
    acc = None
    for k_ref, v_ref, st, sz, off in chunks:
        s = _qk(q, k_ref[0, 0, st:st + sz, :])
        s_w[:, off:off + sz] = s
        for t in range(sz // LANE):
            blk = s[:, t * LANE:(t + 1) * LANE]
            mx = blk if mx is None else jnp.maximum(mx, blk)
        ps = [jnp.exp2(s_r[:, off + t * LANE:off + (t + 1) * LANE] - mb_r).astype(BF16)
              for t in range(sz // LANE)]
        v_ones = jnp.concatenate([v_ref[0, 0, st:st + sz, :], jnp.ones((sz, LANE), BF16)], axis=1)
        part = jnp.dot(jnp.concatenate(ps, axis=1), v_ones, preferred_element_type=F32)
        acc = part if acc is None else acc + part
    mb_new = jnp.broadcast_to(jnp.max(mx, axis=1, keepdims=True), mx.shape)
    return mb_new, (acc[:, :LANE] / acc[:, LANE:LANE + 1]).astype(BF16)


def _key_chunks(kc_ref, vc_ref, kl_ref, vl_ref):
    n_ctx = kc_ref.shape[2]
    chunks = [(kc_ref, vc_ref, 0, n_ctx, 0)]
    if kl_ref is not None:
        for c in range(kl_ref.shape[2] // KEY_CHUNK):
            chunks.append((kl_ref, vl_ref, c * KEY_CHUNK, KEY_CHUNK, n_ctx + c * KEY_CHUNK))
    return chunks


def _dense_kernel(q_ref, kc_ref, kl_ref, vcp_ref, vlp_ref, vcc_ref, vlc_ref, o_ref,
                  sa_sc, sb_sc, mba_sc, mbb_sc, *, mask_q, n_tiles, tiles_per_sample, tiles_per_head, n_sub):
    step = pl.program_id(0)
    tq = sa_sc.shape[0]

    @pl.when(step == 0)
    def _():
        sa_sc[...] = jnp.zeros(sa_sc.shape, F32)
        mba_sc[...] = jnp.zeros(mba_sc.shape, F32)

    if mask_q:
        first = jnp.minimum(n_sub * step, n_tiles - n_sub)
        half = ((first % tiles_per_sample) // tiles_per_head) // 3
    for k in range(n_sub):
        rows = slice(k * tq, (k + 1) * tq)
        q = q_ref[0, 0, rows, :]
        if mask_q:
            q = _half_mask(q, half)
        v_refs = (vcp_ref, vlp_ref) if k == 0 else (vcc_ref, vlc_ref)
        chunks = _key_chunks(kc_ref, v_refs[0], kl_ref, v_refs[1])
        s_w, s_r, mb_w, mb_r = ((sb_sc, sa_sc, mbb_sc, mba_sc) if k % 2 == 0
                                else (sa_sc, sb_sc, mba_sc, mbb_sc))
        mb_new, o_ref[rows, :] = _fused_passes(q, chunks, s_w, s_r, mb_r[...])
        mb_w[...] = mb_new


def _dense_attention(q, kc, vc, kl, vl, n_heads, maps, mask_q, name):
    b, _, nq, _ = q.shape
    qmap, kmap, vmap = maps
    n_ctx, n_lat = kc.shape[2], kl.shape[2]
    tq = Q_TILE
    tph = nq // tq
    n_sub = next(n for n in (8, 4, 2) if tph % n == 0)
    assert tph % n_sub == 0
    tps = n_heads * tph
    n_tiles = b * tps
    first = lambda s: jnp.minimum(n_sub * s, n_tiles - n_sub)
    before = lambda s: jnp.maximum(n_sub * s - 1, 0)
    kv_blk = lambda n, fmap, tile: pl.BlockSpec(
        (1, 1, n, LANE), lambda s: (tile(s) // tps, fmap((tile(s) % tps) // tph), 0, 0))
    q_spec = pl.BlockSpec(
        (1, 1, n_sub * tq, LANE),
        lambda s: (first(s) // tps, qmap((first(s) % tps) // tph), (first(s) % tph) // n_sub, 0))
    return pl.pallas_call(
        functools.partial(_dense_kernel, mask_q=mask_q, n_tiles=n_tiles, tiles_per_sample=tps,
                          tiles_per_head=tph, n_sub=n_sub),
        grid=(n_tiles // n_sub + 1,),
        in_specs=[q_spec, kv_blk(n_ctx, kmap, first), kv_blk(n_lat, kmap, first),
                  kv_blk(n_ctx, vmap, before), kv_blk(n_lat, vmap, before),
                  kv_blk(n_ctx, vmap, first), kv_blk(n_lat, vmap, first)],
        out_specs=pl.BlockSpec((n_sub * tq, LANE), lambda s: (s, 0)),
        out_shape=jax.ShapeDtypeStruct(((n_tiles + n_sub) * tq, LANE), BF16),
        scratch_shapes=[pltpu.VMEM((tq, n_ctx + n_lat), F32), pltpu.VMEM((tq, n_ctx + n_lat), F32),
                        pltpu.VMEM((tq, LANE), F32), pltpu.VMEM((tq, LANE), F32)],
        compiler_params=_cparams(("arbitrary",)),
        name=name,
    )(q, kc, kl, vc, vl, vc, vl)


def _dense_ctx_kernel(q_ref, k_ref, v_ref, o_ref, *, n_heads, maps, mask_q):
    qmap, kmap, vmap = maps
    n_keys = k_ref.shape[2]
    ones = jnp.ones((n_keys, LANE), BF16)
    for h in range(n_heads):
        q = q_ref[0, qmap(h)]
        if mask_q:
            q = _half_mask(q, h // 3)
        s = _qk(q, k_ref[0, kmap(h)])
        tiles = [s[:, t * LANE:(t + 1) * LANE] for t in range(n_keys // LANE)]
        mx = functools.reduce(jnp.maximum, tiles)
        mb = jnp.broadcast_to(jnp.max(mx, axis=1, keepdims=True), mx.shape)
        p = jnp.concatenate([jnp.exp2(t - mb).astype(BF16) for t in tiles], axis=1)
        o = jnp.dot(p, jnp.concatenate([v_ref[0, vmap(h)], ones], axis=1), preferred_element_type=F32)
        o_ref[0, h] = (o[:, :LANE] / o[:, LANE:LANE + 1]).astype(BF16)


def _dense_attention_ctx(q, kc, vc, n_heads, maps, mask_q, name):
    b, nqb, nq, _ = q.shape
    whole = lambda a: pl.BlockSpec((1,) + a.shape[1:], lambda i: (i, 0, 0, 0))
    return pl.pallas_call(
        functools.partial(_dense_ctx_kernel, n_heads=n_heads, maps=maps, mask_q=mask_q),
        grid=(b,),
        in_specs=[whole(q), whole(kc), whole(vc)],
        out_specs=pl.BlockSpec((1, n_heads, nq, LANE), lambda i: (i, 0, 0, 0)),
        out_shape=jax.ShapeDtypeStruct((b, n_heads, nq, LANE), BF16),
        compiler_params=_cparams(("parallel",)),
        name=name,
    )(q, kc, vc)


def _window_kernel(q_ref, kc_ref, vc_ref, sink_ref, *rest, use_window):
    if use_window:
        kl_ref, vl_ref, o_ref = rest
    else:
        (o_ref,) = rest
    tq = ROW_TILE
    span = tq + 2 * WINDOW
    kcx = kc_ref[0]
    ones_c = jnp.ones((kcx.shape[0], LANE), BF16)
    v_aug = jnp.concatenate([vc_ref[0], ones_c], axis=1)
    lane = lax.broadcasted_iota(jnp.int32, (tq, LANE), 1)
    for ti in range(q_ref.shape[1] // tq):
        rows = slice(ti * tq, (ti + 1) * tq)
        if use_window:
            n_lat = kl_ref.shape[1]
            t0 = (pl.program_id(1) * (q_ref.shape[1] // tq) + ti) * tq
            start = pl.multiple_of(jnp.clip(t0 - WINDOW, 0, n_lat - span), LANE)
            qpos = t0 + lax.broadcasted_iota(jnp.int32, (tq, span), 0)
            kpos = start + lax.broadcasted_iota(jnp.int32, (tq, span), 1)
            valid = jnp.abs(qpos - kpos) <= WINDOW
            kw = kl_ref[0, pl.ds(start, span), :]
            v_all = jnp.concatenate(
                [v_aug, jnp.concatenate([vl_ref[0, pl.ds(start, span), :], jnp.ones((span, LANE), BF16)], axis=1)],
                axis=0)
        else:
            v_all = v_aug
        res = []
        for h in range(C_HEADS):
            half = h // (C_HEADS // C_KV_HEADS)
            q = q_ref[0, rows, (h % 2) * LANE:(h % 2 + 1) * LANE]
            q = jnp.where((lane < 64) if half == 0 else (lane >= 64), q, jnp.zeros_like(q))
            sink = sink_ref[0:1, h:h + 1]
            s = _qk(q, kcx)
            if use_window:
                s = jnp.concatenate([s, jnp.where(valid, _qk(q, kw), NEG_INF)], axis=1)
            mx = s[:, 0:LANE]
            for t in range(1, s.shape[1] // LANE):
                mx = jnp.maximum(mx, s[:, t * LANE:(t + 1) * LANE])
            m = jnp.maximum(jnp.max(mx, axis=1, keepdims=True), sink)
            mb = jnp.broadcast_to(m, (tq, LANE))
            p = jnp.concatenate([jnp.exp2(s[:, t * LANE:(t + 1) * LANE] - mb).astype(BF16)
                                 for t in range(s.shape[1] // LANE)], axis=1)
            o = jnp.dot(p, v_all, preferred_element_type=F32)
            res.append(o[:, :LANE] / (o[:, LANE:LANE + 1] + jnp.exp2(sink - m)))
        for ob in range(2):
            o_ref[0, rows, ob * LANE:(ob + 1) * LANE] = jnp.where(lane < 64, res[ob], res[2 + ob]).astype(BF16)


def _window_attention(q, kc, vc, kl, vl, sink, name):
    b, nq, qw = q.shape
    tq = min(4 * ROW_TILE, nq)
    n_ctx = kc.shape[1]
    use_window = kl is not None
    in_specs = [pl.BlockSpec((1, tq, qw), lambda i, j: (i, j, 0)),
                pl.BlockSpec((1, n_ctx, LANE), lambda i, j: (i, 0, 0)),
                pl.BlockSpec((1, n_ctx, LANE), lambda i, j: (i, 0, 0)),
                pl.BlockSpec((8, LANE), lambda i, j: (0, 0))]
    args = [q, kc, vc, sink]
    if use_window:
        n_lat = kl.shape[1]
        in_specs += [pl.BlockSpec((1, n_lat, LANE), lambda i, j: (i, 0, 0)),
                     pl.BlockSpec((1, n_lat, LANE), lambda i, j: (i, 0, 0))]
        args += [kl, vl]
    return pl.pallas_call(
        functools.partial(_window_kernel, use_window=use_window),
        grid=(b, nq // tq),
        in_specs=in_specs,
        out_specs=pl.BlockSpec((1, tq, qw), lambda i, j: (i, j, 0)),
        out_shape=jax.ShapeDtypeStruct((b, nq, qw), BF16),
        compiler_params=_cparams(("parallel", "parallel")),
        name=name,
    )(*args)


def _outproj_kernel(*refs):
    oa = refs[0:A_HEADS]
    ob = refs[A_HEADS:A_HEADS + B_HEADS]
    oc_ref, x_ref, mod_ref, g2n_ref, w_ref, wr_ref, xo_ref, h2_ref, aff_ref = refs[A_HEADS + B_HEADS:]
    tm = x_ref.shape[1]
    low = lax.broadcasted_iota(jnp.int32, (tm, LANE), 1) < 64
    parts = [jnp.where(low, oa[i][...], oa[3 + i][...]) for i in range(3)]
    parts += [jnp.where(low, ob[2 * i][...], ob[2 * i + 1][...]) for i in range(3)]
    parts.append(oc_ref[0])
    mix = jnp.dot(jnp.concatenate(parts, axis=1), w_ref[...], preferred_element_type=F32)
    x = x_ref[0] + mod_ref[0, 0, 2:3, :] * mix
    xo_ref[0] = x
    y = x * lax.rsqrt(jnp.mean(x * x, axis=-1, keepdims=True) + EPS) * g2n_ref[...]
    h2 = y * (1.0 + mod_ref[0, 0, 4:5, :]) + mod_ref[0, 0, 3:4, :]
    h2_ref[0] = h2.astype(BF16)
    wr = wr_ref[...]
    wr_hi = wr.astype(BF16)
    wr_lo = (wr - wr_hi.astype(F32)).astype(BF16)
    h2_hi = h2.astype(BF16)
    h2_lo = (h2 - h2_hi.astype(F32)).astype(BF16)
    both = jnp.dot(h2_hi, jnp.concatenate([wr_hi, wr_lo], axis=1), preferred_element_type=F32)
    logits = both[:, :LANE] + both[:, LANE:] + jnp.dot(h2_lo, wr_hi, preferred_element_type=F32)
    logits = logits.T[0:aff_ref.shape[1], :]
    z = jnp.exp(logits - jnp.max(logits, axis=0, keepdims=True))
    aff_ref[0] = z / jnp.sum(z, axis=0, keepdims=True)


def _outproj(oa, ob, oc, xs, modv, kind, g2n, w_mix, wr_t, pad_rows, name):
    b, n, d = xs.shape
    tm = min(Q_TILE, n)
    assert pad_rows % tm == 0 and n % tm == 0
    e = N_EXPERTS
    const = lambda *shape: pl.BlockSpec(shape, lambda i, j: (0,) * len(shape))
    row = lambda w: pl.BlockSpec((1, tm, w), lambda i, j: (i, j, 0))
    n_heads = A_HEADS
    head = lambda h: pl.BlockSpec((tm, LANE), lambda i, j: ((pad_rows + (i * n_heads + h) * n) // tm + j, 0))
    return pl.pallas_call(
        _outproj_kernel,
        grid=(b, n // tm),
        in_specs=[head(h) for h in range(A_HEADS)] + [head(h) for h in range(B_HEADS)] + [
                  row(256), row(d),
                  pl.BlockSpec((1, 1, 8, d), lambda i, j: (i, kind, 0, 0)),
                  const(1, d), const(d, d), const(d, LANE)],
        out_specs=[row(d), row(d), pl.BlockSpec((1, e, tm), lambda i, j: (i, 0, j))],
        out_shape=[jax.ShapeDtypeStruct((b, n, d), F32), jax.ShapeDtypeStruct((b, n, d), BF16),
                   jax.ShapeDtypeStruct((b, e, n), F32)],
        compiler_params=_cparams(("parallel", "parallel")),
        name=name,
    )(*([oa] * A_HEADS), *([ob] * B_HEADS), oc, xs, modv, g2n, w_mix, wr_t)


def _prefix_exclusive(mask_f, tri):
    e, t = mask_f.shape
    ck = tri.shape[0]
    carry = jnp.zeros((e, 1), F32)
    parts, offs = [], [carry]
    for c in range(t // ck):
        blk = mask_f[:, c * ck:(c + 1) * ck]
        parts.append(jnp.dot(blk.astype(BF16), tri, preferred_element_type=F32) + carry)
        carry = carry + jnp.sum(blk, axis=1, keepdims=True)
        offs.append(carry)
    return jnp.concatenate(parts, axis=1) if len(parts) > 1 else parts[0], offs


def _topk_kernel(aff_ref, slot_ref, cnt_ref, *, cap, slot_stride):
    aff = aff_ref[0]
    e, t = aff.shape
    bits = pltpu.bitcast(aff, jnp.int32)
    kf = jnp.float32(cap)

    def search(i, lo):
        cand = lo | (jnp.int32(1) << (30 - i))
        n_ge = jnp.sum((bits >= cand).astype(F32), axis=1, keepdims=True)
        return jnp.where(n_ge >= kf, cand, lo)

    thr = lax.fori_loop(0, 31, search, jnp.zeros((e, 1), jnp.int32))
    gt = bits > thr
    eq = bits == thr
    need = kf - jnp.sum(gt.astype(F32), axis=1, keepdims=True)

    r = lax.broadcasted_iota(jnp.int32, (TOK_CHUNK, TOK_CHUNK), 0)
    c = lax.broadcasted_iota(jnp.int32, (TOK_CHUNK, TOK_CHUNK), 1)
    tri = (r < c).astype(BF16)
    tie_rank, _ = _prefix_exclusive(eq.astype(F32), tri)
    sel = gt | (eq & (tie_rank < need))
    slot, offs = _prefix_exclusive(sel.astype(F32), tri)
    base = pl.program_id(0) * slot_stride
    slot_ref[0] = jnp.where(sel, slot.astype(jnp.int32) + base, -1)
    lane = lax.broadcasted_iota(jnp.int32, (e, LANE), 1)
    cnt = jnp.zeros((e, LANE), jnp.int32)
    for ci, off in enumerate(offs):
        cnt = jnp.where(lane == ci, off.astype(jnp.int32) + base, cnt)
    cnt_ref[0] = cnt


def _topk(aff_t, cap, slot_stride, name):
    b, e, t = aff_t.shape
    return pl.pallas_call(
        functools.partial(_topk_kernel, cap=cap, slot_stride=slot_stride),
        grid=(b,),
        in_specs=[pl.BlockSpec((1, e, t), lambda i: (i, 0, 0))],
        out_specs=[pl.BlockSpec((1, e, t), lambda i: (i, 0, 0)),
                   pl.BlockSpec((1, e, LANE), lambda i: (i, 0, 0))],
        out_shape=[jax.ShapeDtypeStruct((b, e, t), jnp.int32), jax.ShapeDtypeStruct((b, e, LANE), jnp.int32)],
        compiler_params=_cparams(("parallel",)),
        name=name,
    )(aff_t)


def _moe_kernel(cnt_ref, h_ref, slot_ref, slotp_ref, gate_ref, wg_ref, wu_ref, wd_ref, mod_ref, x_hbm, o_hbm,
                acc_sc, xe_sc, gs_sc, ye_sc, in_sem, out_sem, *, n_chunks, n_sb):
    bi = pl.program_id(0)
    ei = pl.program_id(1)
    n_b = pl.num_programs(0)
    n_e = pl.num_programs(1)

    def chunk_rows(c):
        return pl.ds(c * TOK_CHUNK, TOK_CHUNK)

    def load_dma(c):
        return pltpu.make_async_copy(x_hbm.at[bi, chunk_rows(c), :], acc_sc.at[chunk_rows(c), :], in_sem.at[c])

    def store_dma(sample, c):
        return pltpu.make_async_copy(acc_sc.at[chunk_rows(c), :], o_hbm.at[sample, chunk_rows(c), :],
                                     out_sem.at[c])

    def start_x_load():
        @pl.when(ei == 0)
        def _():
            for c in range(n_chunks):
                @pl.when(bi > 0)
                def _(c=c):
                    store_dma(bi - 1, c).wait()
                load_dma(c).start()

    def wait_for_x(at_expert):
        @pl.when(ei == at_expert)
        def _():
            for c in range(n_chunks):
                load_dma(c).wait()

    g2 = mod_ref[0, 0, 5:6, :]
    cap = n_sb * SLOT_BLOCK

    def counts_of(e):
        base = (bi * n_e + e) * 32
        return [cnt_ref[base + c] for c in range(n_chunks + 1)]

    def all_chunks_at_most(cs, limit):
        ok = cs[1] - cs[0] <= limit
        for c in range(1, n_chunks):
            ok = ok & (cs[c + 1] - cs[c] <= limit)
        return ok

    odd = (ei % 2) == 1
    counts = counts_of(ei)
    counts_other = counts_of(jnp.where(odd, ei - 1, ei + 1))
    pair_narrow = all_chunks_at_most(counts, PAIR_LIMIT) & all_chunks_at_most(counts_other, PAIR_LIMIT)

    def ffn():
        xe = xe_sc[0:cap, :].astype(BF16)
        f = wg_ref.shape[2]
        fb = 512
        y = None
        for f0 in range(0, f, fb):
            hg = jnp.dot(xe, wg_ref[0, :, f0:f0 + fb], preferred_element_type=F32)
            hu = jnp.dot(xe, wu_ref[0, :, f0:f0 + fb], preferred_element_type=F32)
            hid = (hg * jax.nn.sigmoid(hg) * hu).astype(BF16)
            part = jnp.dot(hid, wd_ref[0, f0:f0 + fb, :], preferred_element_type=F32)
            y = part if y is None else y + part
        return y

    def tok(c):
        return slice(c * TOK_CHUNK, (c + 1) * TOK_CHUNK)

    def slot_columns(ref):
        pad = jnp.zeros((LANE - n_chunks, TOK_CHUNK), F32)
        return jnp.concatenate([ref[0, 0].astype(F32), pad], axis=0).T

    @pl.when(pair_narrow)
    def _():
        xe_sc[...] = jnp.zeros_like(xe_sc)
        gs_sc[...] = jnp.zeros_like(gs_sc)
        srow = lax.broadcasted_iota(jnp.int32, (GATHER_WINDOW, TOK_CHUNK), 0)
        for c in range(n_chunks):
            g0 = pl.multiple_of((counts[c] // 8) * 8, 8)
            oh = (slot_ref[0, 0, c:c + 1, :] - g0) == srow
            xe_sc[pl.ds(g0, GATHER_WINDOW), :] += jnp.dot(
                oh.astype(BF16), h_ref[0, tok(c), :], preferred_element_type=F32)
            gs_sc[pl.ds(g0, GATHER_WINDOW), :] += jnp.sum(
                jnp.where(oh, gate_ref[0, 0, c:c + 1, :], 0.0), axis=1, keepdims=True)
        ye = ye_sc.at[ei % 2]
        ye[0:cap, :] = (ffn() * gs_sc[0:cap, :]).astype(BF16)
        ye[cap:cap + SLOT_BLOCK, :] = jnp.zeros((SLOT_BLOCK, ye_sc.shape[2]), BF16)
        start_x_load()

        @pl.when(odd)
        def _():
            wait_for_x(1)
            cols_prev = slot_columns(slotp_ref)
            cols_cur = slot_columns(slot_ref)
            scol = lax.broadcasted_iota(jnp.int32, (TOK_CHUNK, SLOT_BLOCK), 1).astype(F32)
            for c in range(n_chunks):
                w_prev = pl.multiple_of((counts_other[c] // 16) * 16, 16)
                w_cur = pl.multiple_of((counts[c] // 16) * 16, 16)
                oh_t = jnp.concatenate(
                    [((cols_prev[:, c:c + 1] - w_prev.astype(F32)) == scol).astype(BF16),
                     ((cols_cur[:, c:c + 1] - w_cur.astype(F32)) == scol).astype(BF16)], axis=1)
                rows = jnp.concatenate([ye_sc[0, pl.ds(w_prev, SLOT_BLOCK), :],
                                        ye_sc[1, pl.ds(w_cur, SLOT_BLOCK), :]], axis=0)
                acc_sc[tok(c), :] += g2 * jnp.dot(oh_t, rows, preferred_element_type=F32)

    @pl.when(jnp.logical_not(pair_narrow))
    def _():
        xe_sc[...] = jnp.zeros_like(xe_sc)
        gs_sc[...] = jnp.zeros_like(gs_sc)
        srow = lax.broadcasted_iota(jnp.int32, (SLOT_BLOCK, TOK_CHUNK), 0)
        base = (bi * n_e + ei) * 32

        def overlap(c, sb):
            return (cnt_ref[base + c] < (sb + 1) * SLOT_BLOCK) & (cnt_ref[base + c + 1] > sb * SLOT_BLOCK)

        def onehot(c, sb):
            return slot_ref[0, 0, pl.ds(c, 1), :] == (srow + sb * SLOT_BLOCK)

        def dyn_tok(c):
            return pl.ds(pl.multiple_of(c * TOK_CHUNK, TOK_CHUNK), TOK_CHUNK)

        def gather_chunk(c, carry):
            for sb in range(n_sb):
                @pl.when(overlap(c, sb))
                def _(sb=sb):
                    oh = onehot(c, sb)
                    rows = slice(sb * SLOT_BLOCK, (sb + 1) * SLOT_BLOCK)
                    xe_sc[rows] += jnp.dot(oh.astype(BF16), h_ref[0, dyn_tok(c), :], preferred_element_type=F32)
                    g = jnp.where(oh, gate_ref[0, 0, pl.ds(c, 1), :], 0.0)
                    gs_sc[rows] += jnp.sum(g, axis=1, keepdims=True)
            return carry

        lax.fori_loop(0, n_chunks, gather_chunk, 0)
        ye_sc[0, 0:cap, :] = (ffn() * gs_sc[0:cap, :]).astype(BF16)
        start_x_load()
        wait_for_x(0)

        def scatter_chunk(c, carry):
            for sb in range(n_sb):
                @pl.when(overlap(c, sb))
                def _(sb=sb):
                    oh_t = onehot(c, sb).astype(F32).T.astype(BF16)
                    rows = slice(sb * SLOT_BLOCK, (sb + 1) * SLOT_BLOCK)
                    acc_sc[dyn_tok(c), :] += g2 * jnp.dot(oh_t, ye_sc[0, rows, :], preferred_element_type=F32)
            return carry

        lax.fori_loop(0, n_chunks, scatter_chunk, 0)

    @pl.when(ei == n_e - 1)
    def _():
        for c in range(n_chunks):
            store_dma(bi, c).start()

    @pl.when((ei == n_e - 1) & (bi == n_b - 1))
    def _():
        for c in range(n_chunks):
            store_dma(bi, c).wait()


def _moe(h2, slot, gate, cnt, wg, wu, wd, xs, modv, kind, cap, name):
    b, t, d = h2.shape
    e = slot.shape[1]
    f = wg.shape[2]
    n_chunks = t // TOK_CHUNK
    n_sb = cap // SLOT_BLOCK
    slot4 = slot.reshape(b, e, n_chunks, TOK_CHUNK)
    gate4 = gate.reshape(b, e, n_chunks, TOK_CHUNK)
    cnt_flat = cnt[:, :, :32].reshape(-1)
    grid_spec = pltpu.PrefetchScalarGridSpec(
        num_scalar_prefetch=1,
        grid=(b, e),
        in_specs=[pl.BlockSpec((1, t, d), lambda i, j, s: (i, 0, 0)),
                  pl.BlockSpec((1, 1, n_chunks, TOK_CHUNK), lambda i, j, s: (i, j, 0, 0)),
                  pl.BlockSpec((1, 1, n_chunks, TOK_CHUNK), lambda i, j, s: (i, jnp.maximum(j - 1, 0), 0, 0)),
                  pl.BlockSpec((1, 1, n_chunks, TOK_CHUNK), lambda i, j, s: (i, j, 0, 0)),
                  pl.BlockSpec((1, d, f), lambda i, j, s: (j, 0, 0)),
                  pl.BlockSpec((1, d, f), lambda i, j, s: (j, 0, 0)),
                  pl.BlockSpec((1, f, d), lambda i, j, s: (j, 0, 0)),
                  pl.BlockSpec((1, 1, 8, d), lambda i, j, s: (i, kind, 0, 0)),
                  pl.BlockSpec(memory_space=pl.ANY)],
        out_specs=pl.BlockSpec(memory_space=pl.ANY),
        scratch_shapes=[pltpu.VMEM((t, d), F32),
                        pltpu.VMEM((cap + 2 * SLOT_BLOCK, d), F32), pltpu.VMEM((cap + 2 * SLOT_BLOCK, 1), F32),
                        pltpu.VMEM((2, cap + SLOT_BLOCK, d), BF16),
                        pltpu.SemaphoreType.DMA((n_chunks,)), pltpu.SemaphoreType.DMA((n_chunks,))],
    )
    return pl.pallas_call(
        functools.partial(_moe_kernel, n_chunks=n_chunks, n_sb=n_sb),
        grid_spec=grid_spec,
        out_shape=jax.ShapeDtypeStruct((b, t, d), F32),
        compiler_params=_cparams(("arbitrary", "arbitrary")),
        name=name,
    )(cnt_flat, h2, slot4, slot4, gate4, wg, wu, wd, modv, xs)


def _cast_kernel(w_ref, o_ref):
    o_ref[...] = w_ref[0].astype(BF16)


def _layer_bf16(w, layer):
    _, e, r, c = w.shape
    return pl.pallas_call(
        _cast_kernel,
        grid=(e,),
        in_specs=[pl.BlockSpec((1, 1, r, c), lambda i: (layer, i, 0, 0))],
        out_specs=pl.BlockSpec((1, r, c), lambda i: (i, 0, 0)),
        out_shape=jax.ShapeDtypeStruct((e, r, c), BF16),
        compiler_params=_cparams(("parallel",)),
        name="expert_weights_bf16",
    )(w)


def _pad_lanes(v, width):
    return jnp.pad(v, (0, width - v.shape[0]))


def kernel(x, c, ctx, c_ctx, w_ada, b_ada, norm1_g, norm2_g, w_in, a_q_norm, a_k_norm, b_cq_norm, b_ckv_norm, w_uq, w_ukv, b_qn_norm, b_kn_norm, b_qr_norm, b_kr_norm, c_q_norm, c_k_norm, c_sink, w_out, w_router, w_e_gate, w_e_up, w_e_down):
    b, t, d = x.shape
    n_ctx = ctx.shape[1]
    depth = w_ada.shape[0]
    assert n_ctx == ROW_TILE and t % KEY_CHUNK == 0 and t % GRID_W == 0
    cap = max(1, CAP_FACTOR * t // N_EXPERTS)
    cap_c = max(1, CAP_FACTOR * n_ctx // N_EXPERTS)
    assert cap % SLOT_BLOCK == 0 and (b * cap_c) % SLOT_BLOCK == 0

    rows = ((b + 1 + 7) // 8) * 8
    cc = jnp.zeros((rows, d), F32).at[:b].set(c).at[b].set(c_ctx)
    mod = _modulation(cc, w_ada, b_ada)

    rope = _rope_tables(t)
    mm, mx, mbb = (jnp.asarray(m, BF16) for m in _seg_matrices())
    in_perm = _in_perm()
    uq_perm = _uq_perm()
    ukv_pk, ukv_pv = _ukv_perms()
    oa_rows = np.asarray(_pair_cols(0, A_HEADS))
    oc_rows = np.asarray(_pair_cols(0, C_HEADS)) + 768
    sa = HEAD_DIM ** -0.5
    sb = (B_NOPE + B_ROPE) ** -0.5
    z32, z64 = jnp.zeros((32,), F32), jnp.zeros((64,), F32)

    for l in range(depth):
        last = l == depth - 1
        m6 = mod[l].reshape(rows, 6, d)
        m8 = jnp.pad(m6, ((0, 0), (0, 2), (0, 0)))
        modv = jnp.stack([jnp.broadcast_to(m8[b][None], (b, 8, d)), m8[:b]], axis=1)

        w_in_p = _take_cols(w_in[l], in_perm).astype(BF16)
        wuq = jnp.pad(_take_cols(w_uq[l], uq_perm), ((0, 256 - B_Q_RANK), (0, 0))).astype(BF16)
        wukvk = _take_cols(w_ukv[l], ukv_pk).astype(BF16)
        wukvv = jnp.take(w_ukv[l], jnp.asarray(ukv_pv, jnp.int32), axis=1).astype(BF16)
        gains = jnp.stack([
            jnp.tile(a_q_norm[l], 2) * (sa * LOG2E), jnp.tile(a_k_norm[l], 2),
            jnp.tile(c_q_norm[l], 2) * (sa * LOG2E), jnp.tile(c_k_norm[l], 2),
            jnp.concatenate([b_qn_norm[l] * (sb * LOG2E), b_qr_norm[l] * (sb * LOG2E), z32]),
            jnp.concatenate([b_kn_norm[l], z64]),
            jnp.concatenate([z64, b_kr_norm[l], z32]),
            jnp.zeros((LANE,), F32)])
        gcq = _pad_lanes(b_cq_norm[l], 256)[None]
        gckv = b_ckv_norm[l][None]
        g1 = norm1_g[l][None]
        g2n = norm2_g[l][None]
        mix_rows = np.concatenate([oa_rows, np.arange(384, 768), oc_rows])
        w_mix = jnp.take(w_out[l], jnp.asarray(mix_rows, jnp.int32), axis=0).astype(BF16)
        wr_t = jnp.pad(w_router[l], ((0, 0), (0, LANE - N_EXPERTS)))
        sink = jnp.zeros((8, LANE), F32).at[0, :C_HEADS].set(c_sink[l] * LOG2E)
        wg = _layer_bf16(w_e_gate, l)
        wu = _layer_bf16(w_e_up, l)
        wd = _layer_bf16(w_e_down, l)

        shared = (g1, w_in_p, gains, gcq, gckv, wuq, wukvk, wukvv, mm, mx, mbb)
        qa, ka, va, qb, kb, vb, qc, kc, vc = _inproj(x, modv, 1, *shared, rope)
        qac, kac, vac, qbc, kbc, vbc, qcc, kcc, vcc = _inproj(ctx, modv, 0, *shared, None)

        o_a = _dense_attention(qa, kac, vac, ka, va, A_HEADS, _A_MAPS, True, "attn_a")
        o_b = _dense_attention(qb, kbc, vbc, kb, vb, B_HEADS, _B_MAPS, False, "attn_b")
        o_c = _window_attention(qc, kcc, vcc, kc, vc, sink, "attn_c")
        x_mid, h2, aff = _outproj(o_a, o_b, o_c, x, modv, 1, g2n, w_mix, wr_t, Q_TILE, "outproj")
        slot, cnt = _topk(aff, cap, 0, "topk")
        x = _moe(h2, slot, aff, cnt, wg, wu, wd, x_mid, modv, 1, cap, "moe")

        if not last:
            o_ac = _dense_attention_ctx(qac, kac, vac, A_HEADS, _A_MAPS, True, "attn_a_ctx")
            o_bc = _dense_attention_ctx(qbc, kbc, vbc, B_HEADS, _B_MAPS, False, "attn_b_ctx")
            o_cc = _window_attention(qcc, kcc, vcc, None, None, sink, "attn_c_ctx")
            flat = lambda o: o.reshape(-1, LANE)
            c_mid, hc2, aff_c = _outproj(flat(o_ac), flat(o_bc), o_cc, ctx, modv, 0, g2n, w_mix, wr_t, 0,
                                         "outproj_ctx")
            slot_c, cnt_c = _topk(aff_c, cap_c, cap_c, "topk_ctx")
            e = N_EXPERTS
            slot_f = jnp.transpose(slot_c, (1, 0, 2)).reshape(1, e, b * n_ctx)
            gate_f = jnp.transpose(aff_c, (1, 0, 2)).reshape(1, e, b * n_ctx)
            starts = jnp.transpose(cnt_c[:, :, 0], (1, 0))
            cnt_f = jnp.concatenate([starts, cnt_c[b - 1, :, 1:2],
                                     jnp.zeros((e, LANE - b - 1), jnp.int32)], axis=1)[None]
            ctx = _moe(hc2.reshape(1, b * n_ctx, d), slot_f, gate_f, cnt_f, wg, wu, wd,
                       c_mid.reshape(1, b * n_ctx, d), modv, 0, b * cap_c, "moe_ctx").reshape(b, n_ctx, d)
    return x
```

```python
import functools

import numpy as np
import jax
import jax.numpy as jnp
from jax import lax
from jax.experimental import pallas as pl
from jax.experimental.pallas import tpu as pltpu

F32 = jnp.float32
BF16 = jnp.bfloat16

GRID_W = 64
ROPE_THETA = 10000.0
EPS = 1e-6
NEG_INF = -1e30
HEAD_DIM = 64
A_HEADS, A_KV_HEADS = 6, 2
B_HEADS, B_Q_RANK, B_KV_RANK, B_NOPE, B_ROPE, B_V = 6, 192, 128, 64, 32, 64
C_HEADS, C_KV_HEADS = 4, 2
N_EXPERTS = 16
CAP_FACTOR = 2
WINDOW = 128

LANE = 128
ROW_TILE = 256
PROJ_TILE = 1024
PROJ_SLAB = 256
Q_TILE = 512
KEY_CHUNK = 512
LOG2E = 1.4426950408889634
SLOT_BLOCK = 128
TOK_CHUNK = 256
GATHER_WINDOW = 144
PAIR_LIMIT = SLOT_BLOCK - 16
TOPK_REFINE_STEPS = 24
V7X_VMEM_BYTES = 64 * 1024 * 1024
VMEM_LIMIT = V7X_VMEM_BYTES - 8 * 1024 * 1024

_QA, _KA, _VA = 0, 384, 512
_QC, _KC, _VC = 640, 896, 1024
_BCQ, _BCKV, _BKR = 1152, 1408, 1536
IN_PAD = 1664
_ORIG = dict(aq=0, ak=384, av=512, bcq=640, bckv=832, bkr=960, cq=992, ck=1248, cv=1376)


def _cparams(sem, vmem=VMEM_LIMIT):
    return pltpu.CompilerParams(dimension_semantics=sem, vmem_limit_bytes=vmem)


def _pair_cols(base, n_heads):
    half = n_heads // 2
    cols = []
    for i in range(half):
        cols += list(range(base + i * 64, base + (i + 1) * 64))
        cols += list(range(base + (half + i) * 64, base + (half + i + 1) * 64))
    return cols


def _in_perm():
    perm = -np.ones((IN_PAD,), np.int64)
    perm[_QA:_QA + 384] = _pair_cols(_ORIG["aq"], A_HEADS)
    perm[_KA:_KA + 128] = np.arange(_ORIG["ak"], _ORIG["ak"] + 128)
    perm[_VA:_VA + 128] = np.arange(_ORIG["av"], _ORIG["av"] + 128)
    perm[_QC:_QC + 256] = _pair_cols(_ORIG["cq"], C_HEADS)
    perm[_KC:_KC + 128] = np.arange(_ORIG["ck"], _ORIG["ck"] + 128)
    perm[_VC:_VC + 128] = np.arange(_ORIG["cv"], _ORIG["cv"] + 128)
    perm[_BCQ:_BCQ + B_Q_RANK] = np.arange(_ORIG["bcq"], _ORIG["bcq"] + B_Q_RANK)
    perm[_BCKV:_BCKV + 128] = np.arange(_ORIG["bckv"], _ORIG["bckv"] + 128)
    perm[_BKR + 64:_BKR + 96] = np.arange(_ORIG["bkr"], _ORIG["bkr"] + 32)
    return perm


def _take_cols(w, perm):
    safe = np.where(perm >= 0, perm, 0)
    out = jnp.take(w, jnp.asarray(safe, jnp.int32), axis=1)
    return jnp.where(jnp.asarray(perm >= 0)[None, :], out, 0.0)


def _uq_perm():
    perm = -np.ones((B_HEADS * LANE,), np.int64)
    for h in range(B_HEADS):
        perm[h * LANE:h * LANE + 96] = np.arange(h * 96, h * 96 + 96)
    return perm


def _ukv_perms():
    pk = -np.ones((B_HEADS * LANE,), np.int64)
    pv = np.zeros((B_HEADS * 64,), np.int64)
    for h in range(B_HEADS):
        pk[h * LANE:h * LANE + 64] = np.arange(h * 128, h * 128 + 64)
        pv[h * 64:(h + 1) * 64] = np.arange(h * 128 + 64, h * 128 + 128)
    return pk, pv


def _seg_matrices():
    i = np.arange(LANE)
    m64 = (i[:, None] // 64 == i[None, :] // 64).astype(np.float32) / 64.0
    mb = np.zeros((LANE, LANE), np.float32)
    mb[:64, :64] = 1.0 / 64.0
    mb[64:96, 64:96] = 1.0 / 32.0
    def pair(a, b):
        out = np.zeros((2 * LANE, 2 * LANE), np.float32)
        out[:LANE, :LANE] = a
        out[LANE:, LANE:] = b
        return out

    return pair(m64, m64), pair(m64, mb), pair(mb, mb)


def _rope_tables(length):
    rows = length // GRID_W
    t = np.arange(rows * GRID_W)
    row = np.repeat(np.arange(rows), GRID_W).astype(np.float32)
    col = (t % GRID_W).astype(np.float32)

    def tabs(dim):
        axis_dim = dim // 2
        inv = (np.float32(ROPE_THETA) ** (-np.arange(0, axis_dim, 2, dtype=np.float32) / axis_dim)).astype(np.float32)
        ar = row[:, None] * inv[None, :]
        ac = col[:, None] * inv[None, :]
        cos = np.concatenate([np.cos(ar), np.cos(ar), np.cos(ac), np.cos(ac)], axis=-1)
        sin = np.concatenate([-np.sin(ar), np.sin(ar), -np.sin(ac), np.sin(ac)], axis=-1)
        return cos.astype(np.float32), sin.astype(np.float32)

    c64, s64 = tabs(HEAD_DIM)
    cb, sb = tabs(B_ROPE)
    n = c64.shape[0]
    ones64, zeros64 = np.ones((n, 64), np.float32), np.zeros((n, 64), np.float32)
    ones32, zeros32 = np.ones((n, 32), np.float32), np.zeros((n, 32), np.float32)
    return jnp.asarray(np.concatenate([c64, c64, s64, s64,
                                       ones64, cb, ones32, zeros64, sb, zeros32], axis=-1))


def _mod_kernel(c_ref, w_ref, b_ref, o_ref):
    cv = c_ref[...]
    a = (cv * jax.nn.sigmoid(cv)).astype(BF16)
    o_ref[0] = jnp.dot(a, w_ref[0].astype(BF16), preferred_element_type=F32) + b_ref[0]


def _modulation(cc, w_ada, b_ada):
    n_layers, d, n6 = w_ada.shape
    rows = cc.shape[0]
    tn = 1536
    return pl.pallas_call(
        _mod_kernel,
        grid=(n_layers, n6 // tn),
        in_specs=[pl.BlockSpec((rows, d), lambda l, j: (0, 0)),
                  pl.BlockSpec((1, d, tn), lambda l, j: (l, 0, j)),
                  pl.BlockSpec((1, 1, tn), lambda l, j: (l, 0, j))],
        out_specs=pl.BlockSpec((1, rows, tn), lambda l, j: (l, 0, j)),
        out_shape=jax.ShapeDtypeStruct((n_layers, rows, n6), F32),
        compiler_params=_cparams(("parallel", "parallel")),
        name="adaln_mod",
    )(cc, w_ada, b_ada.reshape(n_layers, 1, n6))


def _seg_rsqrt(x2, m_ref):
    return lax.rsqrt(jnp.dot((x2 * x2).astype(BF16), m_ref[...], preferred_element_type=F32) + EPS)


def _rope(x, cos, sin, half):
    lane = lax.broadcasted_iota(jnp.int32, x.shape, 1)
    lo = (lane % (2 * half)) < half
    partner = jnp.where(lo, pltpu.roll(x, LANE - half, 1), pltpu.roll(x, half, 1))
    return x * cos + partner * sin


def _inproj_kernel(x_ref, mod_ref, g1_ref, w_ref, gains_ref, gcq_ref, gckv_ref, wuq_ref, wukvk_ref,
                   wukvv_ref, mm_ref, mx_ref, mbb_ref, *rest, use_rope):
    if use_rope:
        rope_ref = rest[0]
        outs = rest[1:]
    else:
        rope_ref = None
        outs = rest
    qa_ref, ka_ref, va_ref, qb_ref, kb_ref, vb_ref, qc_ref, kc_ref, vc_ref = outs

    def process(rows):
        x = x_ref[0, rows, :]
        y = x * lax.rsqrt(jnp.mean(x * x, axis=-1, keepdims=True) + EPS) * g1_ref[...]
        h = y * (1.0 + mod_ref[0, 0, 1:2, :]) + mod_ref[0, 0, 0:1, :]
        p = jnp.dot(h.astype(BF16), w_ref[...], preferred_element_type=F32)

        def head_pair(x2, m_ref, gain_rows, rope_cols, half):
            r = _seg_rsqrt(x2, m_ref)
            out = []
            for i in range(2):
                sl = slice(i * LANE, (i + 1) * LANE)
                y = x2[:, sl] * r[:, sl] * gains_ref[gain_rows[i]:gain_rows[i] + 1, :]
                if rope_ref is not None and rope_cols[i] is not None:
                    c0, s0 = rope_cols[i]
                    y = _rope(y, rope_ref[rows, c0:c0 + LANE], rope_ref[rows, s0:s0 + LANE], half[i])
                out.append(y)
            return out

        r64 = (0, 128)
        rb = (256, 384)
        qa0, qa1 = head_pair(p[:, _QA:_QA + 256], mm_ref, (0, 0), (r64, r64), (16, 16))
        qa2, ka = head_pair(p[:, _QA + 256:_QA + 512], mm_ref, (0, 1), (r64, r64), (16, 16))
        for i, blk in enumerate((qa0, qa1, qa2)):
            qa_ref[0, i, rows, :] = blk.astype(BF16)
        ka_ref[0, 0, rows, :] = ka.astype(BF16)
        va_ref[0, 0, rows, :] = p[:, _VA:_VA + LANE].astype(BF16)
        qc0, qc1 = head_pair(p[:, _QC:_QC + 256], mm_ref, (2, 2), (r64, r64), (16, 16))
        qc_ref[0, rows, 0:LANE] = qc0.astype(BF16)
        qc_ref[0, rows, LANE:2 * LANE] = qc1.astype(BF16)
        kc, kpe = head_pair(jnp.concatenate([p[:, _KC:_KC + LANE], p[:, _BKR:_BKR + LANE]], axis=1),
                            mx_ref, (3, 6), (r64, rb), (16, 8))
        kc_ref[0, rows, :] = kc.astype(BF16)
        vc_ref[0, rows, :] = p[:, _VC:_VC + LANE].astype(BF16)

        cq = p[:, _BCQ:_BCQ + 256]
        cq = cq * lax.rsqrt(jnp.sum(cq * cq, axis=-1, keepdims=True) * (1.0 / B_Q_RANK) + EPS) * gcq_ref[...]
        qb = jnp.dot(cq.astype(BF16), wuq_ref[...], preferred_element_type=F32)
        ckv = p[:, _BCKV:_BCKV + LANE]
        ckv = ckv * lax.rsqrt(jnp.mean(ckv * ckv, axis=-1, keepdims=True) + EPS) * gckv_ref[...]
        ckv = ckv.astype(BF16)
        kbn = jnp.dot(ckv, wukvk_ref[...], preferred_element_type=F32)
        vbv = jnp.dot(ckv, wukvv_ref[...], preferred_element_type=F32).astype(BF16)
        for i in range(B_HEADS // 2):
            vb_ref[0, i, rows, :] = vbv[:, i * LANE:(i + 1) * LANE]
        for i in range(B_HEADS // 2):
            sl = slice(2 * i * LANE, (2 * i + 2) * LANE)
            q0, q1 = head_pair(qb[:, sl], mbb_ref, (4, 4), (rb, rb), (8, 8))
            k0, k1 = head_pair(kbn[:, sl], mbb_ref, (5, 5), (None, None), (8, 8))
            qb_ref[0, 2 * i, rows, :] = q0.astype(BF16)
            qb_ref[0, 2 * i + 1, rows, :] = q1.astype(BF16)
            kb_ref[0, 2 * i, rows, :] = (k0 + kpe).astype(BF16)
            kb_ref[0, 2 * i + 1, rows, :] = (k1 + kpe).astype(BF16)

    slab = min(PROJ_SLAB, x_ref.shape[1])
    for r0 in range(0, x_ref.shape[1], slab):
        process(slice(r0, r0 + slab))


def _inproj(xs, modv, kind, g1, w_in_p, gains, gcq, gckv, wuq, wukvk, wukvv, mm, mx, mbb, rope):
    b, n, d = xs.shape
    tm = min(PROJ_TILE, n)
    use_rope = rope is not None
    const = lambda *shape: pl.BlockSpec(shape, lambda i, j: (0,) * len(shape))
    in_specs = [pl.BlockSpec((1, tm, d), lambda i, j: (i, j, 0)),
                pl.BlockSpec((1, 1, 8, d), lambda i, j: (i, kind, 0, 0)),
                const(1, d), const(d, IN_PAD), const(8, LANE), const(1, 256), const(1, LANE),
                const(256, 768), const(LANE, 768), const(LANE, 384),
                const(256, 256), const(256, 256), const(256, 256)]
    args = [xs, modv, g1, w_in_p, gains, gcq, gckv, wuq, wukvk, wukvv, mm, mx, mbb]
    if use_rope:
        in_specs.append(pl.BlockSpec((tm, 512), lambda i, j: (j, 0)))
        args.append(rope)
    blocks = (3, 1, 1, B_HEADS, B_HEADS, B_HEADS // 2)
    widths = (256, 128, 128)
    out_specs = ([pl.BlockSpec((1, nb, tm, LANE), lambda i, j: (i, 0, j, 0)) for nb in blocks]
                 + [pl.BlockSpec((1, tm, w), lambda i, j: (i, j, 0)) for w in widths])
    out_shape = ([jax.ShapeDtypeStruct((b, nb, n, LANE), BF16) for nb in blocks]
                 + [jax.ShapeDtypeStruct((b, n, w), BF16) for w in widths])
    return pl.pallas_call(
        functools.partial(_inproj_kernel, use_rope=use_rope),
        grid=(b, n // tm),
        in_specs=in_specs,
        out_specs=out_specs,
        out_shape=out_shape,
        compiler_params=_cparams(("parallel", "parallel")),
        name="inproj_rope" if use_rope else "inproj_ctx",
    )(*args)


_A_MAPS = (lambda h: h % 3, lambda h: 0, lambda h: 0)
_B_MAPS = (lambda h: h, lambda h: h, lambda h: h // 2)


def _qk(q, k):
    return lax.dot_general(q, k, (((1,), (1,)), ((), ())), preferred_element_type=F32)


def _half_mask(q, half):
    lane = lax.broadcasted_iota(jnp.int32, q.shape, 1)
    return jnp.where((lane // 64) == half, q, jnp.zeros_like(q))


def _fused_passes(q, chunks, s_w, s_r, mb_r):
    mx = None
    acc = None
    for k_ref, v_ref, st, sz, off in chunks:
        s = _qk(q, k_ref[0, 0, st:st + sz, :])
        s_w[:, off:off + sz] = s
        for t in range(sz // LANE):
            blk = s[:, t * LANE:(t + 1) * LANE]
            mx = blk if mx is None else jnp.maximum(mx, blk)
        ps = [jnp.exp2(s_r[:, off + t * LANE:off + (t + 1) * LANE] - mb_r).astype(BF16)
              for t in range(sz // LANE)]
        v_ones = jnp.concatenate([v_ref[0, 0, st:st + sz, :], jnp.ones((sz, LANE), BF16)], axis=1)
        part = jnp.dot(jnp.concatenate(ps, axis=1), v_ones, preferred_element_type=F32)
        acc = part if acc is None else acc + part
    mb_new = jnp.broadcast_to(jnp.max(mx, axis=1, keepdims=True), mx.shape)
    return mb_new, (acc[:, :LANE] / acc[:, LANE:LANE + 1]).astype(BF16)


def _key_chunks(kc_ref, vc_ref, kl_ref, vl_ref):
    n_ctx = kc_ref.shape[2]
    chunks = [(kc_ref, vc_ref, 0, n_ctx, 0)]
    if kl_ref is not None:
        for c in range(kl_ref.shape[2] // KEY_CHUNK):
            chunks.append((kl_ref, vl_ref, c * KEY_CHUNK, KEY_CHUNK, n_ctx + c * KEY_CHUNK))
    return chunks


def _dense_kernel(q_ref, kc_ref, kl_ref, vcp_ref, vlp_ref, vcc_ref, vlc_ref, o_ref,
                  sa_sc, sb_sc, mba_sc, mbb_sc, *, mask_q, n_tiles, tiles_per_sample, tiles_per_head, n_sub):
    step = pl.program_id(0)
    tq = sa_sc.shape[0]

    @pl.when(step == 0)
    def _():
        sa_sc[...] = jnp.zeros(sa_sc.shape, F32)
        mba_sc[...] = jnp.zeros(mba_sc.shape, F32)

    if mask_q:
        first = jnp.minimum(n_sub * step, n_tiles - n_sub)
        half = ((first % tiles_per_sample) // tiles_per_head) // 3
    for k in range(n_sub):
        rows = slice(k * tq, (k + 1) * tq)
        q = q_ref[0, 0, rows, :]
        if mask_q:
            q = _half_mask(q, half)
        v_refs = (vcp_ref, vlp_ref) if k == 0 else (vcc_ref, vlc_ref)
        chunks = _key_chunks(kc_ref, v_refs[0], kl_ref, v_refs[1])
        s_w, s_r, mb_w, mb_r = ((sb_sc, sa_sc, mbb_sc, mba_sc) if k % 2 == 0
                                else (sa_sc, sb_sc, mba_sc, mbb_sc))
        mb_new, o_ref[rows, :] = _fused_passes(q, chunks, s_w, s_r, mb_r[...])
        mb_w[...] = mb_new


def _dense_attention(q, kc, vc, kl, vl, n_heads, maps, mask_q, name):
    b, _, nq, _ = q.shape
    qmap, kmap, vmap = maps
    n_ctx, n_lat = kc.shape[2], kl.shape[2]
    tq = Q_TILE
    tph = nq // tq
    n_sub = next(n for n in (8, 4, 2) if tph % n == 0)
    assert tph % n_sub == 0
    tps = n_heads * tph
    n_tiles = b * tps
    first = lambda s: jnp.minimum(n_sub * s, n_tiles - n_sub)
    before = lambda s: jnp.maximum(n_sub * s - 1, 0)
    kv_blk = lambda n, fmap, tile: pl.BlockSpec(
        (1, 1, n, LANE), lambda s: (tile(s) // tps, fmap((tile(s) % tps) // tph), 0, 0))
    q_spec = pl.BlockSpec(
        (1, 1, n_sub * tq, LANE),
        lambda s: (first(s) // tps, qmap((first(s) % tps) // tph), (first(s) % tph) // n_sub, 0))
    return pl.pallas_call(
        functools.partial(_dense_kernel, mask_q=mask_q, n_tiles=n_tiles, tiles_per_sample=tps,
                          tiles_per_head=tph, n_sub=n_sub),
        grid=(n_tiles // n_sub + 1,),
        in_specs=[q_spec, kv_blk(n_ctx, kmap, first), kv_blk(n_lat, kmap, first),
                  kv_blk(n_ctx, vmap, before), kv_blk(n_lat, vmap, before),
                  kv_blk(n_ctx, vmap, first), kv_blk(n_lat, vmap, first)],
        out_specs=pl.BlockSpec((n_sub * tq, LANE), lambda s: (s, 0)),
        out_shape=jax.ShapeDtypeStruct(((n_tiles + n_sub) * tq, LANE), BF16),
        scratch_shapes=[pltpu.VMEM((tq, n_ctx + n_lat), F32), pltpu.VMEM((tq, n_ctx + n_lat), F32),
                        pltpu.VMEM((tq, LANE), F32), pltpu.VMEM((tq, LANE), F32)],
        compiler_params=_cparams(("arbitrary",)),
        name=name,
    )(q, kc, kl, vc, vl, vc, vl)


def _dense_ctx_kernel(q_ref, k_ref, v_ref, o_ref, *, n_heads, maps, mask_q):
    qmap, kmap, vmap = maps
    n_keys = k_ref.shape[2]
    ones = jnp.ones((n_keys, LANE), BF16)
    for h in range(n_heads):
        q = q_ref[0, qmap(h)]
        if mask_q:
            q = _half_mask(q, h // 3)
        s = _qk(q, k_ref[0, kmap(h)])
        tiles = [s[:, t * LANE:(t + 1) * LANE] for t in range(n_keys // LANE)]
        mx = functools.reduce(jnp.maximum, tiles)
        mb = jnp.broadcast_to(jnp.max(mx, axis=1, keepdims=True), mx.shape)
        p = jnp.concatenate([jnp.exp2(t - mb).astype(BF16) for t in tiles], axis=1)
        o = jnp.dot(p, jnp.concatenate([v_ref[0, vmap(h)], ones], axis=1), preferred_element_type=F32)
        o_ref[0, h] = (o[:, :LANE] / o[:, LANE:LANE + 1]).astype(BF16)


def _dense_attention_ctx(q, kc, vc, n_heads, maps, mask_q, name):
    b, nqb, nq, _ = q.shape
    whole = lambda a: pl.BlockSpec((1,) + a.shape[1:], lambda i: (i, 0, 0, 0))
    return pl.pallas_call(
        functools.partial(_dense_ctx_kernel, n_heads=n_heads, maps=maps, mask_q=mask_q),
        grid=(b,),
        in_specs=[whole(q), whole(kc), whole(vc)],
        out_specs=pl.BlockSpec((1, n_heads, nq, LANE), lambda i: (i, 0, 0, 0)),
        out_shape=jax.ShapeDtypeStruct((b, n_heads, nq, LANE), BF16),
        compiler_params=_cparams(("parallel",)),
        name=name,
    )(q, kc, vc)


def _window_kernel(q_ref, kc_ref, vc_ref, sink_ref, *rest, use_window):
    if use_window:
        kl_ref, vl_ref, o_ref = rest
    else:
        (o_ref,) = rest
    tq = ROW_TILE
    span = tq + 2 * WINDOW
    kcx = kc_ref[0]
    ones_c = jnp.ones((kcx.shape[0], LANE), BF16)
    v_aug = jnp.concatenate([vc_ref[0], ones_c], axis=1)
    lane = lax.broadcasted_iota(jnp.int32, (tq, LANE), 1)
    for ti in range(q_ref.shape[1] // tq):
        rows = slice(ti * tq, (ti + 1) * tq)
        if use_window:
            n_lat = kl_ref.shape[1]
            t0 = (pl.program_id(1) * (q_ref.shape[1] // tq) + ti) * tq
            start = pl.multiple_of(jnp.clip(t0 - WINDOW, 0, n_lat - span), LANE)
            qpos = t0 + lax.broadcasted_iota(jnp.int32, (tq, span), 0)
            kpos = start + lax.broadcasted_iota(jnp.int32, (tq, span), 1)
            valid = jnp.abs(qpos - kpos) <= WINDOW
            kw = kl_ref[0, pl.ds(start, span), :]
            v_all = jnp.concatenate(
                [v_aug, jnp.concatenate([vl_ref[0, pl.ds(start, span), :], jnp.ones((span, LANE), BF16)], axis=1)],
                axis=0)
        else:
            v_all = v_aug
        res = []
        for h in range(C_HEADS):
            half = h // (C_HEADS // C_KV_HEADS)
            q = q_ref[0, rows, (h % 2) * LANE:(h % 2 + 1) * LANE]
            q = jnp.where((lane < 64) if half == 0 else (lane >= 64), q, jnp.zeros_like(q))
            sink = sink_ref[0:1, h:h + 1]
            s = _qk(q, kcx)
            if use_window:
                s = jnp.concatenate([s, jnp.where(valid, _qk(q, kw), NEG_INF)], axis=1)
            mx = s[:, 0:LANE]
            for t in range(1, s.shape[1] // LANE):
                mx = jnp.maximum(mx, s[:, t * LANE:(t + 1) * LANE])
            m = jnp.maximum(jnp.max(mx, axis=1, keepdims=True), sink)
            mb = jnp.broadcast_to(m, (tq, LANE))
            p = jnp.concatenate([jnp.exp2(s[:, t * LANE:(t + 1) * LANE] - mb).astype(BF16)
                                 for t in range(s.shape[1] // LANE)], axis=1)
            o = jnp.dot(p, v_all, preferred_element_type=F32)
            res.append(o[:, :LANE] / (o[:, LANE:LANE + 1] + jnp.exp2(sink - m)))
        for ob in range(2):
            o_ref[0, rows, ob * LANE:(ob + 1) * LANE] = jnp.where(lane < 64, res[ob], res[2 + ob]).astype(BF16)


def _window_attention(q, kc, vc, kl, vl, sink, name):
    b, nq, qw = q.shape
    tq = min(4 * ROW_TILE, nq)
    n_ctx = kc.shape[1]
    use_window = kl is not None
    in_specs = [pl.BlockSpec((1, tq, qw), lambda i, j: (i, j, 0)),
                pl.BlockSpec((1, n_ctx, LANE), lambda i, j: (i, 0, 0)),
                pl.BlockSpec((1, n_ctx, LANE), lambda i, j: (i, 0, 0)),
                pl.BlockSpec((8, LANE), lambda i, j: (0, 0))]
    args = [q, kc, vc, sink]
    if use_window:
        n_lat = kl.shape[1]
        in_specs += [pl.BlockSpec((1, n_lat, LANE), lambda i, j: (i, 0, 0)),
                     pl.BlockSpec((1, n_lat, LANE), lambda i, j: (i, 0, 0))]
        args += [kl, vl]
    return pl.pallas_call(
        functools.partial(_window_kernel, use_window=use_window),
        grid=(b, nq // tq),
        in_specs=in_specs,
        out_specs=pl.BlockSpec((1, tq, qw), lambda i, j: (i, j, 0)),
        out_shape=jax.ShapeDtypeStruct((b, nq, qw), BF16),
        compiler_params=_cparams(("parallel", "parallel")),
        name=name,
    )(*args)


def _outproj_kernel(*refs):
    oa = refs[0:A_HEADS]
    ob = refs[A_HEADS:A_HEADS + B_HEADS]
    oc_ref, x_ref, mod_ref, g2n_ref, w_ref, wr_ref, xo_ref, h2_ref, aff_ref = refs[A_HEADS + B_HEADS:]
    wr = wr_ref[...]
    wr_hi = wr.astype(BF16)
    wr_lo = (wr - wr_hi.astype(F32)).astype(BF16)
    wr_both = jnp.concatenate([wr_hi, wr_lo], axis=1)

    def process(rows):
        n_rows = rows.stop - rows.start
        low = lax.broadcasted_iota(jnp.int32, (n_rows, LANE), 1) < 64
        parts = [jnp.where(low, oa[i][rows, :], oa[3 + i][rows, :]) for i in range(3)]
        parts += [jnp.where(low, ob[2 * i][rows, :], ob[2 * i + 1][rows, :]) for i in range(3)]
        parts.append(oc_ref[0, rows, :])
        mix = jnp.dot(jnp.concatenate(parts, axis=1), w_ref[...], preferred_element_type=F32)
        x = x_ref[0, rows, :] + mod_ref[0, 0, 2:3, :] * mix
        xo_ref[0, rows, :] = x
        y = x * lax.rsqrt(jnp.mean(x * x, axis=-1, keepdims=True) + EPS) * g2n_ref[...]
        h2 = y * (1.0 + mod_ref[0, 0, 4:5, :]) + mod_ref[0, 0, 3:4, :]
        h2_hi = h2.astype(BF16)
        h2_ref[0, rows, :] = h2_hi
        h2_lo = (h2 - h2_hi.astype(F32)).astype(BF16)
        both = jnp.dot(h2_hi, wr_both, preferred_element_type=F32)
        logits = both[:, :LANE] + both[:, LANE:] + jnp.dot(h2_lo, wr_hi, preferred_element_type=F32)
        logits = logits.T[0:aff_ref.shape[1], :]
        z = jnp.exp(logits - jnp.max(logits, axis=0, keepdims=True))
        aff_ref[0, :, rows] = z / jnp.sum(z, axis=0, keepdims=True)

    process(slice(0, x_ref.shape[1]))


def _outproj(oa, ob, oc, xs, modv, kind, g2n, w_mix, wr_t, pad_rows, name):
    b, n, d = xs.shape
    tm = min(Q_TILE, n)
    assert pad_rows % tm == 0 and n % tm == 0
    e = N_EXPERTS
    const = lambda *shape: pl.BlockSpec(shape, lambda i, j: (0,) * len(shape))
    row = lambda w: pl.BlockSpec((1, tm, w), lambda i, j: (i, j, 0))
    n_heads = A_HEADS
    head = lambda h: pl.BlockSpec((tm, LANE), lambda i, j: ((pad_rows + (i * n_heads + h) * n) // tm + j, 0))
    return pl.pallas_call(
        _outproj_kernel,
        grid=(b, n // tm),
        in_specs=[head(h) for h in range(A_HEADS)] + [head(h) for h in range(B_HEADS)] + [
                  row(256), row(d),
                  pl.BlockSpec((1, 1, 8, d), lambda i, j: (i, kind, 0, 0)),
                  const(1, d), const(d, d), const(d, LANE)],
        out_specs=[row(d), row(d), pl.BlockSpec((1, e, tm), lambda i, j: (i, 0, j))],
        out_shape=[jax.ShapeDtypeStruct((b, n, d), F32), jax.ShapeDtypeStruct((b, n, d), BF16),
                   jax.ShapeDtypeStruct((b, e, n), F32)],
        compiler_params=_cparams(("parallel", "parallel")),
        name=name,
    )(*([oa] * A_HEADS), *([ob] * B_HEADS), oc, xs, modv, g2n, w_mix, wr_t)


def _prefix_exclusive(mask_f, tri):
    e, t = mask_f.shape
    ck = tri.shape[0]
    carry = jnp.zeros((e, 1), F32)
    parts, offs = [], [carry]
    for c in range(t // ck):
        blk = mask_f[:, c * ck:(c + 1) * ck]
        parts.append(jnp.dot(blk.astype(BF16), tri, preferred_element_type=F32) + carry)
        carry = carry + jnp.sum(blk, axis=1, keepdims=True)
        offs.append(carry)
    return jnp.concatenate(parts, axis=1) if len(parts) > 1 else parts[0], offs


def _topk_kernel(aff_ref, slot_ref, cnt_ref, *, cap, slot_stride):
    aff = aff_ref[0]
    e, t = aff.shape
    kf = jnp.float32(cap)

    def count_ge(v):
        return jnp.sum((aff >= v).astype(F32), axis=1, keepdims=True)

    def search(i, lo_bits):
        cand = lo_bits | (jnp.int32(1) << (30 - i))
        return jnp.where(count_ge(pltpu.bitcast(cand, F32)) >= kf, cand, lo_bits)

    thr_bits = lax.fori_loop(0, 31, search, jnp.zeros((e, 1), jnp.int32))
    lo = pltpu.bitcast(thr_bits, F32)
    hi = pltpu.bitcast(thr_bits + 1, F32)

    def refine(i, carry):
        lo, hi = carry
        mid = 0.5 * (lo + hi)
        ok = count_ge(mid) >= kf
        return jnp.where(ok, mid, lo), jnp.where(ok, hi, mid)

    thr, _ = lax.fori_loop(0, TOPK_REFINE_STEPS, refine, (lo, hi))
    gt = aff > thr
    eq = aff == thr
    need = kf - jnp.sum(gt.astype(F32), axis=1, keepdims=True)

    r = lax.broadcasted_iota(jnp.int32, (TOK_CHUNK, TOK_CHUNK), 0)
    c = lax.broadcasted_iota(jnp.int32, (TOK_CHUNK, TOK_CHUNK), 1)
    tri = (r < c).astype(BF16)
    tie_rank, _ = _prefix_exclusive(eq.astype(F32), tri)
    sel = gt | (eq & (tie_rank < need))
    slot, offs = _prefix_exclusive(sel.astype(F32), tri)
    base = pl.program_id(0) * slot_stride
    slot_ref[0] = jnp.where(sel, slot.astype(jnp.int32) + base, -1)
    lane = lax.broadcasted_iota(jnp.int32, (e, LANE), 1)
    cnt = jnp.zeros((e, LANE), jnp.int32)
    for ci, off in enumerate(offs):
        cnt = jnp.where(lane == ci, off.astype(jnp.int32) + base, cnt)
    cnt_ref[0] = cnt


def _topk(aff_t, cap, slot_stride, name):
    b, e, t = aff_t.shape
    return pl.pallas_call(
        functools.partial(_topk_kernel, cap=cap, slot_stride=slot_stride),
        grid=(b,),
        in_specs=[pl.BlockSpec((1, e, t), lambda i: (i, 0, 0))],
        out_specs=[pl.BlockSpec((1, e, t), lambda i: (i, 0, 0)),
                   pl.BlockSpec((1, e, LANE), lambda i: (i, 0, 0))],
        out_shape=[jax.ShapeDtypeStruct((b, e, t), jnp.int32), jax.ShapeDtypeStruct((b, e, LANE), jnp.int32)],
        compiler_params=_cparams(("parallel",)),
        name=name,
    )(aff_t)


def _moe_kernel(cnt_ref, h_ref, slot_ref, slotp_ref, gate_ref, wg_ref, wu_ref, wd_ref, mod_ref, x_hbm, o_hbm,
                acc_sc, xe_sc, gs_sc, ye_sc, in_sem, out_sem, *, n_chunks, n_sb):
    bi = pl.program_id(0)
    ei = pl.program_id(1)
    n_b = pl.num_programs(0)
    n_e = pl.num_programs(1)

    def chunk_rows(c):
        return pl.ds(c * TOK_CHUNK, TOK_CHUNK)

    def load_dma(c):
        return pltpu.make_async_copy(x_hbm.at[bi, chunk_rows(c), :], acc_sc.at[chunk_rows(c), :], in_sem.at[c])

    def store_dma(sample, c):
        return pltpu.make_async_copy(acc_sc.at[chunk_rows(c), :], o_hbm.at[sample, chunk_rows(c), :],
                                     out_sem.at[c])

    def start_x_load():
        @pl.when(ei == 0)
        def _():
            for c in range(n_chunks):
                @pl.when(bi > 0)
                def _(c=c):
                    store_dma(bi - 1, c).wait()
                load_dma(c).start()

    def wait_for_x(at_expert):
        @pl.when(ei == at_expert)
        def _():
            for c in range(n_chunks):
                load_dma(c).wait()

    g2 = mod_ref[0, 0, 5:6, :]
    cap = n_sb * SLOT_BLOCK

    def counts_of(e):
        base = (bi * n_e + e) * 32
        return [cnt_ref[base + c] for c in range(n_chunks + 1)]

    def all_chunks_at_most(cs, limit):
        ok = cs[1] - cs[0] <= limit
        for c in range(1, n_chunks):
            ok = ok & (cs[c + 1] - cs[c] <= limit)
        return ok

    odd = (ei % 2) == 1
    counts = counts_of(ei)
    counts_other = counts_of(jnp.where(odd, ei - 1, ei + 1))
    pair_narrow = all_chunks_at_most(counts, PAIR_LIMIT) & all_chunks_at_most(counts_other, PAIR_LIMIT)

    def ffn():
        xe = xe_sc[0:cap, :].astype(BF16)
        f = wg_ref.shape[2]
        fb = 512
        y = None
        for f0 in range(0, f, fb):
            hg = jnp.dot(xe, wg_ref[0, :, f0:f0 + fb], preferred_element_type=F32)
            hu = jnp.dot(xe, wu_ref[0, :, f0:f0 + fb], preferred_element_type=F32)
            hid = (hg * jax.nn.sigmoid(hg) * hu).astype(BF16)
            part = jnp.dot(hid, wd_ref[0, f0:f0 + fb, :], preferred_element_type=F32)
            y = part if y is None else y + part
        return y

    def tok(c):
        return slice(c * TOK_CHUNK, (c + 1) * TOK_CHUNK)

    def slot_columns(ref):
        pad = jnp.zeros((LANE - n_chunks, TOK_CHUNK), F32)
        return jnp.concatenate([ref[0, 0].astype(F32), pad], axis=0).T

    @pl.when(pair_narrow)
    def _():
        xe_sc[...] = jnp.zeros_like(xe_sc)
        gs_sc[...] = jnp.zeros_like(gs_sc)
        srow = lax.broadcasted_iota(jnp.int32, (GATHER_WINDOW, TOK_CHUNK), 0)
        for c in range(n_chunks):
            g0 = pl.multiple_of((counts[c] // 8) * 8, 8)
            oh = (slot_ref[0, 0, c:c + 1, :] - g0) == srow
            xe_sc[pl.ds(g0, GATHER_WINDOW), :] += jnp.dot(
                oh.astype(BF16), h_ref[0, tok(c), :], preferred_element_type=F32)
            gs_sc[pl.ds(g0, GATHER_WINDOW), :] += jnp.sum(
                jnp.where(oh, gate_ref[0, 0, c:c + 1, :], 0.0), axis=1, keepdims=True)
        ye = ye_sc.at[ei % 2]
        ye[0:cap, :] = (ffn() * gs_sc[0:cap, :]).astype(BF16)
        ye[cap:cap + SLOT_BLOCK, :] = jnp.zeros((SLOT_BLOCK, ye_sc.shape[2]), BF16)
        start_x_load()

        @pl.when(odd)
        def _():
            wait_for_x(1)
            cols_prev = slot_columns(slotp_ref)
            cols_cur = slot_columns(slot_ref)
            scol = lax.broadcasted_iota(jnp.int32, (TOK_CHUNK, SLOT_BLOCK), 1).astype(F32)
            for c in range(n_chunks):
                w_prev = pl.multiple_of((counts_other[c] // 16) * 16, 16)
                w_cur = pl.multiple_of((counts[c] // 16) * 16, 16)
                oh_t = jnp.concatenate(
                    [((cols_prev[:, c:c + 1] - w_prev.astype(F32)) == scol).astype(BF16),
                     ((cols_cur[:, c:c + 1] - w_cur.astype(F32)) == scol).astype(BF16)], axis=1)
                rows = jnp.concatenate([ye_sc[0, pl.ds(w_prev, SLOT_BLOCK), :],
                                        ye_sc[1, pl.ds(w_cur, SLOT_BLOCK), :]], axis=0)
                acc_sc[tok(c), :] += g2 * jnp.dot(oh_t, rows, preferred_element_type=F32)

    @pl.when(jnp.logical_not(pair_narrow))
    def _():
        xe_sc[...] = jnp.zeros_like(xe_sc)
        gs_sc[...] = jnp.zeros_like(gs_sc)
        srow = lax.broadcasted_iota(jnp.int32, (SLOT_BLOCK, TOK_CHUNK), 0)
        base = (bi * n_e + ei) * 32

        def overlap(c, sb):
            return (cnt_ref[base + c] < (sb + 1) * SLOT_BLOCK) & (cnt_ref[base + c + 1] > sb * SLOT_BLOCK)

        def onehot(c, sb):
            return slot_ref[0, 0, pl.ds(c, 1), :] == (srow + sb * SLOT_BLOCK)

        def dyn_tok(c):
            return pl.ds(pl.multiple_of(c * TOK_CHUNK, TOK_CHUNK), TOK_CHUNK)

        def gather_chunk(c, carry):
            for sb in range(n_sb):
                @pl.when(overlap(c, sb))
                def _(sb=sb):
                    oh = onehot(c, sb)
                    rows = slice(sb * SLOT_BLOCK, (sb + 1) * SLOT_BLOCK)
                    xe_sc[rows] += jnp.dot(oh.astype(BF16), h_ref[0, dyn_tok(c), :], preferred_element_type=F32)
                    g = jnp.where(oh, gate_ref[0, 0, pl.ds(c, 1), :], 0.0)
                    gs_sc[rows] += jnp.sum(g, axis=1, keepdims=True)
            return carry

        lax.fori_loop(0, n_chunks, gather_chunk, 0)
        ye_sc[0, 0:cap, :] = (ffn() * gs_sc[0:cap, :]).astype(BF16)
        start_x_load()
        wait_for_x(0)

        def scatter_chunk(c, carry):
            for sb in range(n_sb):
                @pl.when(overlap(c, sb))
                def _(sb=sb):
                    oh_t = onehot(c, sb).astype(F32).T.astype(BF16)
                    rows = slice(sb * SLOT_BLOCK, (sb + 1) * SLOT_BLOCK)
                    acc_sc[dyn_tok(c), :] += g2 * jnp.dot(oh_t, ye_sc[0, rows, :], preferred_element_type=F32)
            return carry

        lax.fori_loop(0, n_chunks, scatter_chunk, 0)

    @pl.when(ei == n_e - 1)
    def _():
        for c in range(n_chunks):
            store_dma(bi, c).start()

    @pl.when((ei == n_e - 1) & (bi == n_b - 1))
    def _():
        for c in range(n_chunks):
            store_dma(bi, c).wait()


def _moe(h2, slot, gate, cnt, wg, wu, wd, xs, modv, kind, cap, name):
    b, t, d = h2.shape
    e = slot.shape[1]
    f = wg.shape[2]
    n_chunks = t // TOK_CHUNK
    n_sb = cap // SLOT_BLOCK
    slot4 = slot.reshape(b, e, n_chunks, TOK_CHUNK)
    gate4 = gate.reshape(b, e, n_chunks, TOK_CHUNK)
    cnt_flat = cnt[:, :, :32].reshape(-1)
    grid_spec = pltpu.PrefetchScalarGridSpec(
        num_scalar_prefetch=1,
        grid=(b, e),
        in_specs=[pl.BlockSpec((1, t, d), lambda i, j, s: (i, 0, 0)),
                  pl.BlockSpec((1, 1, n_chunks, TOK_CHUNK), lambda i, j, s: (i, j, 0, 0)),
                  pl.BlockSpec((1, 1, n_chunks, TOK_CHUNK), lambda i, j, s: (i, jnp.maximum(j - 1, 0), 0, 0)),
                  pl.BlockSpec((1, 1, n_chunks, TOK_CHUNK), lambda i, j, s: (i, j, 0, 0)),
                  pl.BlockSpec((1, d, f), lambda i, j, s: (j, 0, 0)),
                  pl.BlockSpec((1, d, f), lambda i, j, s: (j, 0, 0)),
                  pl.BlockSpec((1, f, d), lambda i, j, s: (j, 0, 0)),
                  pl.BlockSpec((1, 1, 8, d), lambda i, j, s: (i, kind, 0, 0)),
                  pl.BlockSpec(memory_space=pl.ANY)],
        out_specs=pl.BlockSpec(memory_space=pl.ANY),
        scratch_shapes=[pltpu.VMEM((t, d), F32),
                        pltpu.VMEM((cap + 2 * SLOT_BLOCK, d), F32), pltpu.VMEM((cap + 2 * SLOT_BLOCK, 1), F32),
                        pltpu.VMEM((2, cap + SLOT_BLOCK, d), BF16),
                        pltpu.SemaphoreType.DMA((n_chunks,)), pltpu.SemaphoreType.DMA((n_chunks,))],
    )
    return pl.pallas_call(
        functools.partial(_moe_kernel, n_chunks=n_chunks, n_sb=n_sb),
        grid_spec=grid_spec,
        out_shape=jax.ShapeDtypeStruct((b, t, d), F32),
        compiler_params=_cparams(("arbitrary", "arbitrary")),
        name=name,
    )(cnt_flat, h2, slot4, slot4, gate4, wg, wu, wd, modv, xs)


def _cast_kernel(w_ref, o_ref):
    o_ref[...] = w_ref[0].astype(BF16)


def _layer_bf16(w, layer):
    _, e, r, c = w.shape
    return pl.pallas_call(
        _cast_kernel,
        grid=(e,),
        in_specs=[pl.BlockSpec((1, 1, r, c), lambda i: (layer, i, 0, 0))],
        out_specs=pl.BlockSpec((1, r, c), lambda i: (i, 0, 0)),
        out_shape=jax.ShapeDtypeStruct((e, r, c), BF16),
        compiler_params=_cparams(("parallel",)),
        name="expert_weights_bf16",
    )(w)


def _pad_lanes(v, width):
    return jnp.pad(v, (0, width - v.shape[0]))


def kernel(x, c, ctx, c_ctx, w_ada, b_ada, norm1_g, norm2_g, w_in, a_q_norm, a_k_norm, b_cq_norm, b_ckv_norm, w_uq, w_ukv, b_qn_norm, b_kn_norm, b_qr_norm, b_kr_norm, c_q_norm, c_k_norm, c_sink, w_out, w_router, w_e_gate, w_e_up, w_e_down):
    b, t, d = x.shape
    n_ctx = ctx.shape[1]
    depth = w_ada.shape[0]
    assert n_ctx == ROW_TILE and t % KEY_CHUNK == 0 and t % GRID_W == 0
    cap = max(1, CAP_FACTOR * t // N_EXPERTS)
    cap_c = max(1, CAP_FACTOR * n_ctx // N_EXPERTS)
    assert cap % SLOT_BLOCK == 0 and (b * cap_c) % SLOT_BLOCK == 0

    rows = ((b + 1 + 7) // 8) * 8
    cc = jnp.zeros((rows, d), F32).at[:b].set(c).at[b].set(c_ctx)
    mod = _modulation(cc, w_ada, b_ada)

    rope = _rope_tables(t)
    mm, mx, mbb = (jnp.asarray(m, BF16) for m in _seg_matrices())
    in_perm = _in_perm()
    uq_perm = _uq_perm()
    ukv_pk, ukv_pv = _ukv_perms()
    oa_rows = np.asarray(_pair_cols(0, A_HEADS))
    oc_rows = np.asarray(_pair_cols(0, C_HEADS)) + 768
    sa = HEAD_DIM ** -0.5
    sb = (B_NOPE + B_ROPE) ** -0.5
    z32, z64 = jnp.zeros((32,), F32), jnp.zeros((64,), F32)

    for l in range(depth):
        last = l == depth - 1
        m6 = mod[l].reshape(rows, 6, d)
        m8 = jnp.pad(m6, ((0, 0), (0, 2), (0, 0)))
        modv = jnp.stack([jnp.broadcast_to(m8[b][None], (b, 8, d)), m8[:b]], axis=1)

        w_in_p = _take_cols(w_in[l], in_perm).astype(BF16)
        wuq = jnp.pad(_take_cols(w_uq[l], uq_perm), ((0, 256 - B_Q_RANK), (0, 0))).astype(BF16)
        wukvk = _take_cols(w_ukv[l], ukv_pk).astype(BF16)
        wukvv = jnp.take(w_ukv[l], jnp.asarray(ukv_pv, jnp.int32), axis=1).astype(BF16)
        gains = jnp.stack([
            jnp.tile(a_q_norm[l], 2) * (sa * LOG2E), jnp.tile(a_k_norm[l], 2),
            jnp.tile(c_q_norm[l], 2) * (sa * LOG2E), jnp.tile(c_k_norm[l], 2),
            jnp.concatenate([b_qn_norm[l] * (sb * LOG2E), b_qr_norm[l] * (sb * LOG2E), z32]),
            jnp.concatenate([b_kn_norm[l], z64]),
            jnp.concatenate([z64, b_kr_norm[l], z32]),
            jnp.zeros((LANE,), F32)])
        gcq = _pad_lanes(b_cq_norm[l], 256)[None]
        gckv = b_ckv_norm[l][None]
        g1 = norm1_g[l][None]
        g2n = norm2_g[l][None]
        mix_rows = np.concatenate([oa_rows, np.arange(384, 768), oc_rows])
        w_mix = jnp.take(w_out[l], jnp.asarray(mix_rows, jnp.int32), axis=0).astype(BF16)
        wr_t = jnp.pad(w_router[l], ((0, 0), (0, LANE - N_EXPERTS)))
        sink = jnp.zeros((8, LANE), F32).at[0, :C_HEADS].set(c_sink[l] * LOG2E)
        wg = _layer_bf16(w_e_gate, l)
        wu = _layer_bf16(w_e_up, l)
        wd = _layer_bf16(w_e_down, l)

        shared = (g1, w_in_p, gains, gcq, gckv, wuq, wukvk, wukvv, mm, mx, mbb)
        qa, ka, va, qb, kb, vb, qc, kc, vc = _inproj(x, modv, 1, *shared, rope)
        qac, kac, vac, qbc, kbc, vbc, qcc, kcc, vcc = _inproj(ctx, modv, 0, *shared, None)

        o_a = _dense_attention(qa, kac, vac, ka, va, A_HEADS, _A_MAPS, True, "attn_a")
        o_b = _dense_attention(qb, kbc, vbc, kb, vb, B_HEADS, _B_MAPS, False, "attn_b")
        o_c = _window_attention(qc, kcc, vcc, kc, vc, sink, "attn_c")
        x_mid, h2, aff = _outproj(o_a, o_b, o_c, x, modv, 1, g2n, w_mix, wr_t, Q_TILE, "outproj")
        slot, cnt = _topk(aff, cap, 0, "topk")
        x = _moe(h2, slot, aff, cnt, wg, wu, wd, x_mid, modv, 1, cap, "moe")

        if not last:
            o_ac = _dense_attention_ctx(qac, kac, vac, A_HEADS, _A_MAPS, True, "attn_a_ctx")
            o_bc = _dense_attention_ctx(qbc, kbc, vbc, B_HEADS, _B_MAPS, False, "attn_b_ctx")
            o_cc = _window_attention(qcc, kcc, vcc, None, None, sink, "attn_c_ctx")
            flat = lambda o: o.reshape(-1, LANE)
            c_mid, hc2, aff_c = _outproj(flat(o_ac), flat(o_bc), o_cc, ctx, modv, 0, g2n, w_mix, wr_t, 0,
                                         "outproj_ctx")
            slot_c, cnt_c = _topk(aff_c, cap_c, cap_c, "topk_ctx")
            e = N_EXPERTS
            slot_f = jnp.transpose(slot_c, (1, 0, 2)).reshape(1, e, b * n_ctx)
            gate_f = jnp.transpose(aff_c, (1, 0, 2)).reshape(1, e, b * n_ctx)
            starts = jnp.transpose(cnt_c[:, :, 0], (1, 0))
            cnt_f = jnp.concatenate([starts, cnt_c[b - 1, :, 1:2],
                                     jnp.zeros((e, LANE - b - 1), jnp.int32)], axis=1)[None]
            ctx = _moe(hc2.reshape(1, b * n_ctx, d), slot_f, gate_f, cnt_f, wg, wu, wd,
                       c_mid.reshape(1, b * n_ctx, d), modv, 0, b * cap_c, "moe_ctx").reshape(b, n_ctx, d)
    return x
```

```python
import functools

import numpy as np
import jax
import jax.numpy as jnp
from jax import lax
from jax.experimental import pallas as pl
from jax.experimental.pallas import tpu as pltpu

F32 = jnp.float32
BF16 = jnp.bfloat16

GRID_W = 64
ROPE_THETA = 10000.0
EPS = 1e-6
NEG_INF = -1e30
HEAD_DIM = 64
A_HEADS, A_KV_HEADS = 6, 2
B_HEADS, B_Q_RANK, B_KV_RANK, B_NOPE, B_ROPE, B_V = 6, 192, 128, 64, 32, 64
C_HEADS, C_KV_HEADS = 4, 2
N_EXPERTS = 16
CAP_FACTOR = 2
WINDOW = 128

LANE = 128
ROW_TILE = 256
PROJ_TILE = 1024
PROJ_SLAB = 256
Q_TILE = 512
KEY_CHUNK = 512
LOG2E = 1.4426950408889634
SLOT_BLOCK = 128
TOK_CHUNK = 256
GATHER_WINDOW = 144
PAIR_LIMIT = SLOT_BLOCK - 16
TOPK_REFINE_STEPS = 10
V7X_VMEM_BYTES = 64 * 1024 * 1024
VMEM_LIMIT = V7X_VMEM_BYTES - 8 * 1024 * 1024

_QA, _KA, _VA = 0, 384, 512
_QC, _KC, _VC = 640, 896, 1024
_BCQ, _BCKV, _BKR = 1152, 1408, 1536
IN_PAD = 1664
_ORIG = dict(aq=0, ak=384, av=512, bcq=640, bckv=832, bkr=960, cq=992, ck=1248, cv=1376)


def _cparams(sem, vmem=VMEM_LIMIT):
    return pltpu.CompilerParams(dimension_semantics=sem, vmem_limit_bytes=vmem)


def _pair_cols(base, n_heads):
    half = n_heads // 2
    cols = []
    for i in range(half):
        cols += list(range(base + i * 64, base + (i + 1) * 64))
        cols += list(range(base + (half + i) * 64, base + (half + i + 1) * 64))
    return cols


def _in_perm():
    perm = -np.ones((IN_PAD,), np.int64)
    perm[_QA:_QA + 384] = _pair_cols(_ORIG["aq"], A_HEADS)
    perm[_KA:_KA + 128] = np.arange(_ORIG["ak"], _ORIG["ak"] + 128)
    perm[_VA:_VA + 128] = np.arange(_ORIG["av"], _ORIG["av"] + 128)
    perm[_QC:_QC + 256] = _pair_cols(_ORIG["cq"], C_HEADS)
    perm[_KC:_KC + 128] = np.arange(_ORIG["ck"], _ORIG["ck"] + 128)
    perm[_VC:_VC + 128] = np.arange(_ORIG["cv"], _ORIG["cv"] + 128)
    perm[_BCQ:_BCQ + B_Q_RANK] = np.arange(_ORIG["bcq"], _ORIG["bcq"] + B_Q_RANK)
    perm[_BCKV:_BCKV + 128] = np.arange(_ORIG["bckv"], _ORIG["bckv"] + 128)
    perm[_BKR + 64:_BKR + 96] = np.arange(_ORIG["bkr"], _ORIG["bkr"] + 32)
    return perm


def _take_cols(w, perm):
    safe = np.where(perm >= 0, perm, 0)
    out = jnp.take(w, jnp.asarray(safe, jnp.int32), axis=1)
    return jnp.where(jnp.asarray(perm >= 0)[None, :], out, 0.0)


def _uq_perm():
    perm = -np.ones((B_HEADS * LANE,), np.int64)
    for h in range(B_HEADS):
        perm[h * LANE:h * LANE + 96] = np.arange(h * 96, h * 96 + 96)
    return perm


def _ukv_perms():
    pk = -np.ones((B_HEADS * LANE,), np.int64)
    pv = np.zeros((B_HEADS * 64,), np.int64)
    for h in range(B_HEADS):
        pk[h * LANE:h * LANE + 64] = np.arange(h * 128, h * 128 + 64)
        pv[h * 64:(h + 1) * 64] = np.arange(h * 128 + 64, h * 128 + 128)
    return pk, pv


def _seg_matrices():
    i = np.arange(LANE)
    m64 = (i[:, None] // 64 == i[None, :] // 64).astype(np.float32) / 64.0
    mb = np.zeros((LANE, LANE), np.float32)
    mb[:64, :64] = 1.0 / 64.0
    mb[64:96, 64:96] = 1.0 / 32.0
    def pair(a, b):
        out = np.zeros((2 * LANE, 2 * LANE), np.float32)
        out[:LANE, :LANE] = a
        out[LANE:, LANE:] = b
        return out

    return pair(m64, m64), pair(m64, mb), pair(mb, mb)


def _rope_tables(length):
    rows = length // GRID_W
    t = np.arange(rows * GRID_W)
    row = np.repeat(np.arange(rows), GRID_W).astype(np.float32)
    col = (t % GRID_W).astype(np.float32)

    def tabs(dim):
        axis_dim = dim // 2
        inv = (np.float32(ROPE_THETA) ** (-np.arange(0, axis_dim, 2, dtype=np.float32) / axis_dim)).astype(np.float32)
        ar = row[:, None] * inv[None, :]
        ac = col[:, None] * inv[None, :]
        cos = np.concatenate([np.cos(ar), np.cos(ar), np.cos(ac), np.cos(ac)], axis=-1)
        sin = np.concatenate([-np.sin(ar), np.sin(ar), -np.sin(ac), np.sin(ac)], axis=-1)
        return cos.astype(np.float32), sin.astype(np.float32)

    c64, s64 = tabs(HEAD_DIM)
    cb, sb = tabs(B_ROPE)
    n = c64.shape[0]
    ones64, zeros64 = np.ones((n, 64), np.float32), np.zeros((n, 64), np.float32)
    ones32, zeros32 = np.ones((n, 32), np.float32), np.zeros((n, 32), np.float32)
    return jnp.asarray(np.concatenate([c64, c64, s64, s64,
                                       ones64, cb, ones32, zeros64, sb, zeros32], axis=-1))


def _mod_kernel(c_ref, w_ref, b_ref, o_ref):
    cv = c_ref[...]
    a = (cv * jax.nn.sigmoid(cv)).astype(BF16)
    o_ref[0] = jnp.dot(a, w_ref[0].astype(BF16), preferred_element_type=F32) + b_ref[0]


def _modulation(cc, w_ada, b_ada):
    n_layers, d, n6 = w_ada.shape
    rows = cc.shape[0]
    tn = 1536
    return pl.pallas_call(
        _mod_kernel,
        grid=(n_layers, n6 // tn),
        in_specs=[pl.BlockSpec((rows, d), lambda l, j: (0, 0)),
                  pl.BlockSpec((1, d, tn), lambda l, j: (l, 0, j)),
                  pl.BlockSpec((1, 1, tn), lambda l, j: (l, 0, j))],
        out_specs=pl.BlockSpec((1, rows, tn), lambda l, j: (l, 0, j)),
        out_shape=jax.ShapeDtypeStruct((n_layers, rows, n6), F32),
        compiler_params=_cparams(("parallel", "parallel")),
        name="adaln_mod",
    )(cc, w_ada, b_ada.reshape(n_layers, 1, n6))


def _seg_rsqrt(x2, m_ref):
    return lax.rsqrt(jnp.dot((x2 * x2).astype(BF16), m_ref[...], preferred_element_type=F32) + EPS)


def _rope(x, cos, sin, half):
    lane = lax.broadcasted_iota(jnp.int32, x.shape, 1)
    lo = (lane % (2 * half)) < half
    partner = jnp.where(lo, pltpu.roll(x, LANE - half, 1), pltpu.roll(x, half, 1))
    return x * cos + partner * sin


def _inproj_kernel(x_ref, mod_ref, g1_ref, w_ref, gains_ref, gcq_ref, gckv_ref, wuq_ref, wukvk_ref,
                   wukvv_ref, mm_ref, mx_ref, mbb_ref, *rest, use_rope):
    if use_rope:
        rope_ref = rest[0]
        outs = rest[1:]
    else:
        rope_ref = None
        outs = rest
    qa_ref, ka_ref, va_ref, qb_ref, kb_ref, vb_ref, qc_ref, kc_ref, vc_ref = outs

    def process(rows):
        x = x_ref[0, rows, :]
        y = x * lax.rsqrt(jnp.mean(x * x, axis=-1, keepdims=True) + EPS) * g1_ref[...]
        h = y * (1.0 + mod_ref[0, 0, 1:2, :]) + mod_ref[0, 0, 0:1, :]
        p = jnp.dot(h.astype(BF16), w_ref[...], preferred_element_type=F32)

        def head_pair(x2, m_ref, gain_rows, rope_cols, half):
            r = _seg_rsqrt(x2, m_ref)
            out = []
            for i in range(2):
                sl = slice(i * LANE, (i + 1) * LANE)
                y = x2[:, sl] * r[:, sl] * gains_ref[gain_rows[i]:gain_rows[i] + 1, :]
                if rope_ref is not None and rope_cols[i] is not None:
                    c0, s0 = rope_cols[i]
                    y = _rope(y, rope_ref[rows, c0:c0 + LANE], rope_ref[rows, s0:s0 + LANE], half[i])
                out.append(y)
            return out

        r64 = (0, 128)
        rb = (256, 384)
        qa0, qa1 = head_pair(p[:, _QA:_QA + 256], mm_ref, (0, 0), (r64, r64), (16, 16))
        qa2, ka = head_pair(p[:, _QA + 256:_QA + 512], mm_ref, (0, 1), (r64, r64), (16, 16))
        for i, blk in enumerate((qa0, qa1, qa2)):
            qa_ref[0, i, rows, :] = blk.astype(BF16)
        ka_ref[0, 0, rows, :] = ka.astype(BF16)
        va_ref[0, 0, rows, :] = p[:, _VA:_VA + LANE].astype(BF16)
        qc0, qc1 = head_pair(p[:, _QC:_QC + 256], mm_ref, (2, 2), (r64, r64), (16, 16))
        qc_ref[0, rows, 0:LANE] = qc0.astype(BF16)
        qc_ref[0, rows, LANE:2 * LANE] = qc1.astype(BF16)
        kc, kpe = head_pair(jnp.concatenate([p[:, _KC:_KC + LANE], p[:, _BKR:_BKR + LANE]], axis=1),
                            mx_ref, (3, 6), (r64, rb), (16, 8))
        kc_ref[0, rows, :] = kc.astype(BF16)
        vc_ref[0, rows, :] = p[:, _VC:_VC + LANE].astype(BF16)

        cq = p[:, _BCQ:_BCQ + 256]
        cq = cq * lax.rsqrt(jnp.sum(cq * cq, axis=-1, keepdims=True) * (1.0 / B_Q_RANK) + EPS) * gcq_ref[...]
        qb = jnp.dot(cq.astype(BF16), wuq_ref[...], preferred_element_type=F32)
        ckv = p[:, _BCKV:_BCKV + LANE]
        ckv = ckv * lax.rsqrt(jnp.mean(ckv * ckv, axis=-1, keepdims=True) + EPS) * gckv_ref[...]
        ckv = ckv.astype(BF16)
        kbn = jnp.dot(ckv, wukvk_ref[...], preferred_element_type=F32)
        vbv = jnp.dot(ckv, wukvv_ref[...], preferred_element_type=F32).astype(BF16)
        for i in range(B_HEADS // 2):
            vb_ref[0, i, rows, :] = vbv[:, i * LANE:(i + 1) * LANE]
        for i in range(B_HEADS // 2):
            sl = slice(2 * i * LANE, (2 * i + 2) * LANE)
            q0, q1 = head_pair(qb[:, sl], mbb_ref, (4, 4), (rb, rb), (8, 8))
            k0, k1 = head_pair(kbn[:, sl], mbb_ref, (5, 5), (None, None), (8, 8))
            qb_ref[0, 2 * i, rows, :] = q0.astype(BF16)
            qb_ref[0, 2 * i + 1, rows, :] = q1.astype(BF16)
            kb_ref[0, 2 * i, rows, :] = (k0 + kpe).astype(BF16)
            kb_ref[0, 2 * i + 1, rows, :] = (k1 + kpe).astype(BF16)

    slab = min(PROJ_SLAB, x_ref.shape[1])
    for r0 in range(0, x_ref.shape[1], slab):
        process(slice(r0, r0 + slab))


def _inproj(xs, modv, kind, g1, w_in_p, gains, gcq, gckv, wuq, wukvk, wukvv, mm, mx, mbb, rope):
    b, n, d = xs.shape
    tm = min(PROJ_TILE, n)
    use_rope = rope is not None
    const = lambda *shape: pl.BlockSpec(shape, lambda i, j: (0,) * len(shape))
    in_specs = [pl.BlockSpec((1, tm, d), lambda i, j: (i, j, 0)),
                pl.BlockSpec((1, 1, 8, d), lambda i, j: (i, kind, 0, 0)),
                const(1, d), const(d, IN_PAD), const(8, LANE), const(1, 256), const(1, LANE),
                const(256, 768), const(LANE, 768), const(LANE, 384),
                const(256, 256), const(256, 256), const(256, 256)]
    args = [xs, modv, g1, w_in_p, gains, gcq, gckv, wuq, wukvk, wukvv, mm, mx, mbb]
    if use_rope:
        in_specs.append(pl.BlockSpec((tm, 512), lambda i, j: (j, 0)))
        args.append(rope)
    blocks = (3, 1, 1, B_HEADS, B_HEADS, B_HEADS // 2)
    widths = (256, 128, 128)
    out_specs = ([pl.BlockSpec((1, nb, tm, LANE), lambda i, j: (i, 0, j, 0)) for nb in blocks]
                 + [pl.BlockSpec((1, tm, w), lambda i, j: (i, j, 0)) for w in widths])
    out_shape = ([jax.ShapeDtypeStruct((b, nb, n, LANE), BF16) for nb in blocks]
                 + [jax.ShapeDtypeStruct((b, n, w), BF16) for w in widths])
    return pl.pallas_call(
        functools.partial(_inproj_kernel, use_rope=use_rope),
        grid=(b, n // tm),
        in_specs=in_specs,
        out_specs=out_specs,
        out_shape=out_shape,
        compiler_params=_cparams(("parallel", "parallel")),
        name="inproj_rope" if use_rope else "inproj_ctx",
    )(*args)


_A_MAPS = (lambda h: h % 3, lambda h: 0, lambda h: 0)
_B_MAPS = (lambda h: h, lambda h: h, lambda h: h // 2)


def _qk(q, k):
    return lax.dot_general(q, k, (((1,), (1,)), ((), ())), preferred_element_type=F32)


def _half_mask(q, half):
    lane = lax.broadcasted_iota(jnp.int32, q.shape, 1)
    return jnp.where((lane // 64) == half, q, jnp.zeros_like(q))


def _fused_passes(q, chunks, s_w, s_r, mb_r):
    mx = None
    acc = None
    for k_ref, v_ref, st, sz, off in chunks:
        s = _qk(q, k_ref[0, 0, st:st + sz, :])
        s_w[:, off:off + sz] = s
        for t in range(sz // LANE):
            blk = s[:, t * LANE:(t + 1) * LANE]
            mx = blk if mx is None else jnp.maximum(mx, blk)
        ps = [jnp.exp2(s_r[:, off + t * LANE:off + (t + 1) * LANE] - mb_r).astype(BF16)
              for t in range(sz // LANE)]
        v_ones = jnp.concatenate([v_ref[0, 0, st:st + sz, :], jnp.ones((sz, LANE), BF16)], axis=1)
        part = jnp.dot(jnp.concatenate(ps, axis=1), v_ones, preferred_element_type=F32)
        acc = part if acc is None else acc + part
    mb_new = jnp.broadcast_to(jnp.max(mx, axis=1, keepdims=True), mx.shape)
    return mb_new, (acc[:, :LANE] / acc[:, LANE:LANE + 1]).astype(BF16)


def _key_chunks(kc_ref, vc_ref, kl_ref, vl_ref):
    n_ctx = kc_ref.shape[2]
    chunks = [(kc_ref, vc_ref, 0, n_ctx, 0)]
    if kl_ref is not None:
        for c in range(kl_ref.shape[2] // KEY_CHUNK):
            chunks.append((kl_ref, vl_ref, c * KEY_CHUNK, KEY_CHUNK, n_ctx + c * KEY_CHUNK))
    return chunks


def _dense_kernel(q_ref, kc_ref, kl_ref, vcp_ref, vlp_ref, vcc_ref, vlc_ref, o_ref,
                  sa_sc, sb_sc, mba_sc, mbb_sc, *, mask_q, n_tiles, tiles_per_sample, tiles_per_head, n_sub):
    step = pl.program_id(0)
    tq = sa_sc.shape[0]

    @pl.when(step == 0)
    def _():
        sa_sc[...] = jnp.zeros(sa_sc.shape, F32)
        mba_sc[...] = jnp.zeros(mba_sc.shape, F32)

    if mask_q:
        first = jnp.minimum(n_sub * step, n_tiles - n_sub)
        half = ((first % tiles_per_sample) // tiles_per_head) // 3
    for k in range(n_sub):
        rows = slice(k * tq, (k + 1) * tq)
        q = q_ref[0, 0, rows, :]
        if mask_q:
            q = _half_mask(q, half)
        v_refs = (vcp_ref, vlp_ref) if k == 0 else (vcc_ref, vlc_ref)
        chunks = _key_chunks(kc_ref, v_refs[0], kl_ref, v_refs[1])
        s_w, s_r, mb_w, mb_r = ((sb_sc, sa_sc, mbb_sc, mba_sc) if k % 2 == 0
                                else (sa_sc, sb_sc, mba_sc, mbb_sc))
        mb_new, o_ref[rows, :] = _fused_passes(q, chunks, s_w, s_r, mb_r[...])
        mb_w[...] = mb_new


def _dense_attention(q, kc, vc, kl, vl, n_heads, maps, mask_q, name):
    b, _, nq, _ = q.shape
    qmap, kmap, vmap = maps
    n_ctx, n_lat = kc.shape[2], kl.shape[2]
    tq = Q_TILE
    tph = nq // tq
    n_sub = next(n for n in (8, 4, 2) if tph % n == 0)
    assert tph % n_sub == 0
    tps = n_heads * tph
    n_tiles = b * tps
    first = lambda s: jnp.minimum(n_sub * s, n_tiles - n_sub)
    before = lambda s: jnp.maximum(n_sub * s - 1, 0)
    kv_blk = lambda n, fmap, tile: pl.BlockSpec(
        (1, 1, n, LANE), lambda s: (tile(s) // tps, fmap((tile(s) % tps) // tph), 0, 0))
    q_spec = pl.BlockSpec(
        (1, 1, n_sub * tq, LANE),
        lambda s: (first(s) // tps, qmap((first(s) % tps) // tph), (first(s) % tph) // n_sub, 0))
    return pl.pallas_call(
        functools.partial(_dense_kernel, mask_q=mask_q, n_tiles=n_tiles, tiles_per_sample=tps,
                          tiles_per_head=tph, n_sub=n_sub),
        grid=(n_tiles // n_sub + 1,),
        in_specs=[q_spec, kv_blk(n_ctx, kmap, first), kv_blk(n_lat, kmap, first),
                  kv_blk(n_ctx, vmap, before), kv_blk(n_lat, vmap, before),
                  kv_blk(n_ctx, vmap, first), kv_blk(n_lat, vmap, first)],
        out_specs=pl.BlockSpec((n_sub * tq, LANE), lambda s: (s, 0)),
        out_shape=jax.ShapeDtypeStruct(((n_tiles + n_sub) * tq, LANE), BF16),
        scratch_shapes=[pltpu.VMEM((tq, n_ctx + n_lat), F32), pltpu.VMEM((tq, n_ctx + n_lat), F32),
                        pltpu.VMEM((tq, LANE), F32), pltpu.VMEM((tq, LANE), F32)],
        compiler_params=_cparams(("arbitrary",)),
        name=name,
    )(q, kc, kl, vc, vl, vc, vl)


def _dense_ctx_kernel(q_ref, k_ref, v_ref, o_ref, *, n_heads, maps, mask_q):
    qmap, kmap, vmap = maps
    n_keys = k_ref.shape[2]
    ones = jnp.ones((n_keys, LANE), BF16)
    for h in range(n_heads):
        q = q_ref[0, qmap(h)]
        if mask_q:
            q = _half_mask(q, h // 3)
        s = _qk(q, k_ref[0, kmap(h)])
        tiles = [s[:, t * LANE:(t + 1) * LANE] for t in range(n_keys // LANE)]
        mx = functools.reduce(jnp.maximum, tiles)
        mb = jnp.broadcast_to(jnp.max(mx, axis=1, keepdims=True), mx.shape)
        p = jnp.concatenate([jnp.exp2(t - mb).astype(BF16) for t in tiles], axis=1)
        o = jnp.dot(p, jnp.concatenate([v_ref[0, vmap(h)], ones], axis=1), preferred_element_type=F32)
        o_ref[0, h] = (o[:, :LANE] / o[:, LANE:LANE + 1]).astype(BF16)


def _dense_attention_ctx(q, kc, vc, n_heads, maps, mask_q, name):
    b, nqb, nq, _ = q.shape
    whole = lambda a: pl.BlockSpec((1,) + a.shape[1:], lambda i: (i, 0, 0, 0))
    return pl.pallas_call(
        functools.partial(_dense_ctx_kernel, n_heads=n_heads, maps=maps, mask_q=mask_q),
        grid=(b,),
        in_specs=[whole(q), whole(kc), whole(vc)],
        out_specs=pl.BlockSpec((1, n_heads, nq, LANE), lambda i: (i, 0, 0, 0)),
        out_shape=jax.ShapeDtypeStruct((b, n_heads, nq, LANE), BF16),
        compiler_params=_cparams(("parallel",)),
        name=name,
    )(q, kc, vc)


def _window_kernel(q_ref, kc_ref, vc_ref, sink_ref, *rest, use_window):
    if use_window:
        kl_ref, vl_ref, o_ref = rest
    else:
        (o_ref,) = rest
    tq = ROW_TILE
    span = tq + 2 * WINDOW
    kcx = kc_ref[0]
    ones_c = jnp.ones((kcx.shape[0], LANE), BF16)
    v_aug = jnp.concatenate([vc_ref[0], ones_c], axis=1)
    lane = lax.broadcasted_iota(jnp.int32, (tq, LANE), 1)
    for ti in range(q_ref.shape[1] // tq):
        rows = slice(ti * tq, (ti + 1) * tq)
        if use_window:
            n_lat = kl_ref.shape[1]
            t0 = (pl.program_id(1) * (q_ref.shape[1] // tq) + ti) * tq
            start = pl.multiple_of(jnp.clip(t0 - WINDOW, 0, n_lat - span), LANE)
            qpos = t0 + lax.broadcasted_iota(jnp.int32, (tq, span), 0)
            kpos = start + lax.broadcasted_iota(jnp.int32, (tq, span), 1)
            valid = jnp.abs(qpos - kpos) <= WINDOW
            kw = kl_ref[0, pl.ds(start, span), :]
            v_all = jnp.concatenate(
                [v_aug, jnp.concatenate([vl_ref[0, pl.ds(start, span), :], jnp.ones((span, LANE), BF16)], axis=1)],
                axis=0)
        else:
            v_all = v_aug
        res = []
        for h in range(C_HEADS):
            half = h // (C_HEADS // C_KV_HEADS)
            q = q_ref[0, rows, (h % 2) * LANE:(h % 2 + 1) * LANE]
            q = jnp.where((lane < 64) if half == 0 else (lane >= 64), q, jnp.zeros_like(q))
            sink = sink_ref[0:1, h:h + 1]
            s = _qk(q, kcx)
            if use_window:
                s = jnp.concatenate([s, jnp.where(valid, _qk(q, kw), NEG_INF)], axis=1)
            mx = s[:, 0:LANE]
            for t in range(1, s.shape[1] // LANE):
                mx = jnp.maximum(mx, s[:, t * LANE:(t + 1) * LANE])
            m = jnp.maximum(jnp.max(mx, axis=1, keepdims=True), sink)
            mb = jnp.broadcast_to(m, (tq, LANE))
            p = jnp.concatenate([jnp.exp2(s[:, t * LANE:(t + 1) * LANE] - mb).astype(BF16)
                                 for t in range(s.shape[1] // LANE)], axis=1)
            o = jnp.dot(p, v_all, preferred_element_type=F32)
            res.append(o[:, :LANE] / (o[:, LANE:LANE + 1] + jnp.exp2(sink - m)))
        for ob in range(2):
            o_ref[0, rows, ob * LANE:(ob + 1) * LANE] = jnp.where(lane < 64, res[ob], res[2 + ob]).astype(BF16)


def _window_attention(q, kc, vc, kl, vl, sink, name):
    b, nq, qw = q.shape
    tq = min(4 * ROW_TILE, nq)
    n_ctx = kc.shape[1]
    use_window = kl is not None
    in_specs = [pl.BlockSpec((1, tq, qw), lambda i, j: (i, j, 0)),
                pl.BlockSpec((1, n_ctx, LANE), lambda i, j: (i, 0, 0)),
                pl.BlockSpec((1, n_ctx, LANE), lambda i, j: (i, 0, 0)),
                pl.BlockSpec((8, LANE), lambda i, j: (0, 0))]
    args = [q, kc, vc, sink]
    if use_window:
        n_lat = kl.shape[1]
        in_specs += [pl.BlockSpec((1, n_lat, LANE), lambda i, j: (i, 0, 0)),
                     pl.BlockSpec((1, n_lat, LANE), lambda i, j: (i, 0, 0))]
        args += [kl, vl]
    return pl.pallas_call(
        functools.partial(_window_kernel, use_window=use_window),
        grid=(b, nq // tq),
        in_specs=in_specs,
        out_specs=pl.BlockSpec((1, tq, qw), lambda i, j: (i, j, 0)),
        out_shape=jax.ShapeDtypeStruct((b, nq, qw), BF16),
        compiler_params=_cparams(("parallel", "parallel")),
        name=name,
    )(*args)


def _outproj_kernel(*refs):
    oa = refs[0:A_HEADS]
    ob = refs[A_HEADS:A_HEADS + B_HEADS]
    oc_ref, x_ref, mod_ref, g2n_ref, w_ref, wr_ref, xo_ref, h2_ref, aff_ref = refs[A_HEADS + B_HEADS:]
    wr = wr_ref[...]
    wr_hi = wr.astype(BF16)
    wr_lo = (wr - wr_hi.astype(F32)).astype(BF16)
    wr_both = jnp.concatenate([wr_hi, wr_lo], axis=1)

    def process(rows):
        n_rows = rows.stop - rows.start
        low = lax.broadcasted_iota(jnp.int32, (n_rows, LANE), 1) < 64
        parts = [jnp.where(low, oa[i][rows, :], oa[3 + i][rows, :]) for i in range(3)]
        parts += [jnp.where(low, ob[2 * i][rows, :], ob[2 * i + 1][rows, :]) for i in range(3)]
        parts.append(oc_ref[0, rows, :])
        mix = jnp.dot(jnp.concatenate(parts, axis=1), w_ref[...], preferred_element_type=F32)
        x = x_ref[0, rows, :] + mod_ref[0, 0, 2:3, :] * mix
        xo_ref[0, rows, :] = x
        y = x * lax.rsqrt(jnp.mean(x * x, axis=-1, keepdims=True) + EPS) * g2n_ref[...]
        h2 = y * (1.0 + mod_ref[0, 0, 4:5, :]) + mod_ref[0, 0, 3:4, :]
        h2_hi = h2.astype(BF16)
        h2_ref[0, rows, :] = h2_hi
        h2_lo = (h2 - h2_hi.astype(F32)).astype(BF16)
        both = jnp.dot(h2_hi, wr_both, preferred_element_type=F32)
        logits = both[:, :LANE] + both[:, LANE:] + jnp.dot(h2_lo, wr_hi, preferred_element_type=F32)
        logits = logits.T[0:aff_ref.shape[1], :]
        z = jnp.exp(logits - jnp.max(logits, axis=0, keepdims=True))
        aff_ref[0, :, rows] = z / jnp.sum(z, axis=0, keepdims=True)

    process(slice(0, x_ref.shape[1]))


def _outproj(oa, ob, oc, xs, modv, kind, g2n, w_mix, wr_t, pad_rows, name):
    b, n, d = xs.shape
    tm = min(Q_TILE, n)
    assert pad_rows % tm == 0 and n % tm == 0
    e = N_EXPERTS
    const = lambda *shape: pl.BlockSpec(shape, lambda i, j: (0,) * len(shape))
    row = lambda w: pl.BlockSpec((1, tm, w), lambda i, j: (i, j, 0))
    n_heads = A_HEADS
    head = lambda h: pl.BlockSpec((tm, LANE), lambda i, j: ((pad_rows + (i * n_heads + h) * n) // tm + j, 0))
    return pl.pallas_call(
        _outproj_kernel,
        grid=(b, n // tm),
        in_specs=[head(h) for h in range(A_HEADS)] + [head(h) for h in range(B_HEADS)] + [
                  row(256), row(d),
                  pl.BlockSpec((1, 1, 8, d), lambda i, j: (i, kind, 0, 0)),
                  const(1, d), const(d, d), const(d, LANE)],
        out_specs=[row(d), row(d), pl.BlockSpec((1, e, tm), lambda i, j: (i, 0, j))],
        out_shape=[jax.ShapeDtypeStruct((b, n, d), F32), jax.ShapeDtypeStruct((b, n, d), BF16),
                   jax.ShapeDtypeStruct((b, e, n), F32)],
        compiler_params=_cparams(("parallel", "parallel")),
        name=name,
    )(*([oa] * A_HEADS), *([ob] * B_HEADS), oc, xs, modv, g2n, w_mix, wr_t)


def _prefix_exclusive(mask_f, tri):
    e, t = mask_f.shape
    ck = tri.shape[0]
    carry = jnp.zeros((e, 1), F32)
    parts, offs = [], [carry]
    for c in range(t // ck):
        blk = mask_f[:, c * ck:(c + 1) * ck]
        parts.append(jnp.dot(blk.astype(BF16), tri, preferred_element_type=F32) + carry)
        carry = carry + jnp.sum(blk, axis=1, keepdims=True)
        offs.append(carry)
    return jnp.concatenate(parts, axis=1) if len(parts) > 1 else parts[0], offs


def _topk_kernel(aff_ref, slot_ref, cnt_ref, *, cap, slot_stride):
    aff = aff_ref[0]
    e, t = aff.shape
    kf = jnp.float32(cap)

    def count_ge(v):
        return jnp.sum((aff >= v).astype(F32), axis=1, keepdims=True)

    def search(i, lo_bits):
        cand = lo_bits | (jnp.int32(1) << (30 - i))
        return jnp.where(count_ge(pltpu.bitcast(cand, F32)) >= kf, cand, lo_bits)

    thr_bits = lax.fori_loop(0, 31, search, jnp.zeros((e, 1), jnp.int32))
    lo = pltpu.bitcast(thr_bits, F32)
    hi = pltpu.bitcast(thr_bits + 1, F32)

    def refine(i, carry):
        lo, hi = carry
        mid = 0.5 * (lo + hi)
        ok = count_ge(mid) >= kf
        return jnp.where(ok, mid, lo), jnp.where(ok, hi, mid)

    thr, _ = lax.fori_loop(0, TOPK_REFINE_STEPS, refine, (lo, hi))
    gt = aff > thr
    eq = aff == thr
    need = kf - jnp.sum(gt.astype(F32), axis=1, keepdims=True)

    r = lax.broadcasted_iota(jnp.int32, (TOK_CHUNK, TOK_CHUNK), 0)
    c = lax.broadcasted_iota(jnp.int32, (TOK_CHUNK, TOK_CHUNK), 1)
    tri = (r < c).astype(BF16)
    tie_rank, _ = _prefix_exclusive(eq.astype(F32), tri)
    sel = gt | (eq & (tie_rank < need))
    slot, offs = _prefix_exclusive(sel.astype(F32), tri)
    base = pl.program_id(0) * slot_stride
    slot_ref[0] = jnp.where(sel, slot.astype(jnp.int32) + base, -1)
    lane = lax.broadcasted_iota(jnp.int32, (e, LANE), 1)
    cnt = jnp.zeros((e, LANE), jnp.int32)
    for ci, off in enumerate(offs):
        cnt = jnp.where(lane == ci, off.astype(jnp.int32) + base, cnt)
    cnt_ref[0] = cnt


def _topk(aff_t, cap, slot_stride, name):
    b, e, t = aff_t.shape
    return pl.pallas_call(
        functools.partial(_topk_kernel, cap=cap, slot_stride=slot_stride),
        grid=(b,),
        in_specs=[pl.BlockSpec((1, e, t), lambda i: (i, 0, 0))],
        out_specs=[pl.BlockSpec((1, e, t), lambda i: (i, 0, 0)),
                   pl.BlockSpec((1, e, LANE), lambda i: (i, 0, 0))],
        out_shape=[jax.ShapeDtypeStruct((b, e, t), jnp.int32), jax.ShapeDtypeStruct((b, e, LANE), jnp.int32)],
        compiler_params=_cparams(("parallel",)),
        name=name,
    )(aff_t)


def _moe_kernel(cnt_ref, h_ref, slot_ref, slotp_ref, gate_ref, wg_ref, wu_ref, wd_ref, mod_ref, x_hbm, o_hbm,
                acc_sc, xe_sc, gs_sc, ye_sc, in_sem, out_sem, *, n_chunks, n_sb):
    bi = pl.program_id(0)
    ei = pl.program_id(1)
    n_b = pl.num_programs(0)
    n_e = pl.num_programs(1)

    def chunk_rows(c):
        return pl.ds(c * TOK_CHUNK, TOK_CHUNK)

    def load_dma(c):
        return pltpu.make_async_copy(x_hbm.at[bi, chunk_rows(c), :], acc_sc.at[chunk_rows(c), :], in_sem.at[c])

    def store_dma(sample, c):
        return pltpu.make_async_copy(acc_sc.at[chunk_rows(c), :], o_hbm.at[sample, chunk_rows(c), :],
                                     out_sem.at[c])

    def start_x_load():
        @pl.when(ei == 0)
        def _():
            for c in range(n_chunks):
                @pl.when(bi > 0)
                def _(c=c):
                    store_dma(bi - 1, c).wait()
                load_dma(c).start()

    def wait_for_x(at_expert):
        @pl.when(ei == at_expert)
        def _():
            for c in range(n_chunks):
                load_dma(c).wait()

    g2 = mod_ref[0, 0, 5:6, :]
    cap = n_sb * SLOT_BLOCK

    def counts_of(e):
        base = (bi * n_e + e) * 32
        return [cnt_ref[base + c] for c in range(n_chunks + 1)]

    def all_chunks_at_most(cs, limit):
        ok = cs[1] - cs[0] <= limit
        for c in range(1, n_chunks):
            ok = ok & (cs[c + 1] - cs[c] <= limit)
        return ok

    odd = (ei % 2) == 1
    counts = counts_of(ei)
    counts_other = counts_of(jnp.where(odd, ei - 1, ei + 1))
    pair_narrow = all_chunks_at_most(counts, PAIR_LIMIT) & all_chunks_at_most(counts_other, PAIR_LIMIT)

    def ffn():
        xe = xe_sc[0:cap, :].astype(BF16)
        f = wg_ref.shape[2]
        fb = 512
        y = None
        for f0 in range(0, f, fb):
            hg = jnp.dot(xe, wg_ref[0, :, f0:f0 + fb], preferred_element_type=F32)
            hu = jnp.dot(xe, wu_ref[0, :, f0:f0 + fb], preferred_element_type=F32)
            hid = (hg * jax.nn.sigmoid(hg) * hu).astype(BF16)
            part = jnp.dot(hid, wd_ref[0, f0:f0 + fb, :], preferred_element_type=F32)
            y = part if y is None else y + part
        return y

    def tok(c):
        return slice(c * TOK_CHUNK, (c + 1) * TOK_CHUNK)

    def slot_columns(ref):
        pad = jnp.zeros((LANE - n_chunks, TOK_CHUNK), F32)
        return jnp.concatenate([ref[0, 0].astype(F32), pad], axis=0).T

    @pl.when(pair_narrow)
    def _():
        xe_sc[...] = jnp.zeros_like(xe_sc)
        gs_sc[...] = jnp.zeros_like(gs_sc)
        srow = lax.broadcasted_iota(jnp.int32, (GATHER_WINDOW, TOK_CHUNK), 0)
        for c in range(n_chunks):
            g0 = pl.multiple_of((counts[c] // 8) * 8, 8)
            oh = (slot_ref[0, 0, c:c + 1, :] - g0) == srow
            xe_sc[pl.ds(g0, GATHER_WINDOW), :] += jnp.dot(
                oh.astype(BF16), h_ref[0, tok(c), :], preferred_element_type=F32)
            gs_sc[pl.ds(g0, GATHER_WINDOW), :] += jnp.sum(
                jnp.where(oh, gate_ref[0, 0, c:c + 1, :], 0.0), axis=1, keepdims=True)
        ye = ye_sc.at[ei % 2]
        ye[0:cap, :] = (ffn() * gs_sc[0:cap, :]).astype(BF16)
        ye[cap:cap + SLOT_BLOCK, :] = jnp.zeros((SLOT_BLOCK, ye_sc.shape[2]), BF16)
        start_x_load()

        @pl.when(odd)
        def _():
            wait_for_x(1)
            cols_prev = slot_columns(slotp_ref)
            cols_cur = slot_columns(slot_ref)
            scol = lax.broadcasted_iota(jnp.int32, (TOK_CHUNK, SLOT_BLOCK), 1).astype(F32)
            for c in range(n_chunks):
                w_prev = pl.multiple_of((counts_other[c] // 16) * 16, 16)
                w_cur = pl.multiple_of((counts[c] // 16) * 16, 16)
                oh_t = jnp.concatenate(
                    [((cols_prev[:, c:c + 1] - w_prev.astype(F32)) == scol).astype(BF16),
                     ((cols_cur[:, c:c + 1] - w_cur.astype(F32)) == scol).astype(BF16)], axis=1)
                rows = jnp.concatenate([ye_sc[0, pl.ds(w_prev, SLOT_BLOCK), :],
                                        ye_sc[1, pl.ds(w_cur, SLOT_BLOCK), :]], axis=0)
                acc_sc[tok(c), :] += g2 * jnp.dot(oh_t, rows, preferred_element_type=F32)

    @pl.when(jnp.logical_not(pair_narrow))
    def _():
        xe_sc[...] = jnp.zeros_like(xe_sc)
        gs_sc[...] = jnp.zeros_like(gs_sc)
        srow = lax.broadcasted_iota(jnp.int32, (SLOT_BLOCK, TOK_CHUNK), 0)
        base = (bi * n_e + ei) * 32

        def overlap(c, sb):
            return (cnt_ref[base + c] < (sb + 1) * SLOT_BLOCK) & (cnt_ref[base + c + 1] > sb * SLOT_BLOCK)

        def onehot(c, sb):
            return slot_ref[0, 0, pl.ds(c, 1), :] == (srow + sb * SLOT_BLOCK)

        def dyn_tok(c):
            return pl.ds(pl.multiple_of(c * TOK_CHUNK, TOK_CHUNK), TOK_CHUNK)

        def gather_chunk(c, carry):
            for sb in range(n_sb):
                @pl.when(overlap(c, sb))
                def _(sb=sb):
                    oh = onehot(c, sb)
                    rows = slice(sb * SLOT_BLOCK, (sb + 1) * SLOT_BLOCK)
                    xe_sc[rows] += jnp.dot(oh.astype(BF16), h_ref[0, dyn_tok(c), :], preferred_element_type=F32)
                    g = jnp.where(oh, gate_ref[0, 0, pl.ds(c, 1), :], 0.0)
                    gs_sc[rows] += jnp.sum(g, axis=1, keepdims=True)
            return carry

        lax.fori_loop(0, n_chunks, gather_chunk, 0)
        ye_sc[0, 0:cap, :] = (ffn() * gs_sc[0:cap, :]).astype(BF16)
        start_x_load()
        wait_for_x(0)

        def scatter_chunk(c, carry):
            for sb in range(n_sb):
                @pl.when(overlap(c, sb))
                def _(sb=sb):
                    oh_t = onehot(c, sb).astype(F32).T.astype(BF16)
                    rows = slice(sb * SLOT_BLOCK, (sb + 1) * SLOT_BLOCK)
                    acc_sc[dyn_tok(c), :] += g2 * jnp.dot(oh_t, ye_sc[0, rows, :], preferred_element_type=F32)
            return carry

        lax.fori_loop(0, n_chunks, scatter_chunk, 0)

    @pl.when(ei == n_e - 1)
    def _():
        for c in range(n_chunks):
            store_dma(bi, c).start()

    @pl.when((ei == n_e - 1) & (bi == n_b - 1))
    def _():
        for c in range(n_chunks):
            store_dma(bi, c).wait()


def _moe(h2, slot, gate, cnt, wg, wu, wd, xs, modv, kind, cap, name):
    b, t, d = h2.shape
    e = slot.shape[1]
    f = wg.shape[2]
    n_chunks = t // TOK_CHUNK
    n_sb = cap // SLOT_BLOCK
    slot4 = slot.reshape(b, e, n_chunks, TOK_CHUNK)
    gate4 = gate.reshape(b, e, n_chunks, TOK_CHUNK)
    cnt_flat = cnt[:, :, :32].reshape(-1)
    grid_spec = pltpu.PrefetchScalarGridSpec(
        num_scalar_prefetch=1,
        grid=(b, e),
        in_specs=[pl.BlockSpec((1, t, d), lambda i, j, s: (i, 0, 0)),
                  pl.BlockSpec((1, 1, n_chunks, TOK_CHUNK), lambda i, j, s: (i, j, 0, 0)),
                  pl.BlockSpec((1, 1, n_chunks, TOK_CHUNK), lambda i, j, s: (i, jnp.maximum(j - 1, 0), 0, 0)),
                  pl.BlockSpec((1, 1, n_chunks, TOK_CHUNK), lambda i, j, s: (i, j, 0, 0)),
                  pl.BlockSpec((1, d, f), lambda i, j, s: (j, 0, 0)),
                  pl.BlockSpec((1, d, f), lambda i, j, s: (j, 0, 0)),
                  pl.BlockSpec((1, f, d), lambda i, j, s: (j, 0, 0)),
                  pl.BlockSpec((1, 1, 8, d), lambda i, j, s: (i, kind, 0, 0)),
                  pl.BlockSpec(memory_space=pl.ANY)],
        out_specs=pl.BlockSpec(memory_space=pl.ANY),
        scratch_shapes=[pltpu.VMEM((t, d), F32),
                        pltpu.VMEM((cap + 2 * SLOT_BLOCK, d), F32), pltpu.VMEM((cap + 2 * SLOT_BLOCK, 1), F32),
                        pltpu.VMEM((2, cap + SLOT_BLOCK, d), BF16),
                        pltpu.SemaphoreType.DMA((n_chunks,)), pltpu.SemaphoreType.DMA((n_chunks,))],
    )
    return pl.pallas_call(
        functools.partial(_moe_kernel, n_chunks=n_chunks, n_sb=n_sb),
        grid_spec=grid_spec,
        out_shape=jax.ShapeDtypeStruct((b, t, d), F32),
        compiler_params=_cparams(("arbitrary", "arbitrary")),
        name=name,
    )(cnt_flat, h2, slot4, slot4, gate4, wg, wu, wd, modv, xs)


def _cast_kernel(w_ref, o_ref):
    o_ref[...] = w_ref[0].astype(BF16)


def _layer_bf16(w, layer):
    _, e, r, c = w.shape
    return pl.pallas_call(
        _cast_kernel,
        grid=(e,),
        in_specs=[pl.BlockSpec((1, 1, r, c), lambda i: (layer, i, 0, 0))],
        out_specs=pl.BlockSpec((1, r, c), lambda i: (i, 0, 0)),
        out_shape=jax.ShapeDtypeStruct((e, r, c), BF16),
        compiler_params=_cparams(("parallel",)),
        name="expert_weights_bf16",
    )(w)


def _pad_lanes(v, width):
    return jnp.pad(v, (0, width - v.shape[0]))


def kernel(x, c, ctx, c_ctx, w_ada, b_ada, norm1_g, norm2_g, w_in, a_q_norm, a_k_norm, b_cq_norm, b_ckv_norm, w_uq, w_ukv, b_qn_norm, b_kn_norm, b_qr_norm, b_kr_norm, c_q_norm, c_k_norm, c_sink, w_out, w_router, w_e_gate, w_e_up, w_e_down):
    b, t, d = x.shape
    n_ctx = ctx.shape[1]
    depth = w_ada.shape[0]
    assert n_ctx == ROW_TILE and t % KEY_CHUNK == 0 and t % GRID_W == 0
    cap = max(1, CAP_FACTOR * t // N_EXPERTS)
    cap_c = max(1, CAP_FACTOR * n_ctx // N_EXPERTS)
    assert cap % SLOT_BLOCK == 0 and (b * cap_c) % SLOT_BLOCK == 0

    rows = ((b + 1 + 7) // 8) * 8
    cc = jnp.zeros((rows, d), F32).at[:b].set(c).at[b].set(c_ctx)
    mod = _modulation(cc, w_ada, b_ada)

    rope = _rope_tables(t)
    mm, mx, mbb = (jnp.asarray(m, BF16) for m in _seg_matrices())
    in_perm = _in_perm()
    uq_perm = _uq_perm()
    ukv_pk, ukv_pv = _ukv_perms()
    oa_rows = np.asarray(_pair_cols(0, A_HEADS))
    oc_rows = np.asarray(_pair_cols(0, C_HEADS)) + 768
    sa = HEAD_DIM ** -0.5
    sb = (B_NOPE + B_ROPE) ** -0.5
    z32, z64 = jnp.zeros((32,), F32), jnp.zeros((64,), F32)

    for l in range(depth):
        last = l == depth - 1
        m6 = mod[l].reshape(rows, 6, d)
        m8 = jnp.pad(m6, ((0, 0), (0, 2), (0, 0)))
        modv = jnp.stack([jnp.broadcast_to(m8[b][None], (b, 8, d)), m8[:b]], axis=1)

        w_in_p = _take_cols(w_in[l], in_perm).astype(BF16)
        wuq = jnp.pad(_take_cols(w_uq[l], uq_perm), ((0, 256 - B_Q_RANK), (0, 0))).astype(BF16)
        wukvk = _take_cols(w_ukv[l], ukv_pk).astype(BF16)
        wukvv = jnp.take(w_ukv[l], jnp.asarray(ukv_pv, jnp.int32), axis=1).astype(BF16)
        gains = jnp.stack([
            jnp.tile(a_q_norm[l], 2) * (sa * LOG2E), jnp.tile(a_k_norm[l], 2),
            jnp.tile(c_q_norm[l], 2) * (sa * LOG2E), jnp.tile(c_k_norm[l], 2),
            jnp.concatenate([b_qn_norm[l] * (sb * LOG2E), b_qr_norm[l] * (sb * LOG2E), z32]),
            jnp.concatenate([b_kn_norm[l], z64]),
            jnp.concatenate([z64, b_kr_norm[l], z32]),
            jnp.zeros((LANE,), F32)])
        gcq = _pad_lanes(b_cq_norm[l], 256)[None]
        gckv = b_ckv_norm[l][None]
        g1 = norm1_g[l][None]
        g2n = norm2_g[l][None]
        mix_rows = np.concatenate([oa_rows, np.arange(384, 768), oc_rows])
        w_mix = jnp.take(w_out[l], jnp.asarray(mix_rows, jnp.int32), axis=0).astype(BF16)
        wr_t = jnp.pad(w_router[l], ((0, 0), (0, LANE - N_EXPERTS)))
        sink = jnp.zeros((8, LANE), F32).at[0, :C_HEADS].set(c_sink[l] * LOG2E)
        wg = _layer_bf16(w_e_gate, l)
        wu = _layer_bf16(w_e_up, l)
        wd = _layer_bf16(w_e_down, l)

        shared = (g1, w_in_p, gains, gcq, gckv, wuq, wukvk, wukvv, mm, mx, mbb)
        qa, ka, va, qb, kb, vb, qc, kc, vc = _inproj(x, modv, 1, *shared, rope)
        qac, kac, vac, qbc, kbc, vbc, qcc, kcc, vcc = _inproj(ctx, modv, 0, *shared, None)

        o_a = _dense_attention(qa, kac, vac, ka, va, A_HEADS, _A_MAPS, True, "attn_a")
        o_b = _dense_attention(qb, kbc, vbc, kb, vb, B_HEADS, _B_MAPS, False, "attn_b")
        o_c = _window_attention(qc, kcc, vcc, kc, vc, sink, "attn_c")
        x_mid, h2, aff = _outproj(o_a, o_b, o_c, x, modv, 1, g2n, w_mix, wr_t, Q_TILE, "outproj")
        slot, cnt = _topk(aff, cap, 0, "topk")
        x = _moe(h2, slot, aff, cnt, wg, wu, wd, x_mid, modv, 1, cap, "moe")

        if not last:
            o_ac = _dense_attention_ctx(qac, kac, vac, A_HEADS, _A_MAPS, True, "attn_a_ctx")
            o_bc = _dense_attention_ctx(qbc, kbc, vbc, B_HEADS, _B_MAPS, False, "attn_b_ctx")
            o_cc = _window_attention(qcc, kcc, vcc, None, None, sink, "attn_c_ctx")
            flat = lambda o: o.reshape(-1, LANE)
            c_mid, hc2, aff_c = _outproj(flat(o_ac), flat(o_bc), o_cc, ctx, modv, 0, g2n, w_mix, wr_t, 0,
                                         "outproj_ctx")
            slot_c, cnt_c = _topk(aff_c, cap_c, cap_c, "topk_ctx")
            e = N_EXPERTS
            slot_f = jnp.transpose(slot_c, (1, 0, 2)).reshape(1, e, b * n_ctx)
            gate_f = jnp.transpose(aff_c, (1, 0, 2)).reshape(1, e, b * n_ctx)
            starts = jnp.transpose(cnt_c[:, :, 0], (1, 0))
            cnt_f = jnp.concatenate([starts, cnt_c[b - 1, :, 1:2],
                                     jnp.zeros((e, LANE - b - 1), jnp.int32)], axis=1)[None]
            ctx = _moe(hc2.reshape(1, b * n_ctx, d), slot_f, gate_f, cnt_f, wg, wu, wd,
                       c_mid.reshape(1, b * n_ctx, d), modv, 0, b * cap_c, "moe_ctx").reshape(b, n_ctx, d)
    return x
```
